```python
import math, functools
import jax, jax.numpy as jnp
from jax import lax
import numpy as np

D_MODEL = 1024
BATCH = 8
SEQ = 2048
DEPTH = 1
DEC_BATCH = 128
DEC_SEQ = 8
PAST_LEN = 2048
PAGE_SIZE = 128

MLSTM_HEADS = 4
MLSTM_HEAD_DIM = D_MODEL // MLSTM_HEADS
MLSTM_WIDTH = MLSTM_HEADS * MLSTM_HEAD_DIM
MLSTM_CHUNK = 128
FOX_HEADS = 8
FOX_HEAD_DIM = D_MODEL // FOX_HEADS
FOX_WIDTH = FOX_HEADS * FOX_HEAD_DIM
Q_BLOCK = 128
N_GROUPS = 4
EXPERTS_PER_GROUP = 8
N_EXPERTS = N_GROUPS * EXPERTS_PER_GROUP
TOP_K_IN_GROUP = 2
D_EXPERT = D_MODEL // 4
RMS_EPS = 1e-6
SPLIT_SIZES = (MLSTM_WIDTH, MLSTM_WIDTH, MLSTM_WIDTH, MLSTM_WIDTH, MLSTM_HEADS, MLSTM_HEADS,
               FOX_WIDTH, FOX_WIDTH, FOX_WIDTH, FOX_HEADS, D_MODEL, D_MODEL)
N_IN = sum(SPLIT_SIZES)

kernel_name = "hybrid_mlstm_fox_hmoe_step"


def rmsnorm(x, g):
    xf = x.astype(jnp.float32)
    y = xf * lax.rsqrt(jnp.mean(xf * xf, axis=-1, keepdims=True) + RMS_EPS)
    return (y * g.astype(jnp.float32)).astype(x.dtype)


def mlstm_chunk(carry, xs):
    C, n, m = carry
    q, k, v, ig, lf = xs
    L = q.shape[2]
    b = jnp.cumsum(lf, axis=-1)
    causal = jnp.tril(jnp.ones((L, L), dtype=bool))
    log_w = jnp.where(causal, b[..., :, None] - b[..., None, :] + ig[..., None, :], -jnp.inf)
    log_inter = b + m[..., None]
    m_t = jnp.maximum(log_inter, jnp.max(log_w, axis=-1))
    w_intra = jnp.exp(log_w - m_t[..., None])
    w_inter = jnp.exp(log_inter - m_t)
    s = jnp.einsum('bhtd,bhsd->bhts', q, k) * w_intra
    num = w_inter[..., None] * jnp.einsum('bhtd,bhde->bhte', q, C) + jnp.einsum('bhts,bhse->bhte', s, v)
    den = w_inter * jnp.einsum('bhtd,bhd->bht', q, n) + jnp.sum(s, axis=-1)
    h = num / jnp.maximum(jnp.abs(den), jnp.exp(-m_t))[..., None]
    m_new = m_t[..., -1]
    decay_state = jnp.exp(b[..., -1] + m - m_new)
    w_rows = jnp.exp(b[..., -1:] - b + ig - m_new[..., None])
    C_new = decay_state[..., None, None] * C + jnp.einsum('bhs,bhsd,bhse->bhde', w_rows, k, v)
    n_new = decay_state[..., None] * n + jnp.einsum('bhs,bhsd->bhd', w_rows, k)
    return (C_new, n_new, m_new), h


def mlstm_sequence(q, k, v, ig, lf, C0, n0, m0):
    B, H, T, Dh = q.shape
    L = MLSTM_CHUNK if T % MLSTM_CHUNK == 0 else T
    n_chunks = T // L

    def chunkify(a):
        return jnp.moveaxis(a.reshape(a.shape[:2] + (n_chunks, L) + a.shape[3:]), 2, 0)

    carry0 = (C0.astype(jnp.float32), n0.astype(jnp.float32), m0.astype(jnp.float32))
    (C, n, m), h = lax.scan(mlstm_chunk, carry0,
                            (chunkify(q), chunkify(k), chunkify(v), chunkify(ig), chunkify(lf)))
    h = jnp.moveaxis(h, 0, 2).reshape(B, H, T, Dh)
    return h, C, n, m


def fox_attention(q, k, v, F_q, F_k, q_start):
    B, T, H, Dh = q.shape
    S = k.shape[1]
    k_pos = jnp.arange(S)
    q_pos = q_start + jnp.arange(T)
    Fk_t = jnp.transpose(F_k, (0, 2, 1))
    scale = Dh ** -0.5

    def attend(blk):
        qb, Fqb, pb = blk
        s = jnp.einsum('bqhd,bkhd->bhqk', qb, k).astype(jnp.float32) * scale
        s = s + jnp.transpose(Fqb, (0, 2, 1))[..., None] - Fk_t[:, :, None, :]
        s = jnp.where(k_pos[None, :] <= pb[:, None], s, -jnp.inf)
        p = jax.nn.softmax(s, axis=-1)
        return jnp.einsum('bhqk,bkhd->bqhd', p.astype(v.dtype), v)

    if T > Q_BLOCK and T % Q_BLOCK == 0:
        nb = T // Q_BLOCK
        blocks = (jnp.moveaxis(q.reshape(B, nb, Q_BLOCK, H, Dh), 1, 0),
                  jnp.moveaxis(F_q.reshape(B, nb, Q_BLOCK, H), 1, 0),
                  q_pos.reshape(nb, Q_BLOCK))
        out = lax.map(attend, blocks)
        return jnp.moveaxis(out, 0, 1).reshape(B, T, H, Dh)
    return attend((q, F_q, q_pos))


def hierarchical_moe(h, w_router_group, b_router_group, w_router_expert, b_router_expert, w_gate, w_up, w_down):
    B, T, D = h.shape
    t = h.reshape(B * T, D)
    g_prob = jax.nn.softmax((t @ w_router_group).astype(jnp.float32) + b_router_group.astype(jnp.float32), axis=-1)
    g_p, g_idx = lax.top_k(g_prob, 1)
    g_onehot = jax.nn.one_hot(g_idx[:, 0], N_GROUPS, dtype=jnp.float32)
    e_logits = ((t @ w_router_expert).astype(jnp.float32) + b_router_expert.astype(jnp.float32)
                ).reshape(B * T, N_GROUPS, EXPERTS_PER_GROUP)
    e_in = jnp.einsum('ng,nge->ne', g_onehot, e_logits)
    e_prob = jax.nn.softmax(e_in, axis=-1)
    e_p, e_idx = lax.top_k(e_prob, TOP_K_IN_GROUP)
    e_p = e_p / jnp.sum(e_p, axis=-1, keepdims=True)
    w_group = jnp.sum(jax.nn.one_hot(e_idx, EXPERTS_PER_GROUP, dtype=jnp.float32) * e_p[..., None], axis=1) * g_p
    combine = (g_onehot[:, :, None] * w_group[:, None, :]).astype(h.dtype)
    out = jnp.zeros_like(t)
    for g in range(N_GROUPS):
        sl = slice(g * EXPERTS_PER_GROUP, (g + 1) * EXPERTS_PER_GROUP)
        hg = jax.nn.silu(jnp.einsum('nd,edf->nef', t, w_gate[sl])) * jnp.einsum('nd,edf->nef', t, w_up[sl])
        out = out + jnp.einsum('nef,efd->nd', hg * combine[:, g, :, None], w_down[sl])
    return out.reshape(B, T, D)


def hybrid_layer(x, k_past, v_past, lf_past, C0, n0, m0,
                 norm1_g, w_in, b_igate, b_fgate_mlstm, b_fgate_fox, mlstm_norm_g, q_norm_g, k_norm_g,
                 w_branch_mlstm, w_branch_fox, w_out, norm2_g, w_router_group, b_router_group,
                 w_router_expert, b_router_expert, w_gate, w_up, w_down):
    f32 = jnp.float32
    B, T, _ = x.shape
    h = rmsnorm(x, norm1_g)
    proj = h @ w_in
    bounds = np.cumsum(SPLIT_SIZES)[:-1].tolist()
    mq, mk, mv, mo, mi, mf, fq, fk, fv, ff, ga, gb = jnp.split(proj, bounds, axis=-1)

    def heads_first(a):
        return a.reshape(B, T, MLSTM_HEADS, MLSTM_HEAD_DIM).transpose(0, 2, 1, 3).astype(f32)
    q_a = heads_first(mq)
    k_a = heads_first(mk) * (MLSTM_HEAD_DIM ** -0.5)
    v_a = heads_first(mv)
    ig = (mi.astype(f32) + b_igate.astype(f32)).transpose(0, 2, 1)
    lf_a = jax.nn.log_sigmoid(mf.astype(f32) + b_fgate_mlstm.astype(f32)).transpose(0, 2, 1)
    h_a, C, n, m = mlstm_sequence(q_a, k_a, v_a, ig, lf_a, C0, n0, m0)
    h_a = rmsnorm(h_a.transpose(0, 2, 1, 3), mlstm_norm_g).reshape(B, T, MLSTM_WIDTH)
    h_a = (jax.nn.sigmoid(mo.astype(f32)) * h_a).astype(x.dtype)
    branch_a = h_a @ w_branch_mlstm

    q_b = rmsnorm(fq.reshape(B, T, FOX_HEADS, FOX_HEAD_DIM), q_norm_g)
    k_b = rmsnorm(fk.reshape(B, T, FOX_HEADS, FOX_HEAD_DIM), k_norm_g)
    v_b = fv.reshape(B, T, FOX_HEADS, FOX_HEAD_DIM)
    lf_b = jax.nn.log_sigmoid(ff.astype(f32) + b_fgate_fox.astype(f32))
    past_len = k_past.shape[1]
    k_all = jnp.concatenate([k_past.astype(k_b.dtype), k_b], axis=1)
    v_all = jnp.concatenate([v_past.astype(v_b.dtype), v_b], axis=1)
    F_all = jnp.cumsum(jnp.concatenate([lf_past.astype(f32), lf_b], axis=1), axis=1)
    o_b = fox_attention(q_b, k_all, v_all, F_all[:, past_len:], F_all, past_len)
    branch_b = o_b.reshape(B, T, FOX_WIDTH).astype(x.dtype) @ w_branch_fox

    merged = jax.nn.sigmoid(ga) * branch_a + jax.nn.sigmoid(gb) * branch_b
    x = x + merged @ w_out
    x = x + hierarchical_moe(rmsnorm(x, norm2_g), w_router_group, b_router_group,
                             w_router_expert, b_router_expert, w_gate, w_up, w_down)
    return x, k_b, v_b, lf_b, C, n, m


def setup_inputs(seed: int = 0) -> dict:
    key = jax.random.key(seed)
    ks = jax.random.split(key, 32)
    n_pages = PAST_LEN // PAGE_SIZE
    n_phys = (DEC_BATCH * n_pages * 5) // 4

    def normal(k, shape, scale=1.0, shift=0.0):
        return shift + scale * jax.random.normal(k, shape, jnp.float32)

    page_table = jax.random.permutation(ks[9], n_phys)[:DEC_BATCH * n_pages].reshape(DEC_BATCH, n_pages).astype(jnp.int32)
    return {
        "x_prompt": normal(ks[0], (BATCH, SEQ, D_MODEL)),
        "x_sample": normal(ks[1], (DEC_BATCH, DEC_SEQ, D_MODEL)),
        "cache_k": normal(ks[2], (DEPTH, n_phys, PAGE_SIZE, FOX_HEADS, FOX_HEAD_DIM)),
        "cache_v": normal(ks[3], (DEPTH, n_phys, PAGE_SIZE, FOX_HEADS, FOX_HEAD_DIM)),
        "cache_lf": jax.nn.log_sigmoid(normal(ks[4], (DEPTH, n_phys, PAGE_SIZE, FOX_HEADS), 0.5, 3.0)),
        "state_C": normal(ks[5], (DEPTH, DEC_BATCH, MLSTM_HEADS, MLSTM_HEAD_DIM, MLSTM_HEAD_DIM), 0.1),
        "state_n": normal(ks[6], (DEPTH, DEC_BATCH, MLSTM_HEADS, MLSTM_HEAD_DIM), 0.1),
        "state_m": normal(ks[7], (DEPTH, DEC_BATCH, MLSTM_HEADS), 0.5, 1.0),
        "page_table": page_table,
        "norm1_g": normal(ks[10], (DEPTH, D_MODEL), 0.05, 1.0),
        "w_in": normal(ks[11], (DEPTH, D_MODEL, N_IN), D_MODEL ** -0.5),
        "b_igate": normal(ks[12], (DEPTH, MLSTM_HEADS), 0.1),
        "b_fgate_mlstm": normal(ks[13], (DEPTH, MLSTM_HEADS), 0.5, 3.0),
        "b_fgate_fox": normal(ks[14], (DEPTH, FOX_HEADS), 0.5, 3.0),
        "mlstm_norm_g": normal(ks[15], (DEPTH, MLSTM_HEADS, MLSTM_HEAD_DIM), 0.05, 1.0),
        "q_norm_g": normal(ks[16], (DEPTH, FOX_HEAD_DIM), 0.05, 1.0),
        "k_norm_g": normal(ks[17], (DEPTH, FOX_HEAD_DIM), 0.05, 1.0),
        "w_branch_mlstm": normal(ks[18], (DEPTH, MLSTM_WIDTH, D_MODEL), MLSTM_WIDTH ** -0.5),
        "w_branch_fox": normal(ks[19], (DEPTH, FOX_WIDTH, D_MODEL), FOX_WIDTH ** -0.5),
        "w_out": normal(ks[20], (DEPTH, D_MODEL, D_MODEL), D_MODEL ** -0.5),
        "norm2_g": normal(ks[21], (DEPTH, D_MODEL), 0.05, 1.0),
        "w_router_group": normal(ks[22], (DEPTH, D_MODEL, N_GROUPS), D_MODEL ** -0.5),
        "b_router_group": normal(ks[23], (DEPTH, N_GROUPS), 0.01),
        "w_router_expert": normal(ks[24], (DEPTH, D_MODEL, N_EXPERTS), D_MODEL ** -0.5),
        "b_router_expert": normal(ks[25], (DEPTH, N_EXPERTS), 0.01),
        "w_gate": normal(ks[26], (DEPTH, N_EXPERTS, D_MODEL, D_EXPERT), D_MODEL ** -0.5),
        "w_up": normal(ks[27], (DEPTH, N_EXPERTS, D_MODEL, D_EXPERT), D_MODEL ** -0.5),
        "w_down": normal(ks[28], (DEPTH, N_EXPERTS, D_EXPERT, D_MODEL), D_EXPERT ** -0.5),
    }


def reference(x_prompt, x_sample, cache_k, cache_v, cache_lf, state_C, state_n, state_m, page_table,
              norm1_g, w_in, b_igate, b_fgate_mlstm, b_fgate_fox, mlstm_norm_g, q_norm_g, k_norm_g,
              w_branch_mlstm, w_branch_fox, w_out, norm2_g, w_router_group, b_router_group,
              w_router_expert, b_router_expert, w_gate, w_up, w_down):
    f32 = jnp.float32
    bp = x_prompt.shape[0]
    bs = x_sample.shape[0]
    n_pages = page_table.shape[1]
    past = n_pages * PAGE_SIZE
    xp, xs = x_prompt, x_sample
    kp_l, vp_l, lfp_l, Cp_l, np_l, mp_l = [], [], [], [], [], []
    ks_l, vs_l, lfs_l, Cs_l, ns_l, ms_l = [], [], [], [], [], []
    for l in range(DEPTH):
        layer_params = (norm1_g[l], w_in[l], b_igate[l], b_fgate_mlstm[l], b_fgate_fox[l], mlstm_norm_g[l],
                        q_norm_g[l], k_norm_g[l], w_branch_mlstm[l], w_branch_fox[l], w_out[l], norm2_g[l],
                        w_router_group[l], b_router_group[l], w_router_expert[l], b_router_expert[l],
                        w_gate[l], w_up[l], w_down[l])
        empty_kv = jnp.zeros((bp, 0, FOX_HEADS, FOX_HEAD_DIM), xp.dtype)
        empty_lf = jnp.zeros((bp, 0, FOX_HEADS), f32)
        C0 = jnp.zeros((bp, MLSTM_HEADS, MLSTM_HEAD_DIM, MLSTM_HEAD_DIM), f32)
        n0 = jnp.zeros((bp, MLSTM_HEADS, MLSTM_HEAD_DIM), f32)
        m0 = jnp.zeros((bp, MLSTM_HEADS), f32)
        xp, kp, vp, lfp, Cp, nstp, mp = hybrid_layer(xp, empty_kv, empty_kv, empty_lf, C0, n0, m0, *layer_params)
        k_past = cache_k[l][page_table].reshape(bs, past, FOX_HEADS, FOX_HEAD_DIM)
        v_past = cache_v[l][page_table].reshape(bs, past, FOX_HEADS, FOX_HEAD_DIM)
        lf_past = cache_lf[l][page_table].reshape(bs, past, FOX_HEADS)
        xs, kss, vss, lfss, Cs, nsts, ms = hybrid_layer(xs, k_past, v_past, lf_past,
                                                         state_C[l], state_n[l], state_m[l], *layer_params)
        kp_l.append(kp); vp_l.append(vp); lfp_l.append(lfp); Cp_l.append(Cp); np_l.append(nstp); mp_l.append(mp)
        ks_l.append(kss); vs_l.append(vss); lfs_l.append(lfss); Cs_l.append(Cs); ns_l.append(nsts); ms_l.append(ms)
    return (xp, xs,
            jnp.stack(kp_l), jnp.stack(vp_l), jnp.stack(lfp_l),
            jnp.stack(ks_l), jnp.stack(vs_l), jnp.stack(lfs_l),
            jnp.stack(Cp_l), jnp.stack(np_l), jnp.stack(mp_l),
            jnp.stack(Cs_l), jnp.stack(ns_l), jnp.stack(ms_l))
```

```python
import functools

import jax
import jax.numpy as jnp
from jax import lax
from jax.experimental import pallas as pl
from jax.experimental.pallas import tpu as pltpu

F32 = jnp.float32
BF16 = jnp.bfloat16
I32 = jnp.int32
RMS_EPS = 1e-6
LANES = 128
MLSTM_CHUNK = 128
N_GROUPS = 4
EXPERTS_PER_GROUP = 8
VMEM_LIMIT = 56 * 1024 * 1024
NEG_INF = float("-inf")


def _params(sem, vmem=VMEM_LIMIT):
    return pltpu.CompilerParams(dimension_semantics=sem, vmem_limit_bytes=vmem)


def _log_sigmoid(x):
    return -(jnp.maximum(-x, 0.0) + jnp.log1p(jnp.exp(-jnp.abs(x))))


def _sigmoid(x):
    return 1.0 / (1.0 + jnp.exp(-x))


def _nt_dot(a, b):
    return lax.dot_general(a, b, (((1,), (1,)), ((), ())), preferred_element_type=F32)


def _tn_dot(a, b):
    return lax.dot_general(a, b, (((0,), (0,)), ((), ())), preferred_element_type=F32)


def _col_to_row(col, eye):
    return jnp.sum(jnp.where(eye, col, 0.0), axis=0, keepdims=True)


def _row_to_col(row, eye):
    return jnp.sum(jnp.where(eye, row, 0.0), axis=1, keepdims=True)


def _inproj_kernel(x_ref, g1_ref, w_ref, bias_ref, qg_ref, kg_ref,
                   pm_ref, fq_ref, k3_ref, v3_ref, kb_ref, vb_ref, gate_ref, small_ref,
                   *, d, mh, fh, fd, k_scale, q_scale):
    x = x_ref[...]
    ms = jnp.mean(x * x, axis=-1, keepdims=True)
    h = (x * lax.rsqrt(ms + RMS_EPS) * g1_ref[...]).astype(BF16)

    def proj(c0, width):
        return jnp.dot(h, w_ref[:, c0:c0 + width], preferred_element_type=F32)

    pm_ref[:, 0:d] = proj(0, d).astype(pm_ref.dtype)
    pm_ref[:, d:2 * d] = (proj(d, d) * k_scale).astype(pm_ref.dtype)
    pm_ref[:, 2 * d:3 * d] = proj(2 * d, d).astype(pm_ref.dtype)
    pm_ref[:, 3 * d:4 * d] = proj(3 * d, d).astype(pm_ref.dtype)

    def head_norm(a, g):
        ms_h = jnp.mean(a * a, axis=-1, keepdims=True)
        return a * lax.rsqrt(ms_h + RMS_EPS) * g

    fq = proj(4 * d, d)
    fk = proj(5 * d, d)
    fv = proj(6 * d, d)
    for hh in range(fh):
        sl = slice(hh * fd, (hh + 1) * fd)
        fq_ref[:, sl] = (head_norm(fq[:, sl], qg_ref[...]) * q_scale).astype(fq_ref.dtype)
        kn = head_norm(fk[:, sl], kg_ref[...])
        k3_ref[:, hh, :] = kn
        kb_ref[:, sl] = kn.astype(kb_ref.dtype)
        v3_ref[:, hh, :] = fv[:, sl]
    vb_ref[...] = fv.astype(vb_ref.dtype)

    gate_ref[...] = _sigmoid(proj(7 * d, 2 * d)).astype(gate_ref.dtype)

    sm = proj(9 * d, LANES) + bias_ref[...]
    lane = lax.broadcasted_iota(I32, sm.shape, 1)
    sm = jnp.where(lane < mh, sm, jnp.where(lane < 2 * mh + fh, _log_sigmoid(sm), 0.0))
    small_ref[...] = sm


def _inproj(x2d, g1, w_all, bias_s, qg, kg, *, mh, fh, act_dtype, tm=256):
    n, d = x2d.shape
    fd = d // fh
    md = d // mh
    kern = functools.partial(_inproj_kernel, d=d, mh=mh, fh=fh, fd=fd,
                             k_scale=md ** -0.5, q_scale=fd ** -0.5)
    row = lambda i: (i, 0)
    const = lambda i: (0, 0)
    return pl.pallas_call(
        kern, grid=(n // tm,),
        in_specs=[pl.BlockSpec((tm, d), row),
                  pl.BlockSpec((1, d), const),
                  pl.BlockSpec(w_all.shape, const, pipeline_mode=pl.Buffered(1)),
                  pl.BlockSpec((1, LANES), const),
                  pl.BlockSpec((1, fd), const),
                  pl.BlockSpec((1, fd), const)],
        out_specs=[pl.BlockSpec((tm, 4 * d), row),
                   pl.BlockSpec((tm, d), row),
                   pl.BlockSpec((tm, fh, fd), lambda i: (i, 0, 0)),
                   pl.BlockSpec((tm, fh, fd), lambda i: (i, 0, 0)),
                   pl.BlockSpec((tm, d), row),
                   pl.BlockSpec((tm, d), row),
                   pl.BlockSpec((tm, 2 * d), row),
                   pl.BlockSpec((tm, LANES), row)],
        out_shape=[jax.ShapeDtypeStruct((n, 4 * d), act_dtype),
                   jax.ShapeDtypeStruct((n, d), act_dtype),
                   jax.ShapeDtypeStruct((n, fh, fd), F32),
                   jax.ShapeDtypeStruct((n, fh, fd), F32),
                   jax.ShapeDtypeStruct((n, d), act_dtype),
                   jax.ShapeDtypeStruct((n, d), act_dtype),
                   jax.ShapeDtypeStruct((n, 2 * d), BF16),
                   jax.ShapeDtypeStruct((n, LANES), F32)],
        compiler_params=_params(("parallel",)),
        name="inproj",
    )(x2d, g1, w_all, bias_s, qg, kg)


def _mlstm_kernel(*refs, L, mh, md, has_init):
    if has_init:
        (q_ref, k_ref, v_ref, o_ref, s_ref, g_ref, c0_ref, n0_ref, m0_ref,
         h_ref, c_ref, n_ref, m_ref) = refs
    else:
        q_ref, k_ref, v_ref, o_ref, s_ref, g_ref, h_ref, c_ref, n_ref, m_ref = refs

    @pl.when(pl.program_id(1) == 0)
    def _():
        if has_init:
            c_ref[...] = c0_ref[...]
            n_ref[...] = n0_ref[...]
            m_ref[...] = m0_ref[...]
        else:
            c_ref[...] = jnp.zeros_like(c_ref)
            n_ref[...] = jnp.zeros_like(n_ref)
            m_ref[...] = jnp.zeros_like(m_ref)

    s = s_ref[...]
    ri = lax.broadcasted_iota(I32, (L, L), 0)
    ci = lax.broadcasted_iota(I32, (L, L), 1)
    tri = ci <= ri
    eye = ci == ri
    m_all = m_ref[0]
    lane = lax.broadcasted_iota(I32, m_all.shape, 1)
    m_out = m_all
    for hh in range(mh):
        sl = slice(hh * md, (hh + 1) * md)
        ig_col = s[:, hh:hh + 1]
        lf_col = s[:, mh + hh:mh + hh + 1]
        lf_row = _col_to_row(lf_col, eye)
        ig_row = _col_to_row(ig_col, eye)
        b_col = jnp.sum(jnp.where(tri, lf_row, 0.0), axis=1, keepdims=True)
        b_row = _col_to_row(b_col, eye)
        m_prev = m_all[:, hh:hh + 1]
        log_w = jnp.where(tri, b_col - b_row + ig_row, NEG_INF)
        log_inter = b_col + m_prev
        m_t = jnp.maximum(log_inter, jnp.max(log_w, axis=1, keepdims=True))
        w_intra = jnp.exp(log_w - m_t)
        w_inter = jnp.exp(log_inter - m_t)
        q = q_ref[:, sl].astype(BF16)
        k = k_ref[:, sl].astype(BF16)
        v = v_ref[:, sl].astype(BF16)
        sm = _nt_dot(q, k) * w_intra
        c_prev = c_ref[0, hh]
        n_prev = n_ref[0, hh:hh + 1, :]
        num = (w_inter * jnp.dot(q, c_prev.astype(BF16), preferred_element_type=F32)
               + jnp.dot(sm.astype(BF16), v, preferred_element_type=F32))
        den = (w_inter * jnp.sum(q.astype(F32) * n_prev, axis=1, keepdims=True)
               + jnp.sum(sm, axis=1, keepdims=True))
        hv = num / jnp.maximum(jnp.abs(den), jnp.exp(-m_t))
        m_new = m_t[L - 1:L, :]
        b_last = b_col[L - 1:L, :]
        decay = jnp.exp(b_last + m_prev - m_new)
        w_rows = jnp.exp(b_last - b_col + ig_col - m_new)
        kw = k.astype(F32) * w_rows
        c_ref[0, hh] = decay * c_prev + _tn_dot(kw.astype(BF16), v)
        n_ref[0, hh:hh + 1, :] = decay * n_prev + jnp.sum(kw, axis=0, keepdims=True)
        m_out = jnp.where(lane == hh, m_new, m_out)
        ms = jnp.mean(hv * hv, axis=-1, keepdims=True)
        hn = hv * lax.rsqrt(ms + RMS_EPS) * g_ref[:, sl]
        h_ref[:, sl] = (_sigmoid(o_ref[:, sl].astype(F32)) * hn).astype(h_ref.dtype)
    m_ref[0] = m_out


def _mlstm(pm, small, g_m, *, batch, seq, mh, row0=0, init=None, out_dtype=BF16):
    d = pm.shape[1] // 4
    md = d // mh
    L = MLSTM_CHUNK if seq % MLSTM_CHUNK == 0 else seq
    nc = seq // L
    r0 = row0 // L
    sec = lambda j: (lambda b, c: (r0 + b * nc + c, j))
    in_specs = [pl.BlockSpec((L, d), sec(0)), pl.BlockSpec((L, d), sec(1)),
                pl.BlockSpec((L, d), sec(2)), pl.BlockSpec((L, d), sec(3)),
                pl.BlockSpec((L, LANES), sec(0)),
                pl.BlockSpec((1, d), lambda b, c: (0, 0))]
    args = [pm, pm, pm, pm, small, g_m]
    state_specs = [pl.BlockSpec((1, mh, md, md), lambda b, c: (b, 0, 0, 0)),
                   pl.BlockSpec((1, mh, md), lambda b, c: (b, 0, 0)),
                   pl.BlockSpec((1, 1, LANES), lambda b, c: (b, 0, 0))]
    if init is not None:
        in_specs += state_specs
        args += list(init)
    kern = functools.partial(_mlstm_kernel, L=L, mh=mh, md=md, has_init=init is not None)
    return pl.pallas_call(
        kern, grid=(batch, nc),
        in_specs=in_specs,
        out_specs=[pl.BlockSpec((L, d), lambda b, c: (b * nc + c, 0))] + state_specs,
        out_shape=[jax.ShapeDtypeStruct((batch * seq, d), out_dtype),
                   jax.ShapeDtypeStruct((batch, mh, md, md), F32),
                   jax.ShapeDtypeStruct((batch, mh, md), F32),
                   jax.ShapeDtypeStruct((batch, 1, LANES), F32)],
        compiler_params=_params(("parallel", "arbitrary")),
        name="mlstm",
    )(*args)


def _cumsum_kernel(x_ref, o_ref, *, chunk, carry):
    rows, t = x_ref.shape
    si = lax.broadcasted_iota(I32, (chunk, chunk), 0)
    ti = lax.broadcasted_iota(I32, (chunk, chunk), 1)
    tri = (si <= ti).astype(F32)
    run = jnp.zeros((rows, 1), F32)
    for j in range(t // chunk):
        sl = slice(j * chunk, (j + 1) * chunk)
        loc = jnp.dot(x_ref[:, sl], tri, precision=lax.Precision.HIGHEST, preferred_element_type=F32)
        if carry:
            loc = loc + run
            run = loc[:, chunk - 1:chunk]
        o_ref[:, sl] = loc


def _cumsum_lanes(x, *, carry, block_rows):
    rows, t = x.shape
    kern = functools.partial(_cumsum_kernel, chunk=LANES, carry=carry)
    return pl.pallas_call(
        kern, grid=(rows // block_rows,),
        in_specs=[pl.BlockSpec((block_rows, t), lambda i: (i, 0))],
        out_specs=pl.BlockSpec((block_rows, t), lambda i: (i, 0)),
        out_shape=jax.ShapeDtypeStruct((rows, t), F32),
        compiler_params=_params(("parallel",)),
        name="cumsum",
    )(x)


def _fox_prompt_kernel(q_ref, k_ref, v_ref, f_ref, o_ref, *, seq, tq):
    nq = seq // tq
    fd = q_ref.shape[1]
    ri = lax.broadcasted_iota(I32, (tq, tq), 0)
    ci = lax.broadcasted_iota(I32, (tq, tq), 1)
    eye = ci == ri
    causal = ci <= ri

    def softmax_step(carry, s, v):
        m, l, acc = carry
        m_new = jnp.maximum(m, jnp.max(s, axis=1, keepdims=True))
        a = jnp.exp(m - m_new)
        p = jnp.exp(s - m_new)
        l = a * l + jnp.sum(p, axis=1, keepdims=True)
        acc = a * acc + jnp.dot(p.astype(BF16), v, preferred_element_type=F32)
        return m_new, l, acc

    def q_body(qi, _):
        q0 = pl.multiple_of(qi * tq, tq)
        q = q_ref[pl.ds(q0, tq), :]
        f_row_q = f_ref[qi]
        f_col_q = _row_to_col(f_row_q, eye)

        def kv_body(kj, carry):
            k0 = pl.multiple_of(kj * tq, tq)
            s = _nt_dot(q, k_ref[pl.ds(k0, tq), :]) + (f_col_q - f_ref[kj])
            return softmax_step(carry, s, v_ref[pl.ds(k0, tq), :])

        init = (jnp.full((tq, 1), NEG_INF, F32), jnp.zeros((tq, 1), F32), jnp.zeros((tq, fd), F32))
        carry = lax.fori_loop(0, qi, kv_body, init)
        s = _nt_dot(q, k_ref[pl.ds(q0, tq), :]) + (f_col_q - f_row_q)
        s = jnp.where(causal, s, NEG_INF)
        _, l, acc = softmax_step(carry, s, v_ref[pl.ds(q0, tq), :])
        o_ref[pl.ds(q0, tq), :] = (acc / l).astype(o_ref.dtype)
        return 0

    lax.fori_loop(0, nq, q_body, 0)


def _fox_prompt(fq, kb, vb, f_rows, *, batch, seq, fh, tq=256):
    n, d = fq.shape
    fd = d // fh
    nq = seq // tq
    blk = lambda b, h: (b, h)
    kern = functools.partial(_fox_prompt_kernel, seq=seq, tq=tq)
    return pl.pallas_call(
        kern, grid=(batch, fh),
        in_specs=[pl.BlockSpec((seq, fd), blk), pl.BlockSpec((seq, fd), blk), pl.BlockSpec((seq, fd), blk),
                  pl.BlockSpec((None, nq, 1, tq), lambda b, h: (b * fh + h, 0, 0, 0))],
        out_specs=pl.BlockSpec((seq, fd), blk),
        out_shape=jax.ShapeDtypeStruct((n, d), BF16),
        compiler_params=_params(("parallel", "parallel")),
        name="fox_prompt",
    )(fq, kb, vb, f_rows.reshape(batch * fh, nq, 1, tq))


def _fox_sample_kernel(pt_ref, q_ref, kn_ref, vn_ref, s_ref, *rest, n_pages, page, fh, fd, ts, lf0):
    floc = rest[0:n_pages]
    kpages = rest[n_pages:2 * n_pages]
    vpages = rest[2 * n_pages:3 * n_pages]
    o_ref = rest[3 * n_pages]
    p_scr, pn_scr, l_scr = rest[3 * n_pages + 1:]
    d = fh * fd
    rows = fh * ts
    phase = pl.program_id(1)

    @pl.when(phase == 0)
    def _():
        q = q_ref[...].astype(F32)
        qt = jnp.concatenate([q] * fh, axis=0)
        r_head = lax.broadcasted_iota(I32, (rows, d), 0) // ts
        c_head = lax.broadcasted_iota(I32, (rows, d), 1) // fd
        qbd = jnp.where(r_head == c_head, qt, 0.0).astype(BF16)

        off = jnp.zeros((fh, 1), F32)
        fk_pages = []
        for i in range(n_pages):
            fp = floc[i][...] + off
            fk_pages.append(fp)
            off = fp[:, page - 1:page]
        fk = jnp.concatenate(fk_pages, axis=1)
        fk_rows = jnp.concatenate(
            [jnp.broadcast_to(fk[h:h + 1, :], (ts, fk.shape[1])) for h in range(fh)], axis=0)

        sm = s_ref[...]
        ri = lax.broadcasted_iota(I32, (ts, ts), 0)
        ci = lax.broadcasted_iota(I32, (ts, ts), 1)
        eye = ci == ri
        tri = ci <= ri
        fq_cols, bias_new = [], []
        for h in range(fh):
            lf_col = sm[:, lf0 + h:lf0 + h + 1]
            lf_row = _col_to_row(lf_col, eye)
            cum_col = jnp.sum(jnp.where(tri, lf_row, 0.0), axis=1, keepdims=True)
            fq_h = off[h:h + 1, :] + cum_col
            fq_cols.append(fq_h)
            bias_new.append(fq_h - _col_to_row(fq_h, eye))
        fq_col = jnp.concatenate(fq_cols, axis=0)
        bias_n = jnp.concatenate(bias_new, axis=0)
        causal_n = jnp.concatenate([tri] * fh, axis=0)

        s_parts = []
        for i in range(0, n_pages, 2):
            kp = jnp.concatenate(
                [jnp.concatenate([kpages[i + j][:, h, :] for h in range(fh)], axis=1) for j in range(2)],
                axis=0).astype(BF16)
            s_parts.append(_nt_dot(qbd, kp))
        s_past = jnp.concatenate(s_parts, axis=1) + (fq_col - fk_rows)
        s_new = _nt_dot(qbd, kn_ref[...].astype(BF16)) + bias_n
        s_new = jnp.where(causal_n, s_new, NEG_INF)
        m = jnp.maximum(jnp.max(s_past, axis=1, keepdims=True), jnp.max(s_new, axis=1, keepdims=True))
        p_past = jnp.exp(s_past - m)
        p_new = jnp.exp(s_new - m)
        l_scr[...] = jnp.sum(p_past, axis=1, keepdims=True) + jnp.sum(p_new, axis=1, keepdims=True)
        p_scr[...] = p_past.astype(BF16)
        pn_scr[...] = p_new

    @pl.when(phase == 1)
    def _():
        acc = jnp.dot(pn_scr[...].astype(BF16), vn_ref[...].astype(BF16), preferred_element_type=F32)
        for i in range(0, n_pages, 2):
            vp = jnp.concatenate(
                [jnp.concatenate([vpages[i + j][:, h, :] for h in range(fh)], axis=1) for j in range(2)],
                axis=0).astype(BF16)
            acc = acc + jnp.dot(p_scr[:, i * page:(i + 2) * page], vp, preferred_element_type=F32)
        acc = acc / l_scr[...]
        o_ref[...] = jnp.concatenate(
            [acc[h * ts:(h + 1) * ts, h * fd:(h + 1) * fd] for h in range(fh)], axis=1)


def _fox_sample(page_table, fq, kb, vb, small, floc, cache_k, cache_v, *, fh, lf0):
    bs, n_pages = page_table.shape
    n, d = fq.shape
    ts = n // bs
    fd = d // fh
    page = cache_k.shape[1]
    rows = fh * ts
    tok = lambda b, ph, pt: (b, 0)

    def kmap(i):
        return lambda b, ph, pt: (pt[b, i], 0, 0, 0)

    def vmap_(i):
        return lambda b, ph, pt: (pt[jnp.maximum(b - 1 + ph, 0), i], 0, 0, 0)

    def fmap(i):
        return lambda b, ph, pt: (pt[b, i], 0, 0)

    in_specs = ([pl.BlockSpec((ts, d), tok), pl.BlockSpec((ts, d), tok), pl.BlockSpec((ts, d), tok),
                 pl.BlockSpec((ts, LANES), tok)]
                + [pl.BlockSpec((None, fh, page), fmap(i)) for i in range(n_pages)]
                + [pl.BlockSpec((None, page, fh, fd), kmap(i)) for i in range(n_pages)]
                + [pl.BlockSpec((None, page, fh, fd), vmap_(i)) for i in range(n_pages)])
    kern = functools.partial(_fox_sample_kernel, n_pages=n_pages, page=page, fh=fh, fd=fd, ts=ts, lf0=lf0)
    return pl.pallas_call(
        kern,
        grid_spec=pltpu.PrefetchScalarGridSpec(
            num_scalar_prefetch=1, grid=(bs, 2), in_specs=in_specs,
            out_specs=pl.BlockSpec((ts, d), tok),
            scratch_shapes=[pltpu.VMEM((rows, n_pages * page), BF16),
                            pltpu.VMEM((rows, ts), F32),
                            pltpu.VMEM((rows, 1), F32)]),
        out_shape=jax.ShapeDtypeStruct((n, d), F32),
        compiler_params=_params(("arbitrary", "arbitrary")),
        name="fox_sample",
    )(page_table, fq, kb, vb, small, *([floc] * n_pages), *([cache_k] * n_pages), *([cache_v] * n_pages))


def _post_kernel(x_ref, ha_ref, ob_ref, gate_ref, wa_ref, wb_ref, wo_ref, g2_ref, wr_ref, br_ref,
                 x2_ref, xn_ref, route_ref, *, d, n_exp, n_groups):
    ba = jnp.dot(ha_ref[...].astype(BF16), wa_ref[...], preferred_element_type=F32)
    bb = jnp.dot(ob_ref[...].astype(BF16), wb_ref[...], preferred_element_type=F32)
    gates = gate_ref[...].astype(F32)
    merged = gates[:, 0:d] * ba + gates[:, d:2 * d] * bb
    x2 = x_ref[...] + jnp.dot(merged.astype(BF16), wo_ref[...], preferred_element_type=F32)
    x2_ref[...] = x2
    ms = jnp.mean(x2 * x2, axis=-1, keepdims=True)
    xn = x2 * lax.rsqrt(ms + RMS_EPS) * g2_ref[...]
    xn_ref[...] = xn

    logits = jnp.dot(xn, wr_ref[...], precision=lax.Precision.HIGHEST,
                     preferred_element_type=F32) + br_ref[...]
    lane = lax.broadcasted_iota(I32, logits.shape, 1)
    lane_f = lane.astype(F32)
    big = float(LANES)
    epg = n_exp // n_groups
    in_groups = (lane >= n_exp) & (lane < n_exp + n_groups)
    gl = jnp.where(in_groups, logits, NEG_INF)
    gmax = jnp.max(gl, axis=1, keepdims=True)
    gidx = jnp.min(jnp.where(gl == gmax, lane_f, big), axis=1, keepdims=True) - float(n_exp)
    g_p = 1.0 / jnp.sum(jnp.exp(gl - gmax), axis=1, keepdims=True)
    in_group = (lane < n_exp) & ((lane // epg).astype(F32) == gidx)
    el = jnp.where(in_group, logits, NEG_INF)
    m1 = jnp.max(el, axis=1, keepdims=True)
    i1 = jnp.min(jnp.where(el == m1, lane_f, big), axis=1, keepdims=True)
    el2 = jnp.where(lane_f == i1, NEG_INF, el)
    m2 = jnp.max(el2, axis=1, keepdims=True)
    i2 = jnp.min(jnp.where(el2 == m2, lane_f, big), axis=1, keepdims=True)
    esum = jnp.sum(jnp.exp(el - m1), axis=1, keepdims=True)
    p1 = 1.0 / esum
    p2 = jnp.exp(m2 - m1) / esum
    psum = p1 + p2
    w1 = p1 / psum * g_p
    w2 = p2 / psum * g_p
    route_ref[...] = jnp.where(lane == 0, i1, jnp.where(lane == 1, i2,
                               jnp.where(lane == 2, w1, jnp.where(lane == 3, w2, 0.0))))


def _post(x2d, ha, ob, gates, wa, wb, wo, g2, wr, br, *, n_exp, n_groups, tm=256):
    n, d = x2d.shape
    row = lambda i: (i, 0)
    const = lambda i: (0, 0)
    kern = functools.partial(_post_kernel, d=d, n_exp=n_exp, n_groups=n_groups)
    wspec = lambda: pl.BlockSpec((d, d), const, pipeline_mode=pl.Buffered(1))
    return pl.pallas_call(
        kern, grid=(n // tm,),
        in_specs=[pl.BlockSpec((tm, d), row), pl.BlockSpec((tm, d), row), pl.BlockSpec((tm, d), row),
                  pl.BlockSpec((tm, 2 * d), row), wspec(), wspec(), wspec(),
                  pl.BlockSpec((1, d), const), pl.BlockSpec((d, LANES), const), pl.BlockSpec((1, LANES), const)],
        out_specs=[pl.BlockSpec((tm, d), row), pl.BlockSpec((tm, d), row), pl.BlockSpec((tm, LANES), row)],
        out_shape=[jax.ShapeDtypeStruct((n, d), F32), jax.ShapeDtypeStruct((n, d), F32),
                   jax.ShapeDtypeStruct((n, LANES), F32)],
        compiler_params=_params(("parallel",)),
        name="post",
    )(x2d, ha, ob, gates, wa, wb, wo, g2, wr, br)


def _moe_kernel(te_ref, nu_ref, idx_cur, idx_nxt, x_hbm, wg_ref, wu_ref, wd_ref, y_ref, xbuf, sem, *, tm):
    t = pl.program_id(0)
    n_used = nu_ref[0]
    slot = t % 2

    def gather(idx_ref, dst_slot):
        def body(r, c):
            pltpu.make_async_copy(x_hbm.at[pl.ds(idx_ref[0, 0, r], 1), :],
                                  xbuf.at[dst_slot, pl.ds(r, 1), :], sem.at[dst_slot]).start()
            return c
        lax.fori_loop(0, tm, body, 0)

    @pl.when(t == 0)
    def _():
        gather(idx_cur, 0)

    @pl.when(t + 1 < n_used)
    def _():
        gather(idx_nxt, 1 - slot)

    @pl.when(t < n_used)
    def _():
        pltpu.make_async_copy(x_hbm.at[pl.ds(0, tm), :], xbuf.at[slot], sem.at[slot]).wait()
        x = xbuf[slot].astype(BF16)
        g = jnp.dot(x, wg_ref[0].astype(BF16), preferred_element_type=F32)
        u = jnp.dot(x, wu_ref[0].astype(BF16), preferred_element_type=F32)
        hg = (g * _sigmoid(g)) * u
        y_ref[...] = jnp.dot(hg.astype(BF16), wd_ref[0].astype(BF16), preferred_element_type=F32)

    @pl.when(t >= n_used)
    def _():
        y_ref[...] = jnp.zeros_like(y_ref)


def _moe(tile_expert, n_used, row_tok, xn, w_gate, w_up, w_down, *, tm):
    nt = tile_expert.shape[0]
    n_exp, d, de = w_gate.shape
    idx3 = row_tok.reshape(nt, 1, tm)
    emap = lambda t, te, nu: (te[t], 0, 0)
    kern = functools.partial(_moe_kernel, tm=tm)
    return pl.pallas_call(
        kern,
        grid_spec=pltpu.PrefetchScalarGridSpec(
            num_scalar_prefetch=2, grid=(nt,),
            in_specs=[pl.BlockSpec((1, 1, tm), lambda t, te, nu: (t, 0, 0), memory_space=pltpu.SMEM),
                      pl.BlockSpec((1, 1, tm), lambda t, te, nu: (jnp.minimum(t + 1, nt - 1), 0, 0),
                                   memory_space=pltpu.SMEM),
                      pl.BlockSpec(memory_space=pl.ANY),
                      pl.BlockSpec((1, d, de), emap), pl.BlockSpec((1, d, de), emap),
                      pl.BlockSpec((1, de, d), emap)],
            out_specs=pl.BlockSpec((tm, d), lambda t, te, nu: (t, 0)),
            scratch_shapes=[pltpu.VMEM((2, tm, d), F32), pltpu.SemaphoreType.DMA((2,))]),
        out_shape=jax.ShapeDtypeStruct((nt * tm, d), F32),
        compiler_params=_params(("arbitrary",)),
        name="moe",
    )(tile_expert, n_used, idx3, idx3, xn, w_gate, w_up, w_down)


def _combine_kernel(idx_cur, idx_nxt, x2_ref, route_ref, ys_hbm, y_ref, buf, sem, *, tm, n_tiles):
    i = pl.program_id(0)
    slot = i % 2

    def gather(idx_ref, dst_slot):
        def body(r, c):
            for kk in range(2):
                pltpu.make_async_copy(ys_hbm.at[pl.ds(idx_ref[0, kk, r], 1), :],
                                      buf.at[dst_slot, kk, pl.ds(r, 1), :], sem.at[dst_slot]).start()
            return c
        lax.fori_loop(0, tm, body, 0)

    @pl.when(i == 0)
    def _():
        gather(idx_cur, 0)

    @pl.when(i + 1 < n_tiles)
    def _():
        gather(idx_nxt, 1 - slot)

    for kk in range(2):
        pltpu.make_async_copy(ys_hbm.at[pl.ds(0, tm), :], buf.at[slot, kk], sem.at[slot]).wait()
    route = route_ref[...]
    y_ref[...] = x2_ref[...] + route[:, 2:3] * buf[slot, 0] + route[:, 3:4] * buf[slot, 1]


def _combine(dest, x2, route, y_sorted, *, row0, tm=256):
    rows, d = x2.shape
    t0 = row0 // tm
    n_tiles = rows // tm
    kern = functools.partial(_combine_kernel, tm=tm, n_tiles=n_tiles)
    return pl.pallas_call(
        kern, grid=(n_tiles,),
        in_specs=[pl.BlockSpec((1, 2, tm), lambda i: (t0 + i, 0, 0), memory_space=pltpu.SMEM),
                  pl.BlockSpec((1, 2, tm), lambda i: (t0 + jnp.minimum(i + 1, n_tiles - 1), 0, 0),
                               memory_space=pltpu.SMEM),
                  pl.BlockSpec((tm, d), lambda i: (i, 0)),
                  pl.BlockSpec((tm, LANES), lambda i: (i, 0)),
                  pl.BlockSpec(memory_space=pl.ANY)],
        out_specs=pl.BlockSpec((tm, d), lambda i: (i, 0)),
        out_shape=jax.ShapeDtypeStruct((rows, d), F32),
        scratch_shapes=[pltpu.VMEM((2, 2, tm, d), F32), pltpu.SemaphoreType.DMA((2,))],
        compiler_params=_params(("arbitrary",)),
        name="combine",
    )(dest, dest, x2, route, y_sorted)


def _moe_plan(e_idx, *, n_exp, tm):
    n = e_idx.shape[0]
    a = 2 * n
    nt = (a + n_exp * (tm - 1)) // tm + 1
    flat = e_idx.reshape(a)
    onehot = (flat[:, None] == jnp.arange(n_exp, dtype=I32)[None, :]).astype(I32)
    csum = jnp.cumsum(onehot, axis=0)
    cnt = csum[-1]
    rank = jnp.sum((csum - onehot) * onehot, axis=1)
    ptiles = (cnt + tm - 1) // tm
    tile_end = jnp.cumsum(ptiles)
    tile_start = tile_end - ptiles
    dest = tile_start[flat] * tm + rank
    row_tok = jnp.zeros((nt * tm,), I32).at[dest].set(jnp.arange(a, dtype=I32) // 2)
    n_used = tile_end[-1]
    tiles = jnp.arange(nt, dtype=I32)
    te = jnp.minimum(jnp.searchsorted(tile_end, tiles, side="right").astype(I32), n_exp - 1)
    te_last = te[jnp.maximum(n_used - 1, 0)]
    tile_expert = jnp.where(tiles < n_used, te, te_last)
    return tile_expert, n_used.reshape(1).astype(I32), row_tok, dest.reshape(n, 2), nt


def kernel(x_prompt, x_sample, cache_k, cache_v, cache_lf, state_C, state_n, state_m, page_table,
           norm1_g, w_in, b_igate, b_fgate_mlstm, b_fgate_fox, mlstm_norm_g, q_norm_g, k_norm_g,
           w_branch_mlstm, w_branch_fox, w_out, norm2_g, w_router_group, b_router_group,
           w_router_expert, b_router_expert, w_gate, w_up, w_down):
    depth = w_in.shape[0]
    assert depth == 1, "single-layer step"
    bp, tp, d = x_prompt.shape
    bs, ts, _ = x_sample.shape
    mh = b_igate.shape[-1]
    fh = b_fgate_fox.shape[-1]
    fd = q_norm_g.shape[-1]
    md = d // mh
    n_exp = w_gate.shape[1]
    n_pages = page_table.shape[1]
    page = cache_k.shape[2]
    assert 2 * mh + fh <= LANES and n_exp + N_GROUPS <= LANES
    np_tok, ns_tok = bp * tp, bs * ts
    l = 0

    w = w_in[l]
    o = 0
    secs = {}
    for name, width in (("m", 4 * d), ("mi", mh), ("mf", mh), ("f", 3 * d), ("ff", fh), ("g", 2 * d)):
        secs[name] = w[:, o:o + width]
        o += width
    pad = jnp.zeros((d, LANES - 2 * mh - fh), F32)
    w_all = jnp.concatenate([secs["m"], secs["f"], secs["g"], secs["mi"], secs["mf"], secs["ff"], pad],
                            axis=1).astype(BF16)
    bias_s = jnp.concatenate([b_igate[l], b_fgate_mlstm[l], b_fgate_fox[l],
                              jnp.zeros((LANES - 2 * mh - fh,), F32)]).reshape(1, LANES)
    g1 = norm1_g[l].reshape(1, d)
    qg = q_norm_g[l].reshape(1, fd)
    kg = k_norm_g[l].reshape(1, fd)
    g_m = mlstm_norm_g[l].reshape(1, d)
    wa = w_branch_mlstm[l].astype(BF16)
    wb = w_branch_fox[l].astype(BF16)
    wo = w_out[l].astype(BF16)
    g2 = norm2_g[l].reshape(1, d)
    wr = jnp.concatenate([w_router_expert[l], w_router_group[l],
                          jnp.zeros((d, LANES - n_exp - N_GROUPS), F32)], axis=1)
    br = jnp.concatenate([b_router_expert[l], b_router_group[l],
                          jnp.zeros((LANES - n_exp - N_GROUPS,), F32)]).reshape(1, LANES)

    inproj = functools.partial(_inproj, g1=g1, w_all=w_all, bias_s=bias_s, qg=qg, kg=kg, mh=mh, fh=fh)
    pm_p, fq_p, k3_p, v3_p, kb_p, vb_p, gate_p, small_p = inproj(x_prompt.reshape(np_tok, d), act_dtype=BF16)
    pm_s, fq_s, k3_s, v3_s, kb_s, vb_s, gate_s, small_s = inproj(x_sample.reshape(ns_tok, d), act_dtype=F32)

    ha_p, c_p, n_p, m_p = _mlstm(pm_p, small_p, g_m, batch=bp, seq=tp, mh=mh)
    m0 = jnp.pad(state_m[l], ((0, 0), (0, LANES - mh))).reshape(bs, 1, LANES)
    ha_s, c_s, n_s, m_s = _mlstm(pm_s, small_s, g_m, batch=bs, seq=ts, mh=mh,
                                 init=(state_C[l], state_n[l], m0), out_dtype=F32)

    lf_p = small_p[:, 2 * mh:2 * mh + fh]
    lf_s = small_s[:, 2 * mh:2 * mh + fh]
    lft_p = lf_p.reshape(bp, tp, fh).transpose(0, 2, 1).reshape(bp * fh, tp)
    f_p = _cumsum_lanes(lft_p, carry=True, block_rows=bp * fh)
    ob_p = _fox_prompt(fq_p, kb_p, vb_p, f_p, batch=bp, seq=tp, fh=fh)

    n_phys = cache_k.shape[1]
    lft_c = cache_lf[l].transpose(0, 2, 1).reshape(n_phys * fh, page)
    floc = _cumsum_lanes(lft_c, carry=False, block_rows=n_phys * fh // 8).reshape(n_phys, fh, page)
    ob_s = _fox_sample(page_table, fq_s, kb_s, vb_s, small_s, floc, cache_k[l], cache_v[l], fh=fh, lf0=2 * mh)

    post = functools.partial(_post, wa=wa, wb=wb, wo=wo, g2=g2, wr=wr, br=br, n_exp=n_exp, n_groups=N_GROUPS)
    x2_p, xn_p, route_p = post(x_prompt.reshape(np_tok, d), ha_p, ob_p, gate_p)
    x2_s, xn_s, route_s = post(x_sample.reshape(ns_tok, d), ha_s, ob_s, gate_s)

    tm = 256
    xn = jnp.concatenate([xn_p, xn_s], axis=0)
    route = jnp.concatenate([route_p, route_s], axis=0)
    e_idx = route[:, 0:2].astype(I32)
    tile_expert, n_used, row_tok, dest, _ = _moe_plan(e_idx, n_exp=n_exp, tm=tm)
    y_sorted = _moe(tile_expert, n_used, row_tok, xn, w_gate[l], w_up[l], w_down[l], tm=tm)
    n_all = np_tok + ns_tok
    dest_t = dest.reshape(n_all // tm, tm, 2).transpose(0, 2, 1)
    y_p = _combine(dest_t, x2_p, route_p, y_sorted, row0=0, tm=tm)
    y_s = _combine(dest_t, x2_s, route_s, y_sorted, row0=np_tok, tm=tm)

    return (y_p.reshape(bp, tp, d), y_s.reshape(bs, ts, d),
            k3_p.reshape(1, bp, tp, fh, fd), v3_p.reshape(1, bp, tp, fh, fd), lf_p.reshape(1, bp, tp, fh),
            k3_s.reshape(1, bs, ts, fh, fd), v3_s.reshape(1, bs, ts, fh, fd), lf_s.reshape(1, bs, ts, fh),
            c_p[None], n_p[None], m_p[:, 0, :mh][None],
            c_s[None], n_s[None], m_s[:, 0, :mh][None])
```

```python
import functools

import jax
import jax.numpy as jnp
from jax import lax
from jax.experimental import pallas as pl
from jax.experimental.pallas import tpu as pltpu

F32 = jnp.float32
BF16 = jnp.bfloat16
I32 = jnp.int32
RMS_EPS = 1e-6
LANES = 128
MLSTM_CHUNK = 128
N_GROUPS = 4
EXPERTS_PER_GROUP = 8
VMEM_LIMIT = 56 * 1024 * 1024
NEG_INF = float("-inf")
DMA_UNROLL = 8


def _params(sem, vmem=VMEM_LIMIT):
    return pltpu.CompilerParams(dimension_semantics=sem, vmem_limit_bytes=vmem)


def _log_sigmoid(x):
    return -(jnp.maximum(-x, 0.0) + jnp.log1p(jnp.exp(-jnp.abs(x))))


def _sigmoid(x):
    return 1.0 / (1.0 + jnp.exp(-x))


def _nt_dot(a, b):
    return lax.dot_general(a, b, (((1,), (1,)), ((), ())), preferred_element_type=F32)


def _tn_dot(a, b):
    return lax.dot_general(a, b, (((0,), (0,)), ((), ())), preferred_element_type=F32)


def _col_to_row(col, eye):
    return jnp.sum(jnp.where(eye, col, 0.0), axis=0, keepdims=True)


def _row_to_col(row, eye):
    return jnp.sum(jnp.where(eye, row, 0.0), axis=1, keepdims=True)


def _inproj_kernel(x_ref, g1_ref, w_ref, bias_ref, qg_ref, kg_ref,
                   pm_ref, fq_ref, k3_ref, v3_ref, kb_ref, vb_ref, gate_ref, small_ref,
                   *, d, mh, fh, fd, k_scale, q_scale):
    x = x_ref[...]
    ms = jnp.mean(x * x, axis=-1, keepdims=True)
    h = (x * lax.rsqrt(ms + RMS_EPS) * g1_ref[...]).astype(BF16)

    def proj(c0, width):
        return jnp.dot(h, w_ref[:, c0:c0 + width], preferred_element_type=F32)

    pm_ref[:, 0:d] = proj(0, d).astype(pm_ref.dtype)
    pm_ref[:, d:2 * d] = (proj(d, d) * k_scale).astype(pm_ref.dtype)
    pm_ref[:, 2 * d:3 * d] = proj(2 * d, d).astype(pm_ref.dtype)
    pm_ref[:, 3 * d:4 * d] = proj(3 * d, d).astype(pm_ref.dtype)

    def head_norm(a, g):
        ms_h = jnp.mean(a * a, axis=-1, keepdims=True)
        return a * lax.rsqrt(ms_h + RMS_EPS) * g

    fq = proj(4 * d, d)
    fk = proj(5 * d, d)
    fv = proj(6 * d, d)
    for hh in range(fh):
        sl = slice(hh * fd, (hh + 1) * fd)
        fq_ref[:, sl] = (head_norm(fq[:, sl], qg_ref[...]) * q_scale).astype(fq_ref.dtype)
        kn = head_norm(fk[:, sl], kg_ref[...])
        k3_ref[:, hh, :] = kn
        kb_ref[:, sl] = kn.astype(kb_ref.dtype)
        v3_ref[:, hh, :] = fv[:, sl]
    vb_ref[...] = fv.astype(vb_ref.dtype)

    gate_ref[...] = _sigmoid(proj(7 * d, 2 * d)).astype(gate_ref.dtype)

    sm = proj(9 * d, LANES) + bias_ref[...]
    lane = lax.broadcasted_iota(I32, sm.shape, 1)
    sm = jnp.where(lane < mh, sm, jnp.where(lane < 2 * mh + fh, _log_sigmoid(sm), 0.0))
    small_ref[...] = sm


def _inproj(x2d, g1, w_all, bias_s, qg, kg, *, mh, fh, act_dtype, tm=256):
    n, d = x2d.shape
    fd = d // fh
    md = d // mh
    kern = functools.partial(_inproj_kernel, d=d, mh=mh, fh=fh, fd=fd,
                             k_scale=md ** -0.5, q_scale=fd ** -0.5)
    row = lambda i: (i, 0)
    const = lambda i: (0, 0)
    return pl.pallas_call(
        kern, grid=(n // tm,),
        in_specs=[pl.BlockSpec((tm, d), row),
                  pl.BlockSpec((1, d), const),
                  pl.BlockSpec(w_all.shape, const, pipeline_mode=pl.Buffered(1)),
                  pl.BlockSpec((1, LANES), const),
                  pl.BlockSpec((1, fd), const),
                  pl.BlockSpec((1, fd), const)],
        out_specs=[pl.BlockSpec((tm, 4 * d), row),
                   pl.BlockSpec((tm, d), row),
                   pl.BlockSpec((tm, fh, fd), lambda i: (i, 0, 0)),
                   pl.BlockSpec((tm, fh, fd), lambda i: (i, 0, 0)),
                   pl.BlockSpec((tm, d), row),
                   pl.BlockSpec((tm, d), row),
                   pl.BlockSpec((tm, 2 * d), row),
                   pl.BlockSpec((tm, LANES), row)],
        out_shape=[jax.ShapeDtypeStruct((n, 4 * d), act_dtype),
                   jax.ShapeDtypeStruct((n, d), act_dtype),
                   jax.ShapeDtypeStruct((n, fh, fd), F32),
                   jax.ShapeDtypeStruct((n, fh, fd), F32),
                   jax.ShapeDtypeStruct((n, d), act_dtype),
                   jax.ShapeDtypeStruct((n, d), act_dtype),
                   jax.ShapeDtypeStruct((n, 2 * d), BF16),
                   jax.ShapeDtypeStruct((n, LANES), F32)],
        compiler_params=_params(("parallel",)),
        name="inproj",
    )(x2d, g1, w_all, bias_s, qg, kg)


def _mlstm_kernel(*refs, L, mh, md, nb, has_init):
    if has_init:
        (q_ref, k_ref, v_ref, o_ref, s_ref, g_ref, c0_ref, n0_ref, m0_ref,
         h_ref, c_ref, n_ref, m_ref) = refs
    else:
        q_ref, k_ref, v_ref, o_ref, s_ref, g_ref, h_ref, c_ref, n_ref, m_ref = refs

    @pl.when(pl.program_id(1) == 0)
    def _():
        if has_init:
            c_ref[...] = c0_ref[...]
            n_ref[...] = n0_ref[...]
            m_ref[...] = m0_ref[...]
        else:
            c_ref[...] = jnp.zeros_like(c_ref)
            n_ref[...] = jnp.zeros_like(n_ref)
            m_ref[...] = jnp.zeros_like(m_ref)

    ri = lax.broadcasted_iota(I32, (L, L), 0)
    ci = lax.broadcasted_iota(I32, (L, L), 1)
    tri = ci <= ri
    eye = ci == ri
    for bb in range(nb):
        rs = slice(bb * L, (bb + 1) * L)
        s = s_ref[rs, :]
        m_all = m_ref[bb]
        lane = lax.broadcasted_iota(I32, m_all.shape, 1)
        m_out = m_all
        for hh in range(mh):
            sl = slice(hh * md, (hh + 1) * md)
            ig_col = s[:, hh:hh + 1]
            lf_col = s[:, mh + hh:mh + hh + 1]
            lf_row = _col_to_row(lf_col, eye)
            ig_row = _col_to_row(ig_col, eye)
            b_col = jnp.sum(jnp.where(tri, lf_row, 0.0), axis=1, keepdims=True)
            b_row = _col_to_row(b_col, eye)
            m_prev = m_all[:, hh:hh + 1]
            log_w = jnp.where(tri, b_col - b_row + ig_row, NEG_INF)
            log_inter = b_col + m_prev
            m_t = jnp.maximum(log_inter, jnp.max(log_w, axis=1, keepdims=True))
            w_intra = jnp.exp(log_w - m_t)
            w_inter = jnp.exp(log_inter - m_t)
            q = q_ref[rs, sl].astype(BF16)
            k = k_ref[rs, sl].astype(BF16)
            v = v_ref[rs, sl].astype(BF16)
            sm = _nt_dot(q, k) * w_intra
            c_prev = c_ref[bb, hh]
            n_prev = n_ref[bb, hh:hh + 1, :]
            num = (w_inter * jnp.dot(q, c_prev.astype(BF16), preferred_element_type=F32)
                   + jnp.dot(sm.astype(BF16), v, preferred_element_type=F32))
            den = (w_inter * jnp.sum(q.astype(F32) * n_prev, axis=1, keepdims=True)
                   + jnp.sum(sm, axis=1, keepdims=True))
            hv = num / jnp.maximum(jnp.abs(den), jnp.exp(-m_t))
            m_new = m_t[L - 1:L, :]
            b_last = b_col[L - 1:L, :]
            decay = jnp.exp(b_last + m_prev - m_new)
            w_rows = jnp.exp(b_last - b_col + ig_col - m_new)
            kw = k.astype(F32) * w_rows
            c_ref[bb, hh] = decay * c_prev + _tn_dot(kw.astype(BF16), v)
            n_ref[bb, hh:hh + 1, :] = decay * n_prev + jnp.sum(kw, axis=0, keepdims=True)
            m_out = jnp.where(lane == hh, m_new, m_out)
            ms = jnp.mean(hv * hv, axis=-1, keepdims=True)
            hn = hv * lax.rsqrt(ms + RMS_EPS) * g_ref[:, sl]
            h_ref[rs, sl] = (_sigmoid(o_ref[rs, sl].astype(F32)) * hn).astype(h_ref.dtype)
        m_ref[bb] = m_out


def _mlstm(pm, small, g_m, *, batch, seq, mh, nb=1, init=None, out_dtype=BF16):
    d = pm.shape[1] // 4
    md = d // mh
    L = MLSTM_CHUNK if seq % MLSTM_CHUNK == 0 else seq
    nc = seq // L
    assert nb == 1 or nc == 1, "several sequences per step only when a sequence is one chunk"
    sec = lambda j: (lambda b, c: (b * nc + c, j))
    in_specs = [pl.BlockSpec((nb * L, d), sec(0)), pl.BlockSpec((nb * L, d), sec(1)),
                pl.BlockSpec((nb * L, d), sec(2)), pl.BlockSpec((nb * L, d), sec(3)),
                pl.BlockSpec((nb * L, LANES), sec(0)),
                pl.BlockSpec((1, d), lambda b, c: (0, 0))]
    args = [pm, pm, pm, pm, small, g_m]
    state_specs = [pl.BlockSpec((nb, mh, md, md), lambda b, c: (b, 0, 0, 0)),
                   pl.BlockSpec((nb, mh, md), lambda b, c: (b, 0, 0)),
                   pl.BlockSpec((nb, 1, LANES), lambda b, c: (b, 0, 0))]
    if init is not None:
        in_specs += state_specs
        args += list(init)
    kern = functools.partial(_mlstm_kernel, L=L, mh=mh, md=md, nb=nb, has_init=init is not None)
    return pl.pallas_call(
        kern, grid=(batch // nb, nc),
        in_specs=in_specs,
        out_specs=[pl.BlockSpec((nb * L, d), lambda b, c: (b * nc + c, 0))] + state_specs,
        out_shape=[jax.ShapeDtypeStruct((batch * seq, d), out_dtype),
                   jax.ShapeDtypeStruct((batch, mh, md, md), F32),
                   jax.ShapeDtypeStruct((batch, mh, md), F32),
                   jax.ShapeDtypeStruct((batch, 1, LANES), F32)],
        compiler_params=_params(("parallel", "arbitrary")),
        name="mlstm",
    )(*args)


def _cumsum_kernel(x_ref, o_ref, *, chunk, carry):
    rows, t = x_ref.shape
    si = lax.broadcasted_iota(I32, (chunk, chunk), 0)
    ti = lax.broadcasted_iota(I32, (chunk, chunk), 1)
    tri = (si <= ti).astype(F32)
    run = jnp.zeros((rows, 1), F32)
    for j in range(t // chunk):
        sl = slice(j * chunk, (j + 1) * chunk)
        loc = jnp.dot(x_ref[:, sl], tri, precision=lax.Precision.HIGHEST, preferred_element_type=F32)
        if carry:
            loc = loc + run
            run = loc[:, chunk - 1:chunk]
        o_ref[:, sl] = loc


def _cumsum_lanes(x, *, carry, block_rows):
    rows, t = x.shape
    kern = functools.partial(_cumsum_kernel, chunk=LANES, carry=carry)
    return pl.pallas_call(
        kern, grid=(rows // block_rows,),
        in_specs=[pl.BlockSpec((block_rows, t), lambda i: (i, 0))],
        out_specs=pl.BlockSpec((block_rows, t), lambda i: (i, 0)),
        out_shape=jax.ShapeDtypeStruct((rows, t), F32),
        compiler_params=_params(("parallel",)),
        name="cumsum",
    )(x)


def _fox_prompt_kernel(q_ref, k_ref, v_ref, f_ref, o_ref, *, seq, tq, gh, fd):
    nq = seq // tq
    ri = lax.broadcasted_iota(I32, (tq, tq), 0)
    ci = lax.broadcasted_iota(I32, (tq, tq), 1)
    eye = ci == ri
    causal = ci <= ri

    def softmax_step(carry, s, v):
        m, l, acc = carry
        m_new = jnp.maximum(m, jnp.max(s, axis=1, keepdims=True))
        a = jnp.exp(m - m_new)
        p = jnp.exp(s - m_new)
        l = a * l + jnp.sum(p, axis=1, keepdims=True)
        acc = a * acc + jnp.dot(p.astype(BF16), v, preferred_element_type=F32)
        return m_new, l, acc

    def q_body(qi, _):
        q0 = pl.multiple_of(qi * tq, tq)
        f_rows = [f_ref[g, qi] for g in range(gh)]
        f_cols = [_row_to_col(fr, eye) for fr in f_rows]

        def block(g, carry_g, k0, f_row_k, masked):
            gs = slice(g * fd, (g + 1) * fd)
            s = _nt_dot(q_ref[pl.ds(q0, tq), gs], k_ref[pl.ds(k0, tq), gs]) + (f_cols[g] - f_row_k)
            if masked:
                s = jnp.where(causal, s, NEG_INF)
            return softmax_step(carry_g, s, v_ref[pl.ds(k0, tq), gs])

        def kv_body(kj, carry):
            k0 = pl.multiple_of(kj * tq, tq)
            return tuple(block(g, carry[g], k0, f_ref[g, kj], False) for g in range(gh))

        init = tuple((jnp.full((tq, 1), NEG_INF, F32), jnp.zeros((tq, 1), F32), jnp.zeros((tq, fd), F32))
                     for _ in range(gh))
        carry = lax.fori_loop(0, qi, kv_body, init)
        for g in range(gh):
            _, l, acc = block(g, carry[g], q0, f_rows[g], True)
            o_ref[pl.ds(q0, tq), g * fd:(g + 1) * fd] = (acc / l).astype(o_ref.dtype)
        return 0

    lax.fori_loop(0, nq, q_body, 0)


def _fox_prompt(fq, kb, vb, f_rows, *, batch, seq, fh, tq=256, gh=4):
    n, d = fq.shape
    fd = d // fh
    nq = seq // tq
    blk = lambda b, h: (b, h)
    kern = functools.partial(_fox_prompt_kernel, seq=seq, tq=tq, gh=gh, fd=fd)
    return pl.pallas_call(
        kern, grid=(batch, fh // gh),
        in_specs=[pl.BlockSpec((seq, gh * fd), blk), pl.BlockSpec((seq, gh * fd), blk),
                  pl.BlockSpec((seq, gh * fd), blk),
                  pl.BlockSpec((gh, nq, 1, tq), lambda b, h: (b * (fh // gh) + h, 0, 0, 0))],
        out_specs=pl.BlockSpec((seq, gh * fd), blk),
        out_shape=jax.ShapeDtypeStruct((n, d), BF16),
        compiler_params=_params(("parallel", "parallel")),
        name="fox_prompt",
    )(fq, kb, vb, f_rows.reshape(batch * fh, nq, 1, tq))


def _fox_sample_kernel(pt_ref, q_ref, kn_ref, vn_ref, s_ref, *rest, n_pages, page, fh, fd, ts, lf0):
    floc = rest[0:n_pages]
    kpages = rest[n_pages:2 * n_pages]
    vpages = rest[2 * n_pages:3 * n_pages]
    o_ref = rest[3 * n_pages]
    p_scr, pn_scr, l_scr = rest[3 * n_pages + 1:]
    d = fh * fd
    rows = fh * ts
    phase = pl.program_id(1)

    @pl.when(phase == 0)
    def _():
        q = q_ref[...].astype(F32)
        qt = jnp.concatenate([q] * fh, axis=0)
        r_head = lax.broadcasted_iota(I32, (rows, d), 0) // ts
        c_head = lax.broadcasted_iota(I32, (rows, d), 1) // fd
        qbd = jnp.where(r_head == c_head, qt, 0.0).astype(BF16)

        off = jnp.zeros((fh, 1), F32)
        fk_pages = []
        for i in range(n_pages):
            fp = floc[i][...] + off
            fk_pages.append(fp)
            off = fp[:, page - 1:page]
        fk = jnp.concatenate(fk_pages, axis=1)
        fk_rows = jnp.concatenate(
            [jnp.broadcast_to(fk[h:h + 1, :], (ts, fk.shape[1])) for h in range(fh)], axis=0)

        sm = s_ref[...]
        ri = lax.broadcasted_iota(I32, (ts, ts), 0)
        ci = lax.broadcasted_iota(I32, (ts, ts), 1)
        eye = ci == ri
        tri = ci <= ri
        fq_cols, bias_new = [], []
        for h in range(fh):
            lf_col = sm[:, lf0 + h:lf0 + h + 1]
            lf_row = _col_to_row(lf_col, eye)
            cum_col = jnp.sum(jnp.where(tri, lf_row, 0.0), axis=1, keepdims=True)
            fq_h = off[h:h + 1, :] + cum_col
            fq_cols.append(fq_h)
            bias_new.append(fq_h - _col_to_row(fq_h, eye))
        fq_col = jnp.concatenate(fq_cols, axis=0)
        bias_n = jnp.concatenate(bias_new, axis=0)
        causal_n = jnp.concatenate([tri] * fh, axis=0)

        s_parts = []
        for i in range(0, n_pages, 2):
            kp = jnp.concatenate(
                [jnp.concatenate([kpages[i + j][pl.ds(h, page, stride=fh), :] for h in range(fh)], axis=1)
                 for j in range(2)],
                axis=0).astype(BF16)
            s_parts.append(_nt_dot(qbd, kp))
        s_past = jnp.concatenate(s_parts, axis=1) + (fq_col - fk_rows)
        s_new = _nt_dot(qbd, kn_ref[...].astype(BF16)) + bias_n
        s_new = jnp.where(causal_n, s_new, NEG_INF)
        m = jnp.maximum(jnp.max(s_past, axis=1, keepdims=True), jnp.max(s_new, axis=1, keepdims=True))
        p_past = jnp.exp(s_past - m)
        p_new = jnp.exp(s_new - m)
        l_scr[...] = jnp.sum(p_past, axis=1, keepdims=True) + jnp.sum(p_new, axis=1, keepdims=True)
        p_scr[...] = p_past.astype(BF16)
        pn_scr[...] = p_new

    @pl.when(phase == 1)
    def _():
        acc = jnp.dot(pn_scr[...].astype(BF16), vn_ref[...].astype(BF16), preferred_element_type=F32)
        for i in range(0, n_pages, 2):
            vp = jnp.concatenate(
                [jnp.concatenate([vpages[i + j][pl.ds(h, page, stride=fh), :] for h in range(fh)], axis=1)
                 for j in range(2)],
                axis=0).astype(BF16)
            acc = acc + jnp.dot(p_scr[:, i * page:(i + 2) * page], vp, preferred_element_type=F32)
        acc = acc / l_scr[...]
        o_ref[...] = jnp.concatenate(
            [acc[h * ts:(h + 1) * ts, h * fd:(h + 1) * fd] for h in range(fh)], axis=1)


def _fox_sample(page_table, fq, kb, vb, small, floc, cache_k, cache_v, *, fh, lf0):
    bs, n_pages = page_table.shape
    n, d = fq.shape
    ts = n // bs
    fd = d // fh
    page = cache_k.shape[1] // fh
    rows = fh * ts
    tok = lambda b, ph, pt: (b, 0)

    def kmap(i):
        return lambda b, ph, pt: (pt[b, i], 0, 0)

    def vmap_(i):
        return lambda b, ph, pt: (pt[jnp.maximum(b - 1 + ph, 0), i], 0, 0)

    def fmap(i):
        return lambda b, ph, pt: (pt[b, i], 0, 0)

    in_specs = ([pl.BlockSpec((ts, d), tok), pl.BlockSpec((ts, d), tok), pl.BlockSpec((ts, d), tok),
                 pl.BlockSpec((ts, LANES), tok)]
                + [pl.BlockSpec((None, fh, page), fmap(i)) for i in range(n_pages)]
                + [pl.BlockSpec((None, page * fh, fd), kmap(i)) for i in range(n_pages)]
                + [pl.BlockSpec((None, page * fh, fd), vmap_(i)) for i in range(n_pages)])
    kern = functools.partial(_fox_sample_kernel, n_pages=n_pages, page=page, fh=fh, fd=fd, ts=ts, lf0=lf0)
    return pl.pallas_call(
        kern,
        grid_spec=pltpu.PrefetchScalarGridSpec(
            num_scalar_prefetch=1, grid=(bs, 2), in_specs=in_specs,
            out_specs=pl.BlockSpec((ts, d), tok),
            scratch_shapes=[pltpu.VMEM((rows, n_pages * page), BF16),
                            pltpu.VMEM((rows, ts), F32),
                            pltpu.VMEM((rows, 1), F32)]),
        out_shape=jax.ShapeDtypeStruct((n, d), F32),
        compiler_params=_params(("arbitrary", "arbitrary")),
        name="fox_sample",
    )(page_table, fq, kb, vb, small, *([floc] * n_pages), *([cache_k] * n_pages), *([cache_v] * n_pages))


def _post_kernel(x_ref, ha_ref, ob_ref, gate_ref, wa_ref, wb_ref, wo_ref, g2_ref, wr_ref, br_ref,
                 x2_ref, xn_ref, route_ref, *, d, n_exp, n_groups):
    ba = jnp.dot(ha_ref[...].astype(BF16), wa_ref[...], preferred_element_type=F32)
    bb = jnp.dot(ob_ref[...].astype(BF16), wb_ref[...], preferred_element_type=F32)
    gates = gate_ref[...].astype(F32)
    merged = gates[:, 0:d] * ba + gates[:, d:2 * d] * bb
    x2 = x_ref[...] + jnp.dot(merged.astype(BF16), wo_ref[...], preferred_element_type=F32)
    x2_ref[...] = x2
    ms = jnp.mean(x2 * x2, axis=-1, keepdims=True)
    xn = x2 * lax.rsqrt(ms + RMS_EPS) * g2_ref[...]
    xn_ref[...] = xn

    xh = xn.astype(BF16)
    xl = (xn - xh.astype(F32)).astype(BF16)
    hh = jnp.dot(xh, wr_ref[...], preferred_element_type=F32)
    logits = (hh[:, 0:LANES] + hh[:, LANES:2 * LANES]
              + jnp.dot(xl, wr_ref[:, 0:LANES], preferred_element_type=F32)
              + br_ref[...])
    lane = lax.broadcasted_iota(I32, logits.shape, 1)
    lane_f = lane.astype(F32)
    big = float(LANES)
    epg = n_exp // n_groups
    in_groups = (lane >= n_exp) & (lane < n_exp + n_groups)
    gl = jnp.where(in_groups, logits, NEG_INF)
    gmax = jnp.max(gl, axis=1, keepdims=True)
    gidx = jnp.min(jnp.where(gl == gmax, lane_f, big), axis=1, keepdims=True) - float(n_exp)
    g_p = 1.0 / jnp.sum(jnp.exp(gl - gmax), axis=1, keepdims=True)
    in_group = (lane < n_exp) & ((lane // epg).astype(F32) == gidx)
    el = jnp.where(in_group, logits, NEG_INF)
    m1 = jnp.max(el, axis=1, keepdims=True)
    i1 = jnp.min(jnp.where(el == m1, lane_f, big), axis=1, keepdims=True)
    el2 = jnp.where(lane_f == i1, NEG_INF, el)
    m2 = jnp.max(el2, axis=1, keepdims=True)
    i2 = jnp.min(jnp.where(el2 == m2, lane_f, big), axis=1, keepdims=True)
    esum = jnp.sum(jnp.exp(el - m1), axis=1, keepdims=True)
    p1 = 1.0 / esum
    p2 = jnp.exp(m2 - m1) / esum
    psum = p1 + p2
    w1 = p1 / psum * g_p
    w2 = p2 / psum * g_p
    route_ref[...] = jnp.where(lane == 0, i1, jnp.where(lane == 1, i2,
                               jnp.where(lane == 2, w1, jnp.where(lane == 3, w2, 0.0))))


def _post(x2d, ha, ob, gates, wa, wb, wo, g2, wr, br, *, n_exp, n_groups, tm=256):
    n, d = x2d.shape
    row = lambda i: (i, 0)
    const = lambda i: (0, 0)
    kern = functools.partial(_post_kernel, d=d, n_exp=n_exp, n_groups=n_groups)
    wspec = lambda: pl.BlockSpec((d, d), const, pipeline_mode=pl.Buffered(1))
    return pl.pallas_call(
        kern, grid=(n // tm,),
        in_specs=[pl.BlockSpec((tm, d), row), pl.BlockSpec((tm, d), row), pl.BlockSpec((tm, d), row),
                  pl.BlockSpec((tm, 2 * d), row), wspec(), wspec(), wspec(),
                  pl.BlockSpec((1, d), const), pl.BlockSpec((d, 2 * LANES), const), pl.BlockSpec((1, LANES), const)],
        out_specs=[pl.BlockSpec((tm, d), row), pl.BlockSpec((tm, d), row), pl.BlockSpec((tm, LANES), row)],
        out_shape=[jax.ShapeDtypeStruct((n, d), F32), jax.ShapeDtypeStruct((n, d), F32),
                   jax.ShapeDtypeStruct((n, LANES), F32)],
        compiler_params=_params(("parallel",)),
        name="post",
    )(x2d, ha, ob, gates, wa, wb, wo, g2, wr, br)


def _moe_kernel(te_ref, nu_ref, idx_cur, idx_nxt, x_hbm, wg_ref, wu_ref, wd_ref, y_ref, xbuf, sem, *, tm):
    t = pl.program_id(0)
    n_used = nu_ref[0]
    slot = t % 2

    def gather(idx_ref, dst_slot):
        def body(j, c):
            for u in range(DMA_UNROLL):
                r = j * DMA_UNROLL + u
                pltpu.make_async_copy(x_hbm.at[pl.ds(idx_ref[0, 0, r], 1), :],
                                      xbuf.at[dst_slot, pl.ds(r, 1), :], sem.at[dst_slot]).start()
            return c
        lax.fori_loop(0, tm // DMA_UNROLL, body, 0)

    @pl.when(t == 0)
    def _():
        gather(idx_cur, 0)

    @pl.when(t + 1 < n_used)
    def _():
        gather(idx_nxt, 1 - slot)

    @pl.when(t < n_used)
    def _():
        pltpu.make_async_copy(x_hbm.at[pl.ds(0, tm), :], xbuf.at[slot], sem.at[slot]).wait()
        x = xbuf[slot].astype(BF16)
        g = jnp.dot(x, wg_ref[0].astype(BF16), preferred_element_type=F32)
        u = jnp.dot(x, wu_ref[0].astype(BF16), preferred_element_type=F32)
        hg = (g * _sigmoid(g)) * u
        y_ref[...] = jnp.dot(hg.astype(BF16), wd_ref[0].astype(BF16), preferred_element_type=F32)

    @pl.when(t >= n_used)
    def _():
        y_ref[...] = jnp.zeros_like(y_ref)


def _moe(tile_expert, n_used, row_tok, xn, w_gate, w_up, w_down, *, tm):
    nt = tile_expert.shape[0]
    n_exp, d, de = w_gate.shape
    idx3 = row_tok.reshape(nt, 1, tm)
    emap = lambda t, te, nu: (te[t], 0, 0)
    kern = functools.partial(_moe_kernel, tm=tm)
    return pl.pallas_call(
        kern,
        grid_spec=pltpu.PrefetchScalarGridSpec(
            num_scalar_prefetch=2, grid=(nt,),
            in_specs=[pl.BlockSpec((1, 1, tm), lambda t, te, nu: (t, 0, 0), memory_space=pltpu.SMEM),
                      pl.BlockSpec((1, 1, tm), lambda t, te, nu: (jnp.minimum(t + 1, nt - 1), 0, 0),
                                   memory_space=pltpu.SMEM),
                      pl.BlockSpec(memory_space=pl.ANY),
                      pl.BlockSpec((1, d, de), emap), pl.BlockSpec((1, d, de), emap),
                      pl.BlockSpec((1, de, d), emap)],
            out_specs=pl.BlockSpec((tm, d), lambda t, te, nu: (t, 0)),
            scratch_shapes=[pltpu.VMEM((2, tm, d), F32), pltpu.SemaphoreType.DMA((2,))]),
        out_shape=jax.ShapeDtypeStruct((nt * tm, d), F32),
        compiler_params=_params(("arbitrary",)),
        name="moe",
    )(tile_expert, n_used, idx3, idx3, xn, w_gate, w_up, w_down)


def _combine_kernel(idx_cur, idx_nxt, x2_ref, route_ref, ys_hbm, y_ref, buf, sem, *, tm, n_tiles):
    i = pl.program_id(0)
    slot = i % 2

    def gather(idx_ref, dst_slot):
        def body(j, c):
            for u in range(DMA_UNROLL // 2):
                r = j * (DMA_UNROLL // 2) + u
                for kk in range(2):
                    pltpu.make_async_copy(ys_hbm.at[pl.ds(idx_ref[0, kk, r], 1), :],
                                          buf.at[dst_slot, kk, pl.ds(r, 1), :], sem.at[dst_slot]).start()
            return c
        lax.fori_loop(0, tm // (DMA_UNROLL // 2), body, 0)

    @pl.when(i == 0)
    def _():
        gather(idx_cur, 0)

    @pl.when(i + 1 < n_tiles)
    def _():
        gather(idx_nxt, 1 - slot)

    for kk in range(2):
        pltpu.make_async_copy(ys_hbm.at[pl.ds(0, tm), :], buf.at[slot, kk], sem.at[slot]).wait()
    route = route_ref[...]
    y_ref[...] = x2_ref[...] + route[:, 2:3] * buf[slot, 0] + route[:, 3:4] * buf[slot, 1]


def _combine(dest, x2, route, y_sorted, *, row0, tm=256):
    rows, d = x2.shape
    t0 = row0 // tm
    n_tiles = rows // tm
    kern = functools.partial(_combine_kernel, tm=tm, n_tiles=n_tiles)
    return pl.pallas_call(
        kern, grid=(n_tiles,),
        in_specs=[pl.BlockSpec((1, 2, tm), lambda i: (t0 + i, 0, 0), memory_space=pltpu.SMEM),
                  pl.BlockSpec((1, 2, tm), lambda i: (t0 + jnp.minimum(i + 1, n_tiles - 1), 0, 0),
                               memory_space=pltpu.SMEM),
                  pl.BlockSpec((tm, d), lambda i: (i, 0)),
                  pl.BlockSpec((tm, LANES), lambda i: (i, 0)),
                  pl.BlockSpec(memory_space=pl.ANY)],
        out_specs=pl.BlockSpec((tm, d), lambda i: (i, 0)),
        out_shape=jax.ShapeDtypeStruct((rows, d), F32),
        scratch_shapes=[pltpu.VMEM((2, 2, tm, d), F32), pltpu.SemaphoreType.DMA((2,))],
        compiler_params=_params(("arbitrary",)),
        name="combine",
    )(dest, dest, x2, route, y_sorted)


def _rank_kernel(route_ref, rank_ref, cnt_ref, *, tm):
    @pl.when(pl.program_id(0) == 0)
    def _():
        cnt_ref[...] = jnp.zeros_like(cnt_ref)

    route = route_ref[...]
    lane = lax.broadcasted_iota(I32, route.shape, 1)
    lane_f = lane.astype(F32)
    oh0 = lane_f == route[:, 0:1]
    oh1 = lane_f == route[:, 1:2]
    oh = jnp.where(oh0, 1.0, jnp.where(oh1, 1.0, 0.0))
    ri = lax.broadcasted_iota(I32, (tm, tm), 0)
    ci = lax.broadcasted_iota(I32, (tm, tm), 1)
    earlier = jnp.where(ci < ri, 1.0, 0.0).astype(BF16)
    base = cnt_ref[...] + jnp.dot(earlier, oh.astype(BF16), preferred_element_type=F32)
    r0 = jnp.sum(jnp.where(oh0, base, 0.0), axis=1, keepdims=True)
    r1 = jnp.sum(jnp.where(oh1, base, 0.0), axis=1, keepdims=True)
    cnt_ref[...] += jnp.sum(oh, axis=0, keepdims=True)
    rank_ref[...] = jnp.where(lane == 0, r0, jnp.where(lane == 1, r1, 0.0))


def _ranks(route, *, tm):
    n = route.shape[0]
    return pl.pallas_call(
        functools.partial(_rank_kernel, tm=tm), grid=(n // tm,),
        in_specs=[pl.BlockSpec((tm, LANES), lambda i: (i, 0))],
        out_specs=[pl.BlockSpec((tm, LANES), lambda i: (i, 0)), pl.BlockSpec((1, LANES), lambda i: (0, 0))],
        out_shape=[jax.ShapeDtypeStruct((n, LANES), F32), jax.ShapeDtypeStruct((1, LANES), F32)],
        compiler_params=_params(("arbitrary",)),
        name="ranks",
    )(route)


def _moe_plan(route, *, n_exp, tm):
    n = route.shape[0]
    a = 2 * n
    nt = (a + n_exp * (tm - 1)) // tm
    rank_f, cnt_f = _ranks(route, tm=tm)
    e_idx = route[:, 0:2].astype(I32)
    rank = rank_f[:, 0:2].astype(I32)
    cnt = cnt_f[0, :n_exp].astype(I32)
    ptiles = (cnt + tm - 1) // tm
    tile_end = jnp.cumsum(ptiles)
    tile_start = tile_end - ptiles
    dest = jnp.take(tile_start, e_idx) * tm + rank
    tok = jnp.broadcast_to(jnp.arange(n, dtype=I32)[:, None], (n, 2))
    row_tok = jnp.zeros((nt * tm,), I32).at[dest.reshape(a)].set(tok.reshape(a))
    n_used = tile_end[-1]
    tiles = jnp.arange(nt, dtype=I32)
    te = jnp.minimum(jnp.sum((tile_end[None, :] <= tiles[:, None]).astype(I32), axis=1), n_exp - 1)
    te_last = jnp.take(te, jnp.maximum(n_used - 1, 0))
    tile_expert = jnp.where(tiles < n_used, te, te_last)
    return tile_expert, n_used.reshape(1).astype(I32), row_tok, dest, nt


def kernel(x_prompt, x_sample, cache_k, cache_v, cache_lf, state_C, state_n, state_m, page_table,
           norm1_g, w_in, b_igate, b_fgate_mlstm, b_fgate_fox, mlstm_norm_g, q_norm_g, k_norm_g,
           w_branch_mlstm, w_branch_fox, w_out, norm2_g, w_router_group, b_router_group,
           w_router_expert, b_router_expert, w_gate, w_up, w_down):
    depth = w_in.shape[0]
    assert depth == 1, "single-layer step"
    bp, tp, d = x_prompt.shape
    bs, ts, _ = x_sample.shape
    mh = b_igate.shape[-1]
    fh = b_fgate_fox.shape[-1]
    fd = q_norm_g.shape[-1]
    md = d // mh
    n_exp = w_gate.shape[1]
    n_pages = page_table.shape[1]
    page = cache_k.shape[2]
    assert 2 * mh + fh <= LANES and n_exp + N_GROUPS <= LANES
    np_tok, ns_tok = bp * tp, bs * ts
    l = 0

    w = w_in[l]
    o = 0
    secs = {}
    for name, width in (("m", 4 * d), ("mi", mh), ("mf", mh), ("f", 3 * d), ("ff", fh), ("g", 2 * d)):
        secs[name] = w[:, o:o + width]
        o += width
    pad = jnp.zeros((d, LANES - 2 * mh - fh), F32)
    w_all = jnp.concatenate([secs["m"], secs["f"], secs["g"], secs["mi"], secs["mf"], secs["ff"], pad],
                            axis=1).astype(BF16)
    bias_s = jnp.concatenate([b_igate[l], b_fgate_mlstm[l], b_fgate_fox[l],
                              jnp.zeros((LANES - 2 * mh - fh,), F32)]).reshape(1, LANES)
    g1 = norm1_g[l].reshape(1, d)
    qg = q_norm_g[l].reshape(1, fd)
    kg = k_norm_g[l].reshape(1, fd)
    g_m = mlstm_norm_g[l].reshape(1, d)
    wa = w_branch_mlstm[l].astype(BF16)
    wb = w_branch_fox[l].astype(BF16)
    wo = w_out[l].astype(BF16)
    g2 = norm2_g[l].reshape(1, d)
    wr32 = jnp.concatenate([w_router_expert[l], w_router_group[l],
                            jnp.zeros((d, LANES - n_exp - N_GROUPS), F32)], axis=1)
    wr_hi = wr32.astype(BF16)
    wr = jnp.concatenate([wr_hi, (wr32 - wr_hi.astype(F32)).astype(BF16)], axis=1)
    br = jnp.concatenate([b_router_expert[l], b_router_group[l],
                          jnp.zeros((LANES - n_exp - N_GROUPS,), F32)]).reshape(1, LANES)

    inproj = functools.partial(_inproj, g1=g1, w_all=w_all, bias_s=bias_s, qg=qg, kg=kg, mh=mh, fh=fh)
    pm_p, fq_p, k3_p, v3_p, kb_p, vb_p, gate_p, small_p = inproj(x_prompt.reshape(np_tok, d), act_dtype=BF16)
    pm_s, fq_s, k3_s, v3_s, kb_s, vb_s, gate_s, small_s = inproj(x_sample.reshape(ns_tok, d), act_dtype=F32)

    ha_p, c_p, n_p, m_p = _mlstm(pm_p, small_p, g_m, batch=bp, seq=tp, mh=mh)
    m0 = jnp.pad(state_m[l], ((0, 0), (0, LANES - mh))).reshape(bs, 1, LANES)
    ha_s, c_s, n_s, m_s = _mlstm(pm_s, small_s, g_m, batch=bs, seq=ts, mh=mh, nb=4,
                                 init=(state_C[l], state_n[l], m0), out_dtype=F32)

    lf_p = small_p[:, 2 * mh:2 * mh + fh]
    lf_s = small_s[:, 2 * mh:2 * mh + fh]
    lft_p = lf_p.reshape(bp, tp, fh).transpose(0, 2, 1).reshape(bp * fh, tp)
    f_p = _cumsum_lanes(lft_p, carry=True, block_rows=bp * fh)
    ob_p = _fox_prompt(fq_p, kb_p, vb_p, f_p, batch=bp, seq=tp, fh=fh)

    n_phys = cache_k.shape[1]
    lft_c = cache_lf[l].transpose(0, 2, 1).reshape(n_phys * fh, page)
    floc = _cumsum_lanes(lft_c, carry=False, block_rows=n_phys * fh // 8).reshape(n_phys, fh, page)
    ck = cache_k[l].reshape(n_phys, page * fh, fd)
    cv = cache_v[l].reshape(n_phys, page * fh, fd)
    ob_s = _fox_sample(page_table, fq_s, kb_s, vb_s, small_s, floc, ck, cv, fh=fh, lf0=2 * mh)

    post = functools.partial(_post, wa=wa, wb=wb, wo=wo, g2=g2, wr=wr, br=br, n_exp=n_exp, n_groups=N_GROUPS)
    x2_p, xn_p, route_p = post(x_prompt.reshape(np_tok, d), ha_p, ob_p, gate_p)
    x2_s, xn_s, route_s = post(x_sample.reshape(ns_tok, d), ha_s, ob_s, gate_s)

    tm = 256
    xn = jnp.concatenate([xn_p, xn_s], axis=0)
    route = jnp.concatenate([route_p, route_s], axis=0)
    tile_expert, n_used, row_tok, dest, _ = _moe_plan(route, n_exp=n_exp, tm=tm)
    y_sorted = _moe(tile_expert, n_used, row_tok, xn, w_gate[l], w_up[l], w_down[l], tm=tm)
    n_all = np_tok + ns_tok
    dest_t = dest.reshape(n_all // tm, tm, 2).transpose(0, 2, 1)
    y_p = _combine(dest_t, x2_p, route_p, y_sorted, row0=0, tm=tm)
    y_s = _combine(dest_t, x2_s, route_s, y_sorted, row0=np_tok, tm=tm)

    return (y_p.reshape(bp, tp, d), y_s.reshape(bs, ts, d),
            k3_p.reshape(1, bp, tp, fh, fd), v3_p.reshape(1, bp, tp, fh, fd), lf_p.reshape(1, bp, tp, fh),
            k3_s.reshape(1, bs, ts, fh, fd), v3_s.reshape(1, bs, ts, fh, fd), lf_s.reshape(1, bs, ts, fh),
            c_p[None], n_p[None], m_p[:, 0, :mh][None],
            c_s[None], n_s[None], m_s[:, 0, :mh][None])
```

```python
import functools

import jax
import jax.numpy as jnp
from jax import lax
from jax.experimental import pallas as pl
from jax.experimental.pallas import tpu as pltpu

F32 = jnp.float32
BF16 = jnp.bfloat16
I32 = jnp.int32
RMS_EPS = 1e-6
LANES = 128
MLSTM_CHUNK = 128
N_GROUPS = 4
EXPERTS_PER_GROUP = 8
VMEM_LIMIT = 56 * 1024 * 1024
NEG_INF = float("-inf")
LOG2E = 1.4426950408889634
DMA_UNROLL = 8
MOE_TILE = 256
PAD_CHUNKS = (128, 64, 32, 16, 8, 4, 2, 1)


def _params(sem, vmem=VMEM_LIMIT):
    return pltpu.CompilerParams(dimension_semantics=sem, vmem_limit_bytes=vmem)


def _log_sigmoid(x):
    return -(jnp.maximum(-x, 0.0) + jnp.log1p(jnp.exp(-jnp.abs(x))))


def _sigmoid(x):
    return 1.0 / (1.0 + jnp.exp(-x))


def _nt_dot(a, b):
    return lax.dot_general(a, b, (((1,), (1,)), ((), ())), preferred_element_type=F32)


def _tn_dot(a, b):
    return lax.dot_general(a, b, (((0,), (0,)), ((), ())), preferred_element_type=F32)


def _col_to_row(col, eye):
    return jnp.sum(jnp.where(eye, col, 0.0), axis=0, keepdims=True)


def _row_to_col(row, eye):
    return jnp.sum(jnp.where(eye, row, 0.0), axis=1, keepdims=True)


def _inproj_kernel(x_ref, g1_ref, w_ref, bias_ref, qg_ref, kg_ref,
                   pm_ref, fq_ref, k3_ref, v3_ref, kb_ref, vb_ref, gate_ref, small_ref,
                   *, d, mh, fh, fd, k_scale, q_scale):
    x = x_ref[...]
    ms = jnp.mean(x * x, axis=-1, keepdims=True)
    h = (x * lax.rsqrt(ms + RMS_EPS) * g1_ref[...]).astype(BF16)

    def proj(c0, width):
        return jnp.dot(h, w_ref[:, c0:c0 + width], preferred_element_type=F32)

    pm_ref[:, 0:d] = proj(0, d).astype(pm_ref.dtype)
    pm_ref[:, d:2 * d] = (proj(d, d) * k_scale).astype(pm_ref.dtype)
    pm_ref[:, 2 * d:3 * d] = proj(2 * d, d).astype(pm_ref.dtype)
    pm_ref[:, 3 * d:4 * d] = proj(3 * d, d).astype(pm_ref.dtype)

    def head_norm(a, g):
        ms_h = jnp.mean(a * a, axis=-1, keepdims=True)
        return a * lax.rsqrt(ms_h + RMS_EPS) * g

    fq = proj(4 * d, d)
    fk = proj(5 * d, d)
    fv = proj(6 * d, d)
    for hh in range(fh):
        sl = slice(hh * fd, (hh + 1) * fd)
        fq_ref[:, sl] = (head_norm(fq[:, sl], qg_ref[...]) * q_scale).astype(fq_ref.dtype)
        kn = head_norm(fk[:, sl], kg_ref[...])
        k3_ref[:, hh, :] = kn
        kb_ref[:, sl] = kn.astype(kb_ref.dtype)
        v3_ref[:, hh, :] = fv[:, sl]
    vb_ref[...] = fv.astype(vb_ref.dtype)

    gate_ref[...] = _sigmoid(proj(7 * d, 2 * d)).astype(gate_ref.dtype)

    sm = proj(9 * d, LANES) + bias_ref[...]
    lane = lax.broadcasted_iota(I32, sm.shape, 1)
    sm = jnp.where(lane < mh, sm, jnp.where(lane < 2 * mh + fh, _log_sigmoid(sm), 0.0))
    small_ref[...] = sm


def _inproj(x2d, g1, w_all, bias_s, qg, kg, *, mh, fh, act_dtype, q_unit=1.0, tm=256):
    n, d = x2d.shape
    fd = d // fh
    md = d // mh
    kern = functools.partial(_inproj_kernel, d=d, mh=mh, fh=fh, fd=fd,
                             k_scale=md ** -0.5, q_scale=fd ** -0.5 * q_unit)
    row = lambda i: (i, 0)
    const = lambda i: (0, 0)
    return pl.pallas_call(
        kern, grid=(n // tm,),
        in_specs=[pl.BlockSpec((tm, d), row),
                  pl.BlockSpec((1, d), const),
                  pl.BlockSpec(w_all.shape, const, pipeline_mode=pl.Buffered(1)),
                  pl.BlockSpec((1, LANES), const),
                  pl.BlockSpec((1, fd), const),
                  pl.BlockSpec((1, fd), const)],
        out_specs=[pl.BlockSpec((tm, 4 * d), row),
                   pl.BlockSpec((tm, d), row),
                   pl.BlockSpec((tm, fh, fd), lambda i: (i, 0, 0)),
                   pl.BlockSpec((tm, fh, fd), lambda i: (i, 0, 0)),
                   pl.BlockSpec((tm, d), row),
                   pl.BlockSpec((tm, d), row),
                   pl.BlockSpec((tm, 2 * d), row),
                   pl.BlockSpec((tm, LANES), row)],
        out_shape=[jax.ShapeDtypeStruct((n, 4 * d), act_dtype),
                   jax.ShapeDtypeStruct((n, d), act_dtype),
                   jax.ShapeDtypeStruct((n, fh, fd), F32),
                   jax.ShapeDtypeStruct((n, fh, fd), F32),
                   jax.ShapeDtypeStruct((n, d), act_dtype),
                   jax.ShapeDtypeStruct((n, d), act_dtype),
                   jax.ShapeDtypeStruct((n, 2 * d), BF16),
                   jax.ShapeDtypeStruct((n, LANES), F32)],
        compiler_params=_params(("parallel",)),
        name="inproj",
    )(x2d, g1, w_all, bias_s, qg, kg)


def _mlstm_kernel(*refs, L, mh, md, nb, has_init):
    if has_init:
        (q_ref, k_ref, v_ref, o_ref, s_ref, g_ref, c0_ref, n0_ref, m0_ref,
         h_ref, c_ref, n_ref, m_ref) = refs
    else:
        q_ref, k_ref, v_ref, o_ref, s_ref, g_ref, h_ref, c_ref, n_ref, m_ref = refs

    @pl.when(pl.program_id(1) == 0)
    def _():
        if has_init:
            c_ref[...] = c0_ref[...]
            n_ref[...] = n0_ref[...]
            m_ref[...] = m0_ref[...]
        else:
            c_ref[...] = jnp.zeros_like(c_ref)
            n_ref[...] = jnp.zeros_like(n_ref)
            m_ref[...] = jnp.zeros_like(m_ref)

    ri = lax.broadcasted_iota(I32, (L, L), 0)
    ci = lax.broadcasted_iota(I32, (L, L), 1)
    tri = ci <= ri
    eye = ci == ri
    for bb in range(nb):
        rs = slice(bb * L, (bb + 1) * L)
        s = s_ref[rs, :]
        m_all = m_ref[bb]
        lane = lax.broadcasted_iota(I32, m_all.shape, 1)
        m_out = m_all
        for hh in range(mh):
            sl = slice(hh * md, (hh + 1) * md)
            ig_col = s[:, hh:hh + 1]
            lf_col = s[:, mh + hh:mh + hh + 1]
            lf_row = _col_to_row(lf_col, eye)
            ig_row = _col_to_row(ig_col, eye)
            b_col = jnp.sum(jnp.where(tri, lf_row, 0.0), axis=1, keepdims=True)
            b_row = _col_to_row(b_col, eye)
            m_prev = m_all[:, hh:hh + 1]
            log_w = jnp.where(tri, b_col - b_row + ig_row, NEG_INF)
            log_inter = b_col + m_prev
            m_t = jnp.maximum(log_inter, jnp.max(log_w, axis=1, keepdims=True))
            w_intra = jnp.exp(log_w - m_t)
            w_inter = jnp.exp(log_inter - m_t)
            q = q_ref[rs, sl].astype(BF16)
            k = k_ref[rs, sl].astype(BF16)
            v = v_ref[rs, sl].astype(BF16)
            sm = _nt_dot(q, k) * w_intra
            c_prev = c_ref[bb, hh]
            n_prev = n_ref[bb, hh:hh + 1, :]
            num = (w_inter * jnp.dot(q, c_prev.astype(BF16), preferred_element_type=F32)
                   + jnp.dot(sm.astype(BF16), v, preferred_element_type=F32))
            den = (w_inter * jnp.sum(q.astype(F32) * n_prev, axis=1, keepdims=True)
                   + jnp.sum(sm, axis=1, keepdims=True))
            hv = num / jnp.maximum(jnp.abs(den), jnp.exp(-m_t))
            m_new = m_t[L - 1:L, :]
            b_last = b_col[L - 1:L, :]
            decay = jnp.exp(b_last + m_prev - m_new)
            w_rows = jnp.exp(b_last - b_col + ig_col - m_new)
            kw = k.astype(F32) * w_rows
            c_ref[bb, hh] = decay * c_prev + _tn_dot(kw.astype(BF16), v)
            n_ref[bb, hh:hh + 1, :] = decay * n_prev + jnp.sum(kw, axis=0, keepdims=True)
            m_out = jnp.where(lane == hh, m_new, m_out)
            ms = jnp.mean(hv * hv, axis=-1, keepdims=True)
            hn = hv * lax.rsqrt(ms + RMS_EPS) * g_ref[:, sl]
            h_ref[rs, sl] = (_sigmoid(o_ref[rs, sl].astype(F32)) * hn).astype(h_ref.dtype)
        m_ref[bb] = m_out


def _mlstm(pm, small, g_m, *, batch, seq, mh, nb=1, init=None, out_dtype=BF16):
    d = pm.shape[1] // 4
    md = d // mh
    L = MLSTM_CHUNK if seq % MLSTM_CHUNK == 0 else seq
    nc = seq // L
    assert nb == 1 or nc == 1, "several sequences per step only when a sequence is one chunk"
    sec = lambda j: (lambda b, c: (b * nc + c, j))
    in_specs = [pl.BlockSpec((nb * L, d), sec(0)), pl.BlockSpec((nb * L, d), sec(1)),
                pl.BlockSpec((nb * L, d), sec(2)), pl.BlockSpec((nb * L, d), sec(3)),
                pl.BlockSpec((nb * L, LANES), sec(0)),
                pl.BlockSpec((1, d), lambda b, c: (0, 0))]
    args = [pm, pm, pm, pm, small, g_m]
    state_specs = [pl.BlockSpec((nb, mh, md, md), lambda b, c: (b, 0, 0, 0)),
                   pl.BlockSpec((nb, mh, md), lambda b, c: (b, 0, 0)),
                   pl.BlockSpec((nb, 1, LANES), lambda b, c: (b, 0, 0))]
    if init is not None:
        in_specs += state_specs
        args += list(init)
    kern = functools.partial(_mlstm_kernel, L=L, mh=mh, md=md, nb=nb, has_init=init is not None)
    return pl.pallas_call(
        kern, grid=(batch // nb, nc),
        in_specs=in_specs,
        out_specs=[pl.BlockSpec((nb * L, d), lambda b, c: (b * nc + c, 0))] + state_specs,
        out_shape=[jax.ShapeDtypeStruct((batch * seq, d), out_dtype),
                   jax.ShapeDtypeStruct((batch, mh, md, md), F32),
                   jax.ShapeDtypeStruct((batch, mh, md), F32),
                   jax.ShapeDtypeStruct((batch, 1, LANES), F32)],
        compiler_params=_params(("parallel", "arbitrary")),
        name="mlstm",
    )(*args)


def _cumsum_kernel(x_ref, o_ref, *, chunk, carry):
    rows, t = x_ref.shape
    si = lax.broadcasted_iota(I32, (chunk, chunk), 0)
    ti = lax.broadcasted_iota(I32, (chunk, chunk), 1)
    tri = (si <= ti).astype(F32)
    run = jnp.zeros((rows, 1), F32)
    for j in range(t // chunk):
        sl = slice(j * chunk, (j + 1) * chunk)
        loc = jnp.dot(x_ref[:, sl], tri, precision=lax.Precision.HIGHEST, preferred_element_type=F32)
        if carry:
            loc = loc + run
            run = loc[:, chunk - 1:chunk]
        o_ref[:, sl] = loc


def _cumsum_lanes(x, *, carry, block_rows):
    rows, t = x.shape
    kern = functools.partial(_cumsum_kernel, chunk=LANES, carry=carry)
    return pl.pallas_call(
        kern, grid=(rows // block_rows,),
        in_specs=[pl.BlockSpec((block_rows, t), lambda i: (i, 0))],
        out_specs=pl.BlockSpec((block_rows, t), lambda i: (i, 0)),
        out_shape=jax.ShapeDtypeStruct((rows, t), F32),
        compiler_params=_params(("parallel",)),
        name="cumsum",
    )(x)


def _fbias_kernel(s_ref, qx_ref, kx_ref, run_ref, *, fh, fd, lf0):
    @pl.when(pl.program_id(1) == 0)
    def _():
        run_ref[...] = jnp.zeros_like(run_ref)

    s = s_ref[...]
    L = s.shape[0]
    ti = lax.broadcasted_iota(I32, (L, L), 0)
    si = lax.broadcasted_iota(I32, (L, L), 1)
    tri = jnp.where(si <= ti, 1.0, 0.0)
    f_all = jnp.dot(tri, s, precision=lax.Precision.HIGHEST, preferred_element_type=F32) + run_ref[...]
    run_ref[...] = f_all[L - 1:L, :]
    lane = lax.broadcasted_iota(I32, (L, fd), 1)
    for h in range(fh):
        fb = jnp.broadcast_to(f_all[:, lf0 + h:lf0 + h + 1] * LOG2E, (L, fd))
        hi = fb.astype(BF16).astype(F32)
        r1 = fb - hi
        mid = r1.astype(BF16).astype(F32)
        lo = (r1 - mid).astype(BF16).astype(F32)
        sl = slice(h * fd, (h + 1) * fd)
        qx_ref[:, sl] = jnp.where(lane == 0, hi, jnp.where(lane == 1, mid, jnp.where(
            lane == 2, lo, jnp.where(lane < 6, 1.0, 0.0)))).astype(BF16)
        kx_ref[:, sl] = jnp.where(lane < 3, 1.0, jnp.where(lane == 3, -hi, jnp.where(
            lane == 4, -mid, jnp.where(lane == 5, -lo, 0.0)))).astype(BF16)


def _fbias(small, *, batch, seq, fh, fd, lf0):
    n = small.shape[0]
    L = LANES
    nc = seq // L
    kern = functools.partial(_fbias_kernel, fh=fh, fd=fd, lf0=lf0)
    return pl.pallas_call(
        kern, grid=(batch, nc),
        in_specs=[pl.BlockSpec((L, LANES), lambda b, c: (b * nc + c, 0))],
        out_specs=[pl.BlockSpec((L, fh * fd), lambda b, c: (b * nc + c, 0)),
                   pl.BlockSpec((L, fh * fd), lambda b, c: (b * nc + c, 0))],
        out_shape=[jax.ShapeDtypeStruct((n, fh * fd), BF16), jax.ShapeDtypeStruct((n, fh * fd), BF16)],
        scratch_shapes=[pltpu.VMEM((1, LANES), F32)],
        compiler_params=_params(("parallel", "arbitrary")),
        name="fbias",
    )(small)


def _fox_prompt_kernel(q_ref, qx_ref, k_ref, kx_ref, v_ref, o_ref, *, seq, tq, gh, fd):
    nq = seq // tq
    ri = lax.broadcasted_iota(I32, (tq, tq), 0)
    ci = lax.broadcasted_iota(I32, (tq, tq), 1)
    causal = ci <= ri

    def softmax_step(carry, s, v):
        m, l, acc = carry
        m_new = jnp.maximum(m, jnp.max(s, axis=1, keepdims=True))
        a = jnp.exp2(m - m_new)
        p = jnp.exp2(s - m_new)
        l = a * l + jnp.sum(p, axis=1, keepdims=True)
        acc = a * acc + jnp.dot(p.astype(BF16), v, preferred_element_type=F32)
        return m_new, l, acc

    def q_body(qi, _):
        q0 = pl.multiple_of(qi * tq, tq)

        def block(g, carry_g, k0, masked):
            gs = slice(g * fd, (g + 1) * fd)
            qa = jnp.concatenate([q_ref[pl.ds(q0, tq), gs], qx_ref[pl.ds(q0, tq), gs]], axis=1)
            ka = jnp.concatenate([k_ref[pl.ds(k0, tq), gs], kx_ref[pl.ds(k0, tq), gs]], axis=1)
            s = _nt_dot(qa, ka)
            if masked:
                s = jnp.where(causal, s, NEG_INF)
            return softmax_step(carry_g, s, v_ref[pl.ds(k0, tq), gs])

        def kv_body(kj, carry):
            k0 = pl.multiple_of(kj * tq, tq)
            return tuple(block(g, carry[g], k0, False) for g in range(gh))

        init = tuple((jnp.full((tq, 1), NEG_INF, F32), jnp.zeros((tq, 1), F32), jnp.zeros((tq, fd), F32))
                     for _ in range(gh))
        carry = lax.fori_loop(0, qi, kv_body, init)
        for g in range(gh):
            _, l, acc = block(g, carry[g], q0, True)
            o_ref[pl.ds(q0, tq), g * fd:(g + 1) * fd] = (acc / l).astype(o_ref.dtype)
        return 0

    lax.fori_loop(0, nq, q_body, 0)


def _fox_prompt(fq, qx, kb, kx, vb, *, batch, seq, fh, tq=512, gh=2):
    n, d = fq.shape
    fd = d // fh
    blk = lambda b, h: (b, h)
    kern = functools.partial(_fox_prompt_kernel, seq=seq, tq=tq, gh=gh, fd=fd)
    return pl.pallas_call(
        kern, grid=(batch, fh // gh),
        in_specs=[pl.BlockSpec((seq, gh * fd), blk)] * 5,
        out_specs=pl.BlockSpec((seq, gh * fd), blk),
        out_shape=jax.ShapeDtypeStruct((n, d), BF16),
        compiler_params=_params(("parallel", "parallel")),
        name="fox_prompt",
    )(fq, qx, kb, kx, vb)


def _fox_sample_kernel(pt_ref, q_ref, kn_ref, vn_ref, s_ref, *rest, n_pages, page, fh, fd, ts, lf0):
    floc = rest[0:n_pages]
    kpages = rest[n_pages:2 * n_pages]
    vpages = rest[2 * n_pages:3 * n_pages]
    o_ref = rest[3 * n_pages]
    p_scr, pn_scr, l_scr = rest[3 * n_pages + 1:]
    d = fh * fd
    rows = fh * ts
    phase = pl.program_id(1)

    @pl.when(phase == 0)
    def _():
        q = q_ref[...].astype(F32)
        qt = jnp.concatenate([q] * fh, axis=0)
        r_head = lax.broadcasted_iota(I32, (rows, d), 0) // ts
        c_head = lax.broadcasted_iota(I32, (rows, d), 1) // fd
        qbd = jnp.where(r_head == c_head, qt, 0.0).astype(BF16)

        off = jnp.zeros((fh, 1), F32)
        fk_pages = []
        for i in range(n_pages):
            fp = floc[i][...] + off
            fk_pages.append(fp)
            off = fp[:, page - 1:page]
        fk = jnp.concatenate(fk_pages, axis=1)
        fk_rows = jnp.concatenate(
            [jnp.broadcast_to(fk[h:h + 1, :], (ts, fk.shape[1])) for h in range(fh)], axis=0)

        sm = s_ref[...]
        ri = lax.broadcasted_iota(I32, (ts, ts), 0)
        ci = lax.broadcasted_iota(I32, (ts, ts), 1)
        eye = ci == ri
        tri = ci <= ri
        fq_cols, bias_new = [], []
        for h in range(fh):
            lf_col = sm[:, lf0 + h:lf0 + h + 1]
            lf_row = _col_to_row(lf_col, eye)
            cum_col = jnp.sum(jnp.where(tri, lf_row, 0.0), axis=1, keepdims=True)
            fq_h = off[h:h + 1, :] + cum_col
            fq_cols.append(fq_h)
            bias_new.append(fq_h - _col_to_row(fq_h, eye))
        fq_col = jnp.concatenate(fq_cols, axis=0)
        bias_n = jnp.concatenate(bias_new, axis=0)
        causal_n = jnp.concatenate([tri] * fh, axis=0)

        s_parts = []
        for i in range(0, n_pages, 2):
            kp = jnp.concatenate(
                [jnp.concatenate([kpages[i + j][pl.ds(h, page, stride=fh), :] for h in range(fh)], axis=1)
                 for j in range(2)],
                axis=0).astype(BF16)
            s_parts.append(_nt_dot(qbd, kp))
        s_past = jnp.concatenate(s_parts, axis=1) + (fq_col - fk_rows)
        s_new = _nt_dot(qbd, kn_ref[...].astype(BF16)) + bias_n
        s_new = jnp.where(causal_n, s_new, NEG_INF)
        m = jnp.maximum(jnp.max(s_past, axis=1, keepdims=True), jnp.max(s_new, axis=1, keepdims=True))
        p_past = jnp.exp(s_past - m)
        p_new = jnp.exp(s_new - m)
        l_scr[...] = jnp.sum(p_past, axis=1, keepdims=True) + jnp.sum(p_new, axis=1, keepdims=True)
        p_scr[...] = p_past.astype(BF16)
        pn_scr[...] = p_new

    @pl.when(phase == 1)
    def _():
        acc = jnp.dot(pn_scr[...].astype(BF16), vn_ref[...].astype(BF16), preferred_element_type=F32)
        for i in range(0, n_pages, 2):
            vp = jnp.concatenate(
                [jnp.concatenate([vpages[i + j][pl.ds(h, page, stride=fh), :] for h in range(fh)], axis=1)
                 for j in range(2)],
                axis=0).astype(BF16)
            acc = acc + jnp.dot(p_scr[:, i * page:(i + 2) * page], vp, preferred_element_type=F32)
        acc = acc / l_scr[...]
        o_ref[...] = jnp.concatenate(
            [acc[h * ts:(h + 1) * ts, h * fd:(h + 1) * fd] for h in range(fh)], axis=1)


def _fox_sample(page_table, fq, kb, vb, small, floc, cache_k, cache_v, *, fh, lf0):
    bs, n_pages = page_table.shape
    n, d = fq.shape
    ts = n // bs
    fd = d // fh
    page = cache_k.shape[1] // fh
    rows = fh * ts
    tok = lambda b, ph, pt: (b, 0)

    def kmap(i):
        return lambda b, ph, pt: (pt[b, i], 0, 0)

    def vmap_(i):
        return lambda b, ph, pt: (pt[jnp.maximum(b - 1 + ph, 0), i], 0, 0)

    def fmap(i):
        return lambda b, ph, pt: (pt[b, i], 0, 0)

    in_specs = ([pl.BlockSpec((ts, d), tok), pl.BlockSpec((ts, d), tok), pl.BlockSpec((ts, d), tok),
                 pl.BlockSpec((ts, LANES), tok)]
                + [pl.BlockSpec((None, fh, page), fmap(i)) for i in range(n_pages)]
                + [pl.BlockSpec((None, page * fh, fd), kmap(i)) for i in range(n_pages)]
                + [pl.BlockSpec((None, page * fh, fd), vmap_(i)) for i in range(n_pages)])
    kern = functools.partial(_fox_sample_kernel, n_pages=n_pages, page=page, fh=fh, fd=fd, ts=ts, lf0=lf0)
    return pl.pallas_call(
        kern,
        grid_spec=pltpu.PrefetchScalarGridSpec(
            num_scalar_prefetch=1, grid=(bs, 2), in_specs=in_specs,
            out_specs=pl.BlockSpec((ts, d), tok),
            scratch_shapes=[pltpu.VMEM((rows, n_pages * page), BF16),
                            pltpu.VMEM((rows, ts), F32),
                            pltpu.VMEM((rows, 1), F32)]),
        out_shape=jax.ShapeDtypeStruct((n, d), F32),
        compiler_params=_params(("arbitrary", "arbitrary")),
        name="fox_sample",
    )(page_table, fq, kb, vb, small, *([floc] * n_pages), *([cache_k] * n_pages), *([cache_v] * n_pages))


def _post_kernel(xp_ref, hap_ref, obp_ref, gp_ref, xs_ref, has_ref, obs_ref, gs_ref, *rest, tiles_p, **kw):
    @pl.when(pl.program_id(0) < tiles_p)
    def _():
        _post_body(xp_ref, hap_ref, obp_ref, gp_ref, *rest, **kw)

    @pl.when(pl.program_id(0) >= tiles_p)
    def _():
        _post_body(xs_ref, has_ref, obs_ref, gs_ref, *rest, **kw)


def _post_body(x_ref, ha_ref, ob_ref, gate_ref, wa_ref, wb_ref, wo_ref, g2_ref, wr_ref, br_ref,
               x2_ref, xn_ref, route_ref, *, d, n_exp, n_groups):
    ba = jnp.dot(ha_ref[...].astype(BF16), wa_ref[...], preferred_element_type=F32)
    bb = jnp.dot(ob_ref[...].astype(BF16), wb_ref[...], preferred_element_type=F32)
    gates = gate_ref[...].astype(F32)
    merged = gates[:, 0:d] * ba + gates[:, d:2 * d] * bb
    x2 = x_ref[...] + jnp.dot(merged.astype(BF16), wo_ref[...], preferred_element_type=F32)
    x2_ref[...] = x2
    ms = jnp.mean(x2 * x2, axis=-1, keepdims=True)
    xn = x2 * lax.rsqrt(ms + RMS_EPS) * g2_ref[...]
    tm = xn.shape[0]
    for j in range(d // LANES):
        xn_ref[pl.ds(j, tm, stride=d // LANES), :] = xn[:, j * LANES:(j + 1) * LANES]

    xh = xn.astype(BF16)
    xl = (xn - xh.astype(F32)).astype(BF16)
    hh = jnp.dot(xh, wr_ref[...], preferred_element_type=F32)
    logits = (hh[:, 0:LANES] + hh[:, LANES:2 * LANES]
              + jnp.dot(xl, wr_ref[:, 0:LANES], preferred_element_type=F32)
              + br_ref[...])
    lane = lax.broadcasted_iota(I32, logits.shape, 1)
    lane_f = lane.astype(F32)
    big = float(LANES)
    epg = n_exp // n_groups
    in_groups = (lane >= n_exp) & (lane < n_exp + n_groups)
    gl = jnp.where(in_groups, logits, NEG_INF)
    gmax = jnp.max(gl, axis=1, keepdims=True)
    gidx = jnp.min(jnp.where(gl == gmax, lane_f, big), axis=1, keepdims=True) - float(n_exp)
    g_p = 1.0 / jnp.sum(jnp.exp(gl - gmax), axis=1, keepdims=True)
    in_group = (lane < n_exp) & ((lane // epg).astype(F32) == gidx)
    el = jnp.where(in_group, logits, NEG_INF)
    m1 = jnp.max(el, axis=1, keepdims=True)
    i1 = jnp.min(jnp.where(el == m1, lane_f, big), axis=1, keepdims=True)
    el2 = jnp.where(lane_f == i1, NEG_INF, el)
    m2 = jnp.max(el2, axis=1, keepdims=True)
    i2 = jnp.min(jnp.where(el2 == m2, lane_f, big), axis=1, keepdims=True)
    esum = jnp.sum(jnp.exp(el - m1), axis=1, keepdims=True)
    p1 = 1.0 / esum
    p2 = jnp.exp(m2 - m1) / esum
    psum = p1 + p2
    w1 = p1 / psum * g_p
    w2 = p2 / psum * g_p
    route_ref[...] = jnp.where(lane == 0, i1, jnp.where(lane == 1, i2,
                               jnp.where(lane == 2, w1, jnp.where(lane == 3, w2, 0.0))))


def _post(acts_p, acts_s, wa, wb, wo, g2, wr, br, *, n_exp, n_groups, tm=256):
    n_p, d = acts_p[0].shape
    n_s = acts_s[0].shape[0]
    tiles_p, tiles_s = n_p // tm, n_s // tm
    n = n_p + n_s
    rpt = d // LANES
    row_p = lambda i: (jnp.minimum(i, tiles_p - 1), 0)
    row_s = lambda i: (jnp.maximum(i - tiles_p, 0), 0)
    row = lambda i: (i, 0)
    const = lambda i: (0, 0)
    kern = functools.partial(_post_kernel, tiles_p=tiles_p, d=d, n_exp=n_exp, n_groups=n_groups)
    wspec = lambda: pl.BlockSpec((d, d), const, pipeline_mode=pl.Buffered(1))
    act_specs = lambda r: [pl.BlockSpec((tm, d), r), pl.BlockSpec((tm, d), r), pl.BlockSpec((tm, d), r),
                           pl.BlockSpec((tm, 2 * d), r)]
    return pl.pallas_call(
        kern, grid=(tiles_p + tiles_s,),
        in_specs=act_specs(row_p) + act_specs(row_s) + [
            wspec(), wspec(), wspec(),
            pl.BlockSpec((1, d), const), pl.BlockSpec((d, 2 * LANES), const), pl.BlockSpec((1, LANES), const)],
        out_specs=[pl.BlockSpec((tm, d), row), pl.BlockSpec((tm * rpt, LANES), row),
                   pl.BlockSpec((tm, LANES), row)],
        out_shape=[jax.ShapeDtypeStruct((n, d), F32), jax.ShapeDtypeStruct((n * rpt, LANES), F32),
                   jax.ShapeDtypeStruct((n, LANES), F32)],
        compiler_params=_params(("arbitrary",)),
        name="post",
    )(*acts_p, *acts_s, wa, wb, wo, g2, wr, br)


def _dispatch_kernel(ps_ref, pl_ref, nu_ref, dest_ref, xn_hbm, xs_hbm, zbuf, sem_rows, sem_pad,
                     *, tm, n_tiles, n_exp, nt):
    i = pl.program_id(0)

    def pad_dmas(act):
        for e in range(n_exp):
            pos = ps_ref[e]
            left = pl_ref[e]
            for c in PAD_CHUNKS:
                @pl.when((left & c) != 0)
                def _(pos=pos, c=c):
                    act(pltpu.make_async_copy(zbuf.at[pl.ds(0, c)], xs_hbm.at[pl.ds(pos, c)], sem_pad))
                pos = pos + (left & c)

        def unused_tile(t, carry):
            for part in range(tm // PAD_CHUNKS[0]):
                act(pltpu.make_async_copy(
                    zbuf, xs_hbm.at[pl.ds(t * tm + part * PAD_CHUNKS[0], PAD_CHUNKS[0])], sem_pad))
            return carry
        lax.fori_loop(nu_ref[0], nt, unused_tile, 0)

    @pl.when(i == 0)
    def _():
        zbuf[...] = jnp.zeros_like(zbuf)
        pad_dmas(lambda cp: cp.start())

    def wait_tile():
        for _ in range(2):
            pltpu.make_async_copy(xn_hbm.at[pl.ds(0, tm)], xs_hbm.at[pl.ds(0, tm)], sem_rows).wait()

    @pl.when(i > 0)
    def _():
        wait_tile()

    def body(j, c):
        for u in range(DMA_UNROLL // 2):
            r = j * (DMA_UNROLL // 2) + u
            for kk in range(2):
                pltpu.make_async_copy(xn_hbm.at[i * tm + r], xs_hbm.at[dest_ref[0, kk, r]], sem_rows).start()
        return c
    lax.fori_loop(0, tm // (DMA_UNROLL // 2), body, 0)

    @pl.when(i == n_tiles - 1)
    def _():
        wait_tile()
        pad_dmas(lambda cp: cp.wait())


def _dispatch(pad_start, pad_len, n_used, dest_t, xn_rows, *, nt, tm):
    n_tiles = dest_t.shape[0]
    n_exp = pad_start.shape[0]
    rpt = xn_rows.shape[0] // (n_tiles * tm)
    xn3 = xn_rows.reshape(n_tiles * tm, rpt, LANES)
    kern = functools.partial(_dispatch_kernel, tm=tm, n_tiles=n_tiles, n_exp=n_exp, nt=nt)
    return pl.pallas_call(
        kern,
        grid_spec=pltpu.PrefetchScalarGridSpec(
            num_scalar_prefetch=3, grid=(n_tiles,),
            in_specs=[pl.BlockSpec((1, 8, tm), lambda i, ps, pln, nu: (i, 0, 0), memory_space=pltpu.SMEM),
                      pl.BlockSpec(memory_space=pl.ANY)],
            out_specs=pl.BlockSpec(memory_space=pl.ANY),
            scratch_shapes=[pltpu.VMEM((PAD_CHUNKS[0], rpt, LANES), F32),
                            pltpu.SemaphoreType.DMA(()), pltpu.SemaphoreType.DMA(())]),
        out_shape=jax.ShapeDtypeStruct((nt * tm, rpt, LANES), F32),
        compiler_params=_params(("arbitrary",)),
        name="dispatch",
    )(pad_start, pad_len, n_used, dest_t, xn3)


def _experts_kernel(te_ref, nu_ref, x_ref, wg_ref, wu_ref, wd_ref, y_ref, *, tm, rpt):
    t = pl.program_id(0)

    @pl.when(t < nu_ref[0])
    def _():
        x = jnp.concatenate([x_ref[pl.ds(j, tm, stride=rpt), :] for j in range(rpt)], axis=1).astype(BF16)
        g = jnp.dot(x, wg_ref[0].astype(BF16), preferred_element_type=F32)
        u = jnp.dot(x, wu_ref[0].astype(BF16), preferred_element_type=F32)
        hg = (g * _sigmoid(g)) * u
        y = jnp.dot(hg.astype(BF16), wd_ref[0].astype(BF16), preferred_element_type=F32)
        for j in range(rpt):
            y_ref[pl.ds(j, tm, stride=rpt), :] = y[:, j * LANES:(j + 1) * LANES]

    @pl.when(t >= nu_ref[0])
    def _():
        y_ref[...] = jnp.zeros_like(y_ref)


def _experts(tile_expert, n_used, x_sorted, w_gate, w_up, w_down, *, tm):
    nt = tile_expert.shape[0]
    n_exp, d, de = w_gate.shape
    rpt = d // LANES
    emap = lambda t, te, nu: (te[t], 0, 0)
    kern = functools.partial(_experts_kernel, tm=tm, rpt=rpt)
    return pl.pallas_call(
        kern,
        grid_spec=pltpu.PrefetchScalarGridSpec(
            num_scalar_prefetch=2, grid=(nt,),
            in_specs=[pl.BlockSpec((tm * rpt, LANES), lambda t, te, nu: (jnp.minimum(t, nu[0] - 1), 0)),
                      pl.BlockSpec((1, d, de), emap), pl.BlockSpec((1, d, de), emap),
                      pl.BlockSpec((1, de, d), emap)],
            out_specs=pl.BlockSpec((tm * rpt, LANES), lambda t, te, nu: (t, 0))),
        out_shape=jax.ShapeDtypeStruct((nt * tm * rpt, LANES), F32),
        compiler_params=_params(("arbitrary",)),
        name="experts",
    )(tile_expert, n_used, x_sorted, w_gate, w_up, w_down)


def _combine_kernel(idx_cur, idx_nxt, x2_ref, route_ref, ys_hbm, y_ref, buf, sem, *, tm, n_tiles, rpt):
    i = pl.program_id(0)
    slot = i % 2

    def gather(idx_ref, dst_slot):
        def body(j, c):
            for u in range(DMA_UNROLL // 2):
                r = j * (DMA_UNROLL // 2) + u
                for kk in range(2):
                    src = pl.multiple_of(idx_ref[0, kk, r] * rpt, rpt)
                    pltpu.make_async_copy(ys_hbm.at[pl.ds(src, rpt), :],
                                          buf.at[dst_slot, kk, pl.ds(r * rpt, rpt), :], sem.at[dst_slot]).start()
            return c
        lax.fori_loop(0, tm // (DMA_UNROLL // 2), body, 0)

    @pl.when(i == 0)
    def _():
        gather(idx_cur, 0)

    @pl.when(i + 1 < n_tiles)
    def _():
        gather(idx_nxt, 1 - slot)

    for kk in range(2):
        pltpu.make_async_copy(ys_hbm.at[pl.ds(0, tm * rpt), :], buf.at[slot, kk], sem.at[slot]).wait()
    route = route_ref[...]
    w0 = route[:, 2:3]
    w1 = route[:, 3:4]
    for j in range(rpt):
        sl = slice(j * LANES, (j + 1) * LANES)
        y_ref[:, sl] = (x2_ref[:, sl] + w0 * buf[slot, 0, pl.ds(j, tm, stride=rpt), :]
                        + w1 * buf[slot, 1, pl.ds(j, tm, stride=rpt), :])


def _combine(dest_t, x2, route, y_sorted, *, row0, rows, tm=256):
    d = x2.shape[1]
    rpt = d // LANES
    t0 = row0 // tm
    n_tiles = rows // tm
    kern = functools.partial(_combine_kernel, tm=tm, n_tiles=n_tiles, rpt=rpt)
    return pl.pallas_call(
        kern, grid=(n_tiles,),
        in_specs=[pl.BlockSpec((1, 8, tm), lambda i: (t0 + i, 0, 0), memory_space=pltpu.SMEM),
                  pl.BlockSpec((1, 8, tm), lambda i: (t0 + jnp.minimum(i + 1, n_tiles - 1), 0, 0),
                               memory_space=pltpu.SMEM),
                  pl.BlockSpec((tm, d), lambda i: (t0 + i, 0)),
                  pl.BlockSpec((tm, LANES), lambda i: (t0 + i, 0)),
                  pl.BlockSpec(memory_space=pl.ANY)],
        out_specs=pl.BlockSpec((tm, d), lambda i: (i, 0)),
        out_shape=jax.ShapeDtypeStruct((rows, d), F32),
        scratch_shapes=[pltpu.VMEM((2, 2, tm * rpt, LANES), F32), pltpu.SemaphoreType.DMA((2,))],
        compiler_params=_params(("arbitrary",)),
        name="combine",
    )(dest_t, dest_t, x2, route, y_sorted)


def _rank_kernel(route_ref, rank_ref, cnt_ref, *, tm):
    @pl.when(pl.program_id(0) == 0)
    def _():
        cnt_ref[...] = jnp.zeros_like(cnt_ref)

    route = route_ref[...]
    lane = lax.broadcasted_iota(I32, route.shape, 1)
    lane_f = lane.astype(F32)
    oh0 = lane_f == route[:, 0:1]
    oh1 = lane_f == route[:, 1:2]
    oh = jnp.where(oh0, 1.0, jnp.where(oh1, 1.0, 0.0))
    ri = lax.broadcasted_iota(I32, (tm, tm), 0)
    ci = lax.broadcasted_iota(I32, (tm, tm), 1)
    earlier = jnp.where(ci < ri, 1.0, 0.0).astype(BF16)
    base = cnt_ref[...] + jnp.dot(earlier, oh.astype(BF16), preferred_element_type=F32)
    r0 = jnp.sum(jnp.where(oh0, base, 0.0), axis=1, keepdims=True)
    r1 = jnp.sum(jnp.where(oh1, base, 0.0), axis=1, keepdims=True)
    cnt_ref[...] += jnp.sum(oh, axis=0, keepdims=True)
    rank_ref[...] = jnp.where(lane == 0, r0, jnp.where(lane == 1, r1, 0.0))


def _ranks(route, *, tm):
    n = route.shape[0]
    return pl.pallas_call(
        functools.partial(_rank_kernel, tm=tm), grid=(n // tm,),
        in_specs=[pl.BlockSpec((tm, LANES), lambda i: (i, 0))],
        out_specs=[pl.BlockSpec((tm, LANES), lambda i: (i, 0)), pl.BlockSpec((1, LANES), lambda i: (0, 0))],
        out_shape=[jax.ShapeDtypeStruct((n, LANES), F32), jax.ShapeDtypeStruct((1, LANES), F32)],
        compiler_params=_params(("arbitrary",)),
        name="ranks",
    )(route)


def _dest_kernel(route_ref, rank_ref, start_ref, dest_ref):
    route = route_ref[...]
    rank = rank_ref[...]
    lane = lax.broadcasted_iota(I32, route.shape, 1)
    lane_f = lane.astype(F32)
    start = start_ref[...]
    d0 = rank[:, 0:1] + jnp.sum(jnp.where(lane_f == route[:, 0:1], start, 0.0), axis=1, keepdims=True)
    d1 = rank[:, 1:2] + jnp.sum(jnp.where(lane_f == route[:, 1:2], start, 0.0), axis=1, keepdims=True)
    dd = jnp.where(lane == 0, d0, jnp.where(lane == 1, d1, 0.0))
    dest_ref[0] = dd.T[0:8, :].astype(I32)


def _dest(route, rank, start_row, *, tm):
    n = route.shape[0]
    return pl.pallas_call(
        _dest_kernel, grid=(n // tm,),
        in_specs=[pl.BlockSpec((tm, LANES), lambda i: (i, 0)), pl.BlockSpec((tm, LANES), lambda i: (i, 0)),
                  pl.BlockSpec((1, LANES), lambda i: (0, 0))],
        out_specs=pl.BlockSpec((1, 8, tm), lambda i: (i, 0, 0)),
        out_shape=jax.ShapeDtypeStruct((n // tm, 8, tm), I32),
        compiler_params=_params(("parallel",)),
        name="dest",
    )(route, rank, start_row)


def _moe_plan(route, *, n_exp, tm):
    n = route.shape[0]
    nt = (2 * n + n_exp * (tm - 1)) // tm
    rank, cnt_f = _ranks(route, tm=tm)
    cnt = cnt_f[0, :n_exp].astype(I32)
    ptiles = (cnt + tm - 1) // tm
    tile_end = jnp.cumsum(ptiles)
    tile_start = tile_end - ptiles
    start_row = jnp.pad((tile_start * tm).astype(F32), (0, LANES - n_exp)).reshape(1, LANES)
    dest_t = _dest(route, rank, start_row, tm=tm)
    n_used = tile_end[-1]
    tiles = jnp.arange(nt, dtype=I32)
    te = jnp.minimum(jnp.sum((tile_end[None, :] <= tiles[:, None]).astype(I32), axis=1), n_exp - 1)
    te_last = jnp.take(te, jnp.maximum(n_used - 1, 0))
    tile_expert = jnp.where(tiles < n_used, te, te_last)
    pad_start = tile_start * tm + cnt
    pad_len = ptiles * tm - cnt
    return tile_expert, n_used.reshape(1).astype(I32), dest_t, pad_start, pad_len, nt


def kernel(x_prompt, x_sample, cache_k, cache_v, cache_lf, state_C, state_n, state_m, page_table,
           norm1_g, w_in, b_igate, b_fgate_mlstm, b_fgate_fox, mlstm_norm_g, q_norm_g, k_norm_g,
           w_branch_mlstm, w_branch_fox, w_out, norm2_g, w_router_group, b_router_group,
           w_router_expert, b_router_expert, w_gate, w_up, w_down):
    depth = w_in.shape[0]
    assert depth == 1, "single-layer step"
    bp, tp, d = x_prompt.shape
    bs, ts, _ = x_sample.shape
    mh = b_igate.shape[-1]
    fh = b_fgate_fox.shape[-1]
    fd = q_norm_g.shape[-1]
    md = d // mh
    n_exp = w_gate.shape[1]
    n_pages = page_table.shape[1]
    page = cache_k.shape[2]
    assert 2 * mh + fh <= LANES and n_exp + N_GROUPS <= LANES
    np_tok, ns_tok = bp * tp, bs * ts
    l = 0

    w = w_in[l]
    o = 0
    secs = {}
    for name, width in (("m", 4 * d), ("mi", mh), ("mf", mh), ("f", 3 * d), ("ff", fh), ("g", 2 * d)):
        secs[name] = w[:, o:o + width]
        o += width
    pad = jnp.zeros((d, LANES - 2 * mh - fh), F32)
    w_all = jnp.concatenate([secs["m"], secs["f"], secs["g"], secs["mi"], secs["mf"], secs["ff"], pad],
                            axis=1).astype(BF16)
    bias_s = jnp.concatenate([b_igate[l], b_fgate_mlstm[l], b_fgate_fox[l],
                              jnp.zeros((LANES - 2 * mh - fh,), F32)]).reshape(1, LANES)
    g1 = norm1_g[l].reshape(1, d)
    qg = q_norm_g[l].reshape(1, fd)
    kg = k_norm_g[l].reshape(1, fd)
    g_m = mlstm_norm_g[l].reshape(1, d)
    wa = w_branch_mlstm[l].astype(BF16)
    wb = w_branch_fox[l].astype(BF16)
    wo = w_out[l].astype(BF16)
    g2 = norm2_g[l].reshape(1, d)
    wr32 = jnp.concatenate([w_router_expert[l], w_router_group[l],
                            jnp.zeros((d, LANES - n_exp - N_GROUPS), F32)], axis=1)
    wr_hi = wr32.astype(BF16)
    wr = jnp.concatenate([wr_hi, (wr32 - wr_hi.astype(F32)).astype(BF16)], axis=1)
    br = jnp.concatenate([b_router_expert[l], b_router_group[l],
                          jnp.zeros((LANES - n_exp - N_GROUPS,), F32)]).reshape(1, LANES)

    inproj = functools.partial(_inproj, g1=g1, w_all=w_all, bias_s=bias_s, qg=qg, kg=kg, mh=mh, fh=fh)
    pm_p, fq_p, k3_p, v3_p, kb_p, vb_p, gate_p, small_p = inproj(x_prompt.reshape(np_tok, d), act_dtype=BF16,
                                                                 q_unit=LOG2E)
    pm_s, fq_s, k3_s, v3_s, kb_s, vb_s, gate_s, small_s = inproj(x_sample.reshape(ns_tok, d), act_dtype=F32)

    ha_p, c_p, n_p, m_p = _mlstm(pm_p, small_p, g_m, batch=bp, seq=tp, mh=mh)
    m0 = jnp.pad(state_m[l], ((0, 0), (0, LANES - mh))).reshape(bs, 1, LANES)
    ha_s, c_s, n_s, m_s = _mlstm(pm_s, small_s, g_m, batch=bs, seq=ts, mh=mh, nb=4,
                                 init=(state_C[l], state_n[l], m0), out_dtype=F32)

    lf_p = small_p[:, 2 * mh:2 * mh + fh]
    lf_s = small_s[:, 2 * mh:2 * mh + fh]
    qx_p, kx_p = _fbias(small_p, batch=bp, seq=tp, fh=fh, fd=fd, lf0=2 * mh)
    ob_p = _fox_prompt(fq_p, qx_p, kb_p, kx_p, vb_p, batch=bp, seq=tp, fh=fh)

    n_phys = cache_k.shape[1]
    lft_c = cache_lf[l].transpose(0, 2, 1).reshape(n_phys * fh, page)
    floc = _cumsum_lanes(lft_c, carry=False, block_rows=n_phys * fh // 8).reshape(n_phys, fh, page)
    ck = cache_k[l].reshape(n_phys, page * fh, fd)
    cv = cache_v[l].reshape(n_phys, page * fh, fd)
    ob_s = _fox_sample(page_table, fq_s, kb_s, vb_s, small_s, floc, ck, cv, fh=fh, lf0=2 * mh)

    tm = MOE_TILE
    x2, xn_rows, route = _post((x_prompt.reshape(np_tok, d), ha_p, ob_p, gate_p),
                               (x_sample.reshape(ns_tok, d), ha_s, ob_s, gate_s),
                               wa, wb, wo, g2, wr, br, n_exp=n_exp, n_groups=N_GROUPS, tm=tm)

    tile_expert, n_used, dest_t, pad_start, pad_len, nt = _moe_plan(route, n_exp=n_exp, tm=tm)
    x_sorted = _dispatch(pad_start, pad_len, n_used, dest_t, xn_rows, nt=nt, tm=tm)
    y_sorted = _experts(tile_expert, n_used, x_sorted.reshape(nt * tm * (d // LANES), LANES),
                        w_gate[l], w_up[l], w_down[l], tm=tm)
    y_p = _combine(dest_t, x2, route, y_sorted, row0=0, rows=np_tok, tm=tm)
    y_s = _combine(dest_t, x2, route, y_sorted, row0=np_tok, rows=ns_tok, tm=tm)

    return (y_p.reshape(bp, tp, d), y_s.reshape(bs, ts, d),
            k3_p.reshape(1, bp, tp, fh, fd), v3_p.reshape(1, bp, tp, fh, fd), lf_p.reshape(1, bp, tp, fh),
            k3_s.reshape(1, bs, ts, fh, fd), v3_s.reshape(1, bs, ts, fh, fd), lf_s.reshape(1, bs, ts, fh),
            c_p[None], n_p[None], m_p[:, 0, :mh][None],
            c_s[None], n_s[None], m_s[:, 0, :mh][None])
```

```python
import functools

import jax
import jax.numpy as jnp
from jax import lax
from jax.experimental import pallas as pl
from jax.experimental.pallas import tpu as pltpu

F32 = jnp.float32
BF16 = jnp.bfloat16
I32 = jnp.int32
RMS_EPS = 1e-6
LANES = 128
MLSTM_CHUNK = 128
N_GROUPS = 4
EXPERTS_PER_GROUP = 8
VMEM_LIMIT = 56 * 1024 * 1024
NEG_INF = float("-inf")
LOG2E = 1.4426950408889634
DMA_UNROLL = 8
MOE_TILE = 256
PAD_CHUNKS = (128, 64, 32, 16, 8, 4, 2, 1)


def _params(sem, vmem=VMEM_LIMIT):
    return pltpu.CompilerParams(dimension_semantics=sem, vmem_limit_bytes=vmem)


def _log_sigmoid(x):
    return -(jnp.maximum(-x, 0.0) + jnp.log1p(jnp.exp(-jnp.abs(x))))


def _sigmoid(x):
    return 1.0 / (1.0 + jnp.exp(-x))


def _nt_dot(a, b):
    return lax.dot_general(a, b, (((1,), (1,)), ((), ())), preferred_element_type=F32)


def _tn_dot(a, b):
    return lax.dot_general(a, b, (((0,), (0,)), ((), ())), preferred_element_type=F32)


def _col_to_row(col, eye):
    return jnp.sum(jnp.where(eye, col, 0.0), axis=0, keepdims=True)


def _row_to_col(row, eye):
    return jnp.sum(jnp.where(eye, row, 0.0), axis=1, keepdims=True)


def _inproj_kernel(x_ref, g1_ref, w_ref, bias_ref, qg_ref, kg_ref,
                   pm_ref, fq_ref, k3_ref, v3_ref, kb_ref, vb_ref, gate_ref, small_ref,
                   *, d, mh, fh, fd, k_scale, q_scale):
    x = x_ref[...]
    ms = jnp.mean(x * x, axis=-1, keepdims=True)
    h = (x * lax.rsqrt(ms + RMS_EPS) * g1_ref[...]).astype(BF16)

    def proj(c0, width):
        return jnp.dot(h, w_ref[:, c0:c0 + width], preferred_element_type=F32)

    pm_ref[:, 0:d] = proj(0, d).astype(pm_ref.dtype)
    pm_ref[:, d:2 * d] = (proj(d, d) * k_scale).astype(pm_ref.dtype)
    pm_ref[:, 2 * d:3 * d] = proj(2 * d, d).astype(pm_ref.dtype)
    pm_ref[:, 3 * d:4 * d] = proj(3 * d, d).astype(pm_ref.dtype)

    def head_norm(a, g):
        ms_h = jnp.mean(a * a, axis=-1, keepdims=True)
        return a * lax.rsqrt(ms_h + RMS_EPS) * g

    fq = proj(4 * d, d)
    fk = proj(5 * d, d)
    fv = proj(6 * d, d)
    for hh in range(fh):
        sl = slice(hh * fd, (hh + 1) * fd)
        fq_ref[:, sl] = (head_norm(fq[:, sl], qg_ref[...]) * q_scale).astype(fq_ref.dtype)
        kn = head_norm(fk[:, sl], kg_ref[...])
        k3_ref[:, hh, :] = kn
        kb_ref[:, sl] = kn.astype(kb_ref.dtype)
        v3_ref[:, hh, :] = fv[:, sl]
    vb_ref[...] = fv.astype(vb_ref.dtype)

    gate_ref[...] = _sigmoid(proj(7 * d, 2 * d)).astype(gate_ref.dtype)

    sm = proj(9 * d, LANES) + bias_ref[...]
    lane = lax.broadcasted_iota(I32, sm.shape, 1)
    sm = jnp.where(lane < mh, sm, jnp.where(lane < 2 * mh + fh, _log_sigmoid(sm), 0.0))
    small_ref[...] = sm


def _inproj(x2d, g1, w_all, bias_s, qg, kg, *, mh, fh, act_dtype, q_unit=1.0, tm=256):
    n, d = x2d.shape
    fd = d // fh
    md = d // mh
    kern = functools.partial(_inproj_kernel, d=d, mh=mh, fh=fh, fd=fd,
                             k_scale=md ** -0.5, q_scale=fd ** -0.5 * q_unit)
    row = lambda i: (i, 0)
    const = lambda i: (0, 0)
    return pl.pallas_call(
        kern, grid=(n // tm,),
        in_specs=[pl.BlockSpec((tm, d), row),
                  pl.BlockSpec((1, d), const),
                  pl.BlockSpec(w_all.shape, const, pipeline_mode=pl.Buffered(1)),
                  pl.BlockSpec((1, LANES), const),
                  pl.BlockSpec((1, fd), const),
                  pl.BlockSpec((1, fd), const)],
        out_specs=[pl.BlockSpec((tm, 4 * d), row),
                   pl.BlockSpec((tm, d), row),
                   pl.BlockSpec((tm, fh, fd), lambda i: (i, 0, 0)),
                   pl.BlockSpec((tm, fh, fd), lambda i: (i, 0, 0)),
                   pl.BlockSpec((tm, d), row),
                   pl.BlockSpec((tm, d), row),
                   pl.BlockSpec((tm, 2 * d), row),
                   pl.BlockSpec((tm, LANES), row)],
        out_shape=[jax.ShapeDtypeStruct((n, 4 * d), act_dtype),
                   jax.ShapeDtypeStruct((n, d), act_dtype),
                   jax.ShapeDtypeStruct((n, fh, fd), F32),
                   jax.ShapeDtypeStruct((n, fh, fd), F32),
                   jax.ShapeDtypeStruct((n, d), act_dtype),
                   jax.ShapeDtypeStruct((n, d), act_dtype),
                   jax.ShapeDtypeStruct((n, 2 * d), BF16),
                   jax.ShapeDtypeStruct((n, LANES), F32)],
        compiler_params=_params(("parallel",)),
        name="inproj",
    )(x2d, g1, w_all, bias_s, qg, kg)


def _mlstm_kernel(*refs, L, mh, md, nb, has_init):
    if has_init:
        (q_ref, k_ref, v_ref, o_ref, s_ref, g_ref, c0_ref, n0_ref, m0_ref,
         h_ref, c_ref, n_ref, m_ref) = refs
    else:
        q_ref, k_ref, v_ref, o_ref, s_ref, g_ref, h_ref, c_ref, n_ref, m_ref = refs

    @pl.when(pl.program_id(1) == 0)
    def _():
        if has_init:
            c_ref[...] = c0_ref[...]
            n_ref[...] = n0_ref[...]
            m_ref[...] = m0_ref[...]
        else:
            c_ref[...] = jnp.zeros_like(c_ref)
            n_ref[...] = jnp.zeros_like(n_ref)
            m_ref[...] = jnp.zeros_like(m_ref)

    ri = lax.broadcasted_iota(I32, (L, L), 0)
    ci = lax.broadcasted_iota(I32, (L, L), 1)
    tri = ci <= ri
    eye = ci == ri
    for bb in range(nb):
        rs = slice(bb * L, (bb + 1) * L)
        s = s_ref[rs, :]
        m_all = m_ref[bb]
        lane = lax.broadcasted_iota(I32, m_all.shape, 1)
        m_out = m_all
        for hh in range(mh):
            sl = slice(hh * md, (hh + 1) * md)
            ig_col = s[:, hh:hh + 1]
            lf_col = s[:, mh + hh:mh + hh + 1]
            lf_row = _col_to_row(lf_col, eye)
            ig_row = _col_to_row(ig_col, eye)
            b_col = jnp.sum(jnp.where(tri, lf_row, 0.0), axis=1, keepdims=True)
            b_row = _col_to_row(b_col, eye)
            m_prev = m_all[:, hh:hh + 1]
            log_w = jnp.where(tri, b_col - b_row + ig_row, NEG_INF)
            log_inter = b_col + m_prev
            m_t = jnp.maximum(log_inter, jnp.max(log_w, axis=1, keepdims=True))
            w_intra = jnp.exp(log_w - m_t)
            w_inter = jnp.exp(log_inter - m_t)
            q = q_ref[rs, sl].astype(BF16)
            k = k_ref[rs, sl].astype(BF16)
            v = v_ref[rs, sl].astype(BF16)
            sm = _nt_dot(q, k) * w_intra
            c_prev = c_ref[bb, hh]
            n_prev = n_ref[bb, hh:hh + 1, :]
            num = (w_inter * jnp.dot(q, c_prev.astype(BF16), preferred_element_type=F32)
                   + jnp.dot(sm.astype(BF16), v, preferred_element_type=F32))
            den = (w_inter * jnp.sum(q.astype(F32) * n_prev, axis=1, keepdims=True)
                   + jnp.sum(sm, axis=1, keepdims=True))
            hv = num / jnp.maximum(jnp.abs(den), jnp.exp(-m_t))
            m_new = m_t[L - 1:L, :]
            b_last = b_col[L - 1:L, :]
            decay = jnp.exp(b_last + m_prev - m_new)
            w_rows = jnp.exp(b_last - b_col + ig_col - m_new)
            kw = k.astype(F32) * w_rows
            c_ref[bb, hh] = decay * c_prev + _tn_dot(kw.astype(BF16), v)
            n_ref[bb, hh:hh + 1, :] = decay * n_prev + jnp.sum(kw, axis=0, keepdims=True)
            m_out = jnp.where(lane == hh, m_new, m_out)
            ms = jnp.mean(hv * hv, axis=-1, keepdims=True)
            hn = hv * lax.rsqrt(ms + RMS_EPS) * g_ref[:, sl]
            h_ref[rs, sl] = (_sigmoid(o_ref[rs, sl].astype(F32)) * hn).astype(h_ref.dtype)
        m_ref[bb] = m_out


def _mlstm(pm, small, g_m, *, batch, seq, mh, nb=1, init=None, out_dtype=BF16):
    d = pm.shape[1] // 4
    md = d // mh
    L = MLSTM_CHUNK if seq % MLSTM_CHUNK == 0 else seq
    nc = seq // L
    assert nb == 1 or nc == 1, "several sequences per step only when a sequence is one chunk"
    sec = lambda j: (lambda b, c: (b * nc + c, j))
    in_specs = [pl.BlockSpec((nb * L, d), sec(0)), pl.BlockSpec((nb * L, d), sec(1)),
                pl.BlockSpec((nb * L, d), sec(2)), pl.BlockSpec((nb * L, d), sec(3)),
                pl.BlockSpec((nb * L, LANES), sec(0)),
                pl.BlockSpec((1, d), lambda b, c: (0, 0))]
    args = [pm, pm, pm, pm, small, g_m]
    state_specs = [pl.BlockSpec((nb, mh, md, md), lambda b, c: (b, 0, 0, 0)),
                   pl.BlockSpec((nb, mh, md), lambda b, c: (b, 0, 0)),
                   pl.BlockSpec((nb, 1, LANES), lambda b, c: (b, 0, 0))]
    if init is not None:
        in_specs += state_specs
        args += list(init)
    kern = functools.partial(_mlstm_kernel, L=L, mh=mh, md=md, nb=nb, has_init=init is not None)
    return pl.pallas_call(
        kern, grid=(batch // nb, nc),
        in_specs=in_specs,
        out_specs=[pl.BlockSpec((nb * L, d), lambda b, c: (b * nc + c, 0))] + state_specs,
        out_shape=[jax.ShapeDtypeStruct((batch * seq, d), out_dtype),
                   jax.ShapeDtypeStruct((batch, mh, md, md), F32),
                   jax.ShapeDtypeStruct((batch, mh, md), F32),
                   jax.ShapeDtypeStruct((batch, 1, LANES), F32)],
        compiler_params=_params(("parallel", "arbitrary")),
        name="mlstm",
    )(*args)


def _cumsum_kernel(x_ref, o_ref, *, chunk, carry):
    rows, t = x_ref.shape
    si = lax.broadcasted_iota(I32, (chunk, chunk), 0)
    ti = lax.broadcasted_iota(I32, (chunk, chunk), 1)
    tri = (si <= ti).astype(F32)
    run = jnp.zeros((rows, 1), F32)
    for j in range(t // chunk):
        sl = slice(j * chunk, (j + 1) * chunk)
        loc = jnp.dot(x_ref[:, sl], tri, precision=lax.Precision.HIGHEST, preferred_element_type=F32)
        if carry:
            loc = loc + run
            run = loc[:, chunk - 1:chunk]
        o_ref[:, sl] = loc


def _cumsum_lanes(x, *, carry, block_rows):
    rows, t = x.shape
    kern = functools.partial(_cumsum_kernel, chunk=LANES, carry=carry)
    return pl.pallas_call(
        kern, grid=(rows // block_rows,),
        in_specs=[pl.BlockSpec((block_rows, t), lambda i: (i, 0))],
        out_specs=pl.BlockSpec((block_rows, t), lambda i: (i, 0)),
        out_shape=jax.ShapeDtypeStruct((rows, t), F32),
        compiler_params=_params(("parallel",)),
        name="cumsum",
    )(x)


def _fbias_kernel(s_ref, qx_ref, kx_ref, run_ref, *, fh, fd, lf0):
    @pl.when(pl.program_id(1) == 0)
    def _():
        run_ref[...] = jnp.zeros_like(run_ref)

    s = s_ref[...]
    L = s.shape[0]
    ti = lax.broadcasted_iota(I32, (L, L), 0)
    si = lax.broadcasted_iota(I32, (L, L), 1)
    tri = jnp.where(si <= ti, 1.0, 0.0)
    f_all = jnp.dot(tri, s, precision=lax.Precision.HIGHEST, preferred_element_type=F32) + run_ref[...]
    run_ref[...] = f_all[L - 1:L, :]
    lane = lax.broadcasted_iota(I32, (L, fd), 1)
    for h in range(fh):
        fb = jnp.broadcast_to(f_all[:, lf0 + h:lf0 + h + 1] * LOG2E, (L, fd))
        hi = fb.astype(BF16).astype(F32)
        r1 = fb - hi
        mid = r1.astype(BF16).astype(F32)
        lo = (r1 - mid).astype(BF16).astype(F32)
        sl = slice(h * fd, (h + 1) * fd)
        qx_ref[:, sl] = jnp.where(lane == 0, hi, jnp.where(lane == 1, mid, jnp.where(
            lane == 2, lo, jnp.where(lane < 6, 1.0, 0.0)))).astype(BF16)
        kx_ref[:, sl] = jnp.where(lane < 3, 1.0, jnp.where(lane == 3, -hi, jnp.where(
            lane == 4, -mid, jnp.where(lane == 5, -lo, 0.0)))).astype(BF16)


def _fbias(small, *, batch, seq, fh, fd, lf0):
    n = small.shape[0]
    L = 4 * LANES if seq % (4 * LANES) == 0 else LANES
    nc = seq // L
    kern = functools.partial(_fbias_kernel, fh=fh, fd=fd, lf0=lf0)
    return pl.pallas_call(
        kern, grid=(batch, nc),
        in_specs=[pl.BlockSpec((L, LANES), lambda b, c: (b * nc + c, 0))],
        out_specs=[pl.BlockSpec((L, fh * fd), lambda b, c: (b * nc + c, 0)),
                   pl.BlockSpec((L, fh * fd), lambda b, c: (b * nc + c, 0))],
        out_shape=[jax.ShapeDtypeStruct((n, fh * fd), BF16), jax.ShapeDtypeStruct((n, fh * fd), BF16)],
        scratch_shapes=[pltpu.VMEM((1, LANES), F32)],
        compiler_params=_params(("parallel", "arbitrary")),
        name="fbias",
    )(small)


def _fox_prompt_kernel(q_ref, qx_ref, k_ref, kx_ref, v_ref, o_ref, *, seq, tq, gh, fd):
    nq = seq // tq
    ri = lax.broadcasted_iota(I32, (tq, tq), 0)
    ci = lax.broadcasted_iota(I32, (tq, tq), 1)
    causal = ci <= ri

    def softmax_step(carry, s, v):
        m, l, acc = carry
        m_new = jnp.maximum(m, jnp.max(s, axis=1, keepdims=True))
        a = jnp.exp2(m - m_new)
        p = jnp.exp2(s - m_new)
        l = a * l + jnp.sum(p, axis=1, keepdims=True)
        acc = a * acc + jnp.dot(p.astype(BF16), v, preferred_element_type=F32)
        return m_new, l, acc

    def q_body(qi, _):
        q0 = pl.multiple_of(qi * tq, tq)

        def block(g, carry_g, k0, masked):
            gs = slice(g * fd, (g + 1) * fd)
            qa = jnp.concatenate([q_ref[pl.ds(q0, tq), gs], qx_ref[pl.ds(q0, tq), gs]], axis=1)
            ka = jnp.concatenate([k_ref[pl.ds(k0, tq), gs], kx_ref[pl.ds(k0, tq), gs]], axis=1)
            s = _nt_dot(qa, ka)
            if masked:
                s = jnp.where(causal, s, NEG_INF)
            return softmax_step(carry_g, s, v_ref[pl.ds(k0, tq), gs])

        def kv_body(kj, carry):
            k0 = pl.multiple_of(kj * tq, tq)
            return tuple(block(g, carry[g], k0, False) for g in range(gh))

        init = tuple((jnp.full((tq, 1), NEG_INF, F32), jnp.zeros((tq, 1), F32), jnp.zeros((tq, fd), F32))
                     for _ in range(gh))
        carry = lax.fori_loop(0, qi, kv_body, init)
        for g in range(gh):
            _, l, acc = block(g, carry[g], q0, True)
            o_ref[pl.ds(q0, tq), g * fd:(g + 1) * fd] = (acc / l).astype(o_ref.dtype)
        return 0

    lax.fori_loop(0, nq, q_body, 0)


def _fox_prompt(fq, qx, kb, kx, vb, *, batch, seq, fh, tq=512, gh=2):
    n, d = fq.shape
    fd = d // fh
    blk = lambda b, h: (b, h)
    kern = functools.partial(_fox_prompt_kernel, seq=seq, tq=tq, gh=gh, fd=fd)
    return pl.pallas_call(
        kern, grid=(batch, fh // gh),
        in_specs=[pl.BlockSpec((seq, gh * fd), blk)] * 5,
        out_specs=pl.BlockSpec((seq, gh * fd), blk),
        out_shape=jax.ShapeDtypeStruct((n, d), BF16),
        compiler_params=_params(("parallel", "parallel")),
        name="fox_prompt",
    )(fq, qx, kb, kx, vb)


def _fox_sample_kernel(pt_ref, q_ref, kn_ref, vn_ref, s_ref, *rest, n_pages, page, fh, fd, ts, lf0):
    floc = rest[0:n_pages]
    kpages = rest[n_pages:2 * n_pages]
    vpages = rest[2 * n_pages:3 * n_pages]
    o_ref = rest[3 * n_pages]
    p_scr, pn_scr, l_scr = rest[3 * n_pages + 1:]
    d = fh * fd
    rows = fh * ts
    phase = pl.program_id(1)

    @pl.when(phase == 0)
    def _():
        q = q_ref[...].astype(F32)
        qt = jnp.concatenate([q] * fh, axis=0)
        r_head = lax.broadcasted_iota(I32, (rows, d), 0) // ts
        c_head = lax.broadcasted_iota(I32, (rows, d), 1) // fd
        qbd = jnp.where(r_head == c_head, qt, 0.0).astype(BF16)

        off = jnp.zeros((fh, 1), F32)
        fk_pages = []
        for i in range(n_pages):
            fp = floc[i][...] + off
            fk_pages.append(fp)
            off = fp[:, page - 1:page]
        fk = jnp.concatenate(fk_pages, axis=1)
        fk_rows = jnp.concatenate(
            [jnp.broadcast_to(fk[h:h + 1, :], (ts, fk.shape[1])) for h in range(fh)], axis=0)

        sm = s_ref[...]
        ri = lax.broadcasted_iota(I32, (ts, ts), 0)
        ci = lax.broadcasted_iota(I32, (ts, ts), 1)
        eye = ci == ri
        tri = ci <= ri
        fq_cols, bias_new = [], []
        for h in range(fh):
            lf_col = sm[:, lf0 + h:lf0 + h + 1]
            lf_row = _col_to_row(lf_col, eye)
            cum_col = jnp.sum(jnp.where(tri, lf_row, 0.0), axis=1, keepdims=True)
            fq_h = off[h:h + 1, :] + cum_col
            fq_cols.append(fq_h)
            bias_new.append(fq_h - _col_to_row(fq_h, eye))
        fq_col = jnp.concatenate(fq_cols, axis=0)
        bias_n = jnp.concatenate(bias_new, axis=0)
        causal_n = jnp.concatenate([tri] * fh, axis=0)

        s_parts = []
        for i in range(0, n_pages, 2):
            kp = jnp.concatenate(
                [jnp.concatenate([kpages[i + j][pl.ds(h, page, stride=fh), :] for h in range(fh)], axis=1)
                 for j in range(2)],
                axis=0).astype(BF16)
            s_parts.append(_nt_dot(qbd, kp))
        s_past = jnp.concatenate(s_parts, axis=1) + (fq_col - fk_rows)
        s_new = _nt_dot(qbd, kn_ref[...].astype(BF16)) + bias_n
        s_new = jnp.where(causal_n, s_new, NEG_INF)
        m = jnp.maximum(jnp.max(s_past, axis=1, keepdims=True), jnp.max(s_new, axis=1, keepdims=True))
        p_past = jnp.exp(s_past - m)
        p_new = jnp.exp(s_new - m)
        l_scr[...] = jnp.sum(p_past, axis=1, keepdims=True) + jnp.sum(p_new, axis=1, keepdims=True)
        p_scr[...] = p_past.astype(BF16)
        pn_scr[...] = p_new

    @pl.when(phase == 1)
    def _():
        acc = jnp.dot(pn_scr[...].astype(BF16), vn_ref[...].astype(BF16), preferred_element_type=F32)
        for i in range(0, n_pages, 2):
            vp = jnp.concatenate(
                [jnp.concatenate([vpages[i + j][pl.ds(h, page, stride=fh), :] for h in range(fh)], axis=1)
                 for j in range(2)],
                axis=0).astype(BF16)
            acc = acc + jnp.dot(p_scr[:, i * page:(i + 2) * page], vp, preferred_element_type=F32)
        acc = acc / l_scr[...]
        o_ref[...] = jnp.concatenate(
            [acc[h * ts:(h + 1) * ts, h * fd:(h + 1) * fd] for h in range(fh)], axis=1)


def _fox_sample(page_table, fq, kb, vb, small, floc, cache_k, cache_v, *, fh, lf0):
    bs, n_pages = page_table.shape
    n, d = fq.shape
    ts = n // bs
    fd = d // fh
    page = cache_k.shape[1] // fh
    rows = fh * ts
    tok = lambda b, ph, pt: (b, 0)

    def kmap(i):
        return lambda b, ph, pt: (pt[b, i], 0, 0)

    def vmap_(i):
        return lambda b, ph, pt: (pt[jnp.maximum(b - 1 + ph, 0), i], 0, 0)

    def fmap(i):
        return lambda b, ph, pt: (pt[b, i], 0, 0)

    in_specs = ([pl.BlockSpec((ts, d), tok), pl.BlockSpec((ts, d), tok), pl.BlockSpec((ts, d), tok),
                 pl.BlockSpec((ts, LANES), tok)]
                + [pl.BlockSpec((None, fh, page), fmap(i)) for i in range(n_pages)]
                + [pl.BlockSpec((None, page * fh, fd), kmap(i)) for i in range(n_pages)]
                + [pl.BlockSpec((None, page * fh, fd), vmap_(i)) for i in range(n_pages)])
    kern = functools.partial(_fox_sample_kernel, n_pages=n_pages, page=page, fh=fh, fd=fd, ts=ts, lf0=lf0)
    return pl.pallas_call(
        kern,
        grid_spec=pltpu.PrefetchScalarGridSpec(
            num_scalar_prefetch=1, grid=(bs, 2), in_specs=in_specs,
            out_specs=pl.BlockSpec((ts, d), tok),
            scratch_shapes=[pltpu.VMEM((rows, n_pages * page), BF16),
                            pltpu.VMEM((rows, ts), F32),
                            pltpu.VMEM((rows, 1), F32)]),
        out_shape=jax.ShapeDtypeStruct((n, d), F32),
        compiler_params=_params(("arbitrary", "arbitrary")),
        name="fox_sample",
    )(page_table, fq, kb, vb, small, *([floc] * n_pages), *([cache_k] * n_pages), *([cache_v] * n_pages))


def _post_kernel(xp_ref, hap_ref, obp_ref, gp_ref, xs_ref, has_ref, obs_ref, gs_ref, *rest, tiles_p, **kw):
    cnt_ref = rest[-1]

    @pl.when(pl.program_id(0) == 0)
    def _():
        cnt_ref[...] = jnp.zeros_like(cnt_ref)

    @pl.when(pl.program_id(0) < tiles_p)
    def _():
        _post_body(xp_ref, hap_ref, obp_ref, gp_ref, *rest, **kw)

    @pl.when(pl.program_id(0) >= tiles_p)
    def _():
        _post_body(xs_ref, has_ref, obs_ref, gs_ref, *rest, **kw)


def _post_body(x_ref, ha_ref, ob_ref, gate_ref, wa_ref, wb_ref, wo_ref, g2_ref, wr_ref, br_ref,
               x2_ref, xn_ref, route_ref, cnt_ref, *, d, n_exp, n_groups):
    ba = jnp.dot(ha_ref[...].astype(BF16), wa_ref[...], preferred_element_type=F32)
    bb = jnp.dot(ob_ref[...].astype(BF16), wb_ref[...], preferred_element_type=F32)
    gates = gate_ref[...].astype(F32)
    merged = gates[:, 0:d] * ba + gates[:, d:2 * d] * bb
    x2 = x_ref[...] + jnp.dot(merged.astype(BF16), wo_ref[...], preferred_element_type=F32)
    x2_ref[...] = x2
    ms = jnp.mean(x2 * x2, axis=-1, keepdims=True)
    xn = x2 * lax.rsqrt(ms + RMS_EPS) * g2_ref[...]
    tm = xn.shape[0]
    for j in range(d // LANES):
        xn_ref[pl.ds(j, tm, stride=d // LANES), :] = xn[:, j * LANES:(j + 1) * LANES]

    xh = xn.astype(BF16)
    xl = (xn - xh.astype(F32)).astype(BF16)
    hh = jnp.dot(xh, wr_ref[...], preferred_element_type=F32)
    logits = (hh[:, 0:LANES] + hh[:, LANES:2 * LANES]
              + jnp.dot(xl, wr_ref[:, 0:LANES], preferred_element_type=F32)
              + br_ref[...])
    lane = lax.broadcasted_iota(I32, logits.shape, 1)
    lane_f = lane.astype(F32)
    big = float(LANES)
    epg = n_exp // n_groups
    in_groups = (lane >= n_exp) & (lane < n_exp + n_groups)
    gl = jnp.where(in_groups, logits, NEG_INF)
    gmax = jnp.max(gl, axis=1, keepdims=True)
    gidx = jnp.min(jnp.where(gl == gmax, lane_f, big), axis=1, keepdims=True) - float(n_exp)
    g_p = 1.0 / jnp.sum(jnp.exp(gl - gmax), axis=1, keepdims=True)
    in_group = (lane < n_exp) & ((lane // epg).astype(F32) == gidx)
    el = jnp.where(in_group, logits, NEG_INF)
    m1 = jnp.max(el, axis=1, keepdims=True)
    i1 = jnp.min(jnp.where(el == m1, lane_f, big), axis=1, keepdims=True)
    el2 = jnp.where(lane_f == i1, NEG_INF, el)
    m2 = jnp.max(el2, axis=1, keepdims=True)
    i2 = jnp.min(jnp.where(el2 == m2, lane_f, big), axis=1, keepdims=True)
    esum = jnp.sum(jnp.exp(el - m1), axis=1, keepdims=True)
    p1 = 1.0 / esum
    p2 = jnp.exp(m2 - m1) / esum
    psum = p1 + p2
    w1 = p1 / psum * g_p
    w2 = p2 / psum * g_p
    r0, r1 = _rank_block(i1, i2, lane_f, cnt_ref)
    route_ref[...] = jnp.where(lane == 0, i1, jnp.where(lane == 1, i2, jnp.where(
        lane == 2, w1, jnp.where(lane == 3, w2, jnp.where(lane == 4, r0, jnp.where(lane == 5, r1, 0.0))))))


def _post(acts_p, acts_s, wa, wb, wo, g2, wr, br, *, n_exp, n_groups, tm=256):
    n_p, d = acts_p[0].shape
    n_s = acts_s[0].shape[0]
    tiles_p, tiles_s = n_p // tm, n_s // tm
    n = n_p + n_s
    rpt = d // LANES
    row_p = lambda i: (jnp.minimum(i, tiles_p - 1), 0)
    row_s = lambda i: (jnp.maximum(i - tiles_p, 0), 0)
    row = lambda i: (i, 0)
    const = lambda i: (0, 0)
    kern = functools.partial(_post_kernel, tiles_p=tiles_p, d=d, n_exp=n_exp, n_groups=n_groups)
    wspec = lambda: pl.BlockSpec((d, d), const, pipeline_mode=pl.Buffered(1))
    act_specs = lambda r: [pl.BlockSpec((tm, d), r), pl.BlockSpec((tm, d), r), pl.BlockSpec((tm, d), r),
                           pl.BlockSpec((tm, 2 * d), r)]
    return pl.pallas_call(
        kern, grid=(tiles_p + tiles_s,),
        in_specs=act_specs(row_p) + act_specs(row_s) + [
            wspec(), wspec(), wspec(),
            pl.BlockSpec((1, d), const), pl.BlockSpec((d, 2 * LANES), const), pl.BlockSpec((1, LANES), const)],
        out_specs=[pl.BlockSpec((tm, d), row), pl.BlockSpec((tm * rpt, LANES), row),
                   pl.BlockSpec((tm, LANES), row), pl.BlockSpec((1, LANES), const)],
        out_shape=[jax.ShapeDtypeStruct((n, d), F32), jax.ShapeDtypeStruct((n * rpt, LANES), F32),
                   jax.ShapeDtypeStruct((n, LANES), F32), jax.ShapeDtypeStruct((1, LANES), F32)],
        compiler_params=_params(("arbitrary",)),
        name="post",
    )(*acts_p, *acts_s, wa, wb, wo, g2, wr, br)


def _dispatch_kernel(ps_ref, pl_ref, nu_ref, dest_ref, xn_ref, xs_hbm, stage, zbuf, sem_rows, sem_pad,
                     *, tm, n_tiles, n_exp, nt):
    i = pl.program_id(0)
    slot = i % 2

    def pad_dmas(act):
        for e in range(n_exp):
            pos = ps_ref[e]
            left = pl_ref[e]
            for c in PAD_CHUNKS:
                @pl.when((left & c) != 0)
                def _(pos=pos, c=c):
                    act(pltpu.make_async_copy(zbuf.at[pl.ds(0, c)], xs_hbm.at[pl.ds(pos, c)], sem_pad))
                pos = pos + (left & c)

        def unused_tile(t, carry):
            for part in range(tm // PAD_CHUNKS[0]):
                act(pltpu.make_async_copy(
                    zbuf, xs_hbm.at[pl.ds(t * tm + part * PAD_CHUNKS[0], PAD_CHUNKS[0])], sem_pad))
            return carry
        lax.fori_loop(nu_ref[0], nt, unused_tile, 0)

    @pl.when(i == 0)
    def _():
        zbuf[...] = jnp.zeros_like(zbuf)
        pad_dmas(lambda cp: cp.start())

    def wait_tile(s):
        for _ in range(2):
            pltpu.make_async_copy(stage.at[s], xs_hbm.at[pl.ds(0, tm)], sem_rows.at[s]).wait()

    @pl.when(i >= 2)
    def _():
        wait_tile(slot)

    stage[slot] = xn_ref[...]

    def body(j, c):
        for u in range(DMA_UNROLL // 2):
            r = j * (DMA_UNROLL // 2) + u
            for kk in range(2):
                pltpu.make_async_copy(stage.at[slot, r], xs_hbm.at[dest_ref[0, kk, r]], sem_rows.at[slot]).start()
        return c
    lax.fori_loop(0, tm // (DMA_UNROLL // 2), body, 0)

    @pl.when(i == n_tiles - 1)
    def _():
        if n_tiles > 1:
            wait_tile(1 - slot)
        wait_tile(slot)
        pad_dmas(lambda cp: cp.wait())


def _dispatch(pad_start, pad_len, n_used, dest_t, xn_rows, *, nt, tm):
    n_tiles = dest_t.shape[0]
    n_exp = pad_start.shape[0]
    rpt = xn_rows.shape[0] // (n_tiles * tm)
    xn3 = xn_rows.reshape(n_tiles * tm, rpt, LANES)
    kern = functools.partial(_dispatch_kernel, tm=tm, n_tiles=n_tiles, n_exp=n_exp, nt=nt)
    return pl.pallas_call(
        kern,
        grid_spec=pltpu.PrefetchScalarGridSpec(
            num_scalar_prefetch=3, grid=(n_tiles,),
            in_specs=[pl.BlockSpec((1, 8, tm), lambda i, ps, pln, nu: (i, 0, 0), memory_space=pltpu.SMEM),
                      pl.BlockSpec((tm, rpt, LANES), lambda i, ps, pln, nu: (i, 0, 0))],
            out_specs=pl.BlockSpec(memory_space=pl.ANY),
            scratch_shapes=[pltpu.VMEM((2, tm, rpt, LANES), F32),
                            pltpu.VMEM((PAD_CHUNKS[0], rpt, LANES), F32),
                            pltpu.SemaphoreType.DMA((2,)), pltpu.SemaphoreType.DMA(())]),
        out_shape=jax.ShapeDtypeStruct((nt * tm, rpt, LANES), F32),
        compiler_params=_params(("arbitrary",)),
        name="dispatch",
    )(pad_start, pad_len, n_used, dest_t, xn3)


def _experts_kernel(te_ref, nu_ref, x_ref, wg_ref, wu_ref, wd_ref, y_ref, *, tm, rpt):
    t = pl.program_id(0)

    @pl.when(t < nu_ref[0])
    def _():
        x = jnp.concatenate([x_ref[pl.ds(j, tm, stride=rpt), :] for j in range(rpt)], axis=1).astype(BF16)
        g = jnp.dot(x, wg_ref[0].astype(BF16), preferred_element_type=F32)
        u = jnp.dot(x, wu_ref[0].astype(BF16), preferred_element_type=F32)
        hg = (g * _sigmoid(g)) * u
        y = jnp.dot(hg.astype(BF16), wd_ref[0].astype(BF16), preferred_element_type=F32)
        for j in range(rpt):
            y_ref[pl.ds(j, tm, stride=rpt), :] = y[:, j * LANES:(j + 1) * LANES]

    @pl.when(t >= nu_ref[0])
    def _():
        y_ref[...] = jnp.zeros_like(y_ref)


def _experts(tile_expert, n_used, x_sorted, w_gate, w_up, w_down, *, tm):
    nt = tile_expert.shape[0]
    n_exp, d, de = w_gate.shape
    rpt = d // LANES
    emap = lambda t, te, nu: (te[t], 0, 0)
    kern = functools.partial(_experts_kernel, tm=tm, rpt=rpt)
    return pl.pallas_call(
        kern,
        grid_spec=pltpu.PrefetchScalarGridSpec(
            num_scalar_prefetch=2, grid=(nt,),
            in_specs=[pl.BlockSpec((tm * rpt, LANES), lambda t, te, nu: (jnp.minimum(t, nu[0] - 1), 0)),
                      pl.BlockSpec((1, d, de), emap), pl.BlockSpec((1, d, de), emap),
                      pl.BlockSpec((1, de, d), emap)],
            out_specs=pl.BlockSpec((tm * rpt, LANES), lambda t, te, nu: (t, 0))),
        out_shape=jax.ShapeDtypeStruct((nt * tm * rpt, LANES), F32),
        compiler_params=_params(("arbitrary",)),
        name="experts",
    )(tile_expert, n_used, x_sorted, w_gate, w_up, w_down)


def _combine_kernel(idx_cur, idx_nxt, x2_ref, route_ref, ys_hbm, y_ref, buf, sem, *, tm, n_tiles, rpt):
    i = pl.program_id(0)
    slot = i % 2

    def gather(idx_ref, dst_slot):
        def body(j, c):
            for u in range(DMA_UNROLL // 2):
                r = j * (DMA_UNROLL // 2) + u
                for kk in range(2):
                    src = pl.multiple_of(idx_ref[0, kk, r] * rpt, rpt)
                    pltpu.make_async_copy(ys_hbm.at[pl.ds(src, rpt), :],
                                          buf.at[dst_slot, kk, pl.ds(r * rpt, rpt), :], sem.at[dst_slot]).start()
            return c
        lax.fori_loop(0, tm // (DMA_UNROLL // 2), body, 0)

    @pl.when(i == 0)
    def _():
        gather(idx_cur, 0)

    @pl.when(i + 1 < n_tiles)
    def _():
        gather(idx_nxt, 1 - slot)

    for kk in range(2):
        pltpu.make_async_copy(ys_hbm.at[pl.ds(0, tm * rpt), :], buf.at[slot, kk], sem.at[slot]).wait()
    route = route_ref[...]
    w0 = route[:, 2:3]
    w1 = route[:, 3:4]
    for j in range(rpt):
        sl = slice(j * LANES, (j + 1) * LANES)
        y_ref[:, sl] = (x2_ref[:, sl] + w0 * buf[slot, 0, pl.ds(j, tm, stride=rpt), :]
                        + w1 * buf[slot, 1, pl.ds(j, tm, stride=rpt), :])


def _combine(dest_t, x2, route, y_sorted, *, row0, rows, tm=256):
    d = x2.shape[1]
    rpt = d // LANES
    t0 = row0 // tm
    n_tiles = rows // tm
    kern = functools.partial(_combine_kernel, tm=tm, n_tiles=n_tiles, rpt=rpt)
    return pl.pallas_call(
        kern, grid=(n_tiles,),
        in_specs=[pl.BlockSpec((1, 8, tm), lambda i: (t0 + i, 0, 0), memory_space=pltpu.SMEM),
                  pl.BlockSpec((1, 8, tm), lambda i: (t0 + jnp.minimum(i + 1, n_tiles - 1), 0, 0),
                               memory_space=pltpu.SMEM),
                  pl.BlockSpec((tm, d), lambda i: (t0 + i, 0)),
                  pl.BlockSpec((tm, LANES), lambda i: (t0 + i, 0)),
                  pl.BlockSpec(memory_space=pl.ANY)],
        out_specs=pl.BlockSpec((tm, d), lambda i: (i, 0)),
        out_shape=jax.ShapeDtypeStruct((rows, d), F32),
        scratch_shapes=[pltpu.VMEM((2, 2, tm * rpt, LANES), F32), pltpu.SemaphoreType.DMA((2,))],
        compiler_params=_params(("arbitrary",)),
        name="combine",
    )(dest_t, dest_t, x2, route, y_sorted)


def _rank_block(i1, i2, lane_f, cnt_ref):
    tm = i1.shape[0]
    oh0 = lane_f == i1
    oh1 = lane_f == i2
    oh = jnp.where(oh0, 1.0, jnp.where(oh1, 1.0, 0.0))
    ri = lax.broadcasted_iota(I32, (tm, tm), 0)
    ci = lax.broadcasted_iota(I32, (tm, tm), 1)
    earlier = jnp.where(ci < ri, 1.0, 0.0).astype(BF16)
    base = cnt_ref[...] + jnp.dot(earlier, oh.astype(BF16), preferred_element_type=F32)
    r0 = jnp.sum(jnp.where(oh0, base, 0.0), axis=1, keepdims=True)
    r1 = jnp.sum(jnp.where(oh1, base, 0.0), axis=1, keepdims=True)
    cnt_ref[...] += jnp.sum(oh, axis=0, keepdims=True)
    return r0, r1


def _dest_kernel(route_ref, start_ref, dest_ref):
    route = route_ref[...]
    lane = lax.broadcasted_iota(I32, route.shape, 1)
    lane_f = lane.astype(F32)
    start = start_ref[...]
    d0 = route[:, 4:5] + jnp.sum(jnp.where(lane_f == route[:, 0:1], start, 0.0), axis=1, keepdims=True)
    d1 = route[:, 5:6] + jnp.sum(jnp.where(lane_f == route[:, 1:2], start, 0.0), axis=1, keepdims=True)
    dd = jnp.where(lane == 0, d0, jnp.where(lane == 1, d1, 0.0))
    dest_ref[0] = dd.T[0:8, :].astype(I32)


def _dest(route, start_row, *, tm):
    n = route.shape[0]
    return pl.pallas_call(
        _dest_kernel, grid=(n // tm,),
        in_specs=[pl.BlockSpec((tm, LANES), lambda i: (i, 0)), pl.BlockSpec((1, LANES), lambda i: (0, 0))],
        out_specs=pl.BlockSpec((1, 8, tm), lambda i: (i, 0, 0)),
        out_shape=jax.ShapeDtypeStruct((n // tm, 8, tm), I32),
        compiler_params=_params(("parallel",)),
        name="dest",
    )(route, start_row)


def _moe_plan(route, cnt_f, *, n_exp, tm):
    n = route.shape[0]
    nt = (2 * n + n_exp * (tm - 1)) // tm
    cnt = cnt_f[0, :n_exp].astype(I32)
    ptiles = (cnt + tm - 1) // tm
    tile_end = jnp.cumsum(ptiles)
    tile_start = tile_end - ptiles
    start_row = jnp.pad((tile_start * tm).astype(F32), (0, LANES - n_exp)).reshape(1, LANES)
    dest_t = _dest(route, start_row, tm=tm)
    n_used = tile_end[-1]
    tiles = jnp.arange(nt, dtype=I32)
    te = jnp.minimum(jnp.sum((tile_end[None, :] <= tiles[:, None]).astype(I32), axis=1), n_exp - 1)
    te_last = jnp.take(te, jnp.maximum(n_used - 1, 0))
    tile_expert = jnp.where(tiles < n_used, te, te_last)
    pad_start = tile_start * tm + cnt
    pad_len = ptiles * tm - cnt
    return tile_expert, n_used.reshape(1).astype(I32), dest_t, pad_start, pad_len, nt


def kernel(x_prompt, x_sample, cache_k, cache_v, cache_lf, state_C, state_n, state_m, page_table,
           norm1_g, w_in, b_igate, b_fgate_mlstm, b_fgate_fox, mlstm_norm_g, q_norm_g, k_norm_g,
           w_branch_mlstm, w_branch_fox, w_out, norm2_g, w_router_group, b_router_group,
           w_router_expert, b_router_expert, w_gate, w_up, w_down):
    depth = w_in.shape[0]
    assert depth == 1, "single-layer step"
    bp, tp, d = x_prompt.shape
    bs, ts, _ = x_sample.shape
    mh = b_igate.shape[-1]
    fh = b_fgate_fox.shape[-1]
    fd = q_norm_g.shape[-1]
    md = d // mh
    n_exp = w_gate.shape[1]
    n_pages = page_table.shape[1]
    page = cache_k.shape[2]
    assert 2 * mh + fh <= LANES and n_exp + N_GROUPS <= LANES
    np_tok, ns_tok = bp * tp, bs * ts
    l = 0

    w = w_in[l]
    o = 0
    secs = {}
    for name, width in (("m", 4 * d), ("mi", mh), ("mf", mh), ("f", 3 * d), ("ff", fh), ("g", 2 * d)):
        secs[name] = w[:, o:o + width]
        o += width
    pad = jnp.zeros((d, LANES - 2 * mh - fh), F32)
    w_all = jnp.concatenate([secs["m"], secs["f"], secs["g"], secs["mi"], secs["mf"], secs["ff"], pad],
                            axis=1).astype(BF16)
    bias_s = jnp.concatenate([b_igate[l], b_fgate_mlstm[l], b_fgate_fox[l],
                              jnp.zeros((LANES - 2 * mh - fh,), F32)]).reshape(1, LANES)
    g1 = norm1_g[l].reshape(1, d)
    qg = q_norm_g[l].reshape(1, fd)
    kg = k_norm_g[l].reshape(1, fd)
    g_m = mlstm_norm_g[l].reshape(1, d)
    wa = w_branch_mlstm[l].astype(BF16)
    wb = w_branch_fox[l].astype(BF16)
    wo = w_out[l].astype(BF16)
    g2 = norm2_g[l].reshape(1, d)
    wr32 = jnp.concatenate([w_router_expert[l], w_router_group[l],
                            jnp.zeros((d, LANES - n_exp - N_GROUPS), F32)], axis=1)
    wr_hi = wr32.astype(BF16)
    wr = jnp.concatenate([wr_hi, (wr32 - wr_hi.astype(F32)).astype(BF16)], axis=1)
    br = jnp.concatenate([b_router_expert[l], b_router_group[l],
                          jnp.zeros((LANES - n_exp - N_GROUPS,), F32)]).reshape(1, LANES)

    inproj = functools.partial(_inproj, g1=g1, w_all=w_all, bias_s=bias_s, qg=qg, kg=kg, mh=mh, fh=fh)
    pm_p, fq_p, k3_p, v3_p, kb_p, vb_p, gate_p, small_p = inproj(x_prompt.reshape(np_tok, d), act_dtype=BF16,
                                                                 q_unit=LOG2E)
    pm_s, fq_s, k3_s, v3_s, kb_s, vb_s, gate_s, small_s = inproj(x_sample.reshape(ns_tok, d), act_dtype=F32)

    ha_p, c_p, n_p, m_p = _mlstm(pm_p, small_p, g_m, batch=bp, seq=tp, mh=mh)
    m0 = jnp.pad(state_m[l], ((0, 0), (0, LANES - mh))).reshape(bs, 1, LANES)
    ha_s, c_s, n_s, m_s = _mlstm(pm_s, small_s, g_m, batch=bs, seq=ts, mh=mh, nb=4,
                                 init=(state_C[l], state_n[l], m0), out_dtype=F32)

    lf_p = small_p[:, 2 * mh:2 * mh + fh]
    lf_s = small_s[:, 2 * mh:2 * mh + fh]
    qx_p, kx_p = _fbias(small_p, batch=bp, seq=tp, fh=fh, fd=fd, lf0=2 * mh)
    ob_p = _fox_prompt(fq_p, qx_p, kb_p, kx_p, vb_p, batch=bp, seq=tp, fh=fh)

    n_phys = cache_k.shape[1]
    lft_c = cache_lf[l].transpose(0, 2, 1).reshape(n_phys * fh, page)
    floc = _cumsum_lanes(lft_c, carry=False, block_rows=n_phys * fh // 8).reshape(n_phys, fh, page)
    ck = cache_k[l].reshape(n_phys, page * fh, fd)
    cv = cache_v[l].reshape(n_phys, page * fh, fd)
    ob_s = _fox_sample(page_table, fq_s, kb_s, vb_s, small_s, floc, ck, cv, fh=fh, lf0=2 * mh)

    tm = MOE_TILE
    x2, xn_rows, route, cnt_f = _post((x_prompt.reshape(np_tok, d), ha_p, ob_p, gate_p),
                                      (x_sample.reshape(ns_tok, d), ha_s, ob_s, gate_s),
                                      wa, wb, wo, g2, wr, br, n_exp=n_exp, n_groups=N_GROUPS, tm=tm)

    tile_expert, n_used, dest_t, pad_start, pad_len, nt = _moe_plan(route, cnt_f, n_exp=n_exp, tm=tm)
    x_sorted = _dispatch(pad_start, pad_len, n_used, dest_t, xn_rows, nt=nt, tm=tm)
    y_sorted = _experts(tile_expert, n_used, x_sorted.reshape(nt * tm * (d // LANES), LANES),
                        w_gate[l], w_up[l], w_down[l], tm=tm)
    y_p = _combine(dest_t, x2, route, y_sorted, row0=0, rows=np_tok, tm=tm)
    y_s = _combine(dest_t, x2, route, y_sorted, row0=np_tok, rows=ns_tok, tm=tm)

    return (y_p.reshape(bp, tp, d), y_s.reshape(bs, ts, d),
            k3_p.reshape(1, bp, tp, fh, fd), v3_p.reshape(1, bp, tp, fh, fd), lf_p.reshape(1, bp, tp, fh),
            k3_s.reshape(1, bs, ts, fh, fd), v3_s.reshape(1, bs, ts, fh, fd), lf_s.reshape(1, bs, ts, fh),
            c_p[None], n_p[None], m_p[:, 0, :mh][None],
            c_s[None], n_s[None], m_s[:, 0, :mh][None])
```

```python
import functools

import jax
import jax.numpy as jnp
from jax import lax
from jax.experimental import pallas as pl
from jax.experimental.pallas import tpu as pltpu

F32 = jnp.float32
BF16 = jnp.bfloat16
I32 = jnp.int32
RMS_EPS = 1e-6
LANES = 128
MLSTM_CHUNK = 128
MLSTM_PROMPT_CHUNK = 256
N_GROUPS = 4
EXPERTS_PER_GROUP = 8
VMEM_LIMIT = 56 * 1024 * 1024
NEG_INF = float("-inf")
LOG2E = 1.4426950408889634
DMA_UNROLL = 8
MOE_TILE = 256
PAD_CHUNKS = (128, 64, 32, 16, 8, 4, 2, 1)


def _params(sem, vmem=VMEM_LIMIT):
    return pltpu.CompilerParams(dimension_semantics=sem, vmem_limit_bytes=vmem)


def _log_sigmoid(x):
    return -(jnp.maximum(-x, 0.0) + jnp.log1p(jnp.exp(-jnp.abs(x))))


def _sigmoid(x):
    return 1.0 / (1.0 + jnp.exp(-x))


def _nt_dot(a, b):
    return lax.dot_general(a, b, (((1,), (1,)), ((), ())), preferred_element_type=F32)


def _tn_dot(a, b):
    return lax.dot_general(a, b, (((0,), (0,)), ((), ())), preferred_element_type=F32)


def _col_to_row(col, eye):
    return jnp.sum(jnp.where(eye, col, 0.0), axis=0, keepdims=True)


def _row_to_col(row, eye):
    return jnp.sum(jnp.where(eye, row, 0.0), axis=1, keepdims=True)


def _inproj_kernel(x_ref, g1_ref, w_ref, bias_ref, qg_ref, kg_ref,
                   pm_ref, fq_ref, k3_ref, v3_ref, kb_ref, vb_ref, gate_ref, small_ref, *maybe_kt_ref,
                   d, mh, fh, fd, k_scale, q_scale):
    x = x_ref[...]
    ms = jnp.mean(x * x, axis=-1, keepdims=True)
    h = (x * lax.rsqrt(ms + RMS_EPS) * g1_ref[...]).astype(BF16)

    def proj(c0, width):
        return jnp.dot(h, w_ref[:, c0:c0 + width], preferred_element_type=F32)

    pm_ref[:, 0:d] = proj(0, d).astype(pm_ref.dtype)
    mk = proj(d, d) * k_scale
    pm_ref[:, d:2 * d] = mk.astype(pm_ref.dtype)
    if maybe_kt_ref:
        maybe_kt_ref[0][...] = mk.T.astype(maybe_kt_ref[0].dtype)
    pm_ref[:, 2 * d:3 * d] = proj(2 * d, d).astype(pm_ref.dtype)
    pm_ref[:, 3 * d:4 * d] = proj(3 * d, d).astype(pm_ref.dtype)

    def head_norm(a, g):
        ms_h = jnp.mean(a * a, axis=-1, keepdims=True)
        return a * lax.rsqrt(ms_h + RMS_EPS) * g

    fq = proj(4 * d, d)
    fk = proj(5 * d, d)
    fv = proj(6 * d, d)
    for hh in range(fh):
        sl = slice(hh * fd, (hh + 1) * fd)
        fq_ref[:, sl] = (head_norm(fq[:, sl], qg_ref[...]) * q_scale).astype(fq_ref.dtype)
        kn = head_norm(fk[:, sl], kg_ref[...])
        k3_ref[:, hh, :] = kn
        kb_ref[:, sl] = kn.astype(kb_ref.dtype)
        v3_ref[:, hh, :] = fv[:, sl]
    vb_ref[...] = fv.astype(vb_ref.dtype)

    gate_ref[...] = _sigmoid(proj(7 * d, 2 * d)).astype(gate_ref.dtype)

    sm = proj(9 * d, LANES) + bias_ref[...]
    lane = lax.broadcasted_iota(I32, sm.shape, 1)
    sm = jnp.where(lane < mh, sm, jnp.where(lane < 2 * mh + fh, _log_sigmoid(sm), 0.0))
    small_ref[...] = sm


def _inproj(x2d, g1, w_all, bias_s, qg, kg, *, mh, fh, act_dtype, q_unit=1.0, emit_kt=False, tm=256):
    n, d = x2d.shape
    fd = d // fh
    md = d // mh
    kern = functools.partial(_inproj_kernel, d=d, mh=mh, fh=fh, fd=fd,
                             k_scale=md ** -0.5, q_scale=fd ** -0.5 * q_unit)
    row = lambda i: (i, 0)
    const = lambda i: (0, 0)
    kt_specs = [pl.BlockSpec((d, tm), lambda i: (0, i))] if emit_kt else []
    kt_shapes = [jax.ShapeDtypeStruct((d, n), act_dtype)] if emit_kt else []
    return pl.pallas_call(
        kern, grid=(n // tm,),
        in_specs=[pl.BlockSpec((tm, d), row),
                  pl.BlockSpec((1, d), const),
                  pl.BlockSpec(w_all.shape, const, pipeline_mode=pl.Buffered(1)),
                  pl.BlockSpec((1, LANES), const),
                  pl.BlockSpec((1, fd), const),
                  pl.BlockSpec((1, fd), const)],
        out_specs=[pl.BlockSpec((tm, 4 * d), row),
                   pl.BlockSpec((tm, d), row),
                   pl.BlockSpec((tm, fh, fd), lambda i: (i, 0, 0)),
                   pl.BlockSpec((tm, fh, fd), lambda i: (i, 0, 0)),
                   pl.BlockSpec((tm, d), row),
                   pl.BlockSpec((tm, d), row),
                   pl.BlockSpec((tm, 2 * d), row),
                   pl.BlockSpec((tm, LANES), row)] + kt_specs,
        out_shape=[jax.ShapeDtypeStruct((n, 4 * d), act_dtype),
                   jax.ShapeDtypeStruct((n, d), act_dtype),
                   jax.ShapeDtypeStruct((n, fh, fd), F32),
                   jax.ShapeDtypeStruct((n, fh, fd), F32),
                   jax.ShapeDtypeStruct((n, d), act_dtype),
                   jax.ShapeDtypeStruct((n, d), act_dtype),
                   jax.ShapeDtypeStruct((n, 2 * d), BF16),
                   jax.ShapeDtypeStruct((n, LANES), F32)] + kt_shapes,
        compiler_params=_params(("parallel",)),
        name="inproj",
    )(x2d, g1, w_all, bias_s, qg, kg)


def _mlstm_kernel(*refs, L, mh, md, nb, has_init):
    if has_init:
        (q_ref, k_ref, v_ref, o_ref, s_ref, g_ref, c0_ref, n0_ref, m0_ref,
         h_ref, c_ref, n_ref, m_ref) = refs
    else:
        q_ref, k_ref, v_ref, o_ref, s_ref, g_ref, h_ref, c_ref, n_ref, m_ref = refs

    @pl.when(pl.program_id(1) == 0)
    def _():
        if has_init:
            c_ref[...] = c0_ref[...]
            n_ref[...] = n0_ref[...]
            m_ref[...] = m0_ref[...]
        else:
            c_ref[...] = jnp.zeros_like(c_ref)
            n_ref[...] = jnp.zeros_like(n_ref)
            m_ref[...] = jnp.zeros_like(m_ref)

    ri = lax.broadcasted_iota(I32, (L, L), 0)
    ci = lax.broadcasted_iota(I32, (L, L), 1)
    tri = ci <= ri
    eye = ci == ri
    for bb in range(nb):
        s = s_ref[bb]
        m_all = m_ref[bb]
        lane = lax.broadcasted_iota(I32, m_all.shape, 1)
        m_out = m_all
        for hh in range(mh):
            sl = slice(hh * md, (hh + 1) * md)
            ig_col = s[:, hh:hh + 1]
            lf_col = s[:, mh + hh:mh + hh + 1]
            lf_row = _col_to_row(lf_col, eye)
            ig_row = _col_to_row(ig_col, eye)
            b_col = jnp.sum(jnp.where(tri, lf_row, 0.0), axis=1, keepdims=True)
            b_row = _col_to_row(b_col, eye)
            m_prev = m_all[:, hh:hh + 1]
            log_w = jnp.where(tri, b_col - b_row + ig_row, NEG_INF)
            log_inter = b_col + m_prev
            m_t = jnp.maximum(log_inter, jnp.max(log_w, axis=1, keepdims=True))
            w_intra = jnp.exp(log_w - m_t)
            w_inter = jnp.exp(log_inter - m_t)
            q = q_ref[bb, :, sl].astype(BF16)
            k = k_ref[bb, :, sl].astype(BF16)
            v = v_ref[bb, :, sl].astype(BF16)
            sm = _nt_dot(q, k) * w_intra
            c_prev = c_ref[bb, hh]
            n_prev = n_ref[bb, hh:hh + 1, :]
            num = (w_inter * jnp.dot(q, c_prev.astype(BF16), preferred_element_type=F32)
                   + jnp.dot(sm.astype(BF16), v, preferred_element_type=F32))
            den = (w_inter * jnp.sum(q.astype(F32) * n_prev, axis=1, keepdims=True)
                   + jnp.sum(sm, axis=1, keepdims=True))
            hv = num / jnp.maximum(jnp.abs(den), jnp.exp(-m_t))
            m_new = m_t[L - 1:L, :]
            b_last = b_col[L - 1:L, :]
            decay = jnp.exp(b_last + m_prev - m_new)
            w_rows = jnp.exp(b_last - b_col + ig_col - m_new)
            kw = k.astype(F32) * w_rows
            c_ref[bb, hh] = decay * c_prev + _tn_dot(kw.astype(BF16), v)
            n_ref[bb, hh:hh + 1, :] = decay * n_prev + jnp.sum(kw, axis=0, keepdims=True)
            m_out = jnp.where(lane == hh, m_new, m_out)
            ms = jnp.mean(hv * hv, axis=-1, keepdims=True)
            hn = hv * lax.rsqrt(ms + RMS_EPS) * g_ref[:, sl]
            h_ref[bb, :, sl] = (_sigmoid(o_ref[bb, :, sl].astype(F32)) * hn).astype(h_ref.dtype)
        m_ref[bb] = m_out


def _mlstm(pm, small, g_m, *, batch, seq, mh, nb=1, init=None, out_dtype=BF16):
    d = pm.shape[1] // 4
    md = d // mh
    L = MLSTM_CHUNK if seq % MLSTM_CHUNK == 0 else seq
    nc = seq // L
    ng = batch // nb
    pm4 = pm.reshape(ng, nb, seq, 4 * d)
    small4 = small.reshape(ng, nb, seq, LANES)
    sec = lambda j: (lambda b, c: (b, 0, c, j))
    in_specs = [pl.BlockSpec((None, nb, L, d), sec(0)), pl.BlockSpec((None, nb, L, d), sec(1)),
                pl.BlockSpec((None, nb, L, d), sec(2)), pl.BlockSpec((None, nb, L, d), sec(3)),
                pl.BlockSpec((None, nb, L, LANES), sec(0)),
                pl.BlockSpec((1, d), lambda b, c: (0, 0))]
    args = [pm4, pm4, pm4, pm4, small4, g_m]
    state_specs = [pl.BlockSpec((nb, mh, md, md), lambda b, c: (b, 0, 0, 0)),
                   pl.BlockSpec((nb, mh, md), lambda b, c: (b, 0, 0)),
                   pl.BlockSpec((nb, 1, LANES), lambda b, c: (b, 0, 0))]
    if init is not None:
        in_specs += state_specs
        args += list(init)
    kern = functools.partial(_mlstm_kernel, L=L, mh=mh, md=md, nb=nb, has_init=init is not None)
    h4, c_out, n_out, m_out = pl.pallas_call(
        kern, grid=(ng, nc),
        in_specs=in_specs,
        out_specs=[pl.BlockSpec((None, nb, L, d), lambda b, c: (b, 0, c, 0))] + state_specs,
        out_shape=[jax.ShapeDtypeStruct((ng, nb, seq, d), out_dtype),
                   jax.ShapeDtypeStruct((batch, mh, md, md), F32),
                   jax.ShapeDtypeStruct((batch, mh, md), F32),
                   jax.ShapeDtypeStruct((batch, 1, LANES), F32)],
        compiler_params=_params(("parallel", "arbitrary")),
        name="mlstm",
    )(*args)
    return h4.reshape(batch * seq, d), c_out, n_out, m_out


def _mlstm_prompt_kernel(q_ref, kt_ref, v_ref, o_ref, s_ref, g_ref, h_ref, c_ref, n_ref, m_ref, naug,
                         *, L, mh, md, nc):
    c = pl.program_id(1)

    @pl.when(c == 0)
    def _():
        c_ref[...] = jnp.zeros_like(c_ref)
        m_ref[...] = jnp.zeros_like(m_ref)
        naug[...] = jnp.zeros_like(naug)

    s = s_ref[...]
    ri = lax.broadcasted_iota(I32, (L, L), 0)
    ci = lax.broadcasted_iota(I32, (L, L), 1)
    tri = ci <= ri
    hp = lax.Precision.HIGHEST
    r = s.T[0:8, :]
    b_cols = jnp.dot(jnp.where(tri, 1.0, 0.0), s, precision=hp, preferred_element_type=F32)
    b_rows = pltpu.roll(jnp.dot(r, jnp.where(ri <= ci, 1.0, 0.0), precision=hp, preferred_element_type=F32),
                        8 - mh, axis=0)
    g = r - b_rows
    m_all = m_ref[0]
    lane = lax.broadcasted_iota(I32, m_all.shape, 1)
    m_out = m_all
    ones_v = jnp.ones((L, LANES), BF16)
    ones_r = jnp.ones((md, LANES), BF16)
    for hh in range(mh):
        sl = slice(hh * md, (hh + 1) * md)
        m_prev = m_all[:, hh:hh + 1]
        g_row = g[hh:hh + 1, :]
        b_col = b_cols[:, mh + hh:mh + hh + 1]
        a_col = jnp.maximum(m_prev, jnp.max(jnp.where(tri, g_row, NEG_INF), axis=1, keepdims=True))
        m_t = b_col + a_col
        m_new = m_t[L - 1:L, :]
        b_last = b_col[L - 1:L, :]
        w_rows = jnp.exp(b_last - b_rows[hh:hh + 1, :] + r[hh:hh + 1, :] - m_new)
        dc = jnp.exp(b_last + m_prev - m_new)
        m_out = jnp.where(lane == hh, m_new, m_out)
        q = q_ref[:, sl]
        kt = kt_ref[sl, :]
        v_aug = jnp.concatenate([v_ref[:, sl], ones_v], axis=1)
        w_intra = jnp.exp(jnp.where(tri, g_row - a_col, NEG_INF))
        sm = jnp.dot(q, kt, preferred_element_type=F32) * w_intra
        c_prev = c_ref[0, hh]
        n_prev = naug[hh]
        c_aug = jnp.concatenate([c_prev, n_prev], axis=1).astype(BF16)
        comb = (jnp.exp(m_prev - a_col) * jnp.dot(q, c_aug, preferred_element_type=F32)
                + jnp.dot(sm.astype(BF16), v_aug, preferred_element_type=F32))
        den = jnp.maximum(jnp.abs(comb[:, md:]), jnp.exp(-m_t))
        inv = 1.0 / den
        hv = comb[:, 0:md] * jnp.concatenate([inv] * (md // LANES), axis=1)
        ms = jnp.dot((hv * hv).astype(BF16), ones_r, preferred_element_type=F32) * (1.0 / md)
        rs = lax.rsqrt(ms + RMS_EPS)
        hn = hv * jnp.concatenate([rs] * (md // LANES), axis=1) * g_ref[:, sl]
        h_ref[:, sl] = (_sigmoid(o_ref[:, sl].astype(F32)) * hn).astype(h_ref.dtype)
        kw = (kt.astype(F32) * w_rows).astype(BF16)
        upd = jnp.dot(kw, v_aug, preferred_element_type=F32)
        c_ref[0, hh] = dc * c_prev + upd[:, 0:md]
        naug[hh] = dc * n_prev + upd[:, md:]
    m_ref[0] = m_out

    @pl.when(c == nc - 1)
    def _():
        ei = lax.broadcasted_iota(I32, (md, md), 0) == lax.broadcasted_iota(I32, (md, md), 1)
        for hh in range(mh):
            cols = jnp.concatenate([naug[hh]] * (md // LANES), axis=1)
            n_ref[0, hh:hh + 1, :] = jnp.sum(jnp.where(ei, cols, 0.0), axis=0, keepdims=True)


def _mlstm_prompt(pm, kt, small, g_m, *, batch, seq, mh):
    d = pm.shape[1] // 4
    md = d // mh
    L = MLSTM_PROMPT_CHUNK
    assert seq % L == 0
    nc = seq // L
    tok = lambda j: (lambda b, c: (b * nc + c, j))
    kern = functools.partial(_mlstm_prompt_kernel, L=L, mh=mh, md=md, nc=nc)
    return pl.pallas_call(
        kern, grid=(batch, nc),
        in_specs=[pl.BlockSpec((L, d), tok(0)),
                  pl.BlockSpec((d, L), lambda b, c: (0, b * nc + c)),
                  pl.BlockSpec((L, d), tok(2)), pl.BlockSpec((L, d), tok(3)),
                  pl.BlockSpec((L, LANES), tok(0)),
                  pl.BlockSpec((1, d), lambda b, c: (0, 0))],
        out_specs=[pl.BlockSpec((L, d), tok(0)),
                   pl.BlockSpec((1, mh, md, md), lambda b, c: (b, 0, 0, 0)),
                   pl.BlockSpec((1, mh, md), lambda b, c: (b, 0, 0)),
                   pl.BlockSpec((1, 1, LANES), lambda b, c: (b, 0, 0))],
        out_shape=[jax.ShapeDtypeStruct((batch * seq, d), BF16),
                   jax.ShapeDtypeStruct((batch, mh, md, md), F32),
                   jax.ShapeDtypeStruct((batch, mh, md), F32),
                   jax.ShapeDtypeStruct((batch, 1, LANES), F32)],
        scratch_shapes=[pltpu.VMEM((mh, md, LANES), F32)],
        compiler_params=_params(("parallel", "arbitrary")),
        name="mlstm_prompt",
    )(pm, kt, pm, pm, small, g_m)


def _cumsum_kernel(x_ref, o_ref, *, chunk, carry):
    rows, t = x_ref.shape
    si = lax.broadcasted_iota(I32, (chunk, chunk), 0)
    ti = lax.broadcasted_iota(I32, (chunk, chunk), 1)
    tri = (si <= ti).astype(F32)
    run = jnp.zeros((rows, 1), F32)
    for j in range(t // chunk):
        sl = slice(j * chunk, (j + 1) * chunk)
        loc = jnp.dot(x_ref[:, sl], tri, precision=lax.Precision.HIGHEST, preferred_element_type=F32)
        if carry:
            loc = loc + run
            run = loc[:, chunk - 1:chunk]
        o_ref[:, sl] = loc


def _cumsum_lanes(x, *, carry, block_rows):
    rows, t = x.shape
    kern = functools.partial(_cumsum_kernel, chunk=LANES, carry=carry)
    return pl.pallas_call(
        kern, grid=(rows // block_rows,),
        in_specs=[pl.BlockSpec((block_rows, t), lambda i: (i, 0))],
        out_specs=pl.BlockSpec((block_rows, t), lambda i: (i, 0)),
        out_shape=jax.ShapeDtypeStruct((rows, t), F32),
        compiler_params=_params(("parallel",)),
        name="cumsum",
    )(x)


def _fbias_kernel(s_ref, qx_ref, kx_ref, run_ref, *, fh, fd, lf0):
    @pl.when(pl.program_id(1) == 0)
    def _():
        run_ref[...] = jnp.zeros_like(run_ref)

    s = s_ref[...]
    L = s.shape[0]
    ti = lax.broadcasted_iota(I32, (L, L), 0)
    si = lax.broadcasted_iota(I32, (L, L), 1)
    tri = jnp.where(si <= ti, 1.0, 0.0)
    f_all = jnp.dot(tri, s, precision=lax.Precision.HIGHEST, preferred_element_type=F32) + run_ref[...]
    run_ref[...] = f_all[L - 1:L, :]
    lane = lax.broadcasted_iota(I32, (L, fd), 1)
    for h in range(fh):
        fb = jnp.broadcast_to(f_all[:, lf0 + h:lf0 + h + 1] * LOG2E, (L, fd))
        hi = fb.astype(BF16).astype(F32)
        r1 = fb - hi
        mid = r1.astype(BF16).astype(F32)
        lo = (r1 - mid).astype(BF16).astype(F32)
        sl = slice(h * fd, (h + 1) * fd)
        qx_ref[:, sl] = jnp.where(lane == 0, hi, jnp.where(lane == 1, mid, jnp.where(
            lane == 2, lo, jnp.where(lane < 6, 1.0, 0.0)))).astype(BF16)
        kx_ref[:, sl] = jnp.where(lane < 3, 1.0, jnp.where(lane == 3, -hi, jnp.where(
            lane == 4, -mid, jnp.where(lane == 5, -lo, 0.0)))).astype(BF16)


def _fbias(small, *, batch, seq, fh, fd, lf0):
    n = small.shape[0]
    L = 4 * LANES if seq % (4 * LANES) == 0 else LANES
    nc = seq // L
    kern = functools.partial(_fbias_kernel, fh=fh, fd=fd, lf0=lf0)
    return pl.pallas_call(
        kern, grid=(batch, nc),
        in_specs=[pl.BlockSpec((L, LANES), lambda b, c: (b * nc + c, 0))],
        out_specs=[pl.BlockSpec((L, fh * fd), lambda b, c: (b * nc + c, 0)),
                   pl.BlockSpec((L, fh * fd), lambda b, c: (b * nc + c, 0))],
        out_shape=[jax.ShapeDtypeStruct((n, fh * fd), BF16), jax.ShapeDtypeStruct((n, fh * fd), BF16)],
        scratch_shapes=[pltpu.VMEM((1, LANES), F32)],
        compiler_params=_params(("parallel", "arbitrary")),
        name="fbias",
    )(small)


def _fox_prompt_kernel(q_ref, qx_ref, k_ref, kx_ref, v_ref, o_ref, *, seq, tq, gh, fd):
    nq = seq // tq
    ri = lax.broadcasted_iota(I32, (tq, tq), 0)
    ci = lax.broadcasted_iota(I32, (tq, tq), 1)
    causal = ci <= ri

    def softmax_step(carry, s, v):
        m, l, acc = carry
        m_new = jnp.maximum(m, jnp.max(s, axis=1, keepdims=True))
        a = jnp.exp2(m - m_new)
        p = jnp.exp2(s - m_new)
        l = a * l + jnp.sum(p, axis=1, keepdims=True)
        acc = a * acc + jnp.dot(p.astype(BF16), v, preferred_element_type=F32)
        return m_new, l, acc

    def q_body(qi, _):
        q0 = pl.multiple_of(qi * tq, tq)

        def block(g, carry_g, k0, masked):
            gs = slice(g * fd, (g + 1) * fd)
            qa = jnp.concatenate([q_ref[pl.ds(q0, tq), gs], qx_ref[pl.ds(q0, tq), gs]], axis=1)
            ka = jnp.concatenate([k_ref[pl.ds(k0, tq), gs], kx_ref[pl.ds(k0, tq), gs]], axis=1)
            s = _nt_dot(qa, ka)
            if masked:
                s = jnp.where(causal, s, NEG_INF)
            return softmax_step(carry_g, s, v_ref[pl.ds(k0, tq), gs])

        def kv_body(kj, carry):
            k0 = pl.multiple_of(kj * tq, tq)
            return tuple(block(g, carry[g], k0, False) for g in range(gh))

        init = tuple((jnp.full((tq, 1), NEG_INF, F32), jnp.zeros((tq, 1), F32), jnp.zeros((tq, fd), F32))
                     for _ in range(gh))
        carry = lax.fori_loop(0, qi, kv_body, init)
        for g in range(gh):
            _, l, acc = block(g, carry[g], q0, True)
            o_ref[pl.ds(q0, tq), g * fd:(g + 1) * fd] = (acc / l).astype(o_ref.dtype)
        return 0

    lax.fori_loop(0, nq, q_body, 0)


def _fox_prompt(fq, qx, kb, kx, vb, *, batch, seq, fh, tq=512, gh=2):
    n, d = fq.shape
    fd = d // fh
    blk = lambda b, h: (b, h)
    kern = functools.partial(_fox_prompt_kernel, seq=seq, tq=tq, gh=gh, fd=fd)
    return pl.pallas_call(
        kern, grid=(batch, fh // gh),
        in_specs=[pl.BlockSpec((seq, gh * fd), blk)] * 5,
        out_specs=pl.BlockSpec((seq, gh * fd), blk),
        out_shape=jax.ShapeDtypeStruct((n, d), BF16),
        compiler_params=_params(("parallel", "parallel")),
        name="fox_prompt",
    )(fq, qx, kb, kx, vb)


def _fox_sample_kernel(pt_ref, q_ref, kn_ref, vn_ref, s_ref, *rest, n_pages, page, fh, fd, ts, lf0):
    floc = rest[0:n_pages]
    kpages = rest[n_pages:2 * n_pages]
    vpages = rest[2 * n_pages:3 * n_pages]
    o_ref = rest[3 * n_pages]
    p_scr, pn_scr, l_scr = rest[3 * n_pages + 1:]
    d = fh * fd
    rows = fh * ts
    phase = pl.program_id(1)

    @pl.when(phase == 0)
    def _():
        q = q_ref[...].astype(F32)
        qt = jnp.concatenate([q] * fh, axis=0)
        r_head = lax.broadcasted_iota(I32, (rows, d), 0) // ts
        c_head = lax.broadcasted_iota(I32, (rows, d), 1) // fd
        qbd = jnp.where(r_head == c_head, qt, 0.0).astype(BF16)

        off = jnp.zeros((fh, 1), F32)
        fk_pages = []
        for i in range(n_pages):
            fp = floc[i][...] + off
            fk_pages.append(fp)
            off = fp[:, page - 1:page]
        fk = jnp.concatenate(fk_pages, axis=1)
        fk_rows = jnp.concatenate(
            [jnp.broadcast_to(fk[h:h + 1, :], (ts, fk.shape[1])) for h in range(fh)], axis=0)

        sm = s_ref[...]
        ri = lax.broadcasted_iota(I32, (ts, ts), 0)
        ci = lax.broadcasted_iota(I32, (ts, ts), 1)
        eye = ci == ri
        tri = ci <= ri
        fq_cols, bias_new = [], []
        for h in range(fh):
            lf_col = sm[:, lf0 + h:lf0 + h + 1]
            lf_row = _col_to_row(lf_col, eye)
            cum_col = jnp.sum(jnp.where(tri, lf_row, 0.0), axis=1, keepdims=True)
            fq_h = off[h:h + 1, :] + cum_col
            fq_cols.append(fq_h)
            bias_new.append(fq_h - _col_to_row(fq_h, eye))
        fq_col = jnp.concatenate(fq_cols, axis=0)
        bias_n = jnp.concatenate(bias_new, axis=0)
        causal_n = jnp.concatenate([tri] * fh, axis=0)

        s_parts = []
        for i in range(0, n_pages, 2):
            kp = jnp.concatenate(
                [jnp.concatenate([kpages[i + j][pl.ds(h, page, stride=fh), :] for h in range(fh)], axis=1)
                 for j in range(2)],
                axis=0).astype(BF16)
            s_parts.append(_nt_dot(qbd, kp))
        s_past = jnp.concatenate(s_parts, axis=1) + (fq_col - fk_rows)
        s_new = _nt_dot(qbd, kn_ref[...].astype(BF16)) + bias_n
        s_new = jnp.where(causal_n, s_new, NEG_INF)
        m = jnp.maximum(jnp.max(s_past, axis=1, keepdims=True), jnp.max(s_new, axis=1, keepdims=True))
        p_past = jnp.exp(s_past - m)
        p_new = jnp.exp(s_new - m)
        l_scr[...] = jnp.sum(p_past, axis=1, keepdims=True) + jnp.sum(p_new, axis=1, keepdims=True)
        p_scr[...] = p_past.astype(BF16)
        pn_scr[...] = p_new

    @pl.when(phase == 1)
    def _():
        acc = jnp.dot(pn_scr[...].astype(BF16), vn_ref[...].astype(BF16), preferred_element_type=F32)
        for i in range(0, n_pages, 2):
            vp = jnp.concatenate(
                [jnp.concatenate([vpages[i + j][pl.ds(h, page, stride=fh), :] for h in range(fh)], axis=1)
                 for j in range(2)],
                axis=0).astype(BF16)
            acc = acc + jnp.dot(p_scr[:, i * page:(i + 2) * page], vp, preferred_element_type=F32)
        acc = acc / l_scr[...]
        o_ref[...] = jnp.concatenate(
            [acc[h * ts:(h + 1) * ts, h * fd:(h + 1) * fd] for h in range(fh)], axis=1)


def _fox_sample(page_table, fq, kb, vb, small, floc, cache_k, cache_v, *, fh, lf0):
    bs, n_pages = page_table.shape
    n, d = fq.shape
    ts = n // bs
    fd = d // fh
    page = cache_k.shape[1] // fh
    rows = fh * ts
    tok = lambda b, ph, pt: (b, 0)

    def kmap(i):
        return lambda b, ph, pt: (pt[b, i], 0, 0)

    def vmap_(i):
        return lambda b, ph, pt: (pt[jnp.maximum(b - 1 + ph, 0), i], 0, 0)

    def fmap(i):
        return lambda b, ph, pt: (pt[b, i], 0, 0)

    in_specs = ([pl.BlockSpec((ts, d), tok), pl.BlockSpec((ts, d), tok), pl.BlockSpec((ts, d), tok),
                 pl.BlockSpec((ts, LANES), tok)]
                + [pl.BlockSpec((None, fh, page), fmap(i)) for i in range(n_pages)]
                + [pl.BlockSpec((None, page * fh, fd), kmap(i)) for i in range(n_pages)]
                + [pl.BlockSpec((None, page * fh, fd), vmap_(i)) for i in range(n_pages)])
    kern = functools.partial(_fox_sample_kernel, n_pages=n_pages, page=page, fh=fh, fd=fd, ts=ts, lf0=lf0)
    return pl.pallas_call(
        kern,
        grid_spec=pltpu.PrefetchScalarGridSpec(
            num_scalar_prefetch=1, grid=(bs, 2), in_specs=in_specs,
            out_specs=pl.BlockSpec((ts, d), tok),
            scratch_shapes=[pltpu.VMEM((rows, n_pages * page), BF16),
                            pltpu.VMEM((rows, ts), F32),
                            pltpu.VMEM((rows, 1), F32)]),
        out_shape=jax.ShapeDtypeStruct((n, d), F32),
        compiler_params=_params(("arbitrary", "arbitrary")),
        name="fox_sample",
    )(page_table, fq, kb, vb, small, *([floc] * n_pages), *([cache_k] * n_pages), *([cache_v] * n_pages))


def _post_kernel(xp_ref, hap_ref, obp_ref, gp_ref, xs_ref, has_ref, obs_ref, gs_ref, *rest, tiles_p, **kw):
    cnt_ref = rest[-1]

    @pl.when(pl.program_id(0) == 0)
    def _():
        cnt_ref[...] = jnp.zeros_like(cnt_ref)

    @pl.when(pl.program_id(0) < tiles_p)
    def _():
        _post_body(xp_ref, hap_ref, obp_ref, gp_ref, *rest, **kw)

    @pl.when(pl.program_id(0) >= tiles_p)
    def _():
        _post_body(xs_ref, has_ref, obs_ref, gs_ref, *rest, **kw)


def _post_body(x_ref, ha_ref, ob_ref, gate_ref, wa_ref, wb_ref, wo_ref, g2_ref, wr_ref, br_ref,
               x2_ref, xn_ref, route_ref, cnt_ref, *, d, n_exp, n_groups):
    ba = jnp.dot(ha_ref[...].astype(BF16), wa_ref[...], preferred_element_type=F32)
    bb = jnp.dot(ob_ref[...].astype(BF16), wb_ref[...], preferred_element_type=F32)
    gates = gate_ref[...].astype(F32)
    merged = gates[:, 0:d] * ba + gates[:, d:2 * d] * bb
    x2 = x_ref[...] + jnp.dot(merged.astype(BF16), wo_ref[...], preferred_element_type=F32)
    x2_ref[...] = x2
    ms = jnp.mean(x2 * x2, axis=-1, keepdims=True)
    xn = x2 * lax.rsqrt(ms + RMS_EPS) * g2_ref[...]
    tm = xn.shape[0]
    for j in range(d // LANES):
        xn_ref[pl.ds(j, tm, stride=d // LANES), :] = xn[:, j * LANES:(j + 1) * LANES]

    xh = xn.astype(BF16)
    xl = (xn - xh.astype(F32)).astype(BF16)
    hh = jnp.dot(xh, wr_ref[...], preferred_element_type=F32)
    logits = (hh[:, 0:LANES] + hh[:, LANES:2 * LANES]
              + jnp.dot(xl, wr_ref[:, 0:LANES], preferred_element_type=F32)
              + br_ref[...])
    lane = lax.broadcasted_iota(I32, logits.shape, 1)
    lane_f = lane.astype(F32)
    big = float(LANES)
    epg = n_exp // n_groups
    in_groups = (lane >= n_exp) & (lane < n_exp + n_groups)
    gl = jnp.where(in_groups, logits, NEG_INF)
    gmax = jnp.max(gl, axis=1, keepdims=True)
    gidx = jnp.min(jnp.where(gl == gmax, lane_f, big), axis=1, keepdims=True) - float(n_exp)
    g_p = 1.0 / jnp.sum(jnp.exp(gl - gmax), axis=1, keepdims=True)
    in_group = (lane < n_exp) & ((lane // epg).astype(F32) == gidx)
    el = jnp.where(in_group, logits, NEG_INF)
    m1 = jnp.max(el, axis=1, keepdims=True)
    i1 = jnp.min(jnp.where(el == m1, lane_f, big), axis=1, keepdims=True)
    el2 = jnp.where(lane_f == i1, NEG_INF, el)
    m2 = jnp.max(el2, axis=1, keepdims=True)
    i2 = jnp.min(jnp.where(el2 == m2, lane_f, big), axis=1, keepdims=True)
    esum = jnp.sum(jnp.exp(el - m1), axis=1, keepdims=True)
    p1 = 1.0 / esum
    p2 = jnp.exp(m2 - m1) / esum
    psum = p1 + p2
    w1 = p1 / psum * g_p
    w2 = p2 / psum * g_p
    r0, r1 = _rank_block(i1, i2, lane_f, cnt_ref)
    route_ref[...] = jnp.where(lane == 0, i1, jnp.where(lane == 1, i2, jnp.where(
        lane == 2, w1, jnp.where(lane == 3, w2, jnp.where(lane == 4, r0, jnp.where(lane == 5, r1, 0.0))))))


def _post(acts_p, acts_s, wa, wb, wo, g2, wr, br, *, n_exp, n_groups, tm=256):
    n_p, d = acts_p[0].shape
    n_s = acts_s[0].shape[0]
    tiles_p, tiles_s = n_p // tm, n_s // tm
    n = n_p + n_s
    rpt = d // LANES
    row_p = lambda i: (jnp.minimum(i, tiles_p - 1), 0)
    row_s = lambda i: (jnp.maximum(i - tiles_p, 0), 0)
    row = lambda i: (i, 0)
    const = lambda i: (0, 0)
    kern = functools.partial(_post_kernel, tiles_p=tiles_p, d=d, n_exp=n_exp, n_groups=n_groups)
    wspec = lambda: pl.BlockSpec((d, d), const, pipeline_mode=pl.Buffered(1))
    act_specs = lambda r: [pl.BlockSpec((tm, d), r), pl.BlockSpec((tm, d), r), pl.BlockSpec((tm, d), r),
                           pl.BlockSpec((tm, 2 * d), r)]
    return pl.pallas_call(
        kern, grid=(tiles_p + tiles_s,),
        in_specs=act_specs(row_p) + act_specs(row_s) + [
            wspec(), wspec(), wspec(),
            pl.BlockSpec((1, d), const), pl.BlockSpec((d, 2 * LANES), const), pl.BlockSpec((1, LANES), const)],
        out_specs=[pl.BlockSpec((tm, d), row), pl.BlockSpec((tm * rpt, LANES), row),
                   pl.BlockSpec((tm, LANES), row), pl.BlockSpec((1, LANES), const)],
        out_shape=[jax.ShapeDtypeStruct((n, d), F32), jax.ShapeDtypeStruct((n * rpt, LANES), F32),
                   jax.ShapeDtypeStruct((n, LANES), F32), jax.ShapeDtypeStruct((1, LANES), F32)],
        compiler_params=_params(("arbitrary",)),
        name="post",
    )(*acts_p, *acts_s, wa, wb, wo, g2, wr, br)


def _dispatch_kernel(ps_ref, pl_ref, nu_ref, dest_ref, xn_ref, xs_hbm, stage, zbuf, sem_rows, sem_pad,
                     *, tm, n_tiles, n_exp, nt):
    i = pl.program_id(0)
    slot = i % 2

    def pad_dmas(act):
        for e in range(n_exp):
            pos = ps_ref[e]
            left = pl_ref[e]
            for c in PAD_CHUNKS:
                @pl.when((left & c) != 0)
                def _(pos=pos, c=c):
                    act(pltpu.make_async_copy(zbuf.at[pl.ds(0, c)], xs_hbm.at[pl.ds(pos, c)], sem_pad))
                pos = pos + (left & c)

        def unused_tile(t, carry):
            for part in range(tm // PAD_CHUNKS[0]):
                act(pltpu.make_async_copy(
                    zbuf, xs_hbm.at[pl.ds(t * tm + part * PAD_CHUNKS[0], PAD_CHUNKS[0])], sem_pad))
            return carry
        lax.fori_loop(nu_ref[0], nt, unused_tile, 0)

    @pl.when(i == 0)
    def _():
        zbuf[...] = jnp.zeros_like(zbuf)
        pad_dmas(lambda cp: cp.start())

    def wait_tile(s):
        for _ in range(2):
            pltpu.make_async_copy(stage.at[s], xs_hbm.at[pl.ds(0, tm)], sem_rows.at[s]).wait()

    @pl.when(i >= 2)
    def _():
        wait_tile(slot)

    stage[slot] = xn_ref[...]

    def body(j, c):
        for u in range(DMA_UNROLL // 2):
            r = j * (DMA_UNROLL // 2) + u
            for kk in range(2):
                pltpu.make_async_copy(stage.at[slot, r], xs_hbm.at[dest_ref[0, kk, r]], sem_rows.at[slot]).start()
        return c
    lax.fori_loop(0, tm // (DMA_UNROLL // 2), body, 0)

    @pl.when(i == n_tiles - 1)
    def _():
        if n_tiles > 1:
            wait_tile(1 - slot)
        wait_tile(slot)
        pad_dmas(lambda cp: cp.wait())


def _dispatch(pad_start, pad_len, n_used, dest_t, xn_rows, *, nt, tm):
    n_tiles = dest_t.shape[0]
    n_exp = pad_start.shape[0]
    rpt = xn_rows.shape[0] // (n_tiles * tm)
    xn3 = xn_rows.reshape(n_tiles * tm, rpt, LANES)
    kern = functools.partial(_dispatch_kernel, tm=tm, n_tiles=n_tiles, n_exp=n_exp, nt=nt)
    return pl.pallas_call(
        kern,
        grid_spec=pltpu.PrefetchScalarGridSpec(
            num_scalar_prefetch=3, grid=(n_tiles,),
            in_specs=[pl.BlockSpec((1, 8, tm), lambda i, ps, pln, nu: (i, 0, 0), memory_space=pltpu.SMEM),
                      pl.BlockSpec((tm, rpt, LANES), lambda i, ps, pln, nu: (i, 0, 0))],
            out_specs=pl.BlockSpec(memory_space=pl.ANY),
            scratch_shapes=[pltpu.VMEM((2, tm, rpt, LANES), F32),
                            pltpu.VMEM((PAD_CHUNKS[0], rpt, LANES), F32),
                            pltpu.SemaphoreType.DMA((2,)), pltpu.SemaphoreType.DMA(())]),
        out_shape=jax.ShapeDtypeStruct((nt * tm, rpt, LANES), F32),
        compiler_params=_params(("arbitrary",)),
        name="dispatch",
    )(pad_start, pad_len, n_used, dest_t, xn3)


def _experts_kernel(te_ref, nu_ref, x_ref, wg_ref, wu_ref, wd_ref, y_ref, *, tm, rpt):
    t = pl.program_id(0)

    @pl.when(t < nu_ref[0])
    def _():
        x = jnp.concatenate([x_ref[pl.ds(j, tm, stride=rpt), :] for j in range(rpt)], axis=1).astype(BF16)
        g = jnp.dot(x, wg_ref[0].astype(BF16), preferred_element_type=F32)
        u = jnp.dot(x, wu_ref[0].astype(BF16), preferred_element_type=F32)
        hg = (g * _sigmoid(g)) * u
        y = jnp.dot(hg.astype(BF16), wd_ref[0].astype(BF16), preferred_element_type=F32)
        for j in range(rpt):
            y_ref[pl.ds(j, tm, stride=rpt), :] = y[:, j * LANES:(j + 1) * LANES]

    @pl.when(t >= nu_ref[0])
    def _():
        y_ref[...] = jnp.zeros_like(y_ref)


def _experts(tile_expert, n_used, x_sorted, w_gate, w_up, w_down, *, tm):
    nt = tile_expert.shape[0]
    n_exp, d, de = w_gate.shape
    rpt = d // LANES
    emap = lambda t, te, nu: (te[t], 0, 0)
    kern = functools.partial(_experts_kernel, tm=tm, rpt=rpt)
    return pl.pallas_call(
        kern,
        grid_spec=pltpu.PrefetchScalarGridSpec(
            num_scalar_prefetch=2, grid=(nt,),
            in_specs=[pl.BlockSpec((tm * rpt, LANES), lambda t, te, nu: (jnp.minimum(t, nu[0] - 1), 0)),
                      pl.BlockSpec((1, d, de), emap), pl.BlockSpec((1, d, de), emap),
                      pl.BlockSpec((1, de, d), emap)],
            out_specs=pl.BlockSpec((tm * rpt, LANES), lambda t, te, nu: (t, 0))),
        out_shape=jax.ShapeDtypeStruct((nt * tm * rpt, LANES), F32),
        compiler_params=_params(("arbitrary",)),
        name="experts",
    )(tile_expert, n_used, x_sorted, w_gate, w_up, w_down)


def _combine_kernel(idx_cur, idx_nxt, x2_ref, route_ref, ys_hbm, y_ref, buf, sem, *, tm, n_tiles, rpt):
    i = pl.program_id(0)
    slot = i % 2

    def gather(idx_ref, dst_slot):
        def body(j, c):
            for u in range(DMA_UNROLL // 2):
                r = j * (DMA_UNROLL // 2) + u
                for kk in range(2):
                    src = pl.multiple_of(idx_ref[0, kk, r] * rpt, rpt)
                    pltpu.make_async_copy(ys_hbm.at[pl.ds(src, rpt), :],
                                          buf.at[dst_slot, kk, pl.ds(r * rpt, rpt), :], sem.at[dst_slot]).start()
            return c
        lax.fori_loop(0, tm // (DMA_UNROLL // 2), body, 0)

    @pl.when(i == 0)
    def _():
        gather(idx_cur, 0)

    @pl.when(i + 1 < n_tiles)
    def _():
        gather(idx_nxt, 1 - slot)

    for kk in range(2):
        pltpu.make_async_copy(ys_hbm.at[pl.ds(0, tm * rpt), :], buf.at[slot, kk], sem.at[slot]).wait()
    route = route_ref[...]
    w0 = route[:, 2:3]
    w1 = route[:, 3:4]
    for j in range(rpt):
        sl = slice(j * LANES, (j + 1) * LANES)
        y_ref[:, sl] = (x2_ref[:, sl] + w0 * buf[slot, 0, pl.ds(j, tm, stride=rpt), :]
                        + w1 * buf[slot, 1, pl.ds(j, tm, stride=rpt), :])


def _combine(dest_t, x2, route, y_sorted, *, row0, rows, tm=256):
    d = x2.shape[1]
    rpt = d // LANES
    t0 = row0 // tm
    n_tiles = rows // tm
    kern = functools.partial(_combine_kernel, tm=tm, n_tiles=n_tiles, rpt=rpt)
    return pl.pallas_call(
        kern, grid=(n_tiles,),
        in_specs=[pl.BlockSpec((1, 8, tm), lambda i: (t0 + i, 0, 0), memory_space=pltpu.SMEM),
                  pl.BlockSpec((1, 8, tm), lambda i: (t0 + jnp.minimum(i + 1, n_tiles - 1), 0, 0),
                               memory_space=pltpu.SMEM),
                  pl.BlockSpec((tm, d), lambda i: (t0 + i, 0)),
                  pl.BlockSpec((tm, LANES), lambda i: (t0 + i, 0)),
                  pl.BlockSpec(memory_space=pl.ANY)],
        out_specs=pl.BlockSpec((tm, d), lambda i: (i, 0)),
        out_shape=jax.ShapeDtypeStruct((rows, d), F32),
        scratch_shapes=[pltpu.VMEM((2, 2, tm * rpt, LANES), F32), pltpu.SemaphoreType.DMA((2,))],
        compiler_params=_params(("arbitrary",)),
        name="combine",
    )(dest_t, dest_t, x2, route, y_sorted)


def _rank_block(i1, i2, lane_f, cnt_ref):
    tm = i1.shape[0]
    oh0 = lane_f == i1
    oh1 = lane_f == i2
    oh = jnp.where(oh0, 1.0, jnp.where(oh1, 1.0, 0.0))
    ri = lax.broadcasted_iota(I32, (tm, tm), 0)
    ci = lax.broadcasted_iota(I32, (tm, tm), 1)
    earlier = jnp.where(ci < ri, 1.0, 0.0).astype(BF16)
    base = cnt_ref[...] + jnp.dot(earlier, oh.astype(BF16), preferred_element_type=F32)
    r0 = jnp.sum(jnp.where(oh0, base, 0.0), axis=1, keepdims=True)
    r1 = jnp.sum(jnp.where(oh1, base, 0.0), axis=1, keepdims=True)
    cnt_ref[...] += jnp.sum(oh, axis=0, keepdims=True)
    return r0, r1


def _dest_kernel(route_ref, start_ref, dest_ref):
    route = route_ref[...]
    lane = lax.broadcasted_iota(I32, route.shape, 1)
    lane_f = lane.astype(F32)
    start = start_ref[...]
    d0 = route[:, 4:5] + jnp.sum(jnp.where(lane_f == route[:, 0:1], start, 0.0), axis=1, keepdims=True)
    d1 = route[:, 5:6] + jnp.sum(jnp.where(lane_f == route[:, 1:2], start, 0.0), axis=1, keepdims=True)
    dd = jnp.where(lane == 0, d0, jnp.where(lane == 1, d1, 0.0))
    dest_ref[0] = dd.T[0:8, :].astype(I32)


def _dest(route, start_row, *, tm):
    n = route.shape[0]
    return pl.pallas_call(
        _dest_kernel, grid=(n // tm,),
        in_specs=[pl.BlockSpec((tm, LANES), lambda i: (i, 0)), pl.BlockSpec((1, LANES), lambda i: (0, 0))],
        out_specs=pl.BlockSpec((1, 8, tm), lambda i: (i, 0, 0)),
        out_shape=jax.ShapeDtypeStruct((n // tm, 8, tm), I32),
        compiler_params=_params(("parallel",)),
        name="dest",
    )(route, start_row)


def _moe_plan(route, cnt_f, *, n_exp, tm):
    n = route.shape[0]
    nt = (2 * n + n_exp * (tm - 1)) // tm
    cnt = cnt_f[0, :n_exp].astype(I32)
    ptiles = (cnt + tm - 1) // tm
    tile_end = jnp.cumsum(ptiles)
    tile_start = tile_end - ptiles
    start_row = jnp.pad((tile_start * tm).astype(F32), (0, LANES - n_exp)).reshape(1, LANES)
    dest_t = _dest(route, start_row, tm=tm)
    n_used = tile_end[-1]
    tiles = jnp.arange(nt, dtype=I32)
    te = jnp.minimum(jnp.sum((tile_end[None, :] <= tiles[:, None]).astype(I32), axis=1), n_exp - 1)
    te_last = jnp.take(te, jnp.maximum(n_used - 1, 0))
    tile_expert = jnp.where(tiles < n_used, te, te_last)
    pad_start = tile_start * tm + cnt
    pad_len = ptiles * tm - cnt
    return tile_expert, n_used.reshape(1).astype(I32), dest_t, pad_start, pad_len, nt


def kernel(x_prompt, x_sample, cache_k, cache_v, cache_lf, state_C, state_n, state_m, page_table,
           norm1_g, w_in, b_igate, b_fgate_mlstm, b_fgate_fox, mlstm_norm_g, q_norm_g, k_norm_g,
           w_branch_mlstm, w_branch_fox, w_out, norm2_g, w_router_group, b_router_group,
           w_router_expert, b_router_expert, w_gate, w_up, w_down):
    depth = w_in.shape[0]
    assert depth == 1, "single-layer step"
    bp, tp, d = x_prompt.shape
    bs, ts, _ = x_sample.shape
    mh = b_igate.shape[-1]
    fh = b_fgate_fox.shape[-1]
    fd = q_norm_g.shape[-1]
    md = d // mh
    n_exp = w_gate.shape[1]
    n_pages = page_table.shape[1]
    page = cache_k.shape[2]
    assert 2 * mh + fh <= LANES and n_exp + N_GROUPS <= LANES
    np_tok, ns_tok = bp * tp, bs * ts
    l = 0

    w = w_in[l]
    o = 0
    secs = {}
    for name, width in (("m", 4 * d), ("mi", mh), ("mf", mh), ("f", 3 * d), ("ff", fh), ("g", 2 * d)):
        secs[name] = w[:, o:o + width]
        o += width
    pad = jnp.zeros((d, LANES - 2 * mh - fh), F32)
    w_all = jnp.concatenate([secs["m"], secs["f"], secs["g"], secs["mi"], secs["mf"], secs["ff"], pad],
                            axis=1).astype(BF16)
    bias_s = jnp.concatenate([b_igate[l], b_fgate_mlstm[l], b_fgate_fox[l],
                              jnp.zeros((LANES - 2 * mh - fh,), F32)]).reshape(1, LANES)
    g1 = norm1_g[l].reshape(1, d)
    qg = q_norm_g[l].reshape(1, fd)
    kg = k_norm_g[l].reshape(1, fd)
    g_m = mlstm_norm_g[l].reshape(1, d)
    wa = w_branch_mlstm[l].astype(BF16)
    wb = w_branch_fox[l].astype(BF16)
    wo = w_out[l].astype(BF16)
    g2 = norm2_g[l].reshape(1, d)
    wr32 = jnp.concatenate([w_router_expert[l], w_router_group[l],
                            jnp.zeros((d, LANES - n_exp - N_GROUPS), F32)], axis=1)
    wr_hi = wr32.astype(BF16)
    wr = jnp.concatenate([wr_hi, (wr32 - wr_hi.astype(F32)).astype(BF16)], axis=1)
    br = jnp.concatenate([b_router_expert[l], b_router_group[l],
                          jnp.zeros((LANES - n_exp - N_GROUPS,), F32)]).reshape(1, LANES)

    inproj = functools.partial(_inproj, g1=g1, w_all=w_all, bias_s=bias_s, qg=qg, kg=kg, mh=mh, fh=fh)
    pm_p, fq_p, k3_p, v3_p, kb_p, vb_p, gate_p, small_p, kt_p = inproj(
        x_prompt.reshape(np_tok, d), act_dtype=BF16, q_unit=LOG2E, emit_kt=True)
    pm_s, fq_s, k3_s, v3_s, kb_s, vb_s, gate_s, small_s = inproj(x_sample.reshape(ns_tok, d), act_dtype=F32)

    ha_p, c_p, n_p, m_p = _mlstm_prompt(pm_p, kt_p, small_p, g_m, batch=bp, seq=tp, mh=mh)
    m0 = jnp.pad(state_m[l], ((0, 0), (0, LANES - mh))).reshape(bs, 1, LANES)
    ha_s, c_s, n_s, m_s = _mlstm(pm_s, small_s, g_m, batch=bs, seq=ts, mh=mh, nb=4,
                                 init=(state_C[l], state_n[l], m0), out_dtype=F32)

    lf_p = small_p[:, 2 * mh:2 * mh + fh]
    lf_s = small_s[:, 2 * mh:2 * mh + fh]
    qx_p, kx_p = _fbias(small_p, batch=bp, seq=tp, fh=fh, fd=fd, lf0=2 * mh)
    ob_p = _fox_prompt(fq_p, qx_p, kb_p, kx_p, vb_p, batch=bp, seq=tp, fh=fh)

    n_phys = cache_k.shape[1]
    lft_c = cache_lf[l].transpose(0, 2, 1).reshape(n_phys * fh, page)
    floc = _cumsum_lanes(lft_c, carry=False, block_rows=n_phys * fh // 8).reshape(n_phys, fh, page)
    ck = cache_k[l].reshape(n_phys, page * fh, fd)
    cv = cache_v[l].reshape(n_phys, page * fh, fd)
    ob_s = _fox_sample(page_table, fq_s, kb_s, vb_s, small_s, floc, ck, cv, fh=fh, lf0=2 * mh)

    tm = MOE_TILE
    x2, xn_rows, route, cnt_f = _post((x_prompt.reshape(np_tok, d), ha_p, ob_p, gate_p),
                                      (x_sample.reshape(ns_tok, d), ha_s, ob_s, gate_s),
                                      wa, wb, wo, g2, wr, br, n_exp=n_exp, n_groups=N_GROUPS, tm=tm)

    tile_expert, n_used, dest_t, pad_start, pad_len, nt = _moe_plan(route, cnt_f, n_exp=n_exp, tm=tm)
    x_sorted = _dispatch(pad_start, pad_len, n_used, dest_t, xn_rows, nt=nt, tm=tm)
    y_sorted = _experts(tile_expert, n_used, x_sorted.reshape(nt * tm * (d // LANES), LANES),
                        w_gate[l], w_up[l], w_down[l], tm=tm)
    y_p = _combine(dest_t, x2, route, y_sorted, row0=0, rows=np_tok, tm=tm)
    y_s = _combine(dest_t, x2, route, y_sorted, row0=np_tok, rows=ns_tok, tm=tm)

    return (y_p.reshape(bp, tp, d), y_s.reshape(bs, ts, d),
            k3_p.reshape(1, bp, tp, fh, fd), v3_p.reshape(1, bp, tp, fh, fd), lf_p.reshape(1, bp, tp, fh),
            k3_s.reshape(1, bs, ts, fh, fd), v3_s.reshape(1, bs, ts, fh, fd), lf_s.reshape(1, bs, ts, fh),
            c_p[None], n_p[None], m_p[:, 0, :mh][None],
            c_s[None], n_s[None], m_s[:, 0, :mh][None])
```

```python
import functools

import jax
import jax.numpy as jnp
from jax import lax
from jax.experimental import pallas as pl
from jax.experimental.pallas import tpu as pltpu

F32 = jnp.float32
BF16 = jnp.bfloat16
I32 = jnp.int32
RMS_EPS = 1e-6
LANES = 128
MLSTM_CHUNK = 128
MLSTM_PROMPT_CHUNK = 256
N_GROUPS = 4
EXPERTS_PER_GROUP = 8
VMEM_LIMIT = 56 * 1024 * 1024
NEG_INF = float("-inf")
LOG2E = 1.4426950408889634
DMA_UNROLL = 8
MOE_TILE = 256
PAD_CHUNKS = (128, 64, 32, 16, 8, 4, 2, 1)


def _params(sem, vmem=VMEM_LIMIT):
    return pltpu.CompilerParams(dimension_semantics=sem, vmem_limit_bytes=vmem)


def _log_sigmoid(x):
    return -(jnp.maximum(-x, 0.0) + jnp.log1p(jnp.exp(-jnp.abs(x))))


def _sigmoid(x):
    return 1.0 / (1.0 + jnp.exp(-x))


def _nt_dot(a, b):
    return lax.dot_general(a, b, (((1,), (1,)), ((), ())), preferred_element_type=F32)


def _tn_dot(a, b):
    return lax.dot_general(a, b, (((0,), (0,)), ((), ())), preferred_element_type=F32)


def _col_to_row(col, eye):
    return jnp.sum(jnp.where(eye, col, 0.0), axis=0, keepdims=True)


def _row_to_col(row, eye):
    return jnp.sum(jnp.where(eye, row, 0.0), axis=1, keepdims=True)


def _inproj_kernel(x_ref, g1_ref, w_ref, bias_ref, qg_ref, kg_ref,
                   pm_ref, fq_ref, k3_ref, v3_ref, kb_ref, vb_ref, gate_ref, small_ref, *maybe_kt_ref,
                   d, mh, fh, fd, k_scale, q_scale):
    x = x_ref[...]
    ms = jnp.mean(x * x, axis=-1, keepdims=True)
    h = (x * lax.rsqrt(ms + RMS_EPS) * g1_ref[...]).astype(BF16)

    def proj(c0, width):
        return jnp.dot(h, w_ref[:, c0:c0 + width], preferred_element_type=F32)

    pm_ref[:, 0:d] = proj(0, d).astype(pm_ref.dtype)
    mk = proj(d, d) * k_scale
    pm_ref[:, d:2 * d] = mk.astype(pm_ref.dtype)
    if maybe_kt_ref:
        maybe_kt_ref[0][...] = mk.T.astype(maybe_kt_ref[0].dtype)
    pm_ref[:, 2 * d:3 * d] = proj(2 * d, d).astype(pm_ref.dtype)
    pm_ref[:, 3 * d:4 * d] = proj(3 * d, d).astype(pm_ref.dtype)

    def head_norm(a, g):
        ms_h = jnp.mean(a * a, axis=-1, keepdims=True)
        return a * lax.rsqrt(ms_h + RMS_EPS) * g

    fq = proj(4 * d, d)
    fk = proj(5 * d, d)
    fv = proj(6 * d, d)
    for hh in range(fh):
        sl = slice(hh * fd, (hh + 1) * fd)
        fq_ref[:, sl] = (head_norm(fq[:, sl], qg_ref[...]) * q_scale).astype(fq_ref.dtype)
        kn = head_norm(fk[:, sl], kg_ref[...])
        k3_ref[:, hh, :] = kn
        kb_ref[:, sl] = kn.astype(kb_ref.dtype)
        v3_ref[:, hh, :] = fv[:, sl]
    vb_ref[...] = fv.astype(vb_ref.dtype)

    gate_ref[...] = _sigmoid(proj(7 * d, 2 * d)).astype(gate_ref.dtype)

    sm = proj(9 * d, LANES) + bias_ref[...]
    lane = lax.broadcasted_iota(I32, sm.shape, 1)
    sm = jnp.where(lane < mh, sm, jnp.where(lane < 2 * mh + fh, _log_sigmoid(sm), 0.0))
    small_ref[...] = sm


def _inproj(x2d, g1, w_all, bias_s, qg, kg, *, mh, fh, act_dtype, q_unit=1.0, emit_kt=False, tm=256):
    n, d = x2d.shape
    fd = d // fh
    md = d // mh
    kern = functools.partial(_inproj_kernel, d=d, mh=mh, fh=fh, fd=fd,
                             k_scale=md ** -0.5, q_scale=fd ** -0.5 * q_unit)
    row = lambda i: (i, 0)
    const = lambda i: (0, 0)
    kt_specs = [pl.BlockSpec((d, tm), lambda i: (0, i))] if emit_kt else []
    kt_shapes = [jax.ShapeDtypeStruct((d, n), act_dtype)] if emit_kt else []
    return pl.pallas_call(
        kern, grid=(n // tm,),
        in_specs=[pl.BlockSpec((tm, d), row),
                  pl.BlockSpec((1, d), const),
                  pl.BlockSpec(w_all.shape, const, pipeline_mode=pl.Buffered(1)),
                  pl.BlockSpec((1, LANES), const),
                  pl.BlockSpec((1, fd), const),
                  pl.BlockSpec((1, fd), const)],
        out_specs=[pl.BlockSpec((tm, 4 * d), row),
                   pl.BlockSpec((tm, d), row),
                   pl.BlockSpec((tm, fh, fd), lambda i: (i, 0, 0)),
                   pl.BlockSpec((tm, fh, fd), lambda i: (i, 0, 0)),
                   pl.BlockSpec((tm, d), row),
                   pl.BlockSpec((tm, d), row),
                   pl.BlockSpec((tm, 2 * d), row),
                   pl.BlockSpec((tm, LANES), row)] + kt_specs,
        out_shape=[jax.ShapeDtypeStruct((n, 4 * d), act_dtype),
                   jax.ShapeDtypeStruct((n, d), act_dtype),
                   jax.ShapeDtypeStruct((n, fh, fd), F32),
                   jax.ShapeDtypeStruct((n, fh, fd), F32),
                   jax.ShapeDtypeStruct((n, d), act_dtype),
                   jax.ShapeDtypeStruct((n, d), act_dtype),
                   jax.ShapeDtypeStruct((n, 2 * d), BF16),
                   jax.ShapeDtypeStruct((n, LANES), F32)] + kt_shapes,
        compiler_params=_params(("parallel",)),
        name="inproj",
    )(x2d, g1, w_all, bias_s, qg, kg)


def _mlstm_kernel(*refs, L, mh, md, nb, has_init):
    if has_init:
        (q_ref, k_ref, v_ref, o_ref, s_ref, g_ref, c0_ref, n0_ref, m0_ref,
         h_ref, c_ref, n_ref, m_ref) = refs
    else:
        q_ref, k_ref, v_ref, o_ref, s_ref, g_ref, h_ref, c_ref, n_ref, m_ref = refs

    @pl.when(pl.program_id(1) == 0)
    def _():
        if has_init:
            c_ref[...] = c0_ref[...]
            n_ref[...] = n0_ref[...]
            m_ref[...] = m0_ref[...]
        else:
            c_ref[...] = jnp.zeros_like(c_ref)
            n_ref[...] = jnp.zeros_like(n_ref)
            m_ref[...] = jnp.zeros_like(m_ref)

    ri = lax.broadcasted_iota(I32, (L, L), 0)
    ci = lax.broadcasted_iota(I32, (L, L), 1)
    tri = ci <= ri
    eye = ci == ri
    for bb in range(nb):
        s = s_ref[bb]
        m_all = m_ref[bb]
        lane = lax.broadcasted_iota(I32, m_all.shape, 1)
        m_out = m_all
        for hh in range(mh):
            sl = slice(hh * md, (hh + 1) * md)
            ig_col = s[:, hh:hh + 1]
            lf_col = s[:, mh + hh:mh + hh + 1]
            lf_row = _col_to_row(lf_col, eye)
            ig_row = _col_to_row(ig_col, eye)
            b_col = jnp.sum(jnp.where(tri, lf_row, 0.0), axis=1, keepdims=True)
            b_row = _col_to_row(b_col, eye)
            m_prev = m_all[:, hh:hh + 1]
            log_w = jnp.where(tri, b_col - b_row + ig_row, NEG_INF)
            log_inter = b_col + m_prev
            m_t = jnp.maximum(log_inter, jnp.max(log_w, axis=1, keepdims=True))
            w_intra = jnp.exp(log_w - m_t)
            w_inter = jnp.exp(log_inter - m_t)
            q = q_ref[bb, :, sl].astype(BF16)
            k = k_ref[bb, :, sl].astype(BF16)
            v = v_ref[bb, :, sl].astype(BF16)
            sm = _nt_dot(q, k) * w_intra
            c_prev = c_ref[bb, hh]
            n_prev = n_ref[bb, hh:hh + 1, :]
            num = (w_inter * jnp.dot(q, c_prev.astype(BF16), preferred_element_type=F32)
                   + jnp.dot(sm.astype(BF16), v, preferred_element_type=F32))
            den = (w_inter * jnp.sum(q.astype(F32) * n_prev, axis=1, keepdims=True)
                   + jnp.sum(sm, axis=1, keepdims=True))
            hv = num / jnp.maximum(jnp.abs(den), jnp.exp(-m_t))
            m_new = m_t[L - 1:L, :]
            b_last = b_col[L - 1:L, :]
            decay = jnp.exp(b_last + m_prev - m_new)
            w_rows = jnp.exp(b_last - b_col + ig_col - m_new)
            kw = k.astype(F32) * w_rows
            c_ref[bb, hh] = decay * c_prev + _tn_dot(kw.astype(BF16), v)
            n_ref[bb, hh:hh + 1, :] = decay * n_prev + jnp.sum(kw, axis=0, keepdims=True)
            m_out = jnp.where(lane == hh, m_new, m_out)
            ms = jnp.mean(hv * hv, axis=-1, keepdims=True)
            hn = hv * lax.rsqrt(ms + RMS_EPS) * g_ref[:, sl]
            h_ref[bb, :, sl] = (_sigmoid(o_ref[bb, :, sl].astype(F32)) * hn).astype(h_ref.dtype)
        m_ref[bb] = m_out


def _mlstm(pm, small, g_m, *, batch, seq, mh, nb=1, init=None, out_dtype=BF16):
    d = pm.shape[1] // 4
    md = d // mh
    L = MLSTM_CHUNK if seq % MLSTM_CHUNK == 0 else seq
    nc = seq // L
    ng = batch // nb
    pm4 = pm.reshape(ng, nb, seq, 4 * d)
    small4 = small.reshape(ng, nb, seq, LANES)
    sec = lambda j: (lambda b, c: (b, 0, c, j))
    in_specs = [pl.BlockSpec((None, nb, L, d), sec(0)), pl.BlockSpec((None, nb, L, d), sec(1)),
                pl.BlockSpec((None, nb, L, d), sec(2)), pl.BlockSpec((None, nb, L, d), sec(3)),
                pl.BlockSpec((None, nb, L, LANES), sec(0)),
                pl.BlockSpec((1, d), lambda b, c: (0, 0))]
    args = [pm4, pm4, pm4, pm4, small4, g_m]
    state_specs = [pl.BlockSpec((nb, mh, md, md), lambda b, c: (b, 0, 0, 0)),
                   pl.BlockSpec((nb, mh, md), lambda b, c: (b, 0, 0)),
                   pl.BlockSpec((nb, 1, LANES), lambda b, c: (b, 0, 0))]
    if init is not None:
        in_specs += state_specs
        args += list(init)
    kern = functools.partial(_mlstm_kernel, L=L, mh=mh, md=md, nb=nb, has_init=init is not None)
    h4, c_out, n_out, m_out = pl.pallas_call(
        kern, grid=(ng, nc),
        in_specs=in_specs,
        out_specs=[pl.BlockSpec((None, nb, L, d), lambda b, c: (b, 0, c, 0))] + state_specs,
        out_shape=[jax.ShapeDtypeStruct((ng, nb, seq, d), out_dtype),
                   jax.ShapeDtypeStruct((batch, mh, md, md), F32),
                   jax.ShapeDtypeStruct((batch, mh, md), F32),
                   jax.ShapeDtypeStruct((batch, 1, LANES), F32)],
        compiler_params=_params(("parallel", "arbitrary")),
        name="mlstm",
    )(*args)
    return h4.reshape(batch * seq, d), c_out, n_out, m_out


def _mlstm_prompt_kernel(q_ref, kt_ref, v_ref, o_ref, s_ref, g_ref, h_ref, c_ref, n_ref, m_ref, naug,
                         *, L, mh, md, nc):
    c = pl.program_id(1)

    @pl.when(c == 0)
    def _():
        c_ref[...] = jnp.zeros_like(c_ref)
        m_ref[...] = jnp.zeros_like(m_ref)
        naug[...] = jnp.zeros_like(naug)

    s = s_ref[...]
    ri = lax.broadcasted_iota(I32, (L, L), 0)
    ci = lax.broadcasted_iota(I32, (L, L), 1)
    tri = ci <= ri
    hp = lax.Precision.HIGHEST
    r = s.T[0:8, :]
    b_cols = jnp.dot(jnp.where(tri, 1.0, 0.0), s, precision=hp, preferred_element_type=F32)
    b_rows = pltpu.roll(jnp.dot(r, jnp.where(ri <= ci, 1.0, 0.0), precision=hp, preferred_element_type=F32),
                        8 - mh, axis=0)
    g = r - b_rows
    m_all = m_ref[0]
    lane = lax.broadcasted_iota(I32, m_all.shape, 1)
    m_out = m_all
    ones_v = jnp.ones((L, LANES), BF16)
    ones_r = jnp.ones((md, LANES), BF16)
    for hh in range(mh):
        sl = slice(hh * md, (hh + 1) * md)
        m_prev = m_all[:, hh:hh + 1]
        g_row = g[hh:hh + 1, :]
        b_col = b_cols[:, mh + hh:mh + hh + 1]
        a_col = jnp.maximum(m_prev, jnp.max(jnp.where(tri, g_row, NEG_INF), axis=1, keepdims=True))
        m_t = b_col + a_col
        m_new = m_t[L - 1:L, :]
        b_last = b_col[L - 1:L, :]
        w_rows = jnp.exp(b_last - b_rows[hh:hh + 1, :] + r[hh:hh + 1, :] - m_new)
        dc = jnp.exp(b_last + m_prev - m_new)
        m_out = jnp.where(lane == hh, m_new, m_out)
        q = q_ref[:, sl]
        kt = kt_ref[sl, :]
        v_aug = jnp.concatenate([v_ref[:, sl], ones_v], axis=1)
        w_intra = jnp.exp(jnp.where(tri, g_row - a_col, NEG_INF))
        sm = jnp.dot(q, kt, preferred_element_type=F32) * w_intra
        c_prev = c_ref[0, hh]
        n_prev = naug[hh]
        c_aug = jnp.concatenate([c_prev, n_prev], axis=1).astype(BF16)
        comb = (jnp.exp(m_prev - a_col) * jnp.dot(q, c_aug, preferred_element_type=F32)
                + jnp.dot(sm.astype(BF16), v_aug, preferred_element_type=F32))
        den = jnp.maximum(jnp.abs(comb[:, md:]), jnp.exp(-m_t))
        inv = 1.0 / den
        hv = comb[:, 0:md] * jnp.concatenate([inv] * (md // LANES), axis=1)
        ms = jnp.dot((hv * hv).astype(BF16), ones_r, preferred_element_type=F32) * (1.0 / md)
        rs = lax.rsqrt(ms + RMS_EPS)
        hn = hv * jnp.concatenate([rs] * (md // LANES), axis=1) * g_ref[:, sl]
        h_ref[:, sl] = (_sigmoid(o_ref[:, sl].astype(F32)) * hn).astype(h_ref.dtype)
        kw = (kt.astype(F32) * w_rows).astype(BF16)
        upd = jnp.dot(kw, v_aug, preferred_element_type=F32)
        c_ref[0, hh] = dc * c_prev + upd[:, 0:md]
        naug[hh] = dc * n_prev + upd[:, md:]
    m_ref[0] = m_out

    @pl.when(c == nc - 1)
    def _():
        ei = lax.broadcasted_iota(I32, (md, md), 0) == lax.broadcasted_iota(I32, (md, md), 1)
        for hh in range(mh):
            cols = jnp.concatenate([naug[hh]] * (md // LANES), axis=1)
            n_ref[0, hh:hh + 1, :] = jnp.sum(jnp.where(ei, cols, 0.0), axis=0, keepdims=True)


def _mlstm_prompt(pm, kt, small, g_m, *, batch, seq, mh):
    d = pm.shape[1] // 4
    md = d // mh
    L = MLSTM_PROMPT_CHUNK
    assert seq % L == 0
    nc = seq // L
    tok = lambda j: (lambda b, c: (b * nc + c, j))
    kern = functools.partial(_mlstm_prompt_kernel, L=L, mh=mh, md=md, nc=nc)
    return pl.pallas_call(
        kern, grid=(batch, nc),
        in_specs=[pl.BlockSpec((L, d), tok(0)),
                  pl.BlockSpec((d, L), lambda b, c: (0, b * nc + c)),
                  pl.BlockSpec((L, d), tok(2)), pl.BlockSpec((L, d), tok(3)),
                  pl.BlockSpec((L, LANES), tok(0)),
                  pl.BlockSpec((1, d), lambda b, c: (0, 0))],
        out_specs=[pl.BlockSpec((L, d), tok(0)),
                   pl.BlockSpec((1, mh, md, md), lambda b, c: (b, 0, 0, 0)),
                   pl.BlockSpec((1, mh, md), lambda b, c: (b, 0, 0)),
                   pl.BlockSpec((1, 1, LANES), lambda b, c: (b, 0, 0))],
        out_shape=[jax.ShapeDtypeStruct((batch * seq, d), BF16),
                   jax.ShapeDtypeStruct((batch, mh, md, md), F32),
                   jax.ShapeDtypeStruct((batch, mh, md), F32),
                   jax.ShapeDtypeStruct((batch, 1, LANES), F32)],
        scratch_shapes=[pltpu.VMEM((mh, md, LANES), F32)],
        compiler_params=_params(("parallel", "arbitrary")),
        name="mlstm_prompt",
    )(pm, kt, pm, pm, small, g_m)


def _cumsum_kernel(x_ref, o_ref, *, chunk, carry):
    rows, t = x_ref.shape
    si = lax.broadcasted_iota(I32, (chunk, chunk), 0)
    ti = lax.broadcasted_iota(I32, (chunk, chunk), 1)
    tri = (si <= ti).astype(F32)
    run = jnp.zeros((rows, 1), F32)
    for j in range(t // chunk):
        sl = slice(j * chunk, (j + 1) * chunk)
        loc = jnp.dot(x_ref[:, sl], tri, precision=lax.Precision.HIGHEST, preferred_element_type=F32)
        if carry:
            loc = loc + run
            run = loc[:, chunk - 1:chunk]
        o_ref[:, sl] = loc


def _cumsum_lanes(x, *, carry, block_rows):
    rows, t = x.shape
    kern = functools.partial(_cumsum_kernel, chunk=LANES, carry=carry)
    return pl.pallas_call(
        kern, grid=(rows // block_rows,),
        in_specs=[pl.BlockSpec((block_rows, t), lambda i: (i, 0))],
        out_specs=pl.BlockSpec((block_rows, t), lambda i: (i, 0)),
        out_shape=jax.ShapeDtypeStruct((rows, t), F32),
        compiler_params=_params(("parallel",)),
        name="cumsum",
    )(x)


def _fbias_kernel(s_ref, qx_ref, kx_ref, run_ref, *, fh, fd, lf0):
    @pl.when(pl.program_id(1) == 0)
    def _():
        run_ref[...] = jnp.zeros_like(run_ref)

    s = s_ref[...]
    L = s.shape[0]
    ti = lax.broadcasted_iota(I32, (L, L), 0)
    si = lax.broadcasted_iota(I32, (L, L), 1)
    tri = jnp.where(si <= ti, 1.0, 0.0)
    f_all = jnp.dot(tri, s, precision=lax.Precision.HIGHEST, preferred_element_type=F32) + run_ref[...]
    run_ref[...] = f_all[L - 1:L, :]
    lane = lax.broadcasted_iota(I32, (L, fd), 1)
    for h in range(fh):
        fb = jnp.broadcast_to(f_all[:, lf0 + h:lf0 + h + 1] * LOG2E, (L, fd))
        hi = fb.astype(BF16).astype(F32)
        r1 = fb - hi
        mid = r1.astype(BF16).astype(F32)
        lo = (r1 - mid).astype(BF16).astype(F32)
        sl = slice(h * fd, (h + 1) * fd)
        qx_ref[:, sl] = jnp.where(lane == 0, hi, jnp.where(lane == 1, mid, jnp.where(
            lane == 2, lo, jnp.where(lane < 6, 1.0, 0.0)))).astype(BF16)
        kx_ref[:, sl] = jnp.where(lane < 3, 1.0, jnp.where(lane == 3, -hi, jnp.where(
            lane == 4, -mid, jnp.where(lane == 5, -lo, 0.0)))).astype(BF16)


def _fbias(small, *, batch, seq, fh, fd, lf0):
    n = small.shape[0]
    L = 4 * LANES if seq % (4 * LANES) == 0 else LANES
    nc = seq // L
    kern = functools.partial(_fbias_kernel, fh=fh, fd=fd, lf0=lf0)
    return pl.pallas_call(
        kern, grid=(batch, nc),
        in_specs=[pl.BlockSpec((L, LANES), lambda b, c: (b * nc + c, 0))],
        out_specs=[pl.BlockSpec((L, fh * fd), lambda b, c: (b * nc + c, 0)),
                   pl.BlockSpec((L, fh * fd), lambda b, c: (b * nc + c, 0))],
        out_shape=[jax.ShapeDtypeStruct((n, fh * fd), BF16), jax.ShapeDtypeStruct((n, fh * fd), BF16)],
        scratch_shapes=[pltpu.VMEM((1, LANES), F32)],
        compiler_params=_params(("parallel", "arbitrary")),
        name="fbias",
    )(small)


def _fox_prompt_kernel(q_ref, qx_ref, k_ref, kx_ref, v_ref, o_ref, *, seq, tq, gh, fd):
    nq = seq // tq
    ri = lax.broadcasted_iota(I32, (tq, tq), 0)
    ci = lax.broadcasted_iota(I32, (tq, tq), 1)
    causal = ci <= ri

    def softmax_step(carry, s, v):
        m, l, acc = carry
        m_new = jnp.maximum(m, jnp.max(s, axis=1, keepdims=True))
        a = jnp.exp2(m - m_new)
        p = jnp.exp2(s - m_new)
        l = a * l + jnp.sum(p, axis=1, keepdims=True)
        acc = a * acc + jnp.dot(p.astype(BF16), v, preferred_element_type=F32)
        return m_new, l, acc

    def q_body(qi, _):
        q0 = pl.multiple_of(qi * tq, tq)

        def block(g, carry_g, k0, masked):
            gs = slice(g * fd, (g + 1) * fd)
            qa = jnp.concatenate([q_ref[pl.ds(q0, tq), gs], qx_ref[pl.ds(q0, tq), gs]], axis=1)
            ka = jnp.concatenate([k_ref[pl.ds(k0, tq), gs], kx_ref[pl.ds(k0, tq), gs]], axis=1)
            s = _nt_dot(qa, ka)
            if masked:
                s = jnp.where(causal, s, NEG_INF)
            return softmax_step(carry_g, s, v_ref[pl.ds(k0, tq), gs])

        def kv_body(kj, carry):
            k0 = pl.multiple_of(kj * tq, tq)
            return tuple(block(g, carry[g], k0, False) for g in range(gh))

        init = tuple((jnp.full((tq, 1), NEG_INF, F32), jnp.zeros((tq, 1), F32), jnp.zeros((tq, fd), F32))
                     for _ in range(gh))
        carry = lax.fori_loop(0, qi, kv_body, init)
        for g in range(gh):
            _, l, acc = block(g, carry[g], q0, True)
            o_ref[pl.ds(q0, tq), g * fd:(g + 1) * fd] = (acc / l).astype(o_ref.dtype)
        return 0

    lax.fori_loop(0, nq, q_body, 0)


def _fox_prompt(fq, qx, kb, kx, vb, *, batch, seq, fh, tq=512, gh=2):
    n, d = fq.shape
    fd = d // fh
    blk = lambda b, h: (b, h)
    kern = functools.partial(_fox_prompt_kernel, seq=seq, tq=tq, gh=gh, fd=fd)
    return pl.pallas_call(
        kern, grid=(batch, fh // gh),
        in_specs=[pl.BlockSpec((seq, gh * fd), blk)] * 5,
        out_specs=pl.BlockSpec((seq, gh * fd), blk),
        out_shape=jax.ShapeDtypeStruct((n, d), BF16),
        compiler_params=_params(("parallel", "parallel")),
        name="fox_prompt",
    )(fq, qx, kb, kx, vb)


def _fox_sample_kernel(pt_ref, q_ref, kn_ref, vn_ref, s_ref, *rest, n_pages, page, fh, fd, ts, lf0):
    floc = rest[0:n_pages]
    kpages = rest[n_pages:2 * n_pages]
    vpages = rest[2 * n_pages:3 * n_pages]
    o_ref = rest[3 * n_pages]
    p_scr, pn_scr, l_scr = rest[3 * n_pages + 1:]
    d = fh * fd
    rows = fh * ts

    def scores():
        q = q_ref[...].astype(F32)
        qt = jnp.concatenate([q] * fh, axis=0)
        r_head = lax.broadcasted_iota(I32, (rows, d), 0) // ts
        c_head = lax.broadcasted_iota(I32, (rows, d), 1) // fd
        qbd = jnp.where(r_head == c_head, qt, 0.0).astype(BF16)

        off = jnp.zeros((fh, 1), F32)
        fk_pages = []
        for i in range(n_pages):
            fp = floc[i][...] + off
            fk_pages.append(fp)
            off = fp[:, page - 1:page]
        fk = jnp.concatenate(fk_pages, axis=1)
        fk_rows = jnp.concatenate(
            [jnp.broadcast_to(fk[h:h + 1, :], (ts, fk.shape[1])) for h in range(fh)], axis=0)

        sm = s_ref[...]
        ri = lax.broadcasted_iota(I32, (ts, ts), 0)
        ci = lax.broadcasted_iota(I32, (ts, ts), 1)
        eye = ci == ri
        tri = ci <= ri
        fq_cols, bias_new = [], []
        for h in range(fh):
            lf_col = sm[:, lf0 + h:lf0 + h + 1]
            lf_row = _col_to_row(lf_col, eye)
            cum_col = jnp.sum(jnp.where(tri, lf_row, 0.0), axis=1, keepdims=True)
            fq_h = off[h:h + 1, :] + cum_col
            fq_cols.append(fq_h)
            bias_new.append(fq_h - _col_to_row(fq_h, eye))
        fq_col = jnp.concatenate(fq_cols, axis=0)
        bias_n = jnp.concatenate(bias_new, axis=0)
        causal_n = jnp.concatenate([tri] * fh, axis=0)

        s_parts = []
        for i in range(0, n_pages, 2):
            kp = jnp.concatenate(
                [jnp.concatenate([kpages[i + j][pl.ds(h, page, stride=fh), :] for h in range(fh)], axis=1)
                 for j in range(2)],
                axis=0).astype(BF16)
            s_parts.append(_nt_dot(qbd, kp))
        s_past = jnp.concatenate(s_parts, axis=1) + (fq_col - fk_rows)
        s_new = _nt_dot(qbd, kn_ref[...].astype(BF16)) + bias_n
        s_new = jnp.where(causal_n, s_new, NEG_INF)
        m = jnp.maximum(jnp.max(s_past, axis=1, keepdims=True), jnp.max(s_new, axis=1, keepdims=True))
        p_past = jnp.exp(s_past - m)
        p_new = jnp.exp(s_new - m)
        l_scr[...] = jnp.sum(p_past, axis=1, keepdims=True) + jnp.sum(p_new, axis=1, keepdims=True)
        p_scr[...] = p_past.astype(BF16)
        pn_scr[...] = p_new

    def values():
        acc = jnp.dot(pn_scr[...].astype(BF16), vn_ref[...].astype(BF16), preferred_element_type=F32)
        for i in range(0, n_pages, 2):
            vp = jnp.concatenate(
                [jnp.concatenate([vpages[i + j][pl.ds(h, page, stride=fh), :] for h in range(fh)], axis=1)
                 for j in range(2)],
                axis=0).astype(BF16)
            acc = acc + jnp.dot(p_scr[:, i * page:(i + 2) * page], vp, preferred_element_type=F32)
        acc = acc / l_scr[...]
        o_ref[...] = jnp.concatenate(
            [acc[h * ts:(h + 1) * ts, h * fd:(h + 1) * fd] for h in range(fh)], axis=1)

    scores()
    values()


def _fox_sample(page_table, fq, kb, vb, small, floc, cache_k, cache_v, *, fh, lf0):
    bs, n_pages = page_table.shape
    n, d = fq.shape
    ts = n // bs
    fd = d // fh
    page = cache_k.shape[1] // fh
    rows = fh * ts
    tok = lambda b, pt: (b, 0)

    def pmap(i):
        return lambda b, pt: (pt[b, i], 0, 0)

    in_specs = ([pl.BlockSpec((ts, d), tok), pl.BlockSpec((ts, d), tok), pl.BlockSpec((ts, d), tok),
                 pl.BlockSpec((ts, LANES), tok)]
                + [pl.BlockSpec((None, fh, page), pmap(i)) for i in range(n_pages)]
                + [pl.BlockSpec((None, page * fh, fd), pmap(i)) for i in range(n_pages)]
                + [pl.BlockSpec((None, page * fh, fd), pmap(i)) for i in range(n_pages)])
    kern = functools.partial(_fox_sample_kernel, n_pages=n_pages, page=page, fh=fh, fd=fd, ts=ts, lf0=lf0)
    return pl.pallas_call(
        kern,
        grid_spec=pltpu.PrefetchScalarGridSpec(
            num_scalar_prefetch=1, grid=(bs,), in_specs=in_specs,
            out_specs=pl.BlockSpec((ts, d), tok),
            scratch_shapes=[pltpu.VMEM((rows, n_pages * page), BF16),
                            pltpu.VMEM((rows, ts), F32),
                            pltpu.VMEM((rows, 1), F32)]),
        out_shape=jax.ShapeDtypeStruct((n, d), F32),
        compiler_params=_params(("arbitrary",)),
        name="fox_sample",
    )(page_table, fq, kb, vb, small, *([floc] * n_pages), *([cache_k] * n_pages), *([cache_v] * n_pages))


def _post_kernel(xp_ref, hap_ref, obp_ref, gp_ref, xs_ref, has_ref, obs_ref, gs_ref, *rest, tiles_p, **kw):
    cnt_ref = rest[-1]

    @pl.when(pl.program_id(0) == 0)
    def _():
        cnt_ref[...] = jnp.zeros_like(cnt_ref)

    @pl.when(pl.program_id(0) < tiles_p)
    def _():
        _post_body(xp_ref, hap_ref, obp_ref, gp_ref, *rest, **kw)

    @pl.when(pl.program_id(0) >= tiles_p)
    def _():
        _post_body(xs_ref, has_ref, obs_ref, gs_ref, *rest, **kw)


def _post_body(x_ref, ha_ref, ob_ref, gate_ref, wa_ref, wb_ref, wo_ref, g2_ref, wr_ref, br_ref,
               x2_ref, xn_ref, route_ref, cnt_ref, *, d, n_exp, n_groups):
    ba = jnp.dot(ha_ref[...].astype(BF16), wa_ref[...], preferred_element_type=F32)
    bb = jnp.dot(ob_ref[...].astype(BF16), wb_ref[...], preferred_element_type=F32)
    gates = gate_ref[...].astype(F32)
    merged = gates[:, 0:d] * ba + gates[:, d:2 * d] * bb
    x2 = x_ref[...] + jnp.dot(merged.astype(BF16), wo_ref[...], preferred_element_type=F32)
    x2_ref[...] = x2
    ms = jnp.mean(x2 * x2, axis=-1, keepdims=True)
    xn = x2 * lax.rsqrt(ms + RMS_EPS) * g2_ref[...]
    tm = xn.shape[0]
    for j in range(d // LANES):
        xn_ref[pl.ds(j, tm, stride=d // LANES), :] = xn[:, j * LANES:(j + 1) * LANES]

    xh = xn.astype(BF16)
    xl = (xn - xh.astype(F32)).astype(BF16)
    hh = jnp.dot(xh, wr_ref[...], preferred_element_type=F32)
    logits = (hh[:, 0:LANES] + hh[:, LANES:2 * LANES]
              + jnp.dot(xl, wr_ref[:, 0:LANES], preferred_element_type=F32)
              + br_ref[...])
    lane = lax.broadcasted_iota(I32, logits.shape, 1)
    lane_f = lane.astype(F32)
    big = float(LANES)
    epg = n_exp // n_groups
    in_groups = (lane >= n_exp) & (lane < n_exp + n_groups)
    gl = jnp.where(in_groups, logits, NEG_INF)
    gmax = jnp.max(gl, axis=1, keepdims=True)
    gidx = jnp.min(jnp.where(gl == gmax, lane_f, big), axis=1, keepdims=True) - float(n_exp)
    g_p = 1.0 / jnp.sum(jnp.exp(gl - gmax), axis=1, keepdims=True)
    in_group = (lane < n_exp) & ((lane // epg).astype(F32) == gidx)
    el = jnp.where(in_group, logits, NEG_INF)
    m1 = jnp.max(el, axis=1, keepdims=True)
    i1 = jnp.min(jnp.where(el == m1, lane_f, big), axis=1, keepdims=True)
    el2 = jnp.where(lane_f == i1, NEG_INF, el)
    m2 = jnp.max(el2, axis=1, keepdims=True)
    i2 = jnp.min(jnp.where(el2 == m2, lane_f, big), axis=1, keepdims=True)
    esum = jnp.sum(jnp.exp(el - m1), axis=1, keepdims=True)
    p1 = 1.0 / esum
    p2 = jnp.exp(m2 - m1) / esum
    psum = p1 + p2
    w1 = p1 / psum * g_p
    w2 = p2 / psum * g_p
    r0, r1 = _rank_block(i1, i2, lane_f, cnt_ref)
    route_ref[...] = jnp.where(lane == 0, i1, jnp.where(lane == 1, i2, jnp.where(
        lane == 2, w1, jnp.where(lane == 3, w2, jnp.where(lane == 4, r0, jnp.where(lane == 5, r1, 0.0))))))


def _post(acts_p, acts_s, wa, wb, wo, g2, wr, br, *, n_exp, n_groups, tm=256):
    n_p, d = acts_p[0].shape
    n_s = acts_s[0].shape[0]
    tiles_p, tiles_s = n_p // tm, n_s // tm
    n = n_p + n_s
    rpt = d // LANES
    row_p = lambda i: (jnp.minimum(i, tiles_p - 1), 0)
    row_s = lambda i: (jnp.maximum(i - tiles_p, 0), 0)
    row = lambda i: (i, 0)
    const = lambda i: (0, 0)
    kern = functools.partial(_post_kernel, tiles_p=tiles_p, d=d, n_exp=n_exp, n_groups=n_groups)
    wspec = lambda: pl.BlockSpec((d, d), const, pipeline_mode=pl.Buffered(1))
    act_specs = lambda r: [pl.BlockSpec((tm, d), r), pl.BlockSpec((tm, d), r), pl.BlockSpec((tm, d), r),
                           pl.BlockSpec((tm, 2 * d), r)]
    return pl.pallas_call(
        kern, grid=(tiles_p + tiles_s,),
        in_specs=act_specs(row_p) + act_specs(row_s) + [
            wspec(), wspec(), wspec(),
            pl.BlockSpec((1, d), const), pl.BlockSpec((d, 2 * LANES), const), pl.BlockSpec((1, LANES), const)],
        out_specs=[pl.BlockSpec((tm, d), row), pl.BlockSpec((tm * rpt, LANES), row),
                   pl.BlockSpec((tm, LANES), row), pl.BlockSpec((1, LANES), const)],
        out_shape=[jax.ShapeDtypeStruct((n, d), F32), jax.ShapeDtypeStruct((n * rpt, LANES), F32),
                   jax.ShapeDtypeStruct((n, LANES), F32), jax.ShapeDtypeStruct((1, LANES), F32)],
        compiler_params=_params(("arbitrary",)),
        name="post",
    )(*acts_p, *acts_s, wa, wb, wo, g2, wr, br)


def _dispatch_kernel(ps_ref, pl_ref, nu_ref, dest_ref, xn_ref, xs_hbm, stage, zbuf, sem_rows, sem_pad,
                     *, tm, n_tiles, n_exp, nt):
    i = pl.program_id(0)
    slot = i % 2

    def pad_dmas(act):
        for e in range(n_exp):
            pos = ps_ref[e]
            left = pl_ref[e]
            for c in PAD_CHUNKS:
                @pl.when((left & c) != 0)
                def _(pos=pos, c=c):
                    act(pltpu.make_async_copy(zbuf.at[pl.ds(0, c)], xs_hbm.at[pl.ds(pos, c)], sem_pad))
                pos = pos + (left & c)

        def unused_tile(t, carry):
            for part in range(tm // PAD_CHUNKS[0]):
                act(pltpu.make_async_copy(
                    zbuf, xs_hbm.at[pl.ds(t * tm + part * PAD_CHUNKS[0], PAD_CHUNKS[0])], sem_pad))
            return carry
        lax.fori_loop(nu_ref[0], nt, unused_tile, 0)

    @pl.when(i == 0)
    def _():
        zbuf[...] = jnp.zeros_like(zbuf)
        pad_dmas(lambda cp: cp.start())

    def wait_tile(s):
        for _ in range(2):
            pltpu.make_async_copy(stage.at[s], xs_hbm.at[pl.ds(0, tm)], sem_rows.at[s]).wait()

    @pl.when(i >= 2)
    def _():
        wait_tile(slot)

    stage[slot] = xn_ref[...]

    def body(j, c):
        for u in range(DMA_UNROLL // 2):
            r = j * (DMA_UNROLL // 2) + u
            for kk in range(2):
                pltpu.make_async_copy(stage.at[slot, r], xs_hbm.at[dest_ref[0, kk, r]],
                                      sem_rows.at[slot]).start(priority=kk)
        return c
    lax.fori_loop(0, tm // (DMA_UNROLL // 2), body, 0)

    @pl.when(i == n_tiles - 1)
    def _():
        if n_tiles > 1:
            wait_tile(1 - slot)
        wait_tile(slot)
        pad_dmas(lambda cp: cp.wait())


def _dispatch(pad_start, pad_len, n_used, dest_t, xn_rows, *, nt, tm):
    n_tiles = dest_t.shape[0]
    n_exp = pad_start.shape[0]
    rpt = xn_rows.shape[0] // (n_tiles * tm)
    xn3 = xn_rows.reshape(n_tiles * tm, rpt, LANES)
    kern = functools.partial(_dispatch_kernel, tm=tm, n_tiles=n_tiles, n_exp=n_exp, nt=nt)
    return pl.pallas_call(
        kern,
        grid_spec=pltpu.PrefetchScalarGridSpec(
            num_scalar_prefetch=3, grid=(n_tiles,),
            in_specs=[pl.BlockSpec((1, 8, tm), lambda i, ps, pln, nu: (i, 0, 0), memory_space=pltpu.SMEM),
                      pl.BlockSpec((tm, rpt, LANES), lambda i, ps, pln, nu: (i, 0, 0))],
            out_specs=pl.BlockSpec(memory_space=pl.ANY),
            scratch_shapes=[pltpu.VMEM((2, tm, rpt, LANES), F32),
                            pltpu.VMEM((PAD_CHUNKS[0], rpt, LANES), F32),
                            pltpu.SemaphoreType.DMA((2,)), pltpu.SemaphoreType.DMA(())]),
        out_shape=jax.ShapeDtypeStruct((nt * tm, rpt, LANES), F32),
        compiler_params=_params(("arbitrary",)),
        name="dispatch",
    )(pad_start, pad_len, n_used, dest_t, xn3)


def _experts_kernel(te_ref, nu_ref, x_ref, wg_ref, wu_ref, wd_ref, y_ref, *, tm, rpt):
    t = pl.program_id(0)

    @pl.when(t < nu_ref[0])
    def _():
        x = jnp.concatenate([x_ref[pl.ds(j, tm, stride=rpt), :] for j in range(rpt)], axis=1).astype(BF16)
        g = jnp.dot(x, wg_ref[0].astype(BF16), preferred_element_type=F32)
        u = jnp.dot(x, wu_ref[0].astype(BF16), preferred_element_type=F32)
        hg = (g * _sigmoid(g)) * u
        y = jnp.dot(hg.astype(BF16), wd_ref[0].astype(BF16), preferred_element_type=F32)
        for j in range(rpt):
            y_ref[pl.ds(j, tm, stride=rpt), :] = y[:, j * LANES:(j + 1) * LANES]

    @pl.when(t >= nu_ref[0])
    def _():
        y_ref[...] = jnp.zeros_like(y_ref)


def _experts(tile_expert, n_used, x_sorted, w_gate, w_up, w_down, *, tm):
    nt = tile_expert.shape[0]
    n_exp, d, de = w_gate.shape
    rpt = d // LANES
    emap = lambda t, te, nu: (te[t], 0, 0)
    kern = functools.partial(_experts_kernel, tm=tm, rpt=rpt)
    return pl.pallas_call(
        kern,
        grid_spec=pltpu.PrefetchScalarGridSpec(
            num_scalar_prefetch=2, grid=(nt,),
            in_specs=[pl.BlockSpec((tm * rpt, LANES), lambda t, te, nu: (jnp.minimum(t, nu[0] - 1), 0)),
                      pl.BlockSpec((1, d, de), emap), pl.BlockSpec((1, d, de), emap),
                      pl.BlockSpec((1, de, d), emap)],
            out_specs=pl.BlockSpec((tm * rpt, LANES), lambda t, te, nu: (t, 0))),
        out_shape=jax.ShapeDtypeStruct((nt * tm * rpt, LANES), F32),
        compiler_params=_params(("arbitrary",)),
        name="experts",
    )(tile_expert, n_used, x_sorted, w_gate, w_up, w_down)


def _combine_kernel(idx_cur, idx_nxt, x2_ref, route_ref, ys_hbm, y_ref, buf, sem, *, tm, n_tiles, rpt):
    i = pl.program_id(0)
    slot = i % 2

    def gather(idx_ref, dst_slot):
        def body(j, c):
            for u in range(DMA_UNROLL // 2):
                r = j * (DMA_UNROLL // 2) + u
                for kk in range(2):
                    src = pl.multiple_of(idx_ref[0, kk, r] * rpt, rpt)
                    pltpu.make_async_copy(ys_hbm.at[pl.ds(src, rpt), :],
                                          buf.at[dst_slot, kk, pl.ds(r * rpt, rpt), :],
                                          sem.at[dst_slot]).start(priority=kk)
            return c
        lax.fori_loop(0, tm // (DMA_UNROLL // 2), body, 0)

    @pl.when(i == 0)
    def _():
        gather(idx_cur, 0)

    @pl.when(i + 1 < n_tiles)
    def _():
        gather(idx_nxt, 1 - slot)

    for kk in range(2):
        pltpu.make_async_copy(ys_hbm.at[pl.ds(0, tm * rpt), :], buf.at[slot, kk], sem.at[slot]).wait()
    route = route_ref[...]
    w0 = route[:, 2:3]
    w1 = route[:, 3:4]
    for j in range(rpt):
        sl = slice(j * LANES, (j + 1) * LANES)
        y_ref[:, sl] = (x2_ref[:, sl] + w0 * buf[slot, 0, pl.ds(j, tm, stride=rpt), :]
                        + w1 * buf[slot, 1, pl.ds(j, tm, stride=rpt), :])


def _combine(dest_t, x2, route, y_sorted, *, row0, rows, tm=256):
    d = x2.shape[1]
    rpt = d // LANES
    t0 = row0 // tm
    n_tiles = rows // tm
    kern = functools.partial(_combine_kernel, tm=tm, n_tiles=n_tiles, rpt=rpt)
    return pl.pallas_call(
        kern, grid=(n_tiles,),
        in_specs=[pl.BlockSpec((1, 8, tm), lambda i: (t0 + i, 0, 0), memory_space=pltpu.SMEM),
                  pl.BlockSpec((1, 8, tm), lambda i: (t0 + jnp.minimum(i + 1, n_tiles - 1), 0, 0),
                               memory_space=pltpu.SMEM),
                  pl.BlockSpec((tm, d), lambda i: (t0 + i, 0)),
                  pl.BlockSpec((tm, LANES), lambda i: (t0 + i, 0)),
                  pl.BlockSpec(memory_space=pl.ANY)],
        out_specs=pl.BlockSpec((tm, d), lambda i: (i, 0)),
        out_shape=jax.ShapeDtypeStruct((rows, d), F32),
        scratch_shapes=[pltpu.VMEM((2, 2, tm * rpt, LANES), F32), pltpu.SemaphoreType.DMA((2,))],
        compiler_params=_params(("arbitrary",)),
        name="combine",
    )(dest_t, dest_t, x2, route, y_sorted)


def _rank_block(i1, i2, lane_f, cnt_ref):
    tm = i1.shape[0]
    oh0 = lane_f == i1
    oh1 = lane_f == i2
    oh = jnp.where(oh0, 1.0, jnp.where(oh1, 1.0, 0.0))
    ri = lax.broadcasted_iota(I32, (tm, tm), 0)
    ci = lax.broadcasted_iota(I32, (tm, tm), 1)
    earlier = jnp.where(ci < ri, 1.0, 0.0).astype(BF16)
    base = cnt_ref[...] + jnp.dot(earlier, oh.astype(BF16), preferred_element_type=F32)
    r0 = jnp.sum(jnp.where(oh0, base, 0.0), axis=1, keepdims=True)
    r1 = jnp.sum(jnp.where(oh1, base, 0.0), axis=1, keepdims=True)
    cnt_ref[...] += jnp.sum(oh, axis=0, keepdims=True)
    return r0, r1


def _dest_kernel(route_ref, start_ref, dest_ref):
    route = route_ref[...]
    lane = lax.broadcasted_iota(I32, route.shape, 1)
    lane_f = lane.astype(F32)
    start = start_ref[...]
    d0 = route[:, 4:5] + jnp.sum(jnp.where(lane_f == route[:, 0:1], start, 0.0), axis=1, keepdims=True)
    d1 = route[:, 5:6] + jnp.sum(jnp.where(lane_f == route[:, 1:2], start, 0.0), axis=1, keepdims=True)
    dd = jnp.where(lane == 0, d0, jnp.where(lane == 1, d1, 0.0))
    dest_ref[0] = dd.T[0:8, :].astype(I32)


def _dest(route, start_row, *, tm):
    n = route.shape[0]
    return pl.pallas_call(
        _dest_kernel, grid=(n // tm,),
        in_specs=[pl.BlockSpec((tm, LANES), lambda i: (i, 0)), pl.BlockSpec((1, LANES), lambda i: (0, 0))],
        out_specs=pl.BlockSpec((1, 8, tm), lambda i: (i, 0, 0)),
        out_shape=jax.ShapeDtypeStruct((n // tm, 8, tm), I32),
        compiler_params=_params(("parallel",)),
        name="dest",
    )(route, start_row)


def _moe_plan(route, cnt_f, *, n_exp, tm):
    n = route.shape[0]
    nt = (2 * n + n_exp * (tm - 1)) // tm
    cnt = cnt_f[0, :n_exp].astype(I32)
    ptiles = (cnt + tm - 1) // tm
    tile_end = jnp.cumsum(ptiles)
    tile_start = tile_end - ptiles
    start_row = jnp.pad((tile_start * tm).astype(F32), (0, LANES - n_exp)).reshape(1, LANES)
    dest_t = _dest(route, start_row, tm=tm)
    n_used = tile_end[-1]
    tiles = jnp.arange(nt, dtype=I32)
    te = jnp.minimum(jnp.sum((tile_end[None, :] <= tiles[:, None]).astype(I32), axis=1), n_exp - 1)
    te_last = jnp.take(te, jnp.maximum(n_used - 1, 0))
    tile_expert = jnp.where(tiles < n_used, te, te_last)
    pad_start = tile_start * tm + cnt
    pad_len = ptiles * tm - cnt
    return tile_expert, n_used.reshape(1).astype(I32), dest_t, pad_start, pad_len, nt


def kernel(x_prompt, x_sample, cache_k, cache_v, cache_lf, state_C, state_n, state_m, page_table,
           norm1_g, w_in, b_igate, b_fgate_mlstm, b_fgate_fox, mlstm_norm_g, q_norm_g, k_norm_g,
           w_branch_mlstm, w_branch_fox, w_out, norm2_g, w_router_group, b_router_group,
           w_router_expert, b_router_expert, w_gate, w_up, w_down):
    depth = w_in.shape[0]
    assert depth == 1, "single-layer step"
    bp, tp, d = x_prompt.shape
    bs, ts, _ = x_sample.shape
    mh = b_igate.shape[-1]
    fh = b_fgate_fox.shape[-1]
    fd = q_norm_g.shape[-1]
    md = d // mh
    n_exp = w_gate.shape[1]
    n_pages = page_table.shape[1]
    page = cache_k.shape[2]
    assert 2 * mh + fh <= LANES and n_exp + N_GROUPS <= LANES
    np_tok, ns_tok = bp * tp, bs * ts
    l = 0

    w = w_in[l]
    o = 0
    secs = {}
    for name, width in (("m", 4 * d), ("mi", mh), ("mf", mh), ("f", 3 * d), ("ff", fh), ("g", 2 * d)):
        secs[name] = w[:, o:o + width]
        o += width
    pad = jnp.zeros((d, LANES - 2 * mh - fh), F32)
    w_all = jnp.concatenate([secs["m"], secs["f"], secs["g"], secs["mi"], secs["mf"], secs["ff"], pad],
                            axis=1).astype(BF16)
    bias_s = jnp.concatenate([b_igate[l], b_fgate_mlstm[l], b_fgate_fox[l],
                              jnp.zeros((LANES - 2 * mh - fh,), F32)]).reshape(1, LANES)
    g1 = norm1_g[l].reshape(1, d)
    qg = q_norm_g[l].reshape(1, fd)
    kg = k_norm_g[l].reshape(1, fd)
    g_m = mlstm_norm_g[l].reshape(1, d)
    wa = w_branch_mlstm[l].astype(BF16)
    wb = w_branch_fox[l].astype(BF16)
    wo = w_out[l].astype(BF16)
    g2 = norm2_g[l].reshape(1, d)
    wr32 = jnp.concatenate([w_router_expert[l], w_router_group[l],
                            jnp.zeros((d, LANES - n_exp - N_GROUPS), F32)], axis=1)
    wr_hi = wr32.astype(BF16)
    wr = jnp.concatenate([wr_hi, (wr32 - wr_hi.astype(F32)).astype(BF16)], axis=1)
    br = jnp.concatenate([b_router_expert[l], b_router_group[l],
                          jnp.zeros((LANES - n_exp - N_GROUPS,), F32)]).reshape(1, LANES)

    inproj = functools.partial(_inproj, g1=g1, w_all=w_all, bias_s=bias_s, qg=qg, kg=kg, mh=mh, fh=fh)
    pm_p, fq_p, k3_p, v3_p, kb_p, vb_p, gate_p, small_p, kt_p = inproj(
        x_prompt.reshape(np_tok, d), act_dtype=BF16, q_unit=LOG2E, emit_kt=True)
    pm_s, fq_s, k3_s, v3_s, kb_s, vb_s, gate_s, small_s = inproj(x_sample.reshape(ns_tok, d), act_dtype=F32)

    ha_p, c_p, n_p, m_p = _mlstm_prompt(pm_p, kt_p, small_p, g_m, batch=bp, seq=tp, mh=mh)
    m0 = jnp.pad(state_m[l], ((0, 0), (0, LANES - mh))).reshape(bs, 1, LANES)
    ha_s, c_s, n_s, m_s = _mlstm(pm_s, small_s, g_m, batch=bs, seq=ts, mh=mh, nb=8,
                                 init=(state_C[l], state_n[l], m0), out_dtype=F32)

    lf_p = small_p[:, 2 * mh:2 * mh + fh]
    lf_s = small_s[:, 2 * mh:2 * mh + fh]
    qx_p, kx_p = _fbias(small_p, batch=bp, seq=tp, fh=fh, fd=fd, lf0=2 * mh)
    ob_p = _fox_prompt(fq_p, qx_p, kb_p, kx_p, vb_p, batch=bp, seq=tp, fh=fh)

    n_phys = cache_k.shape[1]
    lft_c = cache_lf[l].transpose(0, 2, 1).reshape(n_phys * fh, page)
    floc = _cumsum_lanes(lft_c, carry=False, block_rows=n_phys * fh // 8).reshape(n_phys, fh, page)
    ck = cache_k[l].reshape(n_phys, page * fh, fd)
    cv = cache_v[l].reshape(n_phys, page * fh, fd)
    ob_s = _fox_sample(page_table, fq_s, kb_s, vb_s, small_s, floc, ck, cv, fh=fh, lf0=2 * mh)

    tm = MOE_TILE
    x2, xn_rows, route, cnt_f = _post((x_prompt.reshape(np_tok, d), ha_p, ob_p, gate_p),
                                      (x_sample.reshape(ns_tok, d), ha_s, ob_s, gate_s),
                                      wa, wb, wo, g2, wr, br, n_exp=n_exp, n_groups=N_GROUPS, tm=tm)

    tile_expert, n_used, dest_t, pad_start, pad_len, nt = _moe_plan(route, cnt_f, n_exp=n_exp, tm=tm)
    x_sorted = _dispatch(pad_start, pad_len, n_used, dest_t, xn_rows, nt=nt, tm=tm)
    y_sorted = _experts(tile_expert, n_used, x_sorted.reshape(nt * tm * (d // LANES), LANES),
                        w_gate[l], w_up[l], w_down[l], tm=tm)
    y_p = _combine(dest_t, x2, route, y_sorted, row0=0, rows=np_tok, tm=tm)
    y_s = _combine(dest_t, x2, route, y_sorted, row0=np_tok, rows=ns_tok, tm=tm)

    return (y_p.reshape(bp, tp, d), y_s.reshape(bs, ts, d),
            k3_p.reshape(1, bp, tp, fh, fd), v3_p.reshape(1, bp, tp, fh, fd), lf_p.reshape(1, bp, tp, fh),
            k3_s.reshape(1, bs, ts, fh, fd), v3_s.reshape(1, bs, ts, fh, fd), lf_s.reshape(1, bs, ts, fh),
            c_p[None], n_p[None], m_p[:, 0, :mh][None],
            c_s[None], n_s[None], m_s[:, 0, :mh][None])
```

```python
import functools

import jax
import jax.numpy as jnp
import numpy as np
from jax import lax
from jax.experimental import pallas as pl
from jax.experimental.pallas import tpu as pltpu

F32 = jnp.float32
BF16 = jnp.bfloat16
I32 = jnp.int32
RMS_EPS = 1e-6
LANES = 128
MLSTM_CHUNK = 128
MLSTM_PROMPT_CHUNK = 256
N_GROUPS = 4
EXPERTS_PER_GROUP = 8
VMEM_LIMIT = 56 * 1024 * 1024
NEG_INF = float("-inf")
LOG2E = 1.4426950408889634
DMA_UNROLL = 8
MOE_TILE = 256
PAD_CHUNKS = (128, 64, 32, 16, 8, 4, 2, 1)


def _params(sem, vmem=VMEM_LIMIT):
    return pltpu.CompilerParams(dimension_semantics=sem, vmem_limit_bytes=vmem)


def _log_sigmoid(x):
    return -(jnp.maximum(-x, 0.0) + jnp.log1p(jnp.exp(-jnp.abs(x))))


def _sigmoid(x):
    return 1.0 / (1.0 + jnp.exp(-x))


def _nt_dot(a, b):
    return lax.dot_general(a, b, (((1,), (1,)), ((), ())), preferred_element_type=F32)


def _tn_dot(a, b):
    return lax.dot_general(a, b, (((0,), (0,)), ((), ())), preferred_element_type=F32)


def _col_to_row(col, eye):
    return jnp.sum(jnp.where(eye, col, 0.0), axis=0, keepdims=True)


def _row_to_col(row, eye):
    return jnp.sum(jnp.where(eye, row, 0.0), axis=1, keepdims=True)


def _inproj_kernel(x_ref, g1_ref, wm_ref, wf_ref, wg_ref, ws_ref, bias_ref, qg_ref, kg_ref,
                   pm_ref, fq_ref, k3_ref, v3_ref, kb_ref, vb_ref, gate_ref, small_ref, *maybe_kt_ref,
                   d, mh, fh, fd, k_scale, q_scale):
    x = x_ref[...]
    ms = jnp.mean(x * x, axis=-1, keepdims=True)
    h = (x * lax.rsqrt(ms + RMS_EPS) * g1_ref[...]).astype(BF16)
    sections = ((0, wm_ref), (4 * d, wf_ref), (7 * d, wg_ref), (9 * d, ws_ref))

    def proj(c0, width):
        base, w_ref = [(b, r) for b, r in sections if b <= c0][-1]
        return jnp.dot(h, w_ref[:, c0 - base:c0 - base + width], preferred_element_type=F32)

    pm_ref[:, 0:d] = proj(0, d).astype(pm_ref.dtype)
    mk = proj(d, d) * k_scale
    pm_ref[:, d:2 * d] = mk.astype(pm_ref.dtype)
    if maybe_kt_ref:
        maybe_kt_ref[0][...] = mk.T.astype(maybe_kt_ref[0].dtype)
    pm_ref[:, 2 * d:3 * d] = proj(2 * d, d).astype(pm_ref.dtype)
    pm_ref[:, 3 * d:4 * d] = proj(3 * d, d).astype(pm_ref.dtype)

    def head_norm(a, g):
        ms_h = jnp.mean(a * a, axis=-1, keepdims=True)
        return a * lax.rsqrt(ms_h + RMS_EPS) * g

    fq = proj(4 * d, d)
    fk = proj(5 * d, d)
    fv = proj(6 * d, d)
    for hh in range(fh):
        sl = slice(hh * fd, (hh + 1) * fd)
        fq_ref[:, sl] = (head_norm(fq[:, sl], qg_ref[...]) * q_scale).astype(fq_ref.dtype)
        kn = head_norm(fk[:, sl], kg_ref[...])
        k3_ref[:, hh, :] = kn
        kb_ref[:, sl] = kn.astype(kb_ref.dtype)
        v3_ref[:, hh, :] = fv[:, sl]
    vb_ref[...] = fv.astype(vb_ref.dtype)

    gate_ref[...] = _sigmoid(proj(7 * d, 2 * d)).astype(gate_ref.dtype)

    sm = proj(9 * d, LANES) + bias_ref[...]
    lane = lax.broadcasted_iota(I32, sm.shape, 1)
    sm = jnp.where(lane < mh, sm, jnp.where(lane < 2 * mh + fh, _log_sigmoid(sm), 0.0))
    small_ref[...] = sm


def _inproj(x2d, g1, w_secs, bias_s, qg, kg, *, mh, fh, act_dtype, q_unit=1.0, emit_kt=False, tm=256):
    n, d = x2d.shape
    fd = d // fh
    md = d // mh
    kern = functools.partial(_inproj_kernel, d=d, mh=mh, fh=fh, fd=fd,
                             k_scale=md ** -0.5, q_scale=fd ** -0.5 * q_unit)
    row = lambda i: (i, 0)
    const = lambda i: (0, 0)
    kt_specs = [pl.BlockSpec((d, tm), lambda i: (0, i))] if emit_kt else []
    kt_shapes = [jax.ShapeDtypeStruct((d, n), act_dtype)] if emit_kt else []
    return pl.pallas_call(
        kern, grid=(n // tm,),
        in_specs=[pl.BlockSpec((tm, d), row),
                  pl.BlockSpec((1, d), const),
                  *[pl.BlockSpec(w.shape, const, pipeline_mode=pl.Buffered(1)) for w in w_secs],
                  pl.BlockSpec((1, LANES), const),
                  pl.BlockSpec((1, fd), const),
                  pl.BlockSpec((1, fd), const)],
        out_specs=[pl.BlockSpec((tm, 4 * d), row),
                   pl.BlockSpec((tm, d), row),
                   pl.BlockSpec((tm, fh, fd), lambda i: (i, 0, 0)),
                   pl.BlockSpec((tm, fh, fd), lambda i: (i, 0, 0)),
                   pl.BlockSpec((tm, d), row),
                   pl.BlockSpec((tm, d), row),
                   pl.BlockSpec((tm, 2 * d), row),
                   pl.BlockSpec((tm, LANES), row)] + kt_specs,
        out_shape=[jax.ShapeDtypeStruct((n, 4 * d), act_dtype),
                   jax.ShapeDtypeStruct((n, d), act_dtype),
                   jax.ShapeDtypeStruct((n, fh, fd), F32),
                   jax.ShapeDtypeStruct((n, fh, fd), F32),
                   jax.ShapeDtypeStruct((n, d), act_dtype),
                   jax.ShapeDtypeStruct((n, d), act_dtype),
                   jax.ShapeDtypeStruct((n, 2 * d), BF16),
                   jax.ShapeDtypeStruct((n, LANES), F32)] + kt_shapes,
        compiler_params=_params(("parallel",)),
        name="inproj",
    )(x2d, g1, *w_secs, bias_s, qg, kg)


def _mlstm_kernel(*refs, L, mh, md, nb, has_init):
    if has_init:
        (q_ref, k_ref, v_ref, o_ref, s_ref, g_ref, c0_ref, n0_ref, m0_ref,
         h_ref, c_ref, n_ref, m_ref) = refs
    else:
        q_ref, k_ref, v_ref, o_ref, s_ref, g_ref, h_ref, c_ref, n_ref, m_ref = refs

    @pl.when(pl.program_id(1) == 0)
    def _():
        if has_init:
            c_ref[...] = c0_ref[...]
            n_ref[...] = n0_ref[...]
            m_ref[...] = m0_ref[...]
        else:
            c_ref[...] = jnp.zeros_like(c_ref)
            n_ref[...] = jnp.zeros_like(n_ref)
            m_ref[...] = jnp.zeros_like(m_ref)

    ri = lax.broadcasted_iota(I32, (L, L), 0)
    ci = lax.broadcasted_iota(I32, (L, L), 1)
    tri = ci <= ri
    eye = ci == ri
    for bb in range(nb):
        s = s_ref[bb]
        m_all = m_ref[bb]
        lane = lax.broadcasted_iota(I32, m_all.shape, 1)
        m_out = m_all
        for hh in range(mh):
            sl = slice(hh * md, (hh + 1) * md)
            ig_col = s[:, hh:hh + 1]
            lf_col = s[:, mh + hh:mh + hh + 1]
            lf_row = _col_to_row(lf_col, eye)
            ig_row = _col_to_row(ig_col, eye)
            b_col = jnp.sum(jnp.where(tri, lf_row, 0.0), axis=1, keepdims=True)
            b_row = _col_to_row(b_col, eye)
            m_prev = m_all[:, hh:hh + 1]
            log_w = jnp.where(tri, b_col - b_row + ig_row, NEG_INF)
            log_inter = b_col + m_prev
            m_t = jnp.maximum(log_inter, jnp.max(log_w, axis=1, keepdims=True))
            w_intra = jnp.exp(log_w - m_t)
            w_inter = jnp.exp(log_inter - m_t)
            q = q_ref[bb, :, sl].astype(BF16)
            k = k_ref[bb, :, sl].astype(BF16)
            v = v_ref[bb, :, sl].astype(BF16)
            sm = _nt_dot(q, k) * w_intra
            c_prev = c_ref[bb, hh]
            n_prev = n_ref[bb, hh:hh + 1, :]
            num = (w_inter * jnp.dot(q, c_prev.astype(BF16), preferred_element_type=F32)
                   + jnp.dot(sm.astype(BF16), v, preferred_element_type=F32))
            den = (w_inter * jnp.sum(q.astype(F32) * n_prev, axis=1, keepdims=True)
                   + jnp.sum(sm, axis=1, keepdims=True))
            hv = num / jnp.maximum(jnp.abs(den), jnp.exp(-m_t))
            m_new = m_t[L - 1:L, :]
            b_last = b_col[L - 1:L, :]
            decay = jnp.exp(b_last + m_prev - m_new)
            w_rows = jnp.exp(b_last - b_col + ig_col - m_new)
            kw = k.astype(F32) * w_rows
            c_ref[bb, hh] = decay * c_prev + _tn_dot(kw.astype(BF16), v)
            n_ref[bb, hh:hh + 1, :] = decay * n_prev + jnp.sum(kw, axis=0, keepdims=True)
            m_out = jnp.where(lane == hh, m_new, m_out)
            ms = jnp.mean(hv * hv, axis=-1, keepdims=True)
            hn = hv * lax.rsqrt(ms + RMS_EPS) * g_ref[:, sl]
            h_ref[bb, :, sl] = (_sigmoid(o_ref[bb, :, sl].astype(F32)) * hn).astype(h_ref.dtype)
        m_ref[bb] = m_out


def _mlstm(pm, small, g_m, *, batch, seq, mh, nb=1, init=None, out_dtype=BF16):
    d = pm.shape[1] // 4
    md = d // mh
    L = MLSTM_CHUNK if seq % MLSTM_CHUNK == 0 else seq
    nc = seq // L
    ng = batch // nb
    pm4 = pm.reshape(ng, nb, seq, 4 * d)
    small4 = small.reshape(ng, nb, seq, LANES)
    sec = lambda j: (lambda b, c: (b, 0, c, j))
    in_specs = [pl.BlockSpec((None, nb, L, d), sec(0)), pl.BlockSpec((None, nb, L, d), sec(1)),
                pl.BlockSpec((None, nb, L, d), sec(2)), pl.BlockSpec((None, nb, L, d), sec(3)),
                pl.BlockSpec((None, nb, L, LANES), sec(0)),
                pl.BlockSpec((1, d), lambda b, c: (0, 0))]
    args = [pm4, pm4, pm4, pm4, small4, g_m]
    state_specs = [pl.BlockSpec((nb, mh, md, md), lambda b, c: (b, 0, 0, 0)),
                   pl.BlockSpec((nb, mh, md), lambda b, c: (b, 0, 0)),
                   pl.BlockSpec((nb, 1, LANES), lambda b, c: (b, 0, 0))]
    if init is not None:
        in_specs += state_specs
        args += list(init)
    kern = functools.partial(_mlstm_kernel, L=L, mh=mh, md=md, nb=nb, has_init=init is not None)
    h4, c_out, n_out, m_out = pl.pallas_call(
        kern, grid=(ng, nc),
        in_specs=in_specs,
        out_specs=[pl.BlockSpec((None, nb, L, d), lambda b, c: (b, 0, c, 0))] + state_specs,
        out_shape=[jax.ShapeDtypeStruct((ng, nb, seq, d), out_dtype),
                   jax.ShapeDtypeStruct((batch, mh, md, md), F32),
                   jax.ShapeDtypeStruct((batch, mh, md), F32),
                   jax.ShapeDtypeStruct((batch, 1, LANES), F32)],
        compiler_params=_params(("parallel", "arbitrary")),
        name="mlstm",
    )(*args)
    return h4.reshape(batch * seq, d), c_out, n_out, m_out


def _mlstm_prompt_kernel(q_ref, kt_ref, v_ref, o_ref, s_ref, g_ref, h_ref, c_ref, n_ref, m_ref, naug,
                         *, L, mh, md, nc):
    c = pl.program_id(1)

    @pl.when(c == 0)
    def _():
        c_ref[...] = jnp.zeros_like(c_ref)
        m_ref[...] = jnp.zeros_like(m_ref)
        naug[...] = jnp.zeros_like(naug)

    s = s_ref[...]
    ri = lax.broadcasted_iota(I32, (L, L), 0)
    ci = lax.broadcasted_iota(I32, (L, L), 1)
    tri = ci <= ri
    hp = lax.Precision.HIGHEST
    r = s.T[0:8, :]
    b_cols = jnp.dot(jnp.where(tri, 1.0, 0.0), s, precision=hp, preferred_element_type=F32)
    b_rows = pltpu.roll(jnp.dot(r, jnp.where(ri <= ci, 1.0, 0.0), precision=hp, preferred_element_type=F32),
                        8 - mh, axis=0)
    g = r - b_rows
    m_all = m_ref[0]
    lane = lax.broadcasted_iota(I32, m_all.shape, 1)
    m_out = m_all
    ones_v = jnp.ones((L, LANES), BF16)
    ones_r = jnp.ones((md, LANES), BF16)
    for hh in range(mh):
        sl = slice(hh * md, (hh + 1) * md)
        m_prev = m_all[:, hh:hh + 1]
        g_row = g[hh:hh + 1, :]
        b_col = b_cols[:, mh + hh:mh + hh + 1]
        a_col = jnp.maximum(m_prev, jnp.max(jnp.where(tri, g_row, NEG_INF), axis=1, keepdims=True))
        m_t = b_col + a_col
        m_new = m_t[L - 1:L, :]
        b_last = b_col[L - 1:L, :]
        w_rows = jnp.exp(b_last - b_rows[hh:hh + 1, :] + r[hh:hh + 1, :] - m_new)
        dc = jnp.exp(b_last + m_prev - m_new)
        m_out = jnp.where(lane == hh, m_new, m_out)
        q = q_ref[:, sl]
        kt = kt_ref[sl, :]
        v_aug = jnp.concatenate([v_ref[:, sl], ones_v], axis=1)
        w_intra = jnp.exp(jnp.where(tri, g_row - a_col, NEG_INF))
        sm = jnp.dot(q, kt, preferred_element_type=F32) * w_intra
        c_prev = c_ref[0, hh]
        n_prev = naug[hh]
        c_aug = jnp.concatenate([c_prev, n_prev], axis=1).astype(BF16)
        comb = (jnp.exp(m_prev - a_col) * jnp.dot(q, c_aug, preferred_element_type=F32)
                + jnp.dot(sm.astype(BF16), v_aug, preferred_element_type=F32))
        den = jnp.maximum(jnp.abs(comb[:, md:]), jnp.exp(-m_t))
        inv = 1.0 / den
        hv = comb[:, 0:md] * jnp.concatenate([inv] * (md // LANES), axis=1)
        ms = jnp.dot((hv * hv).astype(BF16), ones_r, preferred_element_type=F32) * (1.0 / md)
        rs = lax.rsqrt(ms + RMS_EPS)
        hn = hv * jnp.concatenate([rs] * (md // LANES), axis=1) * g_ref[:, sl]
        h_ref[:, sl] = (_sigmoid(o_ref[:, sl].astype(F32)) * hn).astype(h_ref.dtype)
        kw = (kt.astype(F32) * w_rows).astype(BF16)
        upd = jnp.dot(kw, v_aug, preferred_element_type=F32)
        c_ref[0, hh] = dc * c_prev + upd[:, 0:md]
        naug[hh] = dc * n_prev + upd[:, md:]
    m_ref[0] = m_out

    @pl.when(c == nc - 1)
    def _():
        ei = lax.broadcasted_iota(I32, (md, md), 0) == lax.broadcasted_iota(I32, (md, md), 1)
        for hh in range(mh):
            cols = jnp.concatenate([naug[hh]] * (md // LANES), axis=1)
            n_ref[0, hh:hh + 1, :] = jnp.sum(jnp.where(ei, cols, 0.0), axis=0, keepdims=True)


def _mlstm_prompt(pm, kt, small, g_m, *, batch, seq, mh):
    d = pm.shape[1] // 4
    md = d // mh
    L = MLSTM_PROMPT_CHUNK
    assert seq % L == 0
    nc = seq // L
    tok = lambda j: (lambda b, c: (b * nc + c, j))
    kern = functools.partial(_mlstm_prompt_kernel, L=L, mh=mh, md=md, nc=nc)
    return pl.pallas_call(
        kern, grid=(batch, nc),
        in_specs=[pl.BlockSpec((L, d), tok(0)),
                  pl.BlockSpec((d, L), lambda b, c: (0, b * nc + c)),
                  pl.BlockSpec((L, d), tok(2)), pl.BlockSpec((L, d), tok(3)),
                  pl.BlockSpec((L, LANES), tok(0)),
                  pl.BlockSpec((1, d), lambda b, c: (0, 0))],
        out_specs=[pl.BlockSpec((L, d), tok(0)),
                   pl.BlockSpec((1, mh, md, md), lambda b, c: (b, 0, 0, 0)),
                   pl.BlockSpec((1, mh, md), lambda b, c: (b, 0, 0)),
                   pl.BlockSpec((1, 1, LANES), lambda b, c: (b, 0, 0))],
        out_shape=[jax.ShapeDtypeStruct((batch * seq, d), BF16),
                   jax.ShapeDtypeStruct((batch, mh, md, md), F32),
                   jax.ShapeDtypeStruct((batch, mh, md), F32),
                   jax.ShapeDtypeStruct((batch, 1, LANES), F32)],
        scratch_shapes=[pltpu.VMEM((mh, md, LANES), F32)],
        compiler_params=_params(("parallel", "arbitrary")),
        name="mlstm_prompt",
    )(pm, kt, pm, pm, small, g_m)


def _cumsum_kernel(x_ref, o_ref, *, chunk, carry):
    rows, t = x_ref.shape
    si = lax.broadcasted_iota(I32, (chunk, chunk), 0)
    ti = lax.broadcasted_iota(I32, (chunk, chunk), 1)
    tri = (si <= ti).astype(F32)
    run = jnp.zeros((rows, 1), F32)
    for j in range(t // chunk):
        sl = slice(j * chunk, (j + 1) * chunk)
        loc = jnp.dot(x_ref[:, sl], tri, precision=lax.Precision.HIGHEST, preferred_element_type=F32)
        if carry:
            loc = loc + run
            run = loc[:, chunk - 1:chunk]
        o_ref[:, sl] = loc


def _cumsum_lanes(x, *, carry, block_rows):
    rows, t = x.shape
    kern = functools.partial(_cumsum_kernel, chunk=LANES, carry=carry)
    return pl.pallas_call(
        kern, grid=(rows // block_rows,),
        in_specs=[pl.BlockSpec((block_rows, t), lambda i: (i, 0))],
        out_specs=pl.BlockSpec((block_rows, t), lambda i: (i, 0)),
        out_shape=jax.ShapeDtypeStruct((rows, t), F32),
        compiler_params=_params(("parallel",)),
        name="cumsum",
    )(x)


BIAS_LANES = 8


def _fbias_kernel(s_ref, pq_ref, pk_ref, cq_ref, ck_ref, qx_ref, kx_ref, run_ref):
    @pl.when(pl.program_id(1) == 0)
    def _():
        run_ref[...] = jnp.zeros_like(run_ref)

    s = s_ref[...]
    L = s.shape[0]
    ti = lax.broadcasted_iota(I32, (L, L), 0)
    si = lax.broadcasted_iota(I32, (L, L), 1)
    tri = jnp.where(si <= ti, 1.0, 0.0).astype(BF16)

    def split3(x):
        hi = x.astype(BF16)
        r1 = x - hi.astype(F32)
        mid = r1.astype(BF16)
        return jnp.concatenate([hi, mid, (r1 - mid.astype(F32)).astype(BF16)], axis=1)

    cs = jnp.dot(tri, split3(s), preferred_element_type=F32)
    f_all = (cs[:, 0:LANES] + cs[:, LANES:2 * LANES] + cs[:, 2 * LANES:3 * LANES]) + run_ref[...]
    run_ref[...] = f_all[L - 1:L, :]
    terms = split3(f_all * LOG2E)
    qx_ref[...] = jnp.dot(terms, pq_ref[...], preferred_element_type=F32) + cq_ref[...]
    kx_ref[...] = (jnp.dot(terms, pk_ref[...], preferred_element_type=F32) + ck_ref[...]).astype(BF16)


def _fbias(small, *, batch, seq, fh, lf0):
    assert BIAS_LANES * fh <= LANES
    n = small.shape[0]
    L = 4 * LANES if seq % (4 * LANES) == 0 else LANES
    nc = seq // L
    pq = np.zeros((3 * LANES, LANES), np.float32)
    pk = np.zeros((3 * LANES, LANES), np.float32)
    cq = np.zeros((1, LANES), np.float32)
    ck = np.zeros((1, LANES), np.float32)
    for h in range(fh):
        for j in range(3):
            pq[j * LANES + lf0 + h, BIAS_LANES * h + j] = 1.0
            pk[j * LANES + lf0 + h, BIAS_LANES * h + 3 + j] = -1.0
            cq[0, BIAS_LANES * h + 3 + j] = 1.0
            ck[0, BIAS_LANES * h + j] = 1.0
    const = lambda b, c: (0, 0)
    tok = lambda b, c: (b * nc + c, 0)
    return pl.pallas_call(
        _fbias_kernel, grid=(batch, nc),
        in_specs=[pl.BlockSpec((L, LANES), tok),
                  pl.BlockSpec((3 * LANES, LANES), const), pl.BlockSpec((3 * LANES, LANES), const),
                  pl.BlockSpec((1, LANES), const), pl.BlockSpec((1, LANES), const)],
        out_specs=[pl.BlockSpec((L, LANES), tok), pl.BlockSpec((L, LANES), tok)],
        out_shape=[jax.ShapeDtypeStruct((n, LANES), F32), jax.ShapeDtypeStruct((n, LANES), BF16)],
        scratch_shapes=[pltpu.VMEM((1, LANES), F32)],
        compiler_params=_params(("parallel", "arbitrary")),
        name="fbias",
    )(small, jnp.asarray(pq, BF16), jnp.asarray(pk, BF16), jnp.asarray(cq), jnp.asarray(ck))


def _fox_prompt_kernel(q_ref, qx_ref, k_ref, kx_ref, v_ref, o_ref, *, seq, tq, gh, fd):
    nq = seq // tq
    ri = lax.broadcasted_iota(I32, (tq, tq), 0)
    ci = lax.broadcasted_iota(I32, (tq, tq), 1)
    causal = ci <= ri

    def softmax_step(carry, s, v):
        m, l, acc = carry
        m_new = jnp.maximum(m, jnp.max(s, axis=1, keepdims=True))
        a = jnp.exp2(m - m_new)
        p = jnp.exp2(s - m_new)
        l = a * l + jnp.sum(p, axis=1, keepdims=True)
        acc = a * acc + jnp.dot(p.astype(BF16), v, preferred_element_type=F32)
        return m_new, l, acc

    head0 = pl.program_id(1) * gh
    bias_head = lax.broadcasted_iota(I32, (tq, LANES), 1) // BIAS_LANES

    def q_body(qi, _):
        q0 = pl.multiple_of(qi * tq, tq)
        qx = qx_ref[pl.ds(q0, tq), :]
        qas = [jnp.concatenate([q_ref[pl.ds(q0, tq), g * fd:(g + 1) * fd],
                                jnp.where(bias_head == head0 + g, qx, 0.0).astype(BF16)], axis=1)
               for g in range(gh)]

        def block(g, carry_g, k0, masked):
            gs = slice(g * fd, (g + 1) * fd)
            ka = jnp.concatenate([k_ref[pl.ds(k0, tq), gs], kx_ref[pl.ds(k0, tq), :]], axis=1)
            s = _nt_dot(qas[g], ka)
            if masked:
                s = jnp.where(causal, s, NEG_INF)
            return softmax_step(carry_g, s, v_ref[pl.ds(k0, tq), gs])

        def kv_body(kj, carry):
            k0 = pl.multiple_of(kj * tq, tq)
            return tuple(block(g, carry[g], k0, False) for g in range(gh))

        init = tuple((jnp.full((tq, 1), NEG_INF, F32), jnp.zeros((tq, 1), F32), jnp.zeros((tq, fd), F32))
                     for _ in range(gh))
        carry = lax.fori_loop(0, qi, kv_body, init)
        for g in range(gh):
            _, l, acc = block(g, carry[g], q0, True)
            o_ref[pl.ds(q0, tq), g * fd:(g + 1) * fd] = (acc / l).astype(o_ref.dtype)
        return 0

    lax.fori_loop(0, nq, q_body, 0)


def _fox_prompt(fq, qx, kb, kx, vb, *, batch, seq, fh, tq=512, gh=2):
    n, d = fq.shape
    fd = d // fh
    blk = lambda b, h: (b, h)
    bias = pl.BlockSpec((seq, LANES), lambda b, h: (b, 0))
    kern = functools.partial(_fox_prompt_kernel, seq=seq, tq=tq, gh=gh, fd=fd)
    return pl.pallas_call(
        kern, grid=(batch, fh // gh),
        in_specs=[pl.BlockSpec((seq, gh * fd), blk), bias, pl.BlockSpec((seq, gh * fd), blk), bias,
                  pl.BlockSpec((seq, gh * fd), blk)],
        out_specs=pl.BlockSpec((seq, gh * fd), blk),
        out_shape=jax.ShapeDtypeStruct((n, d), BF16),
        compiler_params=_params(("parallel", "parallel")),
        name="fox_prompt",
    )(fq, qx, kb, kx, vb)


def _fox_sample_kernel(pt_ref, q_ref, kn_ref, vn_ref, s_ref, *rest, n_pages, page, fh, fd, ts, lf0):
    floc = rest[0:n_pages]
    kpages = rest[n_pages:2 * n_pages]
    vpages = rest[2 * n_pages:3 * n_pages]
    o_ref = rest[3 * n_pages]
    p_scr, pn_scr, l_scr = rest[3 * n_pages + 1:]
    d = fh * fd
    rows = fh * ts

    def scores():
        q = q_ref[...].astype(F32)
        qt = jnp.concatenate([q] * fh, axis=0)
        r_head = lax.broadcasted_iota(I32, (rows, d), 0) // ts
        c_head = lax.broadcasted_iota(I32, (rows, d), 1) // fd
        qbd = jnp.where(r_head == c_head, qt, 0.0).astype(BF16)

        off = jnp.zeros((fh, 1), F32)
        fk_pages = []
        for i in range(n_pages):
            fp = floc[i][...] + off
            fk_pages.append(fp)
            off = fp[:, page - 1:page]
        fk = jnp.concatenate(fk_pages, axis=1)
        fk_rows = jnp.concatenate(
            [jnp.broadcast_to(fk[h:h + 1, :], (ts, fk.shape[1])) for h in range(fh)], axis=0)

        sm = s_ref[...]
        ri = lax.broadcasted_iota(I32, (ts, ts), 0)
        ci = lax.broadcasted_iota(I32, (ts, ts), 1)
        eye = ci == ri
        tri = ci <= ri
        fq_cols, bias_new = [], []
        for h in range(fh):
            lf_col = sm[:, lf0 + h:lf0 + h + 1]
            lf_row = _col_to_row(lf_col, eye)
            cum_col = jnp.sum(jnp.where(tri, lf_row, 0.0), axis=1, keepdims=True)
            fq_h = off[h:h + 1, :] + cum_col
            fq_cols.append(fq_h)
            bias_new.append(fq_h - _col_to_row(fq_h, eye))
        fq_col = jnp.concatenate(fq_cols, axis=0)
        bias_n = jnp.concatenate(bias_new, axis=0)
        causal_n = jnp.concatenate([tri] * fh, axis=0)

        s_parts = []
        for i in range(0, n_pages, 2):
            kp = jnp.concatenate(
                [jnp.concatenate([kpages[i + j][pl.ds(h, page, stride=fh), :] for h in range(fh)], axis=1)
                 for j in range(2)],
                axis=0).astype(BF16)
            s_parts.append(_nt_dot(qbd, kp))
        s_past = jnp.concatenate(s_parts, axis=1) + (fq_col - fk_rows)
        s_new = _nt_dot(qbd, kn_ref[...].astype(BF16)) + bias_n
        s_new = jnp.where(causal_n, s_new, NEG_INF)
        m = jnp.maximum(jnp.max(s_past, axis=1, keepdims=True), jnp.max(s_new, axis=1, keepdims=True))
        p_past = jnp.exp(s_past - m)
        p_new = jnp.exp(s_new - m)
        l_scr[...] = jnp.sum(p_past, axis=1, keepdims=True) + jnp.sum(p_new, axis=1, keepdims=True)
        p_scr[...] = p_past.astype(BF16)
        pn_scr[...] = p_new

    def values():
        acc = jnp.dot(pn_scr[...].astype(BF16), vn_ref[...].astype(BF16), preferred_element_type=F32)
        for i in range(0, n_pages, 2):
            vp = jnp.concatenate(
                [jnp.concatenate([vpages[i + j][pl.ds(h, page, stride=fh), :] for h in range(fh)], axis=1)
                 for j in range(2)],
                axis=0).astype(BF16)
            acc = acc + jnp.dot(p_scr[:, i * page:(i + 2) * page], vp, preferred_element_type=F32)
        acc = acc / l_scr[...]
        o_ref[...] = jnp.concatenate(
            [acc[h * ts:(h + 1) * ts, h * fd:(h + 1) * fd] for h in range(fh)], axis=1)

    scores()
    values()


def _fox_sample(page_table, fq, kb, vb, small, floc, cache_k, cache_v, *, fh, lf0):
    bs, n_pages = page_table.shape
    n, d = fq.shape
    ts = n // bs
    fd = d // fh
    page = cache_k.shape[1] // fh
    rows = fh * ts
    tok = lambda b, pt: (b, 0)

    def pmap(i):
        return lambda b, pt: (pt[b, i], 0, 0)

    in_specs = ([pl.BlockSpec((ts, d), tok), pl.BlockSpec((ts, d), tok), pl.BlockSpec((ts, d), tok),
                 pl.BlockSpec((ts, LANES), tok)]
                + [pl.BlockSpec((None, fh, page), pmap(i)) for i in range(n_pages)]
                + [pl.BlockSpec((None, page * fh, fd), pmap(i)) for i in range(n_pages)]
                + [pl.BlockSpec((None, page * fh, fd), pmap(i)) for i in range(n_pages)])
    kern = functools.partial(_fox_sample_kernel, n_pages=n_pages, page=page, fh=fh, fd=fd, ts=ts, lf0=lf0)
    return pl.pallas_call(
        kern,
        grid_spec=pltpu.PrefetchScalarGridSpec(
            num_scalar_prefetch=1, grid=(bs,), in_specs=in_specs,
            out_specs=pl.BlockSpec((ts, d), tok),
            scratch_shapes=[pltpu.VMEM((rows, n_pages * page), BF16),
                            pltpu.VMEM((rows, ts), F32),
                            pltpu.VMEM((rows, 1), F32)]),
        out_shape=jax.ShapeDtypeStruct((n, d), F32),
        compiler_params=_params(("arbitrary",)),
        name="fox_sample",
    )(page_table, fq, kb, vb, small, *([floc] * n_pages), *([cache_k] * n_pages), *([cache_v] * n_pages))


def _post_kernel(xp_ref, hap_ref, obp_ref, gp_ref, xs_ref, has_ref, obs_ref, gs_ref, *rest, tiles_p, **kw):
    cnt_ref = rest[-1]

    @pl.when(pl.program_id(0) == 0)
    def _():
        cnt_ref[...] = jnp.zeros_like(cnt_ref)

    @pl.when(pl.program_id(0) < tiles_p)
    def _():
        _post_body(xp_ref, hap_ref, obp_ref, gp_ref, *rest, **kw)

    @pl.when(pl.program_id(0) >= tiles_p)
    def _():
        _post_body(xs_ref, has_ref, obs_ref, gs_ref, *rest, **kw)


def _post_body(x_ref, ha_ref, ob_ref, gate_ref, wa_ref, wb_ref, wo_ref, g2_ref, wr_ref, br_ref,
               x2_ref, xn_ref, route_ref, cnt_ref, *, d, n_exp, n_groups):
    ba = jnp.dot(ha_ref[...].astype(BF16), wa_ref[...], preferred_element_type=F32)
    bb = jnp.dot(ob_ref[...].astype(BF16), wb_ref[...], preferred_element_type=F32)
    gates = gate_ref[...].astype(F32)
    merged = gates[:, 0:d] * ba + gates[:, d:2 * d] * bb
    x2 = x_ref[...] + jnp.dot(merged.astype(BF16), wo_ref[...], preferred_element_type=F32)
    x2_ref[...] = x2
    ms = jnp.mean(x2 * x2, axis=-1, keepdims=True)
    xn = x2 * lax.rsqrt(ms + RMS_EPS) * g2_ref[...]
    tm = xn.shape[0]
    for j in range(d // LANES):
        xn_ref[pl.ds(j, tm, stride=d // LANES), :] = xn[:, j * LANES:(j + 1) * LANES]

    xh = xn.astype(BF16)
    xl = (xn - xh.astype(F32)).astype(BF16)
    hh = jnp.dot(xh, wr_ref[...], preferred_element_type=F32)
    logits = (hh[:, 0:LANES] + hh[:, LANES:2 * LANES]
              + jnp.dot(xl, wr_ref[:, 0:LANES], preferred_element_type=F32)
              + br_ref[...])
    lane = lax.broadcasted_iota(I32, logits.shape, 1)
    lane_f = lane.astype(F32)
    big = float(LANES)
    epg = n_exp // n_groups
    in_groups = (lane >= n_exp) & (lane < n_exp + n_groups)
    gl = jnp.where(in_groups, logits, NEG_INF)
    gmax = jnp.max(gl, axis=1, keepdims=True)
    gidx = jnp.min(jnp.where(gl == gmax, lane_f, big), axis=1, keepdims=True) - float(n_exp)
    g_p = 1.0 / jnp.sum(jnp.exp(gl - gmax), axis=1, keepdims=True)
    in_group = (lane < n_exp) & ((lane // epg).astype(F32) == gidx)
    el = jnp.where(in_group, logits, NEG_INF)
    m1 = jnp.max(el, axis=1, keepdims=True)
    i1 = jnp.min(jnp.where(el == m1, lane_f, big), axis=1, keepdims=True)
    el2 = jnp.where(lane_f == i1, NEG_INF, el)
    m2 = jnp.max(el2, axis=1, keepdims=True)
    i2 = jnp.min(jnp.where(el2 == m2, lane_f, big), axis=1, keepdims=True)
    esum = jnp.sum(jnp.exp(el - m1), axis=1, keepdims=True)
    p1 = 1.0 / esum
    p2 = jnp.exp(m2 - m1) / esum
    psum = p1 + p2
    w1 = p1 / psum * g_p
    w2 = p2 / psum * g_p
    r0, r1 = _rank_block(i1, i2, lane_f, cnt_ref)
    route_ref[...] = jnp.where(lane == 0, i1, jnp.where(lane == 1, i2, jnp.where(
        lane == 2, w1, jnp.where(lane == 3, w2, jnp.where(lane == 4, r0, jnp.where(lane == 5, r1, 0.0))))))


def _post(acts_p, acts_s, wa, wb, wo, g2, wr, br, *, n_exp, n_groups, tm=256):
    n_p, d = acts_p[0].shape
    n_s = acts_s[0].shape[0]
    tiles_p, tiles_s = n_p // tm, n_s // tm
    n = n_p + n_s
    rpt = d // LANES
    row_p = lambda i: (jnp.minimum(i, tiles_p - 1), 0)
    row_s = lambda i: (jnp.maximum(i - tiles_p, 0), 0)
    row = lambda i: (i, 0)
    const = lambda i: (0, 0)
    kern = functools.partial(_post_kernel, tiles_p=tiles_p, d=d, n_exp=n_exp, n_groups=n_groups)
    wspec = lambda: pl.BlockSpec((d, d), const, pipeline_mode=pl.Buffered(1))
    act_specs = lambda r: [pl.BlockSpec((tm, d), r), pl.BlockSpec((tm, d), r), pl.BlockSpec((tm, d), r),
                           pl.BlockSpec((tm, 2 * d), r)]
    return pl.pallas_call(
        kern, grid=(tiles_p + tiles_s,),
        in_specs=act_specs(row_p) + act_specs(row_s) + [
            wspec(), wspec(), wspec(),
            pl.BlockSpec((1, d), const), pl.BlockSpec((d, 2 * LANES), const), pl.BlockSpec((1, LANES), const)],
        out_specs=[pl.BlockSpec((tm, d), row), pl.BlockSpec((tm * rpt, LANES), row),
                   pl.BlockSpec((tm, LANES), row), pl.BlockSpec((1, LANES), const)],
        out_shape=[jax.ShapeDtypeStruct((n, d), F32), jax.ShapeDtypeStruct((n * rpt, LANES), F32),
                   jax.ShapeDtypeStruct((n, LANES), F32), jax.ShapeDtypeStruct((1, LANES), F32)],
        compiler_params=_params(("arbitrary",)),
        name="post",
    )(*acts_p, *acts_s, wa, wb, wo, g2, wr, br)


def _dispatch_kernel(ps_ref, pl_ref, nu_ref, dest_ref, xn_ref, xs_hbm, stage, zbuf, sem_rows, sem_pad,
                     *, tm, n_tiles, n_exp, nt):
    i = pl.program_id(0)
    slot = i % 2

    def pad_dmas(act):
        for e in range(n_exp):
            pos = ps_ref[e]
            left = pl_ref[e]
            for c in PAD_CHUNKS:
                @pl.when((left & c) != 0)
                def _(pos=pos, c=c):
                    act(pltpu.make_async_copy(zbuf.at[pl.ds(0, c)], xs_hbm.at[pl.ds(pos, c)], sem_pad))
                pos = pos + (left & c)

        def unused_tile(t, carry):
            for part in range(tm // PAD_CHUNKS[0]):
                act(pltpu.make_async_copy(
                    zbuf, xs_hbm.at[pl.ds(t * tm + part * PAD_CHUNKS[0], PAD_CHUNKS[0])], sem_pad))
            return carry
        lax.fori_loop(nu_ref[0], nt, unused_tile, 0)

    @pl.when(i == 0)
    def _():
        zbuf[...] = jnp.zeros_like(zbuf)
        pad_dmas(lambda cp: cp.start())

    def wait_tile(s):
        for _ in range(2):
            pltpu.make_async_copy(stage.at[s], xs_hbm.at[pl.ds(0, tm)], sem_rows.at[s]).wait()

    @pl.when(i >= 2)
    def _():
        wait_tile(slot)

    stage[slot] = xn_ref[...]

    def body(j, c):
        for u in range(DMA_UNROLL // 2):
            r = j * (DMA_UNROLL // 2) + u
            for kk in range(2):
                pltpu.make_async_copy(stage.at[slot, r], xs_hbm.at[dest_ref[0, kk, r]],
                                      sem_rows.at[slot]).start(priority=kk)
        return c
    lax.fori_loop(0, tm // (DMA_UNROLL // 2), body, 0)

    @pl.when(i == n_tiles - 1)
    def _():
        if n_tiles > 1:
            wait_tile(1 - slot)
        wait_tile(slot)
        pad_dmas(lambda cp: cp.wait())


def _dispatch(pad_start, pad_len, n_used, dest_t, xn_rows, *, nt, tm):
    n_tiles = dest_t.shape[0]
    n_exp = pad_start.shape[0]
    rpt = xn_rows.shape[0] // (n_tiles * tm)
    xn3 = xn_rows.reshape(n_tiles * tm, rpt, LANES)
    kern = functools.partial(_dispatch_kernel, tm=tm, n_tiles=n_tiles, n_exp=n_exp, nt=nt)
    return pl.pallas_call(
        kern,
        grid_spec=pltpu.PrefetchScalarGridSpec(
            num_scalar_prefetch=3, grid=(n_tiles,),
            in_specs=[pl.BlockSpec((1, 8, tm), lambda i, ps, pln, nu: (i, 0, 0), memory_space=pltpu.SMEM),
                      pl.BlockSpec((tm, rpt, LANES), lambda i, ps, pln, nu: (i, 0, 0))],
            out_specs=pl.BlockSpec(memory_space=pl.ANY),
            scratch_shapes=[pltpu.VMEM((2, tm, rpt, LANES), F32),
                            pltpu.VMEM((PAD_CHUNKS[0], rpt, LANES), F32),
                            pltpu.SemaphoreType.DMA((2,)), pltpu.SemaphoreType.DMA(())]),
        out_shape=jax.ShapeDtypeStruct((nt * tm, rpt, LANES), F32),
        compiler_params=_params(("arbitrary",)),
        name="dispatch",
    )(pad_start, pad_len, n_used, dest_t, xn3)


def _experts_kernel(te_ref, nu_ref, x_ref, wg_ref, wu_ref, wd_ref, y_ref, wg_b, wu_b, wd_b, *, tm, rpt):
    t = pl.program_id(0)

    @pl.when((t == 0) | (te_ref[t] != te_ref[jnp.maximum(t - 1, 0)]))
    def _():
        wg_b[...] = wg_ref[0].astype(BF16)
        wu_b[...] = wu_ref[0].astype(BF16)
        wd_b[...] = wd_ref[0].astype(BF16)

    @pl.when(t < nu_ref[0])
    def _():
        x = jnp.concatenate([x_ref[pl.ds(j, tm, stride=rpt), :] for j in range(rpt)], axis=1).astype(BF16)
        g = jnp.dot(x, wg_b[...], preferred_element_type=F32)
        u = jnp.dot(x, wu_b[...], preferred_element_type=F32)
        hg = (g * _sigmoid(g)) * u
        y = jnp.dot(hg.astype(BF16), wd_b[...], preferred_element_type=F32)
        for j in range(rpt):
            y_ref[pl.ds(j, tm, stride=rpt), :] = y[:, j * LANES:(j + 1) * LANES]

    @pl.when(t >= nu_ref[0])
    def _():
        y_ref[...] = jnp.zeros_like(y_ref)


def _experts(tile_expert, n_used, x_sorted, w_gate, w_up, w_down, *, tm):
    nt = tile_expert.shape[0]
    n_exp, d, de = w_gate.shape
    rpt = d // LANES
    emap = lambda t, te, nu: (te[t], 0, 0)
    kern = functools.partial(_experts_kernel, tm=tm, rpt=rpt)
    return pl.pallas_call(
        kern,
        grid_spec=pltpu.PrefetchScalarGridSpec(
            num_scalar_prefetch=2, grid=(nt,),
            in_specs=[pl.BlockSpec((tm * rpt, LANES), lambda t, te, nu: (jnp.minimum(t, nu[0] - 1), 0)),
                      pl.BlockSpec((1, d, de), emap), pl.BlockSpec((1, d, de), emap),
                      pl.BlockSpec((1, de, d), emap)],
            out_specs=pl.BlockSpec((tm * rpt, LANES), lambda t, te, nu: (t, 0)),
            scratch_shapes=[pltpu.VMEM((d, de), BF16), pltpu.VMEM((d, de), BF16), pltpu.VMEM((de, d), BF16)]),
        out_shape=jax.ShapeDtypeStruct((nt * tm * rpt, LANES), F32),
        compiler_params=_params(("arbitrary",)),
        name="experts",
    )(tile_expert, n_used, x_sorted, w_gate, w_up, w_down)


def _combine_kernel(idx_cur, idx_nxt, x2_ref, route_ref, ys_hbm, y_ref, buf, sem, *, tm, n_tiles, rpt):
    i = pl.program_id(0)
    slot = i % 2

    def gather(idx_ref, dst_slot):
        def body(j, c):
            for u in range(DMA_UNROLL // 2):
                r = j * (DMA_UNROLL // 2) + u
                for kk in range(2):
                    src = pl.multiple_of(idx_ref[0, kk, r] * rpt, rpt)
                    pltpu.make_async_copy(ys_hbm.at[pl.ds(src, rpt), :],
                                          buf.at[dst_slot, kk, pl.ds(r * rpt, rpt), :],
                                          sem.at[dst_slot]).start(priority=kk)
            return c
        lax.fori_loop(0, tm // (DMA_UNROLL // 2), body, 0)

    @pl.when(i == 0)
    def _():
        gather(idx_cur, 0)

    @pl.when(i + 1 < n_tiles)
    def _():
        gather(idx_nxt, 1 - slot)

    for kk in range(2):
        pltpu.make_async_copy(ys_hbm.at[pl.ds(0, tm * rpt), :], buf.at[slot, kk], sem.at[slot]).wait()
    route = route_ref[...]
    w0 = route[:, 2:3]
    w1 = route[:, 3:4]
    for j in range(rpt):
        sl = slice(j * LANES, (j + 1) * LANES)
        y_ref[:, sl] = (x2_ref[:, sl] + w0 * buf[slot, 0, pl.ds(j, tm, stride=rpt), :]
                        + w1 * buf[slot, 1, pl.ds(j, tm, stride=rpt), :])


def _combine(dest_t, x2, route, y_sorted, *, row0, rows, tm=256):
    d = x2.shape[1]
    rpt = d // LANES
    t0 = row0 // tm
    n_tiles = rows // tm
    kern = functools.partial(_combine_kernel, tm=tm, n_tiles=n_tiles, rpt=rpt)
    return pl.pallas_call(
        kern, grid=(n_tiles,),
        in_specs=[pl.BlockSpec((1, 8, tm), lambda i: (t0 + i, 0, 0), memory_space=pltpu.SMEM),
                  pl.BlockSpec((1, 8, tm), lambda i: (t0 + jnp.minimum(i + 1, n_tiles - 1), 0, 0),
                               memory_space=pltpu.SMEM),
                  pl.BlockSpec((tm, d), lambda i: (t0 + i, 0)),
                  pl.BlockSpec((tm, LANES), lambda i: (t0 + i, 0)),
                  pl.BlockSpec(memory_space=pl.ANY)],
        out_specs=pl.BlockSpec((tm, d), lambda i: (i, 0)),
        out_shape=jax.ShapeDtypeStruct((rows, d), F32),
        scratch_shapes=[pltpu.VMEM((2, 2, tm * rpt, LANES), F32), pltpu.SemaphoreType.DMA((2,))],
        compiler_params=_params(("arbitrary",)),
        name="combine",
    )(dest_t, dest_t, x2, route, y_sorted)


def _rank_block(i1, i2, lane_f, cnt_ref):
    tm = i1.shape[0]
    oh0 = lane_f == i1
    oh1 = lane_f == i2
    oh = jnp.where(oh0, 1.0, jnp.where(oh1, 1.0, 0.0))
    ri = lax.broadcasted_iota(I32, (tm, tm), 0)
    ci = lax.broadcasted_iota(I32, (tm, tm), 1)
    earlier = jnp.where(ci < ri, 1.0, 0.0).astype(BF16)
    base = cnt_ref[...] + jnp.dot(earlier, oh.astype(BF16), preferred_element_type=F32)
    r0 = jnp.sum(jnp.where(oh0, base, 0.0), axis=1, keepdims=True)
    r1 = jnp.sum(jnp.where(oh1, base, 0.0), axis=1, keepdims=True)
    cnt_ref[...] += jnp.sum(oh, axis=0, keepdims=True)
    return r0, r1


def _dest_kernel(route_ref, start_ref, dest_ref, *, tm, group):
    start = start_ref[...]
    for t in range(group):
        route = route_ref[t * tm:(t + 1) * tm, :]
        lane = lax.broadcasted_iota(I32, route.shape, 1)
        lane_f = lane.astype(F32)
        d0 = route[:, 4:5] + jnp.sum(jnp.where(lane_f == route[:, 0:1], start, 0.0), axis=1, keepdims=True)
        d1 = route[:, 5:6] + jnp.sum(jnp.where(lane_f == route[:, 1:2], start, 0.0), axis=1, keepdims=True)
        dd = jnp.where(lane == 0, d0, jnp.where(lane == 1, d1, 0.0))
        dest_ref[t] = dd.T[0:8, :].astype(I32)


def _dest(route, start_row, *, tm):
    n = route.shape[0]
    tiles = n // tm
    group = 4 if tiles % 4 == 0 else 1
    return pl.pallas_call(
        functools.partial(_dest_kernel, tm=tm, group=group), grid=(tiles // group,),
        in_specs=[pl.BlockSpec((group * tm, LANES), lambda i: (i, 0)),
                  pl.BlockSpec((1, LANES), lambda i: (0, 0))],
        out_specs=pl.BlockSpec((group, 8, tm), lambda i: (i, 0, 0)),
        out_shape=jax.ShapeDtypeStruct((tiles, 8, tm), I32),
        compiler_params=_params(("parallel",)),
        name="dest",
    )(route, start_row)


def _moe_plan(route, cnt_f, *, n_exp, tm):
    n = route.shape[0]
    nt = (2 * n + n_exp * (tm - 1)) // tm
    cnt = cnt_f[0, :n_exp].astype(I32)
    ptiles = (cnt + tm - 1) // tm
    tile_end = jnp.cumsum(ptiles)
    tile_start = tile_end - ptiles
    start_row = jnp.pad((tile_start * tm).astype(F32), (0, LANES - n_exp)).reshape(1, LANES)
    dest_t = _dest(route, start_row, tm=tm)
    n_used = tile_end[-1]
    tiles = jnp.arange(nt, dtype=I32)
    te = jnp.minimum(jnp.sum((tile_end[None, :] <= tiles[:, None]).astype(I32), axis=1), n_exp - 1)
    te_last = jnp.take(te, jnp.maximum(n_used - 1, 0))
    tile_expert = jnp.where(tiles < n_used, te, te_last)
    pad_start = tile_start * tm + cnt
    pad_len = ptiles * tm - cnt
    return tile_expert, n_used.reshape(1).astype(I32), dest_t, pad_start, pad_len, nt


def kernel(x_prompt, x_sample, cache_k, cache_v, cache_lf, state_C, state_n, state_m, page_table,
           norm1_g, w_in, b_igate, b_fgate_mlstm, b_fgate_fox, mlstm_norm_g, q_norm_g, k_norm_g,
           w_branch_mlstm, w_branch_fox, w_out, norm2_g, w_router_group, b_router_group,
           w_router_expert, b_router_expert, w_gate, w_up, w_down):
    depth = w_in.shape[0]
    assert depth == 1, "single-layer step"
    bp, tp, d = x_prompt.shape
    bs, ts, _ = x_sample.shape
    mh = b_igate.shape[-1]
    fh = b_fgate_fox.shape[-1]
    fd = q_norm_g.shape[-1]
    md = d // mh
    n_exp = w_gate.shape[1]
    n_pages = page_table.shape[1]
    page = cache_k.shape[2]
    assert 2 * mh + fh <= LANES and n_exp + N_GROUPS <= LANES
    np_tok, ns_tok = bp * tp, bs * ts
    l = 0

    w = w_in[l]
    o = 0
    secs = {}
    for name, width in (("m", 4 * d), ("mi", mh), ("mf", mh), ("f", 3 * d), ("ff", fh), ("g", 2 * d)):
        secs[name] = w[:, o:o + width]
        o += width
    pad = jnp.zeros((d, LANES - 2 * mh - fh), F32)
    w_secs = (secs["m"].astype(BF16), secs["f"].astype(BF16), secs["g"].astype(BF16),
              jnp.concatenate([secs["mi"], secs["mf"], secs["ff"], pad], axis=1).astype(BF16))
    bias_s = jnp.concatenate([b_igate[l], b_fgate_mlstm[l], b_fgate_fox[l],
                              jnp.zeros((LANES - 2 * mh - fh,), F32)]).reshape(1, LANES)
    g1 = norm1_g[l].reshape(1, d)
    qg = q_norm_g[l].reshape(1, fd)
    kg = k_norm_g[l].reshape(1, fd)
    g_m = mlstm_norm_g[l].reshape(1, d)
    wa = w_branch_mlstm[l].astype(BF16)
    wb = w_branch_fox[l].astype(BF16)
    wo = w_out[l].astype(BF16)
    g2 = norm2_g[l].reshape(1, d)
    wr32 = jnp.concatenate([w_router_expert[l], w_router_group[l],
                            jnp.zeros((d, LANES - n_exp - N_GROUPS), F32)], axis=1)
    wr_hi = wr32.astype(BF16)
    wr = jnp.concatenate([wr_hi, (wr32 - wr_hi.astype(F32)).astype(BF16)], axis=1)
    br = jnp.concatenate([b_router_expert[l], b_router_group[l],
                          jnp.zeros((LANES - n_exp - N_GROUPS,), F32)]).reshape(1, LANES)

    inproj = functools.partial(_inproj, g1=g1, w_secs=w_secs, bias_s=bias_s, qg=qg, kg=kg, mh=mh, fh=fh)
    pm_p, fq_p, k3_p, v3_p, kb_p, vb_p, gate_p, small_p, kt_p = inproj(
        x_prompt.reshape(np_tok, d), act_dtype=BF16, q_unit=LOG2E, emit_kt=True)
    pm_s, fq_s, k3_s, v3_s, kb_s, vb_s, gate_s, small_s = inproj(x_sample.reshape(ns_tok, d), act_dtype=F32)

    ha_p, c_p, n_p, m_p = _mlstm_prompt(pm_p, kt_p, small_p, g_m, batch=bp, seq=tp, mh=mh)
    m0 = jnp.pad(state_m[l], ((0, 0), (0, LANES - mh))).reshape(bs, 1, LANES)
    ha_s, c_s, n_s, m_s = _mlstm(pm_s, small_s, g_m, batch=bs, seq=ts, mh=mh, nb=8,
                                 init=(state_C[l], state_n[l], m0), out_dtype=F32)

    lf_p = small_p[:, 2 * mh:2 * mh + fh]
    lf_s = small_s[:, 2 * mh:2 * mh + fh]
    qx_p, kx_p = _fbias(small_p, batch=bp, seq=tp, fh=fh, lf0=2 * mh)
    ob_p = _fox_prompt(fq_p, qx_p, kb_p, kx_p, vb_p, batch=bp, seq=tp, fh=fh)

    n_phys = cache_k.shape[1]
    lft_c = cache_lf[l].transpose(0, 2, 1).reshape(n_phys * fh, page)
    floc = _cumsum_lanes(lft_c, carry=False, block_rows=n_phys * fh // 8).reshape(n_phys, fh, page)
    ck = cache_k[l].reshape(n_phys, page * fh, fd)
    cv = cache_v[l].reshape(n_phys, page * fh, fd)
    ob_s = _fox_sample(page_table, fq_s, kb_s, vb_s, small_s, floc, ck, cv, fh=fh, lf0=2 * mh)

    tm = MOE_TILE
    x2, xn_rows, route, cnt_f = _post((x_prompt.reshape(np_tok, d), ha_p, ob_p, gate_p),
                                      (x_sample.reshape(ns_tok, d), ha_s, ob_s, gate_s),
                                      wa, wb, wo, g2, wr, br, n_exp=n_exp, n_groups=N_GROUPS, tm=tm)

    tile_expert, n_used, dest_t, pad_start, pad_len, nt = _moe_plan(route, cnt_f, n_exp=n_exp, tm=tm)
    x_sorted = _dispatch(pad_start, pad_len, n_used, dest_t, xn_rows, nt=nt, tm=tm)
    y_sorted = _experts(tile_expert, n_used, x_sorted.reshape(nt * tm * (d // LANES), LANES),
                        w_gate[l], w_up[l], w_down[l], tm=tm)
    y_p = _combine(dest_t, x2, route, y_sorted, row0=0, rows=np_tok, tm=tm)
    y_s = _combine(dest_t, x2, route, y_sorted, row0=np_tok, rows=ns_tok, tm=tm)

    return (y_p.reshape(bp, tp, d), y_s.reshape(bs, ts, d),
            k3_p.reshape(1, bp, tp, fh, fd), v3_p.reshape(1, bp, tp, fh, fd), lf_p.reshape(1, bp, tp, fh),
            k3_s.reshape(1, bs, ts, fh, fd), v3_s.reshape(1, bs, ts, fh, fd), lf_s.reshape(1, bs, ts, fh),
            c_p[None], n_p[None], m_p[:, 0, :mh][None],
            c_s[None], n_s[None], m_s[:, 0, :mh][None])
```

```python
import functools

import jax
import jax.numpy as jnp
import numpy as np
from jax import lax
from jax.experimental import pallas as pl
from jax.experimental.pallas import tpu as pltpu

F32 = jnp.float32
BF16 = jnp.bfloat16
I32 = jnp.int32
RMS_EPS = 1e-6
LANES = 128
MLSTM_CHUNK = 128
MLSTM_PROMPT_CHUNK = 256
N_GROUPS = 4
EXPERTS_PER_GROUP = 8
VMEM_LIMIT = 56 * 1024 * 1024
NEG_INF = float("-inf")
LOG2E = 1.4426950408889634
DMA_UNROLL = 8
MOE_TILE = 512
PAD_CHUNKS = (256, 128, 64, 32, 16, 8, 4, 2, 1)


def _params(sem, vmem=VMEM_LIMIT):
    return pltpu.CompilerParams(dimension_semantics=sem, vmem_limit_bytes=vmem)


def _log_sigmoid(x):
    return -(jnp.maximum(-x, 0.0) + jnp.log1p(jnp.exp(-jnp.abs(x))))


def _sigmoid(x):
    return 1.0 / (1.0 + jnp.exp(-x))


def _nt_dot(a, b):
    return lax.dot_general(a, b, (((1,), (1,)), ((), ())), preferred_element_type=F32)


def _tn_dot(a, b):
    return lax.dot_general(a, b, (((0,), (0,)), ((), ())), preferred_element_type=F32)


def _col_to_row(col, eye):
    return jnp.sum(jnp.where(eye, col, 0.0), axis=0, keepdims=True)


def _row_to_col(row, eye):
    return jnp.sum(jnp.where(eye, row, 0.0), axis=1, keepdims=True)


def _inproj_kernel(x_ref, g1_ref, wm_ref, wf_ref, wg_ref, ws_ref, bias_ref, qg_ref, kg_ref,
                   pm_ref, fq_ref, k3_ref, v3_ref, kb_ref, vb_ref, gate_ref, small_ref, *maybe_kt_ref,
                   d, mh, fh, fd, k_scale, q_scale):
    x = x_ref[...]
    ms = jnp.mean(x * x, axis=-1, keepdims=True)
    h = (x * lax.rsqrt(ms + RMS_EPS) * g1_ref[...]).astype(BF16)
    sections = ((0, wm_ref), (4 * d, wf_ref), (7 * d, wg_ref), (9 * d, ws_ref))

    def proj(c0, width):
        base, w_ref = [(b, r) for b, r in sections if b <= c0][-1]
        return jnp.dot(h, w_ref[:, c0 - base:c0 - base + width], preferred_element_type=F32)

    pm_ref[:, 0:d] = proj(0, d).astype(pm_ref.dtype)
    mk = proj(d, d) * k_scale
    pm_ref[:, d:2 * d] = mk.astype(pm_ref.dtype)
    if maybe_kt_ref:
        maybe_kt_ref[0][...] = mk.T.astype(maybe_kt_ref[0].dtype)
    pm_ref[:, 2 * d:3 * d] = proj(2 * d, d).astype(pm_ref.dtype)
    pm_ref[:, 3 * d:4 * d] = proj(3 * d, d).astype(pm_ref.dtype)

    def head_norm(a, g):
        ms_h = jnp.mean(a * a, axis=-1, keepdims=True)
        return a * lax.rsqrt(ms_h + RMS_EPS) * g

    fq = proj(4 * d, d)
    fk = proj(5 * d, d)
    fv = proj(6 * d, d)
    for hh in range(fh):
        sl = slice(hh * fd, (hh + 1) * fd)
        fq_ref[:, sl] = (head_norm(fq[:, sl], qg_ref[...]) * q_scale).astype(fq_ref.dtype)
        kn = head_norm(fk[:, sl], kg_ref[...])
        k3_ref[:, hh, :] = kn
        kb_ref[:, sl] = kn.astype(kb_ref.dtype)
        v3_ref[:, hh, :] = fv[:, sl]
    vb_ref[...] = fv.astype(vb_ref.dtype)

    gate_ref[...] = _sigmoid(proj(7 * d, 2 * d)).astype(gate_ref.dtype)

    sm = proj(9 * d, LANES) + bias_ref[...]
    lane = lax.broadcasted_iota(I32, sm.shape, 1)
    sm = jnp.where(lane < mh, sm, jnp.where(lane < 2 * mh + fh, _log_sigmoid(sm), 0.0))
    small_ref[...] = sm


def _inproj(x2d, g1, w_secs, bias_s, qg, kg, *, mh, fh, act_dtype, q_unit=1.0, emit_kt=False, tm=512):
    n, d = x2d.shape
    fd = d // fh
    md = d // mh
    kern = functools.partial(_inproj_kernel, d=d, mh=mh, fh=fh, fd=fd,
                             k_scale=md ** -0.5, q_scale=fd ** -0.5 * q_unit)
    row = lambda i: (i, 0)
    const = lambda i: (0, 0)
    kt_specs = [pl.BlockSpec((d, tm), lambda i: (0, i))] if emit_kt else []
    kt_shapes = [jax.ShapeDtypeStruct((d, n), act_dtype)] if emit_kt else []
    return pl.pallas_call(
        kern, grid=(n // tm,),
        in_specs=[pl.BlockSpec((tm, d), row),
                  pl.BlockSpec((1, d), const),
                  *[pl.BlockSpec(w.shape, const, pipeline_mode=pl.Buffered(1)) for w in w_secs],
                  pl.BlockSpec((1, LANES), const),
                  pl.BlockSpec((1, fd), const),
                  pl.BlockSpec((1, fd), const)],
        out_specs=[pl.BlockSpec((tm, 4 * d), row),
                   pl.BlockSpec((tm, d), row),
                   pl.BlockSpec((tm, fh, fd), lambda i: (i, 0, 0)),
                   pl.BlockSpec((tm, fh, fd), lambda i: (i, 0, 0)),
                   pl.BlockSpec((tm, d), row),
                   pl.BlockSpec((tm, d), row),
                   pl.BlockSpec((tm, 2 * d), row),
                   pl.BlockSpec((tm, LANES), row)] + kt_specs,
        out_shape=[jax.ShapeDtypeStruct((n, 4 * d), act_dtype),
                   jax.ShapeDtypeStruct((n, d), act_dtype),
                   jax.ShapeDtypeStruct((n, fh, fd), F32),
                   jax.ShapeDtypeStruct((n, fh, fd), F32),
                   jax.ShapeDtypeStruct((n, d), act_dtype),
                   jax.ShapeDtypeStruct((n, d), act_dtype),
                   jax.ShapeDtypeStruct((n, 2 * d), BF16),
                   jax.ShapeDtypeStruct((n, LANES), F32)] + kt_shapes,
        compiler_params=_params(("parallel",)),
        name="inproj",
    )(x2d, g1, *w_secs, bias_s, qg, kg)


def _mlstm_kernel(*refs, L, mh, md, nb, has_init):
    if has_init:
        (q_ref, k_ref, v_ref, o_ref, s_ref, g_ref, c0_ref, n0_ref, m0_ref,
         h_ref, c_ref, n_ref, m_ref) = refs
    else:
        q_ref, k_ref, v_ref, o_ref, s_ref, g_ref, h_ref, c_ref, n_ref, m_ref = refs

    @pl.when(pl.program_id(1) == 0)
    def _():
        if has_init:
            c_ref[...] = c0_ref[...]
            n_ref[...] = n0_ref[...]
            m_ref[...] = m0_ref[...]
        else:
            c_ref[...] = jnp.zeros_like(c_ref)
            n_ref[...] = jnp.zeros_like(n_ref)
            m_ref[...] = jnp.zeros_like(m_ref)

    ri = lax.broadcasted_iota(I32, (L, L), 0)
    ci = lax.broadcasted_iota(I32, (L, L), 1)
    tri = ci <= ri
    eye = ci == ri
    for bb in range(nb):
        s = s_ref[bb]
        m_all = m_ref[bb]
        lane = lax.broadcasted_iota(I32, m_all.shape, 1)
        m_out = m_all
        for hh in range(mh):
            sl = slice(hh * md, (hh + 1) * md)
            ig_col = s[:, hh:hh + 1]
            lf_col = s[:, mh + hh:mh + hh + 1]
            lf_row = _col_to_row(lf_col, eye)
            ig_row = _col_to_row(ig_col, eye)
            b_col = jnp.sum(jnp.where(tri, lf_row, 0.0), axis=1, keepdims=True)
            b_row = _col_to_row(b_col, eye)
            m_prev = m_all[:, hh:hh + 1]
            log_w = jnp.where(tri, b_col - b_row + ig_row, NEG_INF)
            log_inter = b_col + m_prev
            m_t = jnp.maximum(log_inter, jnp.max(log_w, axis=1, keepdims=True))
            w_intra = jnp.exp(log_w - m_t)
            w_inter = jnp.exp(log_inter - m_t)
            q = q_ref[bb, :, sl].astype(BF16)
            k = k_ref[bb, :, sl].astype(BF16)
            v = v_ref[bb, :, sl].astype(BF16)
            sm = _nt_dot(q, k) * w_intra
            c_prev = c_ref[bb, hh]
            n_prev = n_ref[bb, hh:hh + 1, :]
            num = (w_inter * jnp.dot(q, c_prev.astype(BF16), preferred_element_type=F32)
                   + jnp.dot(sm.astype(BF16), v, preferred_element_type=F32))
            den = (w_inter * jnp.sum(q.astype(F32) * n_prev, axis=1, keepdims=True)
                   + jnp.sum(sm, axis=1, keepdims=True))
            hv = num / jnp.maximum(jnp.abs(den), jnp.exp(-m_t))
            m_new = m_t[L - 1:L, :]
            b_last = b_col[L - 1:L, :]
            decay = jnp.exp(b_last + m_prev - m_new)
            w_rows = jnp.exp(b_last - b_col + ig_col - m_new)
            kw = k.astype(F32) * w_rows
            c_ref[bb, hh] = decay * c_prev + _tn_dot(kw.astype(BF16), v)
            n_ref[bb, hh:hh + 1, :] = decay * n_prev + jnp.sum(kw, axis=0, keepdims=True)
            m_out = jnp.where(lane == hh, m_new, m_out)
            ms = jnp.mean(hv * hv, axis=-1, keepdims=True)
            hn = hv * lax.rsqrt(ms + RMS_EPS) * g_ref[:, sl]
            h_ref[bb, :, sl] = (_sigmoid(o_ref[bb, :, sl].astype(F32)) * hn).astype(h_ref.dtype)
        m_ref[bb] = m_out


def _mlstm(pm, small, g_m, *, batch, seq, mh, nb=1, init=None, out_dtype=BF16):
    d = pm.shape[1] // 4
    md = d // mh
    L = MLSTM_CHUNK if seq % MLSTM_CHUNK == 0 else seq
    nc = seq // L
    ng = batch // nb
    pm4 = pm.reshape(ng, nb, seq, 4 * d)
    small4 = small.reshape(ng, nb, seq, LANES)
    sec = lambda j: (lambda b, c: (b, 0, c, j))
    in_specs = [pl.BlockSpec((None, nb, L, d), sec(0)), pl.BlockSpec((None, nb, L, d), sec(1)),
                pl.BlockSpec((None, nb, L, d), sec(2)), pl.BlockSpec((None, nb, L, d), sec(3)),
                pl.BlockSpec((None, nb, L, LANES), sec(0)),
                pl.BlockSpec((1, d), lambda b, c: (0, 0))]
    args = [pm4, pm4, pm4, pm4, small4, g_m]
    state_specs = [pl.BlockSpec((nb, mh, md, md), lambda b, c: (b, 0, 0, 0)),
                   pl.BlockSpec((nb, mh, md), lambda b, c: (b, 0, 0)),
                   pl.BlockSpec((nb, 1, LANES), lambda b, c: (b, 0, 0))]
    if init is not None:
        in_specs += state_specs
        args += list(init)
    kern = functools.partial(_mlstm_kernel, L=L, mh=mh, md=md, nb=nb, has_init=init is not None)
    h4, c_out, n_out, m_out = pl.pallas_call(
        kern, grid=(ng, nc),
        in_specs=in_specs,
        out_specs=[pl.BlockSpec((None, nb, L, d), lambda b, c: (b, 0, c, 0))] + state_specs,
        out_shape=[jax.ShapeDtypeStruct((ng, nb, seq, d), out_dtype),
                   jax.ShapeDtypeStruct((batch, mh, md, md), F32),
                   jax.ShapeDtypeStruct((batch, mh, md), F32),
                   jax.ShapeDtypeStruct((batch, 1, LANES), F32)],
        compiler_params=_params(("parallel", "arbitrary")),
        name="mlstm",
    )(*args)
    return h4.reshape(batch * seq, d), c_out, n_out, m_out


def _mlstm_prompt_kernel(q_ref, kt_ref, v_ref, o_ref, s_ref, g_ref, h_ref, c_ref, n_ref, m_ref, naug,
                         *, L, mh, md, nc):
    c = pl.program_id(1)

    @pl.when(c == 0)
    def _():
        c_ref[...] = jnp.zeros_like(c_ref)
        m_ref[...] = jnp.zeros_like(m_ref)
        naug[...] = jnp.zeros_like(naug)

    s = s_ref[...]
    ri = lax.broadcasted_iota(I32, (L, L), 0)
    ci = lax.broadcasted_iota(I32, (L, L), 1)
    tri = ci <= ri
    hp = lax.Precision.HIGHEST
    r = s.T[0:8, :]
    b_cols = jnp.dot(jnp.where(tri, 1.0, 0.0), s, precision=hp, preferred_element_type=F32)
    b_rows = pltpu.roll(jnp.dot(r, jnp.where(ri <= ci, 1.0, 0.0), precision=hp, preferred_element_type=F32),
                        8 - mh, axis=0)
    g = r - b_rows
    m_all = m_ref[0]
    lane = lax.broadcasted_iota(I32, m_all.shape, 1)
    m_out = m_all
    ones_v = jnp.ones((L, LANES), BF16)
    ones_r = jnp.ones((md, LANES), BF16)
    for hh in range(mh):
        sl = slice(hh * md, (hh + 1) * md)
        m_prev = m_all[:, hh:hh + 1]
        g_row = g[hh:hh + 1, :]
        b_col = b_cols[:, mh + hh:mh + hh + 1]
        a_col = jnp.maximum(m_prev, jnp.max(jnp.where(tri, g_row, NEG_INF), axis=1, keepdims=True))
        m_t = b_col + a_col
        m_new = m_t[L - 1:L, :]
        b_last = b_col[L - 1:L, :]
        w_rows = jnp.exp(b_last - b_rows[hh:hh + 1, :] + r[hh:hh + 1, :] - m_new)
        dc = jnp.exp(b_last + m_prev - m_new)
        m_out = jnp.where(lane == hh, m_new, m_out)
        q = q_ref[:, sl]
        kt = kt_ref[sl, :]
        v_aug = jnp.concatenate([v_ref[:, sl], ones_v], axis=1)
        w_intra = jnp.exp(jnp.where(tri, g_row - a_col, NEG_INF))
        sm = jnp.dot(q, kt, preferred_element_type=F32) * w_intra
        c_prev = c_ref[0, hh]
        n_prev = naug[hh]
        c_aug = jnp.concatenate([c_prev, n_prev], axis=1).astype(BF16)
        comb = (jnp.exp(m_prev - a_col) * jnp.dot(q, c_aug, preferred_element_type=F32)
                + jnp.dot(sm.astype(BF16), v_aug, preferred_element_type=F32))
        den = jnp.maximum(jnp.abs(comb[:, md:]), jnp.exp(-m_t))
        inv = 1.0 / den
        hv = comb[:, 0:md] * jnp.concatenate([inv] * (md // LANES), axis=1)
        ms = jnp.dot((hv * hv).astype(BF16), ones_r, preferred_element_type=F32) * (1.0 / md)
        rs = lax.rsqrt(ms + RMS_EPS)
        hn = hv * jnp.concatenate([rs] * (md // LANES), axis=1) * g_ref[:, sl]
        h_ref[:, sl] = (_sigmoid(o_ref[:, sl].astype(F32)) * hn).astype(h_ref.dtype)
        kw = (kt.astype(F32) * w_rows).astype(BF16)
        upd = jnp.dot(kw, v_aug, preferred_element_type=F32)
        c_ref[0, hh] = dc * c_prev + upd[:, 0:md]
        naug[hh] = dc * n_prev + upd[:, md:]
    m_ref[0] = m_out

    @pl.when(c == nc - 1)
    def _():
        ei = lax.broadcasted_iota(I32, (md, md), 0) == lax.broadcasted_iota(I32, (md, md), 1)
        for hh in range(mh):
            cols = jnp.concatenate([naug[hh]] * (md // LANES), axis=1)
            n_ref[0, hh:hh + 1, :] = jnp.sum(jnp.where(ei, cols, 0.0), axis=0, keepdims=True)


def _mlstm_prompt(pm, kt, small, g_m, *, batch, seq, mh):
    d = pm.shape[1] // 4
    md = d // mh
    L = MLSTM_PROMPT_CHUNK
    assert seq % L == 0
    nc = seq // L
    tok = lambda j: (lambda b, c: (b * nc + c, j))
    kern = functools.partial(_mlstm_prompt_kernel, L=L, mh=mh, md=md, nc=nc)
    return pl.pallas_call(
        kern, grid=(batch, nc),
        in_specs=[pl.BlockSpec((L, d), tok(0)),
                  pl.BlockSpec((d, L), lambda b, c: (0, b * nc + c)),
                  pl.BlockSpec((L, d), tok(2)), pl.BlockSpec((L, d), tok(3)),
                  pl.BlockSpec((L, LANES), tok(0)),
                  pl.BlockSpec((1, d), lambda b, c: (0, 0))],
        out_specs=[pl.BlockSpec((L, d), tok(0)),
                   pl.BlockSpec((1, mh, md, md), lambda b, c: (b, 0, 0, 0)),
                   pl.BlockSpec((1, mh, md), lambda b, c: (b, 0, 0)),
                   pl.BlockSpec((1, 1, LANES), lambda b, c: (b, 0, 0))],
        out_shape=[jax.ShapeDtypeStruct((batch * seq, d), BF16),
                   jax.ShapeDtypeStruct((batch, mh, md, md), F32),
                   jax.ShapeDtypeStruct((batch, mh, md), F32),
                   jax.ShapeDtypeStruct((batch, 1, LANES), F32)],
        scratch_shapes=[pltpu.VMEM((mh, md, LANES), F32)],
        compiler_params=_params(("parallel", "arbitrary")),
        name="mlstm_prompt",
    )(pm, kt, pm, pm, small, g_m)


def _cumsum_kernel(x_ref, o_ref, *, chunk, carry):
    rows, t = x_ref.shape
    si = lax.broadcasted_iota(I32, (chunk, chunk), 0)
    ti = lax.broadcasted_iota(I32, (chunk, chunk), 1)
    tri = (si <= ti).astype(F32)
    run = jnp.zeros((rows, 1), F32)
    for j in range(t // chunk):
        sl = slice(j * chunk, (j + 1) * chunk)
        loc = jnp.dot(x_ref[:, sl], tri, precision=lax.Precision.HIGHEST, preferred_element_type=F32)
        if carry:
            loc = loc + run
            run = loc[:, chunk - 1:chunk]
        o_ref[:, sl] = loc


def _cumsum_lanes(x, *, carry, block_rows):
    rows, t = x.shape
    kern = functools.partial(_cumsum_kernel, chunk=LANES, carry=carry)
    return pl.pallas_call(
        kern, grid=(rows // block_rows,),
        in_specs=[pl.BlockSpec((block_rows, t), lambda i: (i, 0))],
        out_specs=pl.BlockSpec((block_rows, t), lambda i: (i, 0)),
        out_shape=jax.ShapeDtypeStruct((rows, t), F32),
        compiler_params=_params(("parallel",)),
        name="cumsum",
    )(x)


BIAS_LANES = 8


def _fbias_kernel(s_ref, pq_ref, pk_ref, cq_ref, ck_ref, qx_ref, kx_ref, run_ref):
    @pl.when(pl.program_id(1) == 0)
    def _():
        run_ref[...] = jnp.zeros_like(run_ref)

    s = s_ref[...]
    L = s.shape[0]
    ti = lax.broadcasted_iota(I32, (L, L), 0)
    si = lax.broadcasted_iota(I32, (L, L), 1)
    tri = jnp.where(si <= ti, 1.0, 0.0).astype(BF16)

    def split3(x):
        hi = x.astype(BF16)
        r1 = x - hi.astype(F32)
        mid = r1.astype(BF16)
        return jnp.concatenate([hi, mid, (r1 - mid.astype(F32)).astype(BF16)], axis=1)

    cs = jnp.dot(tri, split3(s), preferred_element_type=F32)
    f_all = (cs[:, 0:LANES] + cs[:, LANES:2 * LANES] + cs[:, 2 * LANES:3 * LANES]) + run_ref[...]
    run_ref[...] = f_all[L - 1:L, :]
    terms = split3(f_all * LOG2E)
    qx_ref[...] = jnp.dot(terms, pq_ref[...], preferred_element_type=F32) + cq_ref[...]
    kx_ref[...] = (jnp.dot(terms, pk_ref[...], preferred_element_type=F32) + ck_ref[...]).astype(BF16)


def _fbias(small, *, batch, seq, fh, lf0):
    assert BIAS_LANES * fh <= LANES
    n = small.shape[0]
    L = 4 * LANES if seq % (4 * LANES) == 0 else LANES
    nc = seq // L
    pq = np.zeros((3 * LANES, LANES), np.float32)
    pk = np.zeros((3 * LANES, LANES), np.float32)
    cq = np.zeros((1, LANES), np.float32)
    ck = np.zeros((1, LANES), np.float32)
    for h in range(fh):
        for j in range(3):
            pq[j * LANES + lf0 + h, BIAS_LANES * h + j] = 1.0
            pk[j * LANES + lf0 + h, BIAS_LANES * h + 3 + j] = -1.0
            cq[0, BIAS_LANES * h + 3 + j] = 1.0
            ck[0, BIAS_LANES * h + j] = 1.0
    const = lambda b, c: (0, 0)
    tok = lambda b, c: (b * nc + c, 0)
    return pl.pallas_call(
        _fbias_kernel, grid=(batch, nc),
        in_specs=[pl.BlockSpec((L, LANES), tok),
                  pl.BlockSpec((3 * LANES, LANES), const), pl.BlockSpec((3 * LANES, LANES), const),
                  pl.BlockSpec((1, LANES), const), pl.BlockSpec((1, LANES), const)],
        out_specs=[pl.BlockSpec((L, LANES), tok), pl.BlockSpec((L, LANES), tok)],
        out_shape=[jax.ShapeDtypeStruct((n, LANES), F32), jax.ShapeDtypeStruct((n, LANES), BF16)],
        scratch_shapes=[pltpu.VMEM((1, LANES), F32)],
        compiler_params=_params(("parallel", "arbitrary")),
        name="fbias",
    )(small, jnp.asarray(pq, BF16), jnp.asarray(pk, BF16), jnp.asarray(cq), jnp.asarray(ck))


def _fox_prompt_kernel(q_ref, qx_ref, k_ref, kx_ref, v_ref, o_ref, *, seq, tq, gh, fd):
    nq = seq // tq
    ri = lax.broadcasted_iota(I32, (tq, tq), 0)
    ci = lax.broadcasted_iota(I32, (tq, tq), 1)
    causal = ci <= ri

    def softmax_step(carry, s, v):
        m, l, acc = carry
        m_new = jnp.maximum(m, jnp.max(s, axis=1, keepdims=True))
        a = jnp.exp2(m - m_new)
        p = jnp.exp2(s - m_new)
        l = a * l + jnp.sum(p, axis=1, keepdims=True)
        acc = a * acc + jnp.dot(p.astype(BF16), v, preferred_element_type=F32)
        return m_new, l, acc

    head0 = pl.program_id(1) * gh
    bias_head = lax.broadcasted_iota(I32, (tq, LANES), 1) // BIAS_LANES

    def q_body(qi, _):
        q0 = pl.multiple_of(qi * tq, tq)
        qx = qx_ref[pl.ds(q0, tq), :]
        qas = [jnp.concatenate([q_ref[pl.ds(q0, tq), g * fd:(g + 1) * fd],
                                jnp.where(bias_head == head0 + g, qx, 0.0).astype(BF16)], axis=1)
               for g in range(gh)]

        def block(g, carry_g, k0, masked):
            gs = slice(g * fd, (g + 1) * fd)
            ka = jnp.concatenate([k_ref[pl.ds(k0, tq), gs], kx_ref[pl.ds(k0, tq), :]], axis=1)
            s = _nt_dot(qas[g], ka)
            if masked:
                s = jnp.where(causal, s, NEG_INF)
            return softmax_step(carry_g, s, v_ref[pl.ds(k0, tq), gs])

        def kv_body(kj, carry):
            k0 = pl.multiple_of(kj * tq, tq)
            return tuple(block(g, carry[g], k0, False) for g in range(gh))

        init = tuple((jnp.full((tq, 1), NEG_INF, F32), jnp.zeros((tq, 1), F32), jnp.zeros((tq, fd), F32))
                     for _ in range(gh))
        carry = lax.fori_loop(0, qi, kv_body, init)
        for g in range(gh):
            _, l, acc = block(g, carry[g], q0, True)
            o_ref[pl.ds(q0, tq), g * fd:(g + 1) * fd] = (acc / l).astype(o_ref.dtype)
        return 0

    lax.fori_loop(0, nq, q_body, 0)


def _fox_prompt(fq, qx, kb, kx, vb, *, batch, seq, fh, tq=512, gh=2):
    n, d = fq.shape
    fd = d // fh
    blk = lambda b, h: (b, h)
    bias = pl.BlockSpec((seq, LANES), lambda b, h: (b, 0))
    kern = functools.partial(_fox_prompt_kernel, seq=seq, tq=tq, gh=gh, fd=fd)
    return pl.pallas_call(
        kern, grid=(batch, fh // gh),
        in_specs=[pl.BlockSpec((seq, gh * fd), blk), bias, pl.BlockSpec((seq, gh * fd), blk), bias,
                  pl.BlockSpec((seq, gh * fd), blk)],
        out_specs=pl.BlockSpec((seq, gh * fd), blk),
        out_shape=jax.ShapeDtypeStruct((n, d), BF16),
        compiler_params=_params(("parallel", "parallel")),
        name="fox_prompt",
    )(fq, qx, kb, kx, vb)


def _fox_sample_kernel(pt_ref, q_ref, kn_ref, vn_ref, s_ref, *rest, n_pages, page, fh, fd, ts, lf0):
    floc = rest[0:n_pages]
    kpages = rest[n_pages:2 * n_pages]
    vpages = rest[2 * n_pages:3 * n_pages]
    o_ref = rest[3 * n_pages]
    p_scr, pn_scr, l_scr = rest[3 * n_pages + 1:]
    d = fh * fd
    rows = fh * ts

    def scores():
        q = q_ref[...].astype(F32)
        qt = jnp.concatenate([q] * fh, axis=0)
        r_head = lax.broadcasted_iota(I32, (rows, d), 0) // ts
        c_head = lax.broadcasted_iota(I32, (rows, d), 1) // fd
        qbd = jnp.where(r_head == c_head, qt, 0.0).astype(BF16)

        off = jnp.zeros((fh, 1), F32)
        fk_pages = []
        for i in range(n_pages):
            fp = floc[i][...] + off
            fk_pages.append(fp)
            off = fp[:, page - 1:page]
        fk = jnp.concatenate(fk_pages, axis=1)
        fk_rows = jnp.concatenate(
            [jnp.broadcast_to(fk[h:h + 1, :], (ts, fk.shape[1])) for h in range(fh)], axis=0)

        sm = s_ref[...]
        ri = lax.broadcasted_iota(I32, (ts, ts), 0)
        ci = lax.broadcasted_iota(I32, (ts, ts), 1)
        eye = ci == ri
        tri = ci <= ri
        fq_cols, bias_new = [], []
        for h in range(fh):
            lf_col = sm[:, lf0 + h:lf0 + h + 1]
            lf_row = _col_to_row(lf_col, eye)
            cum_col = jnp.sum(jnp.where(tri, lf_row, 0.0), axis=1, keepdims=True)
            fq_h = off[h:h + 1, :] + cum_col
            fq_cols.append(fq_h)
            bias_new.append(fq_h - _col_to_row(fq_h, eye))
        fq_col = jnp.concatenate(fq_cols, axis=0)
        bias_n = jnp.concatenate(bias_new, axis=0)
        causal_n = jnp.concatenate([tri] * fh, axis=0)

        s_parts = []
        for i in range(0, n_pages, 2):
            kp = jnp.concatenate(
                [jnp.concatenate([kpages[i + j][pl.ds(h, page, stride=fh), :] for h in range(fh)], axis=1)
                 for j in range(2)],
                axis=0).astype(BF16)
            s_parts.append(_nt_dot(qbd, kp))
        s_past = jnp.concatenate(s_parts, axis=1) + (fq_col - fk_rows)
        s_new = _nt_dot(qbd, kn_ref[...].astype(BF16)) + bias_n
        s_new = jnp.where(causal_n, s_new, NEG_INF)
        m = jnp.maximum(jnp.max(s_past, axis=1, keepdims=True), jnp.max(s_new, axis=1, keepdims=True))
        p_past = jnp.exp(s_past - m)
        p_new = jnp.exp(s_new - m)
        l_scr[...] = jnp.sum(p_past, axis=1, keepdims=True) + jnp.sum(p_new, axis=1, keepdims=True)
        p_scr[...] = p_past.astype(BF16)
        pn_scr[...] = p_new

    def values():
        acc = jnp.dot(pn_scr[...].astype(BF16), vn_ref[...].astype(BF16), preferred_element_type=F32)
        for i in range(0, n_pages, 2):
            vp = jnp.concatenate(
                [jnp.concatenate([vpages[i + j][pl.ds(h, page, stride=fh), :] for h in range(fh)], axis=1)
                 for j in range(2)],
                axis=0).astype(BF16)
            acc = acc + jnp.dot(p_scr[:, i * page:(i + 2) * page], vp, preferred_element_type=F32)
        acc = acc / l_scr[...]
        o_ref[...] = jnp.concatenate(
            [acc[h * ts:(h + 1) * ts, h * fd:(h + 1) * fd] for h in range(fh)], axis=1)

    scores()
    values()


def _fox_sample(page_table, fq, kb, vb, small, floc, cache_k, cache_v, *, fh, lf0):
    bs, n_pages = page_table.shape
    n, d = fq.shape
    ts = n // bs
    fd = d // fh
    page = cache_k.shape[1] // fh
    rows = fh * ts
    tok = lambda b, pt: (b, 0)

    def pmap(i):
        return lambda b, pt: (pt[b, i], 0, 0)

    in_specs = ([pl.BlockSpec((ts, d), tok), pl.BlockSpec((ts, d), tok), pl.BlockSpec((ts, d), tok),
                 pl.BlockSpec((ts, LANES), tok)]
                + [pl.BlockSpec((None, fh, page), pmap(i)) for i in range(n_pages)]
                + [pl.BlockSpec((None, page * fh, fd), pmap(i)) for i in range(n_pages)]
                + [pl.BlockSpec((None, page * fh, fd), pmap(i)) for i in range(n_pages)])
    kern = functools.partial(_fox_sample_kernel, n_pages=n_pages, page=page, fh=fh, fd=fd, ts=ts, lf0=lf0)
    return pl.pallas_call(
        kern,
        grid_spec=pltpu.PrefetchScalarGridSpec(
            num_scalar_prefetch=1, grid=(bs,), in_specs=in_specs,
            out_specs=pl.BlockSpec((ts, d), tok),
            scratch_shapes=[pltpu.VMEM((rows, n_pages * page), BF16),
                            pltpu.VMEM((rows, ts), F32),
                            pltpu.VMEM((rows, 1), F32)]),
        out_shape=jax.ShapeDtypeStruct((n, d), F32),
        compiler_params=_params(("arbitrary",)),
        name="fox_sample",
    )(page_table, fq, kb, vb, small, *([floc] * n_pages), *([cache_k] * n_pages), *([cache_v] * n_pages))


def _post_kernel(xp_ref, hap_ref, obp_ref, gp_ref, xs_ref, has_ref, obs_ref, gs_ref, *rest, tiles_p, **kw):
    cnt_ref = rest[-1]

    @pl.when(pl.program_id(0) == 0)
    def _():
        cnt_ref[...] = jnp.zeros_like(cnt_ref)

    @pl.when(pl.program_id(0) < tiles_p)
    def _():
        _post_body(xp_ref, hap_ref, obp_ref, gp_ref, *rest, **kw)

    @pl.when(pl.program_id(0) >= tiles_p)
    def _():
        _post_body(xs_ref, has_ref, obs_ref, gs_ref, *rest, **kw)


def _post_body(x_ref, ha_ref, ob_ref, gate_ref, wa_ref, wb_ref, wo_ref, g2_ref, wr_ref, br_ref,
               x2_ref, xn_ref, route_ref, cnt_ref, *, d, n_exp, n_groups):
    ba = jnp.dot(ha_ref[...].astype(BF16), wa_ref[...], preferred_element_type=F32)
    bb = jnp.dot(ob_ref[...].astype(BF16), wb_ref[...], preferred_element_type=F32)
    gates = gate_ref[...].astype(F32)
    merged = gates[:, 0:d] * ba + gates[:, d:2 * d] * bb
    x2 = x_ref[...] + jnp.dot(merged.astype(BF16), wo_ref[...], preferred_element_type=F32)
    x2_ref[...] = x2
    ms = jnp.mean(x2 * x2, axis=-1, keepdims=True)
    xn = x2 * lax.rsqrt(ms + RMS_EPS) * g2_ref[...]
    tm = xn.shape[0]
    for j in range(d // LANES):
        xn_ref[pl.ds(j, tm, stride=d // LANES), :] = xn[:, j * LANES:(j + 1) * LANES]

    xh = xn.astype(BF16)
    xl = (xn - xh.astype(F32)).astype(BF16)
    hh = jnp.dot(xh, wr_ref[...], preferred_element_type=F32)
    logits = (hh[:, 0:LANES] + hh[:, LANES:2 * LANES]
              + jnp.dot(xl, wr_ref[:, 0:LANES], preferred_element_type=F32)
              + br_ref[...])
    lane = lax.broadcasted_iota(I32, logits.shape, 1)
    lane_f = lane.astype(F32)
    big = float(LANES)
    epg = n_exp // n_groups
    in_groups = (lane >= n_exp) & (lane < n_exp + n_groups)
    gl = jnp.where(in_groups, logits, NEG_INF)
    gmax = jnp.max(gl, axis=1, keepdims=True)
    gidx = jnp.min(jnp.where(gl == gmax, lane_f, big), axis=1, keepdims=True) - float(n_exp)
    g_p = 1.0 / jnp.sum(jnp.exp(gl - gmax), axis=1, keepdims=True)
    in_group = (lane < n_exp) & ((lane // epg).astype(F32) == gidx)
    el = jnp.where(in_group, logits, NEG_INF)
    m1 = jnp.max(el, axis=1, keepdims=True)
    i1 = jnp.min(jnp.where(el == m1, lane_f, big), axis=1, keepdims=True)
    el2 = jnp.where(lane_f == i1, NEG_INF, el)
    m2 = jnp.max(el2, axis=1, keepdims=True)
    i2 = jnp.min(jnp.where(el2 == m2, lane_f, big), axis=1, keepdims=True)
    esum = jnp.sum(jnp.exp(el - m1), axis=1, keepdims=True)
    p1 = 1.0 / esum
    p2 = jnp.exp(m2 - m1) / esum
    psum = p1 + p2
    w1 = p1 / psum * g_p
    w2 = p2 / psum * g_p
    r0, r1 = _rank_block(i1, i2, lane_f, cnt_ref)
    route_ref[...] = jnp.where(lane == 0, i1, jnp.where(lane == 1, i2, jnp.where(
        lane == 2, w1, jnp.where(lane == 3, w2, jnp.where(lane == 4, r0, jnp.where(lane == 5, r1, 0.0))))))


def _post(acts_p, acts_s, wa, wb, wo, g2, wr, br, *, n_exp, n_groups, tm=256):
    n_p, d = acts_p[0].shape
    n_s = acts_s[0].shape[0]
    tiles_p, tiles_s = n_p // tm, n_s // tm
    n = n_p + n_s
    rpt = d // LANES
    row_p = lambda i: (jnp.minimum(i, tiles_p - 1), 0)
    row_s = lambda i: (jnp.maximum(i - tiles_p, 0), 0)
    row = lambda i: (i, 0)
    const = lambda i: (0, 0)
    kern = functools.partial(_post_kernel, tiles_p=tiles_p, d=d, n_exp=n_exp, n_groups=n_groups)
    wspec = lambda: pl.BlockSpec((d, d), const, pipeline_mode=pl.Buffered(1))
    act_specs = lambda r: [pl.BlockSpec((tm, d), r), pl.BlockSpec((tm, d), r), pl.BlockSpec((tm, d), r),
                           pl.BlockSpec((tm, 2 * d), r)]
    return pl.pallas_call(
        kern, grid=(tiles_p + tiles_s,),
        in_specs=act_specs(row_p) + act_specs(row_s) + [
            wspec(), wspec(), wspec(),
            pl.BlockSpec((1, d), const), pl.BlockSpec((d, 2 * LANES), const), pl.BlockSpec((1, LANES), const)],
        out_specs=[pl.BlockSpec((tm, d), row), pl.BlockSpec((tm * rpt, LANES), row),
                   pl.BlockSpec((tm, LANES), row), pl.BlockSpec((1, LANES), const)],
        out_shape=[jax.ShapeDtypeStruct((n, d), F32), jax.ShapeDtypeStruct((n * rpt, LANES), F32),
                   jax.ShapeDtypeStruct((n, LANES), F32), jax.ShapeDtypeStruct((1, LANES), F32)],
        compiler_params=_params(("arbitrary",)),
        name="post",
    )(*acts_p, *acts_s, wa, wb, wo, g2, wr, br)


def _dispatch_kernel(ps_ref, pl_ref, nu_ref, dest_ref, xn_ref, xs_hbm, stage, zbuf, sem_rows, sem_pad,
                     *, tm, n_tiles, n_exp, nt):
    i = pl.program_id(0)
    slot = i % 2

    def pad_dmas(act):
        for e in range(n_exp):
            pos = ps_ref[e]
            left = pl_ref[e]
            for c in PAD_CHUNKS:
                @pl.when((left & c) != 0)
                def _(pos=pos, c=c):
                    act(pltpu.make_async_copy(zbuf.at[pl.ds(0, c)], xs_hbm.at[pl.ds(pos, c)], sem_pad))
                pos = pos + (left & c)

        def unused_tile(t, carry):
            for part in range(tm // PAD_CHUNKS[0]):
                act(pltpu.make_async_copy(
                    zbuf, xs_hbm.at[pl.ds(t * tm + part * PAD_CHUNKS[0], PAD_CHUNKS[0])], sem_pad))
            return carry
        lax.fori_loop(nu_ref[0], nt, unused_tile, 0)

    @pl.when(i == 0)
    def _():
        zbuf[...] = jnp.zeros_like(zbuf)
        pad_dmas(lambda cp: cp.start())

    def wait_tile(s):
        for _ in range(2):
            pltpu.make_async_copy(stage.at[s], xs_hbm.at[pl.ds(0, tm)], sem_rows.at[s]).wait()

    @pl.when(i >= 2)
    def _():
        wait_tile(slot)

    stage[slot] = xn_ref[...]

    def body(j, c):
        for u in range(DMA_UNROLL // 2):
            r = j * (DMA_UNROLL // 2) + u
            for kk in range(2):
                pltpu.make_async_copy(stage.at[slot, r], xs_hbm.at[dest_ref[0, kk, r]],
                                      sem_rows.at[slot]).start(priority=kk)
        return c
    lax.fori_loop(0, tm // (DMA_UNROLL // 2), body, 0)

    @pl.when(i == n_tiles - 1)
    def _():
        if n_tiles > 1:
            wait_tile(1 - slot)
        wait_tile(slot)
        pad_dmas(lambda cp: cp.wait())


def _dispatch(pad_start, pad_len, n_used, dest_t, xn_rows, *, nt, tm):
    n_tiles = dest_t.shape[0]
    n_exp = pad_start.shape[0]
    rpt = xn_rows.shape[0] // (n_tiles * tm)
    xn3 = xn_rows.reshape(n_tiles * tm, rpt, LANES)
    kern = functools.partial(_dispatch_kernel, tm=tm, n_tiles=n_tiles, n_exp=n_exp, nt=nt)
    return pl.pallas_call(
        kern,
        grid_spec=pltpu.PrefetchScalarGridSpec(
            num_scalar_prefetch=3, grid=(n_tiles,),
            in_specs=[pl.BlockSpec((1, 8, tm), lambda i, ps, pln, nu: (i, 0, 0), memory_space=pltpu.SMEM),
                      pl.BlockSpec((tm, rpt, LANES), lambda i, ps, pln, nu: (i, 0, 0))],
            out_specs=pl.BlockSpec(memory_space=pl.ANY),
            scratch_shapes=[pltpu.VMEM((2, tm, rpt, LANES), F32),
                            pltpu.VMEM((PAD_CHUNKS[0], rpt, LANES), F32),
                            pltpu.SemaphoreType.DMA((2,)), pltpu.SemaphoreType.DMA(())]),
        out_shape=jax.ShapeDtypeStruct((nt * tm, rpt, LANES), F32),
        compiler_params=_params(("arbitrary",)),
        name="dispatch",
    )(pad_start, pad_len, n_used, dest_t, xn3)


def _experts_kernel(te_ref, nu_ref, x_ref, wg_ref, wu_ref, wd_ref, y_ref, wg_b, wu_b, wd_b, *, tm, rpt):
    t = pl.program_id(0)

    @pl.when((t == 0) | (te_ref[t] != te_ref[jnp.maximum(t - 1, 0)]))
    def _():
        wg_b[...] = wg_ref[0].astype(BF16)
        wu_b[...] = wu_ref[0].astype(BF16)
        wd_b[...] = wd_ref[0].astype(BF16)

    @pl.when(t < nu_ref[0])
    def _():
        x = jnp.concatenate([x_ref[pl.ds(j, tm, stride=rpt), :] for j in range(rpt)], axis=1).astype(BF16)
        g = jnp.dot(x, wg_b[...], preferred_element_type=F32)
        u = jnp.dot(x, wu_b[...], preferred_element_type=F32)
        hg = (g * _sigmoid(g)) * u
        y = jnp.dot(hg.astype(BF16), wd_b[...], preferred_element_type=F32)
        for j in range(rpt):
            y_ref[pl.ds(j, tm, stride=rpt), :] = y[:, j * LANES:(j + 1) * LANES]

    @pl.when(t >= nu_ref[0])
    def _():
        y_ref[...] = jnp.zeros_like(y_ref)


def _experts(tile_expert, n_used, x_sorted, w_gate, w_up, w_down, *, tm):
    nt = tile_expert.shape[0]
    n_exp, d, de = w_gate.shape
    rpt = d // LANES
    emap = lambda t, te, nu: (te[t], 0, 0)
    kern = functools.partial(_experts_kernel, tm=tm, rpt=rpt)
    return pl.pallas_call(
        kern,
        grid_spec=pltpu.PrefetchScalarGridSpec(
            num_scalar_prefetch=2, grid=(nt,),
            in_specs=[pl.BlockSpec((tm * rpt, LANES), lambda t, te, nu: (jnp.minimum(t, nu[0] - 1), 0)),
                      pl.BlockSpec((1, d, de), emap), pl.BlockSpec((1, d, de), emap),
                      pl.BlockSpec((1, de, d), emap)],
            out_specs=pl.BlockSpec((tm * rpt, LANES), lambda t, te, nu: (t, 0)),
            scratch_shapes=[pltpu.VMEM((d, de), BF16), pltpu.VMEM((d, de), BF16), pltpu.VMEM((de, d), BF16)]),
        out_shape=jax.ShapeDtypeStruct((nt * tm * rpt, LANES), F32),
        compiler_params=_params(("arbitrary",)),
        name="experts",
    )(tile_expert, n_used, x_sorted, w_gate, w_up, w_down)


def _combine_kernel(idx_cur, idx_nxt, x2_ref, route_ref, ys_hbm, y_ref, buf, sem, *, tm, n_tiles, rpt):
    i = pl.program_id(0)
    slot = i % 2

    def gather(idx_ref, dst_slot):
        def body(j, c):
            for u in range(DMA_UNROLL // 2):
                r = j * (DMA_UNROLL // 2) + u
                for kk in range(2):
                    src = pl.multiple_of(idx_ref[0, kk, r] * rpt, rpt)
                    pltpu.make_async_copy(ys_hbm.at[pl.ds(src, rpt), :],
                                          buf.at[dst_slot, kk, pl.ds(r * rpt, rpt), :],
                                          sem.at[dst_slot]).start(priority=kk)
            return c
        lax.fori_loop(0, tm // (DMA_UNROLL // 2), body, 0)

    @pl.when(i == 0)
    def _():
        gather(idx_cur, 0)

    @pl.when(i + 1 < n_tiles)
    def _():
        gather(idx_nxt, 1 - slot)

    for kk in range(2):
        pltpu.make_async_copy(ys_hbm.at[pl.ds(0, tm * rpt), :], buf.at[slot, kk], sem.at[slot]).wait()
    route = route_ref[...]
    w0 = route[:, 2:3]
    w1 = route[:, 3:4]
    for j in range(rpt):
        sl = slice(j * LANES, (j + 1) * LANES)
        y_ref[:, sl] = (x2_ref[:, sl] + w0 * buf[slot, 0, pl.ds(j, tm, stride=rpt), :]
                        + w1 * buf[slot, 1, pl.ds(j, tm, stride=rpt), :])


def _combine(dest_t, x2, route, y_sorted, *, row0, rows, tm=256):
    d = x2.shape[1]
    rpt = d // LANES
    t0 = row0 // tm
    n_tiles = rows // tm
    kern = functools.partial(_combine_kernel, tm=tm, n_tiles=n_tiles, rpt=rpt)
    return pl.pallas_call(
        kern, grid=(n_tiles,),
        in_specs=[pl.BlockSpec((1, 8, tm), lambda i: (t0 + i, 0, 0), memory_space=pltpu.SMEM),
                  pl.BlockSpec((1, 8, tm), lambda i: (t0 + jnp.minimum(i + 1, n_tiles - 1), 0, 0),
                               memory_space=pltpu.SMEM),
                  pl.BlockSpec((tm, d), lambda i: (t0 + i, 0)),
                  pl.BlockSpec((tm, LANES), lambda i: (t0 + i, 0)),
                  pl.BlockSpec(memory_space=pl.ANY)],
        out_specs=pl.BlockSpec((tm, d), lambda i: (i, 0)),
        out_shape=jax.ShapeDtypeStruct((rows, d), F32),
        scratch_shapes=[pltpu.VMEM((2, 2, tm * rpt, LANES), F32), pltpu.SemaphoreType.DMA((2,))],
        compiler_params=_params(("arbitrary",)),
        name="combine",
    )(dest_t, dest_t, x2, route, y_sorted)


def _rank_block(i1, i2, lane_f, cnt_ref):
    tm = i1.shape[0]
    oh0 = lane_f == i1
    oh1 = lane_f == i2
    oh = jnp.where(oh0, 1.0, jnp.where(oh1, 1.0, 0.0))
    ri = lax.broadcasted_iota(I32, (tm, tm), 0)
    ci = lax.broadcasted_iota(I32, (tm, tm), 1)
    earlier = jnp.where(ci < ri, 1.0, 0.0).astype(BF16)
    base = cnt_ref[...] + jnp.dot(earlier, oh.astype(BF16), preferred_element_type=F32)
    r0 = jnp.sum(jnp.where(oh0, base, 0.0), axis=1, keepdims=True)
    r1 = jnp.sum(jnp.where(oh1, base, 0.0), axis=1, keepdims=True)
    cnt_ref[...] += jnp.sum(oh, axis=0, keepdims=True)
    return r0, r1


def _dest_kernel(route_ref, start_ref, dest_ref, *, tm, group):
    start = start_ref[...]
    for t in range(group):
        route = route_ref[t * tm:(t + 1) * tm, :]
        lane = lax.broadcasted_iota(I32, route.shape, 1)
        lane_f = lane.astype(F32)
        d0 = route[:, 4:5] + jnp.sum(jnp.where(lane_f == route[:, 0:1], start, 0.0), axis=1, keepdims=True)
        d1 = route[:, 5:6] + jnp.sum(jnp.where(lane_f == route[:, 1:2], start, 0.0), axis=1, keepdims=True)
        dd = jnp.where(lane == 0, d0, jnp.where(lane == 1, d1, 0.0))
        dest_ref[t] = dd.T[0:8, :].astype(I32)


def _dest(route, start_row, *, tm):
    n = route.shape[0]
    tiles = n // tm
    group = 4 if tiles % 4 == 0 else 1
    return pl.pallas_call(
        functools.partial(_dest_kernel, tm=tm, group=group), grid=(tiles // group,),
        in_specs=[pl.BlockSpec((group * tm, LANES), lambda i: (i, 0)),
                  pl.BlockSpec((1, LANES), lambda i: (0, 0))],
        out_specs=pl.BlockSpec((group, 8, tm), lambda i: (i, 0, 0)),
        out_shape=jax.ShapeDtypeStruct((tiles, 8, tm), I32),
        compiler_params=_params(("parallel",)),
        name="dest",
    )(route, start_row)


def _moe_plan(route, cnt_f, *, n_exp, tm):
    n = route.shape[0]
    nt = (2 * n + n_exp * (tm - 1)) // tm
    cnt = cnt_f[0, :n_exp].astype(I32)
    ptiles = (cnt + tm - 1) // tm
    tile_end = jnp.cumsum(ptiles)
    tile_start = tile_end - ptiles
    start_row = jnp.pad((tile_start * tm).astype(F32), (0, LANES - n_exp)).reshape(1, LANES)
    dest_t = _dest(route, start_row, tm=tm)
    n_used = tile_end[-1]
    tiles = jnp.arange(nt, dtype=I32)
    te = jnp.minimum(jnp.sum((tile_end[None, :] <= tiles[:, None]).astype(I32), axis=1), n_exp - 1)
    te_last = jnp.take(te, jnp.maximum(n_used - 1, 0))
    tile_expert = jnp.where(tiles < n_used, te, te_last)
    pad_start = tile_start * tm + cnt
    pad_len = ptiles * tm - cnt
    return tile_expert, n_used.reshape(1).astype(I32), dest_t, pad_start, pad_len, nt


def kernel(x_prompt, x_sample, cache_k, cache_v, cache_lf, state_C, state_n, state_m, page_table,
           norm1_g, w_in, b_igate, b_fgate_mlstm, b_fgate_fox, mlstm_norm_g, q_norm_g, k_norm_g,
           w_branch_mlstm, w_branch_fox, w_out, norm2_g, w_router_group, b_router_group,
           w_router_expert, b_router_expert, w_gate, w_up, w_down):
    depth = w_in.shape[0]
    assert depth == 1, "single-layer step"
    bp, tp, d = x_prompt.shape
    bs, ts, _ = x_sample.shape
    mh = b_igate.shape[-1]
    fh = b_fgate_fox.shape[-1]
    fd = q_norm_g.shape[-1]
    md = d // mh
    n_exp = w_gate.shape[1]
    n_pages = page_table.shape[1]
    page = cache_k.shape[2]
    assert 2 * mh + fh <= LANES and n_exp + N_GROUPS <= LANES
    np_tok, ns_tok = bp * tp, bs * ts
    l = 0

    w = w_in[l]
    o = 0
    secs = {}
    for name, width in (("m", 4 * d), ("mi", mh), ("mf", mh), ("f", 3 * d), ("ff", fh), ("g", 2 * d)):
        secs[name] = w[:, o:o + width]
        o += width
    pad = jnp.zeros((d, LANES - 2 * mh - fh), F32)
    w_secs = (secs["m"].astype(BF16), secs["f"].astype(BF16), secs["g"].astype(BF16),
              jnp.concatenate([secs["mi"], secs["mf"], secs["ff"], pad], axis=1).astype(BF16))
    bias_s = jnp.concatenate([b_igate[l], b_fgate_mlstm[l], b_fgate_fox[l],
                              jnp.zeros((LANES - 2 * mh - fh,), F32)]).reshape(1, LANES)
    g1 = norm1_g[l].reshape(1, d)
    qg = q_norm_g[l].reshape(1, fd)
    kg = k_norm_g[l].reshape(1, fd)
    g_m = mlstm_norm_g[l].reshape(1, d)
    wa = w_branch_mlstm[l].astype(BF16)
    wb = w_branch_fox[l].astype(BF16)
    wo = w_out[l].astype(BF16)
    g2 = norm2_g[l].reshape(1, d)
    wr32 = jnp.concatenate([w_router_expert[l], w_router_group[l],
                            jnp.zeros((d, LANES - n_exp - N_GROUPS), F32)], axis=1)
    wr_hi = wr32.astype(BF16)
    wr = jnp.concatenate([wr_hi, (wr32 - wr_hi.astype(F32)).astype(BF16)], axis=1)
    br = jnp.concatenate([b_router_expert[l], b_router_group[l],
                          jnp.zeros((LANES - n_exp - N_GROUPS,), F32)]).reshape(1, LANES)

    inproj = functools.partial(_inproj, g1=g1, w_secs=w_secs, bias_s=bias_s, qg=qg, kg=kg, mh=mh, fh=fh)
    pm_p, fq_p, k3_p, v3_p, kb_p, vb_p, gate_p, small_p, kt_p = inproj(
        x_prompt.reshape(np_tok, d), act_dtype=BF16, q_unit=LOG2E, emit_kt=True)
    pm_s, fq_s, k3_s, v3_s, kb_s, vb_s, gate_s, small_s = inproj(x_sample.reshape(ns_tok, d), act_dtype=F32,
                                                                 tm=256)

    ha_p, c_p, n_p, m_p = _mlstm_prompt(pm_p, kt_p, small_p, g_m, batch=bp, seq=tp, mh=mh)
    m0 = jnp.pad(state_m[l], ((0, 0), (0, LANES - mh))).reshape(bs, 1, LANES)
    ha_s, c_s, n_s, m_s = _mlstm(pm_s, small_s, g_m, batch=bs, seq=ts, mh=mh, nb=8,
                                 init=(state_C[l], state_n[l], m0), out_dtype=F32)

    lf_p = small_p[:, 2 * mh:2 * mh + fh]
    lf_s = small_s[:, 2 * mh:2 * mh + fh]
    qx_p, kx_p = _fbias(small_p, batch=bp, seq=tp, fh=fh, lf0=2 * mh)
    ob_p = _fox_prompt(fq_p, qx_p, kb_p, kx_p, vb_p, batch=bp, seq=tp, fh=fh)

    n_phys = cache_k.shape[1]
    lft_c = cache_lf[l].transpose(0, 2, 1).reshape(n_phys * fh, page)
    floc = _cumsum_lanes(lft_c, carry=False, block_rows=n_phys * fh // 8).reshape(n_phys, fh, page)
    ck = cache_k[l].reshape(n_phys, page * fh, fd)
    cv = cache_v[l].reshape(n_phys, page * fh, fd)
    ob_s = _fox_sample(page_table, fq_s, kb_s, vb_s, small_s, floc, ck, cv, fh=fh, lf0=2 * mh)

    tm = MOE_TILE
    x2, xn_rows, route, cnt_f = _post((x_prompt.reshape(np_tok, d), ha_p, ob_p, gate_p),
                                      (x_sample.reshape(ns_tok, d), ha_s, ob_s, gate_s),
                                      wa, wb, wo, g2, wr, br, n_exp=n_exp, n_groups=N_GROUPS, tm=tm)

    tile_expert, n_used, dest_t, pad_start, pad_len, nt = _moe_plan(route, cnt_f, n_exp=n_exp, tm=tm)
    x_sorted = _dispatch(pad_start, pad_len, n_used, dest_t, xn_rows, nt=nt, tm=tm)
    y_sorted = _experts(tile_expert, n_used, x_sorted.reshape(nt * tm * (d // LANES), LANES),
                        w_gate[l], w_up[l], w_down[l], tm=tm)
    y_p = _combine(dest_t, x2, route, y_sorted, row0=0, rows=np_tok, tm=tm)
    y_s = _combine(dest_t, x2, route, y_sorted, row0=np_tok, rows=ns_tok, tm=tm)

    return (y_p.reshape(bp, tp, d), y_s.reshape(bs, ts, d),
            k3_p.reshape(1, bp, tp, fh, fd), v3_p.reshape(1, bp, tp, fh, fd), lf_p.reshape(1, bp, tp, fh),
            k3_s.reshape(1, bs, ts, fh, fd), v3_s.reshape(1, bs, ts, fh, fd), lf_s.reshape(1, bs, ts, fh),
            c_p[None], n_p[None], m_p[:, 0, :mh][None],
            c_s[None], n_s[None], m_s[:, 0, :mh][None])
```

```python
import functools

import jax
import jax.numpy as jnp
import numpy as np
from jax import lax
from jax.experimental import pallas as pl
from jax.experimental.pallas import tpu as pltpu

F32 = jnp.float32
BF16 = jnp.bfloat16
I32 = jnp.int32
RMS_EPS = 1e-6
LANES = 128
MLSTM_CHUNK = 128
MLSTM_PROMPT_CHUNK = 256
N_GROUPS = 4
EXPERTS_PER_GROUP = 8
VMEM_LIMIT = 56 * 1024 * 1024
NEG_INF = float("-inf")
LOG2E = 1.4426950408889634
DMA_UNROLL = 8
MOE_TILE = 512
PAD_CHUNKS = (256, 128, 64, 32, 16, 8, 4, 2, 1)


def _params(sem, vmem=VMEM_LIMIT):
    return pltpu.CompilerParams(dimension_semantics=sem, vmem_limit_bytes=vmem)


def _log_sigmoid(x):
    return -(jnp.maximum(-x, 0.0) + jnp.log1p(jnp.exp(-jnp.abs(x))))


def _sigmoid(x):
    return 1.0 / (1.0 + jnp.exp(-x))


def _nt_dot(a, b):
    return lax.dot_general(a, b, (((1,), (1,)), ((), ())), preferred_element_type=F32)


def _tn_dot(a, b):
    return lax.dot_general(a, b, (((0,), (0,)), ((), ())), preferred_element_type=F32)


def _col_to_row(col, eye):
    return jnp.sum(jnp.where(eye, col, 0.0), axis=0, keepdims=True)


def _row_to_col(row, eye):
    return jnp.sum(jnp.where(eye, row, 0.0), axis=1, keepdims=True)


def _inproj_kernel(x_ref, g1_ref, wm_ref, wf_ref, wg_ref, ws_ref, bias_ref, qg_ref, kg_ref,
                   pm_ref, fq_ref, k3_ref, v3_ref, kb_ref, vb_ref, gate_ref, small_ref, *maybe_kt_ref,
                   d, mh, fh, fd, k_scale, q_scale):
    x = x_ref[...]
    ms = jnp.mean(x * x, axis=-1, keepdims=True)
    h = (x * lax.rsqrt(ms + RMS_EPS) * g1_ref[...]).astype(BF16)
    sections = ((0, wm_ref), (4 * d, wf_ref), (7 * d, wg_ref), (9 * d, ws_ref))

    def proj(c0, width):
        base, w_ref = [(b, r) for b, r in sections if b <= c0][-1]
        return jnp.dot(h, w_ref[:, c0 - base:c0 - base + width], preferred_element_type=F32)

    pm_ref[:, 0:d] = proj(0, d).astype(pm_ref.dtype)
    mk = proj(d, d) * k_scale
    pm_ref[:, d:2 * d] = mk.astype(pm_ref.dtype)
    if maybe_kt_ref:
        maybe_kt_ref[0][...] = mk.T.astype(maybe_kt_ref[0].dtype)
    pm_ref[:, 2 * d:3 * d] = proj(2 * d, d).astype(pm_ref.dtype)
    pm_ref[:, 3 * d:4 * d] = proj(3 * d, d).astype(pm_ref.dtype)

    def head_norm(a, g):
        ms_h = jnp.mean(a * a, axis=-1, keepdims=True)
        return a * lax.rsqrt(ms_h + RMS_EPS) * g

    fq = proj(4 * d, d)
    fk = proj(5 * d, d)
    fv = proj(6 * d, d)
    for hh in range(fh):
        sl = slice(hh * fd, (hh + 1) * fd)
        fq_ref[:, sl] = (head_norm(fq[:, sl], qg_ref[...]) * q_scale).astype(fq_ref.dtype)
        kn = head_norm(fk[:, sl], kg_ref[...])
        k3_ref[:, hh, :] = kn
        kb_ref[:, sl] = kn.astype(kb_ref.dtype)
        v3_ref[:, hh, :] = fv[:, sl]
    vb_ref[...] = fv.astype(vb_ref.dtype)

    gate_ref[...] = _sigmoid(proj(7 * d, 2 * d)).astype(gate_ref.dtype)

    sm = proj(9 * d, LANES) + bias_ref[...]
    lane = lax.broadcasted_iota(I32, sm.shape, 1)
    sm = jnp.where(lane < mh, sm, jnp.where(lane < 2 * mh + fh, _log_sigmoid(sm), 0.0))
    small_ref[...] = sm


def _inproj(x2d, g1, w_secs, bias_s, qg, kg, *, mh, fh, act_dtype, q_unit=1.0, emit_kt=False, tm=512):
    n, d = x2d.shape
    fd = d // fh
    md = d // mh
    kern = functools.partial(_inproj_kernel, d=d, mh=mh, fh=fh, fd=fd,
                             k_scale=md ** -0.5, q_scale=fd ** -0.5 * q_unit)
    row = lambda i: (i, 0)
    const = lambda i: (0, 0)
    kt_specs = [pl.BlockSpec((d, tm), lambda i: (0, i))] if emit_kt else []
    kt_shapes = [jax.ShapeDtypeStruct((d, n), act_dtype)] if emit_kt else []
    return pl.pallas_call(
        kern, grid=(n // tm,),
        in_specs=[pl.BlockSpec((tm, d), row),
                  pl.BlockSpec((1, d), const),
                  *[pl.BlockSpec(w.shape, const, pipeline_mode=pl.Buffered(1)) for w in w_secs],
                  pl.BlockSpec((1, LANES), const),
                  pl.BlockSpec((1, fd), const),
                  pl.BlockSpec((1, fd), const)],
        out_specs=[pl.BlockSpec((tm, 4 * d), row),
                   pl.BlockSpec((tm, d), row),
                   pl.BlockSpec((tm, fh, fd), lambda i: (i, 0, 0)),
                   pl.BlockSpec((tm, fh, fd), lambda i: (i, 0, 0)),
                   pl.BlockSpec((tm, d), row),
                   pl.BlockSpec((tm, d), row),
                   pl.BlockSpec((tm, 2 * d), row),
                   pl.BlockSpec((tm, LANES), row)] + kt_specs,
        out_shape=[jax.ShapeDtypeStruct((n, 4 * d), act_dtype),
                   jax.ShapeDtypeStruct((n, d), act_dtype),
                   jax.ShapeDtypeStruct((n, fh, fd), F32),
                   jax.ShapeDtypeStruct((n, fh, fd), F32),
                   jax.ShapeDtypeStruct((n, d), act_dtype),
                   jax.ShapeDtypeStruct((n, d), act_dtype),
                   jax.ShapeDtypeStruct((n, 2 * d), BF16),
                   jax.ShapeDtypeStruct((n, LANES), F32)] + kt_shapes,
        compiler_params=_params(("parallel",)),
        name="inproj",
    )(x2d, g1, *w_secs, bias_s, qg, kg)


def _mlstm_kernel(*refs, L, mh, md, nb, has_init):
    if has_init:
        (q_ref, k_ref, v_ref, o_ref, s_ref, g_ref, c0_ref, n0_ref, m0_ref,
         h_ref, c_ref, n_ref, m_ref) = refs
    else:
        q_ref, k_ref, v_ref, o_ref, s_ref, g_ref, h_ref, c_ref, n_ref, m_ref = refs

    @pl.when(pl.program_id(1) == 0)
    def _():
        if has_init:
            c_ref[...] = c0_ref[...]
            n_ref[...] = n0_ref[...]
            m_ref[...] = m0_ref[...]
        else:
            c_ref[...] = jnp.zeros_like(c_ref)
            n_ref[...] = jnp.zeros_like(n_ref)
            m_ref[...] = jnp.zeros_like(m_ref)

    ri = lax.broadcasted_iota(I32, (L, L), 0)
    ci = lax.broadcasted_iota(I32, (L, L), 1)
    tri = ci <= ri
    eye = ci == ri
    for bb in range(nb):
        s = s_ref[bb]
        m_all = m_ref[bb]
        lane = lax.broadcasted_iota(I32, m_all.shape, 1)
        m_out = m_all
        for hh in range(mh):
            sl = slice(hh * md, (hh + 1) * md)
            ig_col = s[:, hh:hh + 1]
            lf_col = s[:, mh + hh:mh + hh + 1]
            lf_row = _col_to_row(lf_col, eye)
            ig_row = _col_to_row(ig_col, eye)
            b_col = jnp.sum(jnp.where(tri, lf_row, 0.0), axis=1, keepdims=True)
            b_row = _col_to_row(b_col, eye)
            m_prev = m_all[:, hh:hh + 1]
            log_w = jnp.where(tri, b_col - b_row + ig_row, NEG_INF)
            log_inter = b_col + m_prev
            m_t = jnp.maximum(log_inter, jnp.max(log_w, axis=1, keepdims=True))
            w_intra = jnp.exp(log_w - m_t)
            w_inter = jnp.exp(log_inter - m_t)
            q = q_ref[bb, :, sl].astype(BF16)
            k = k_ref[bb, :, sl].astype(BF16)
            v = v_ref[bb, :, sl].astype(BF16)
            sm = _nt_dot(q, k) * w_intra
            c_prev = c_ref[bb, hh]
            n_prev = n_ref[bb, hh:hh + 1, :]
            num = (w_inter * jnp.dot(q, c_prev.astype(BF16), preferred_element_type=F32)
                   + jnp.dot(sm.astype(BF16), v, preferred_element_type=F32))
            den = (w_inter * jnp.sum(q.astype(F32) * n_prev, axis=1, keepdims=True)
                   + jnp.sum(sm, axis=1, keepdims=True))
            hv = num / jnp.maximum(jnp.abs(den), jnp.exp(-m_t))
            m_new = m_t[L - 1:L, :]
            b_last = b_col[L - 1:L, :]
            decay = jnp.exp(b_last + m_prev - m_new)
            w_rows = jnp.exp(b_last - b_col + ig_col - m_new)
            kw = k.astype(F32) * w_rows
            c_ref[bb, hh] = decay * c_prev + _tn_dot(kw.astype(BF16), v)
            n_ref[bb, hh:hh + 1, :] = decay * n_prev + jnp.sum(kw, axis=0, keepdims=True)
            m_out = jnp.where(lane == hh, m_new, m_out)
            ms = jnp.mean(hv * hv, axis=-1, keepdims=True)
            hn = hv * lax.rsqrt(ms + RMS_EPS) * g_ref[:, sl]
            h_ref[bb, :, sl] = (_sigmoid(o_ref[bb, :, sl].astype(F32)) * hn).astype(h_ref.dtype)
        m_ref[bb] = m_out


def _mlstm(pm, small, g_m, *, batch, seq, mh, nb=1, init=None, out_dtype=BF16):
    d = pm.shape[1] // 4
    md = d // mh
    L = MLSTM_CHUNK if seq % MLSTM_CHUNK == 0 else seq
    nc = seq // L
    ng = batch // nb
    pm4 = pm.reshape(ng, nb, seq, 4 * d)
    small4 = small.reshape(ng, nb, seq, LANES)
    sec = lambda j: (lambda b, c: (b, 0, c, j))
    in_specs = [pl.BlockSpec((None, nb, L, d), sec(0)), pl.BlockSpec((None, nb, L, d), sec(1)),
                pl.BlockSpec((None, nb, L, d), sec(2)), pl.BlockSpec((None, nb, L, d), sec(3)),
                pl.BlockSpec((None, nb, L, LANES), sec(0)),
                pl.BlockSpec((1, d), lambda b, c: (0, 0))]
    args = [pm4, pm4, pm4, pm4, small4, g_m]
    state_specs = [pl.BlockSpec((nb, mh, md, md), lambda b, c: (b, 0, 0, 0)),
                   pl.BlockSpec((nb, mh, md), lambda b, c: (b, 0, 0)),
                   pl.BlockSpec((nb, 1, LANES), lambda b, c: (b, 0, 0))]
    if init is not None:
        in_specs += state_specs
        args += list(init)
    kern = functools.partial(_mlstm_kernel, L=L, mh=mh, md=md, nb=nb, has_init=init is not None)
    h4, c_out, n_out, m_out = pl.pallas_call(
        kern, grid=(ng, nc),
        in_specs=in_specs,
        out_specs=[pl.BlockSpec((None, nb, L, d), lambda b, c: (b, 0, c, 0))] + state_specs,
        out_shape=[jax.ShapeDtypeStruct((ng, nb, seq, d), out_dtype),
                   jax.ShapeDtypeStruct((batch, mh, md, md), F32),
                   jax.ShapeDtypeStruct((batch, mh, md), F32),
                   jax.ShapeDtypeStruct((batch, 1, LANES), F32)],
        compiler_params=_params(("parallel", "arbitrary")),
        name="mlstm",
    )(*args)
    return h4.reshape(batch * seq, d), c_out, n_out, m_out


def _mlstm_prompt_kernel(q_ref, kt_ref, v_ref, o_ref, s_ref, g_ref, h_ref, c_ref, n_ref, m_ref, naug,
                         *, L, mh, md, nc):
    c = pl.program_id(1)

    @pl.when(c == 0)
    def _():
        c_ref[...] = jnp.zeros_like(c_ref)
        m_ref[...] = jnp.zeros_like(m_ref)
        naug[...] = jnp.zeros_like(naug)

    s = s_ref[...]
    ri = lax.broadcasted_iota(I32, (L, L), 0)
    ci = lax.broadcasted_iota(I32, (L, L), 1)
    tri = ci <= ri
    hp = lax.Precision.HIGHEST
    r = s.T[0:8, :]
    b_cols = jnp.dot(jnp.where(tri, 1.0, 0.0), s, precision=hp, preferred_element_type=F32)
    b_rows = pltpu.roll(jnp.dot(r, jnp.where(ri <= ci, 1.0, 0.0), precision=hp, preferred_element_type=F32),
                        8 - mh, axis=0)
    g = r - b_rows
    m_all = m_ref[0]
    lane = lax.broadcasted_iota(I32, m_all.shape, 1)
    m_out = m_all
    ones_v = jnp.ones((L, LANES), BF16)
    ones_r = jnp.ones((md, LANES), BF16)
    for hh in range(mh):
        sl = slice(hh * md, (hh + 1) * md)
        m_prev = m_all[:, hh:hh + 1]
        g_row = g[hh:hh + 1, :]
        b_col = b_cols[:, mh + hh:mh + hh + 1]
        a_col = jnp.maximum(m_prev, jnp.max(jnp.where(tri, g_row, NEG_INF), axis=1, keepdims=True))
        m_t = b_col + a_col
        m_new = m_t[L - 1:L, :]
        b_last = b_col[L - 1:L, :]
        w_rows = jnp.exp(b_last - b_rows[hh:hh + 1, :] + r[hh:hh + 1, :] - m_new)
        dc = jnp.exp(b_last + m_prev - m_new)
        m_out = jnp.where(lane == hh, m_new, m_out)
        q = q_ref[:, sl]
        kt = kt_ref[sl, :]
        v_aug = jnp.concatenate([v_ref[:, sl], ones_v], axis=1)
        w_intra = jnp.exp(jnp.where(tri, g_row - a_col, NEG_INF))
        sm = jnp.dot(q, kt, preferred_element_type=F32) * w_intra
        c_prev = c_ref[0, hh]
        n_prev = naug[hh]
        c_aug = jnp.concatenate([c_prev, n_prev], axis=1).astype(BF16)
        comb = (jnp.exp(m_prev - a_col) * jnp.dot(q, c_aug, preferred_element_type=F32)
                + jnp.dot(sm.astype(BF16), v_aug, preferred_element_type=F32))
        den = jnp.maximum(jnp.abs(comb[:, md:]), jnp.exp(-m_t))
        inv = 1.0 / den
        hv = comb[:, 0:md] * jnp.concatenate([inv] * (md // LANES), axis=1)
        ms = jnp.dot((hv * hv).astype(BF16), ones_r, preferred_element_type=F32) * (1.0 / md)
        rs = lax.rsqrt(ms + RMS_EPS)
        hn = hv * jnp.concatenate([rs] * (md // LANES), axis=1) * g_ref[:, sl]
        h_ref[:, sl] = (_sigmoid(o_ref[:, sl].astype(F32)) * hn).astype(h_ref.dtype)
        kw = (kt.astype(F32) * w_rows).astype(BF16)
        upd = jnp.dot(kw, v_aug, preferred_element_type=F32)
        c_ref[0, hh] = dc * c_prev + upd[:, 0:md]
        naug[hh] = dc * n_prev + upd[:, md:]
    m_ref[0] = m_out

    @pl.when(c == nc - 1)
    def _():
        ei = lax.broadcasted_iota(I32, (md, md), 0) == lax.broadcasted_iota(I32, (md, md), 1)
        for hh in range(mh):
            cols = jnp.concatenate([naug[hh]] * (md // LANES), axis=1)
            n_ref[0, hh:hh + 1, :] = jnp.sum(jnp.where(ei, cols, 0.0), axis=0, keepdims=True)


def _mlstm_prompt(pm, kt, small, g_m, *, batch, seq, mh):
    d = pm.shape[1] // 4
    md = d // mh
    L = MLSTM_PROMPT_CHUNK
    assert seq % L == 0
    nc = seq // L
    tok = lambda j: (lambda b, c: (b * nc + c, j))
    kern = functools.partial(_mlstm_prompt_kernel, L=L, mh=mh, md=md, nc=nc)
    return pl.pallas_call(
        kern, grid=(batch, nc),
        in_specs=[pl.BlockSpec((L, d), tok(0)),
                  pl.BlockSpec((d, L), lambda b, c: (0, b * nc + c)),
                  pl.BlockSpec((L, d), tok(2)), pl.BlockSpec((L, d), tok(3)),
                  pl.BlockSpec((L, LANES), tok(0)),
                  pl.BlockSpec((1, d), lambda b, c: (0, 0))],
        out_specs=[pl.BlockSpec((L, d), tok(0)),
                   pl.BlockSpec((1, mh, md, md), lambda b, c: (b, 0, 0, 0)),
                   pl.BlockSpec((1, mh, md), lambda b, c: (b, 0, 0)),
                   pl.BlockSpec((1, 1, LANES), lambda b, c: (b, 0, 0))],
        out_shape=[jax.ShapeDtypeStruct((batch * seq, d), BF16),
                   jax.ShapeDtypeStruct((batch, mh, md, md), F32),
                   jax.ShapeDtypeStruct((batch, mh, md), F32),
                   jax.ShapeDtypeStruct((batch, 1, LANES), F32)],
        scratch_shapes=[pltpu.VMEM((mh, md, LANES), F32)],
        compiler_params=_params(("parallel", "arbitrary")),
        name="mlstm_prompt",
    )(pm, kt, pm, pm, small, g_m)


def _cumsum_kernel(x_ref, o_ref, *, chunk, carry):
    rows, t = x_ref.shape
    si = lax.broadcasted_iota(I32, (chunk, chunk), 0)
    ti = lax.broadcasted_iota(I32, (chunk, chunk), 1)
    tri = (si <= ti).astype(F32)
    run = jnp.zeros((rows, 1), F32)
    for j in range(t // chunk):
        sl = slice(j * chunk, (j + 1) * chunk)
        loc = jnp.dot(x_ref[:, sl], tri, precision=lax.Precision.HIGHEST, preferred_element_type=F32)
        if carry:
            loc = loc + run
            run = loc[:, chunk - 1:chunk]
        o_ref[:, sl] = loc


def _cumsum_lanes(x, *, carry, block_rows):
    rows, t = x.shape
    kern = functools.partial(_cumsum_kernel, chunk=LANES, carry=carry)
    return pl.pallas_call(
        kern, grid=(rows // block_rows,),
        in_specs=[pl.BlockSpec((block_rows, t), lambda i: (i, 0))],
        out_specs=pl.BlockSpec((block_rows, t), lambda i: (i, 0)),
        out_shape=jax.ShapeDtypeStruct((rows, t), F32),
        compiler_params=_params(("parallel",)),
        name="cumsum",
    )(x)


BIAS_LANES = 8


def _fbias_kernel(s_ref, pq_ref, pk_ref, cq_ref, ck_ref, qx_ref, kx_ref, run_ref):
    @pl.when(pl.program_id(1) == 0)
    def _():
        run_ref[...] = jnp.zeros_like(run_ref)

    s = s_ref[...]
    L = s.shape[0]
    ti = lax.broadcasted_iota(I32, (L, L), 0)
    si = lax.broadcasted_iota(I32, (L, L), 1)
    tri = jnp.where(si <= ti, 1.0, 0.0).astype(BF16)

    def split3(x):
        hi = x.astype(BF16)
        r1 = x - hi.astype(F32)
        mid = r1.astype(BF16)
        return jnp.concatenate([hi, mid, (r1 - mid.astype(F32)).astype(BF16)], axis=1)

    cs = jnp.dot(tri, split3(s), preferred_element_type=F32)
    f_all = (cs[:, 0:LANES] + cs[:, LANES:2 * LANES] + cs[:, 2 * LANES:3 * LANES]) + run_ref[...]
    run_ref[...] = f_all[L - 1:L, :]
    terms = split3(f_all * LOG2E)
    qx_ref[...] = jnp.dot(terms, pq_ref[...], preferred_element_type=F32) + cq_ref[...]
    kx_ref[...] = (jnp.dot(terms, pk_ref[...], preferred_element_type=F32) + ck_ref[...]).astype(BF16)


def _fbias(small, *, batch, seq, fh, lf0):
    assert BIAS_LANES * fh <= LANES
    n = small.shape[0]
    L = 4 * LANES if seq % (4 * LANES) == 0 else LANES
    nc = seq // L
    pq = np.zeros((3 * LANES, LANES), np.float32)
    pk = np.zeros((3 * LANES, LANES), np.float32)
    cq = np.zeros((1, LANES), np.float32)
    ck = np.zeros((1, LANES), np.float32)
    for h in range(fh):
        for j in range(3):
            pq[j * LANES + lf0 + h, BIAS_LANES * h + j] = 1.0
            pk[j * LANES + lf0 + h, BIAS_LANES * h + 3 + j] = -1.0
            cq[0, BIAS_LANES * h + 3 + j] = 1.0
            ck[0, BIAS_LANES * h + j] = 1.0
    const = lambda b, c: (0, 0)
    tok = lambda b, c: (b * nc + c, 0)
    return pl.pallas_call(
        _fbias_kernel, grid=(batch, nc),
        in_specs=[pl.BlockSpec((L, LANES), tok),
                  pl.BlockSpec((3 * LANES, LANES), const), pl.BlockSpec((3 * LANES, LANES), const),
                  pl.BlockSpec((1, LANES), const), pl.BlockSpec((1, LANES), const)],
        out_specs=[pl.BlockSpec((L, LANES), tok), pl.BlockSpec((L, LANES), tok)],
        out_shape=[jax.ShapeDtypeStruct((n, LANES), F32), jax.ShapeDtypeStruct((n, LANES), BF16)],
        scratch_shapes=[pltpu.VMEM((1, LANES), F32)],
        compiler_params=_params(("parallel", "arbitrary")),
        name="fbias",
    )(small, jnp.asarray(pq, BF16), jnp.asarray(pk, BF16), jnp.asarray(cq), jnp.asarray(ck))


def _fox_prompt_kernel(q_ref, qx_ref, k_ref, kx_ref, v_ref, o_ref, *, seq, tq, gh, fd):
    nq = seq // tq
    ri = lax.broadcasted_iota(I32, (tq, tq), 0)
    ci = lax.broadcasted_iota(I32, (tq, tq), 1)
    causal = ci <= ri

    def softmax_step(carry, s, v):
        m, l, acc = carry
        m_new = jnp.maximum(m, jnp.max(s, axis=1, keepdims=True))
        a = jnp.exp2(m - m_new)
        p = jnp.exp2(s - m_new)
        l = a * l + jnp.sum(p, axis=1, keepdims=True)
        acc = a * acc + jnp.dot(p.astype(BF16), v, preferred_element_type=F32)
        return m_new, l, acc

    head0 = pl.program_id(1) * gh
    bias_head = lax.broadcasted_iota(I32, (tq, LANES), 1) // BIAS_LANES

    def q_body(qi, _):
        q0 = pl.multiple_of(qi * tq, tq)
        qx = qx_ref[pl.ds(q0, tq), :]
        qas = [jnp.concatenate([q_ref[pl.ds(q0, tq), g * fd:(g + 1) * fd],
                                jnp.where(bias_head == head0 + g, qx, 0.0).astype(BF16)], axis=1)
               for g in range(gh)]

        def block(g, carry_g, k0, masked):
            gs = slice(g * fd, (g + 1) * fd)
            ka = jnp.concatenate([k_ref[pl.ds(k0, tq), gs], kx_ref[pl.ds(k0, tq), :]], axis=1)
            s = _nt_dot(qas[g], ka)
            if masked:
                s = jnp.where(causal, s, NEG_INF)
            return softmax_step(carry_g, s, v_ref[pl.ds(k0, tq), gs])

        def kv_body(kj, carry):
            k0 = pl.multiple_of(kj * tq, tq)
            return tuple(block(g, carry[g], k0, False) for g in range(gh))

        init = tuple((jnp.full((tq, 1), NEG_INF, F32), jnp.zeros((tq, 1), F32), jnp.zeros((tq, fd), F32))
                     for _ in range(gh))
        carry = lax.fori_loop(0, qi, kv_body, init)
        for g in range(gh):
            _, l, acc = block(g, carry[g], q0, True)
            o_ref[pl.ds(q0, tq), g * fd:(g + 1) * fd] = (acc / l).astype(o_ref.dtype)
        return 0

    lax.fori_loop(0, nq, q_body, 0)


def _fox_prompt(fq, qx, kb, kx, vb, *, batch, seq, fh, tq=1024, gh=2):
    n, d = fq.shape
    fd = d // fh
    blk = lambda b, h: (b, h)
    bias = pl.BlockSpec((seq, LANES), lambda b, h: (b, 0))
    kern = functools.partial(_fox_prompt_kernel, seq=seq, tq=tq, gh=gh, fd=fd)
    return pl.pallas_call(
        kern, grid=(batch, fh // gh),
        in_specs=[pl.BlockSpec((seq, gh * fd), blk), bias, pl.BlockSpec((seq, gh * fd), blk), bias,
                  pl.BlockSpec((seq, gh * fd), blk)],
        out_specs=pl.BlockSpec((seq, gh * fd), blk),
        out_shape=jax.ShapeDtypeStruct((n, d), BF16),
        compiler_params=_params(("parallel", "parallel")),
        name="fox_prompt",
    )(fq, qx, kb, kx, vb)


def _fox_sample_kernel(pt_ref, q_ref, kn_ref, vn_ref, s_ref, *rest, n_pages, page, fh, fd, ts, lf0):
    floc = rest[0:n_pages]
    kpages = rest[n_pages:2 * n_pages]
    vpages = rest[2 * n_pages:3 * n_pages]
    o_ref = rest[3 * n_pages]
    p_scr, pn_scr, l_scr = rest[3 * n_pages + 1:]
    d = fh * fd
    rows = fh * ts

    def scores():
        q = q_ref[...].astype(F32)
        qt = jnp.concatenate([q] * fh, axis=0)
        r_head = lax.broadcasted_iota(I32, (rows, d), 0) // ts
        c_head = lax.broadcasted_iota(I32, (rows, d), 1) // fd
        qbd = jnp.where(r_head == c_head, qt, 0.0).astype(BF16)

        off = jnp.zeros((fh, 1), F32)
        fk_pages = []
        for i in range(n_pages):
            fp = floc[i][...] + off
            fk_pages.append(fp)
            off = fp[:, page - 1:page]
        fk = jnp.concatenate(fk_pages, axis=1)
        fk_rows = jnp.concatenate(
            [jnp.broadcast_to(fk[h:h + 1, :], (ts, fk.shape[1])) for h in range(fh)], axis=0)

        sm = s_ref[...]
        ri = lax.broadcasted_iota(I32, (ts, ts), 0)
        ci = lax.broadcasted_iota(I32, (ts, ts), 1)
        eye = ci == ri
        tri = ci <= ri
        fq_cols, bias_new = [], []
        for h in range(fh):
            lf_col = sm[:, lf0 + h:lf0 + h + 1]
            lf_row = _col_to_row(lf_col, eye)
            cum_col = jnp.sum(jnp.where(tri, lf_row, 0.0), axis=1, keepdims=True)
            fq_h = off[h:h + 1, :] + cum_col
            fq_cols.append(fq_h)
            bias_new.append(fq_h - _col_to_row(fq_h, eye))
        fq_col = jnp.concatenate(fq_cols, axis=0)
        bias_n = jnp.concatenate(bias_new, axis=0)
        causal_n = jnp.concatenate([tri] * fh, axis=0)

        s_parts = []
        for i in range(0, n_pages, 2):
            kp = jnp.concatenate(
                [jnp.concatenate([kpages[i + j][pl.ds(h, page, stride=fh), :] for h in range(fh)], axis=1)
                 for j in range(2)],
                axis=0).astype(BF16)
            s_parts.append(_nt_dot(qbd, kp))
        s_past = jnp.concatenate(s_parts, axis=1) + (fq_col - fk_rows)
        s_new = _nt_dot(qbd, kn_ref[...].astype(BF16)) + bias_n
        s_new = jnp.where(causal_n, s_new, NEG_INF)
        m = jnp.maximum(jnp.max(s_past, axis=1, keepdims=True), jnp.max(s_new, axis=1, keepdims=True))
        p_past = jnp.exp(s_past - m)
        p_new = jnp.exp(s_new - m)
        l_scr[...] = jnp.sum(p_past, axis=1, keepdims=True) + jnp.sum(p_new, axis=1, keepdims=True)
        p_scr[...] = p_past.astype(BF16)
        pn_scr[...] = p_new

    def values():
        acc = jnp.dot(pn_scr[...].astype(BF16), vn_ref[...].astype(BF16), preferred_element_type=F32)
        for i in range(0, n_pages, 2):
            vp = jnp.concatenate(
                [jnp.concatenate([vpages[i + j][pl.ds(h, page, stride=fh), :] for h in range(fh)], axis=1)
                 for j in range(2)],
                axis=0).astype(BF16)
            acc = acc + jnp.dot(p_scr[:, i * page:(i + 2) * page], vp, preferred_element_type=F32)
        acc = acc / l_scr[...]
        o_ref[...] = jnp.concatenate(
            [acc[h * ts:(h + 1) * ts, h * fd:(h + 1) * fd] for h in range(fh)], axis=1)

    scores()
    values()


def _fox_sample(page_table, fq, kb, vb, small, floc, cache_k, cache_v, *, fh, lf0):
    bs, n_pages = page_table.shape
    n, d = fq.shape
    ts = n // bs
    fd = d // fh
    page = cache_k.shape[1] // fh
    rows = fh * ts
    tok = lambda b, pt: (b, 0)

    def pmap(i):
        return lambda b, pt: (pt[b, i], 0, 0)

    in_specs = ([pl.BlockSpec((ts, d), tok), pl.BlockSpec((ts, d), tok), pl.BlockSpec((ts, d), tok),
                 pl.BlockSpec((ts, LANES), tok)]
                + [pl.BlockSpec((None, fh, page), pmap(i)) for i in range(n_pages)]
                + [pl.BlockSpec((None, page * fh, fd), pmap(i)) for i in range(n_pages)]
                + [pl.BlockSpec((None, page * fh, fd), pmap(i)) for i in range(n_pages)])
    kern = functools.partial(_fox_sample_kernel, n_pages=n_pages, page=page, fh=fh, fd=fd, ts=ts, lf0=lf0)
    return pl.pallas_call(
        kern,
        grid_spec=pltpu.PrefetchScalarGridSpec(
            num_scalar_prefetch=1, grid=(bs,), in_specs=in_specs,
            out_specs=pl.BlockSpec((ts, d), tok),
            scratch_shapes=[pltpu.VMEM((rows, n_pages * page), BF16),
                            pltpu.VMEM((rows, ts), F32),
                            pltpu.VMEM((rows, 1), F32)]),
        out_shape=jax.ShapeDtypeStruct((n, d), F32),
        compiler_params=_params(("arbitrary",)),
        name="fox_sample",
    )(page_table, fq, kb, vb, small, *([floc] * n_pages), *([cache_k] * n_pages), *([cache_v] * n_pages))


def _post_kernel(xp_ref, hap_ref, obp_ref, gp_ref, xs_ref, has_ref, obs_ref, gs_ref, *rest, tiles_p, **kw):
    cnt_ref = rest[-1]

    @pl.when(pl.program_id(0) == 0)
    def _():
        cnt_ref[...] = jnp.zeros_like(cnt_ref)

    @pl.when(pl.program_id(0) < tiles_p)
    def _():
        _post_body(xp_ref, hap_ref, obp_ref, gp_ref, *rest, **kw)

    @pl.when(pl.program_id(0) >= tiles_p)
    def _():
        _post_body(xs_ref, has_ref, obs_ref, gs_ref, *rest, **kw)


def _post_body(x_ref, ha_ref, ob_ref, gate_ref, wa_ref, wb_ref, wo_ref, g2_ref, wr_ref, br_ref,
               x2_ref, xn_ref, route_ref, cnt_ref, *, d, n_exp, n_groups):
    ba = jnp.dot(ha_ref[...].astype(BF16), wa_ref[...], preferred_element_type=F32)
    bb = jnp.dot(ob_ref[...].astype(BF16), wb_ref[...], preferred_element_type=F32)
    gates = gate_ref[...].astype(F32)
    merged = gates[:, 0:d] * ba + gates[:, d:2 * d] * bb
    x2 = x_ref[...] + jnp.dot(merged.astype(BF16), wo_ref[...], preferred_element_type=F32)
    x2_ref[...] = x2
    ms = jnp.mean(x2 * x2, axis=-1, keepdims=True)
    xn = x2 * lax.rsqrt(ms + RMS_EPS) * g2_ref[...]
    tm = xn.shape[0]
    for j in range(d // LANES):
        xn_ref[pl.ds(j, tm, stride=d // LANES), :] = xn[:, j * LANES:(j + 1) * LANES]

    xh = xn.astype(BF16)
    xl = (xn - xh.astype(F32)).astype(BF16)
    hh = jnp.dot(xh, wr_ref[...], preferred_element_type=F32)
    logits = (hh[:, 0:LANES] + hh[:, LANES:2 * LANES]
              + jnp.dot(xl, wr_ref[:, 0:LANES], preferred_element_type=F32)
              + br_ref[...])
    lane = lax.broadcasted_iota(I32, logits.shape, 1)
    lane_f = lane.astype(F32)
    big = float(LANES)
    epg = n_exp // n_groups
    in_groups = (lane >= n_exp) & (lane < n_exp + n_groups)
    gl = jnp.where(in_groups, logits, NEG_INF)
    gmax = jnp.max(gl, axis=1, keepdims=True)
    gidx = jnp.min(jnp.where(gl == gmax, lane_f, big), axis=1, keepdims=True) - float(n_exp)
    g_p = 1.0 / jnp.sum(jnp.exp(gl - gmax), axis=1, keepdims=True)
    in_group = (lane < n_exp) & ((lane // epg).astype(F32) == gidx)
    el = jnp.where(in_group, logits, NEG_INF)
    m1 = jnp.max(el, axis=1, keepdims=True)
    i1 = jnp.min(jnp.where(el == m1, lane_f, big), axis=1, keepdims=True)
    el2 = jnp.where(lane_f == i1, NEG_INF, el)
    m2 = jnp.max(el2, axis=1, keepdims=True)
    i2 = jnp.min(jnp.where(el2 == m2, lane_f, big), axis=1, keepdims=True)
    esum = jnp.sum(jnp.exp(el - m1), axis=1, keepdims=True)
    p1 = 1.0 / esum
    p2 = jnp.exp(m2 - m1) / esum
    psum = p1 + p2
    w1 = p1 / psum * g_p
    w2 = p2 / psum * g_p
    r0, r1 = _rank_block(i1, i2, lane_f, cnt_ref)
    route_ref[...] = jnp.where(lane == 0, i1, jnp.where(lane == 1, i2, jnp.where(
        lane == 2, w1, jnp.where(lane == 3, w2, jnp.where(lane == 4, r0, jnp.where(lane == 5, r1, 0.0))))))


def _post(acts_p, acts_s, wa, wb, wo, g2, wr, br, *, n_exp, n_groups, tm=256):
    n_p, d = acts_p[0].shape
    n_s = acts_s[0].shape[0]
    tiles_p, tiles_s = n_p // tm, n_s // tm
    n = n_p + n_s
    rpt = d // LANES
    row_p = lambda i: (jnp.minimum(i, tiles_p - 1), 0)
    row_s = lambda i: (jnp.maximum(i - tiles_p, 0), 0)
    row = lambda i: (i, 0)
    const = lambda i: (0, 0)
    kern = functools.partial(_post_kernel, tiles_p=tiles_p, d=d, n_exp=n_exp, n_groups=n_groups)
    wspec = lambda: pl.BlockSpec((d, d), const, pipeline_mode=pl.Buffered(1))
    act_specs = lambda r: [pl.BlockSpec((tm, d), r), pl.BlockSpec((tm, d), r), pl.BlockSpec((tm, d), r),
                           pl.BlockSpec((tm, 2 * d), r)]
    return pl.pallas_call(
        kern, grid=(tiles_p + tiles_s,),
        in_specs=act_specs(row_p) + act_specs(row_s) + [
            wspec(), wspec(), wspec(),
            pl.BlockSpec((1, d), const), pl.BlockSpec((d, 2 * LANES), const), pl.BlockSpec((1, LANES), const)],
        out_specs=[pl.BlockSpec((tm, d), row), pl.BlockSpec((tm * rpt, LANES), row),
                   pl.BlockSpec((tm, LANES), row), pl.BlockSpec((1, LANES), const)],
        out_shape=[jax.ShapeDtypeStruct((n, d), F32), jax.ShapeDtypeStruct((n * rpt, LANES), F32),
                   jax.ShapeDtypeStruct((n, LANES), F32), jax.ShapeDtypeStruct((1, LANES), F32)],
        compiler_params=_params(("arbitrary",)),
        name="post",
    )(*acts_p, *acts_s, wa, wb, wo, g2, wr, br)


def _dispatch_kernel(ps_ref, pl_ref, nu_ref, dest_ref, xn_ref, xs_hbm, stage, zbuf, sem_rows, sem_pad,
                     *, tm, n_tiles, n_exp, nt):
    i = pl.program_id(0)
    slot = i % 2

    def pad_dmas(act):
        for e in range(n_exp):
            pos = ps_ref[e]
            left = pl_ref[e]
            for c in PAD_CHUNKS:
                @pl.when((left & c) != 0)
                def _(pos=pos, c=c):
                    act(pltpu.make_async_copy(zbuf.at[pl.ds(0, c)], xs_hbm.at[pl.ds(pos, c)], sem_pad))
                pos = pos + (left & c)

        def unused_tile(t, carry):
            for part in range(tm // PAD_CHUNKS[0]):
                act(pltpu.make_async_copy(
                    zbuf, xs_hbm.at[pl.ds(t * tm + part * PAD_CHUNKS[0], PAD_CHUNKS[0])], sem_pad))
            return carry
        lax.fori_loop(nu_ref[0], nt, unused_tile, 0)

    @pl.when(i == 0)
    def _():
        zbuf[...] = jnp.zeros_like(zbuf)
        pad_dmas(lambda cp: cp.start())

    def wait_tile(s):
        for _ in range(2):
            pltpu.make_async_copy(stage.at[s], xs_hbm.at[pl.ds(0, tm)], sem_rows.at[s]).wait()

    @pl.when(i >= 2)
    def _():
        wait_tile(slot)

    stage[slot] = xn_ref[...]

    def body(j, c):
        for u in range(DMA_UNROLL // 2):
            r = j * (DMA_UNROLL // 2) + u
            for kk in range(2):
                pltpu.make_async_copy(stage.at[slot, r], xs_hbm.at[dest_ref[0, kk, r]],
                                      sem_rows.at[slot]).start(priority=kk)
        return c
    lax.fori_loop(0, tm // (DMA_UNROLL // 2), body, 0)

    @pl.when(i == n_tiles - 1)
    def _():
        if n_tiles > 1:
            wait_tile(1 - slot)
        wait_tile(slot)
        pad_dmas(lambda cp: cp.wait())


def _dispatch(pad_start, pad_len, n_used, dest_t, xn_rows, *, nt, tm):
    n_tiles = dest_t.shape[0]
    n_exp = pad_start.shape[0]
    rpt = xn_rows.shape[0] // (n_tiles * tm)
    xn3 = xn_rows.reshape(n_tiles * tm, rpt, LANES)
    kern = functools.partial(_dispatch_kernel, tm=tm, n_tiles=n_tiles, n_exp=n_exp, nt=nt)
    return pl.pallas_call(
        kern,
        grid_spec=pltpu.PrefetchScalarGridSpec(
            num_scalar_prefetch=3, grid=(n_tiles,),
            in_specs=[pl.BlockSpec((1, 8, tm), lambda i, ps, pln, nu: (i, 0, 0), memory_space=pltpu.SMEM),
                      pl.BlockSpec((tm, rpt, LANES), lambda i, ps, pln, nu: (i, 0, 0))],
            out_specs=pl.BlockSpec(memory_space=pl.ANY),
            scratch_shapes=[pltpu.VMEM((2, tm, rpt, LANES), F32),
                            pltpu.VMEM((PAD_CHUNKS[0], rpt, LANES), F32),
                            pltpu.SemaphoreType.DMA((2,)), pltpu.SemaphoreType.DMA(())]),
        out_shape=jax.ShapeDtypeStruct((nt * tm, rpt, LANES), F32),
        compiler_params=_params(("arbitrary",)),
        name="dispatch",
    )(pad_start, pad_len, n_used, dest_t, xn3)


def _experts_kernel(te_ref, nu_ref, x_ref, wg_ref, wu_ref, wd_ref, y_ref, wg_b, wu_b, wd_b, *, tm, rpt):
    t = pl.program_id(0)

    @pl.when((t == 0) | (te_ref[t] != te_ref[jnp.maximum(t - 1, 0)]))
    def _():
        wg_b[...] = wg_ref[0].astype(BF16)
        wu_b[...] = wu_ref[0].astype(BF16)
        wd_b[...] = wd_ref[0].astype(BF16)

    @pl.when(t < nu_ref[0])
    def _():
        x = jnp.concatenate([x_ref[pl.ds(j, tm, stride=rpt), :] for j in range(rpt)], axis=1).astype(BF16)
        g = jnp.dot(x, wg_b[...], preferred_element_type=F32)
        u = jnp.dot(x, wu_b[...], preferred_element_type=F32)
        hg = (g * _sigmoid(g)) * u
        y = jnp.dot(hg.astype(BF16), wd_b[...], preferred_element_type=F32)
        for j in range(rpt):
            y_ref[pl.ds(j, tm, stride=rpt), :] = y[:, j * LANES:(j + 1) * LANES]

    @pl.when(t >= nu_ref[0])
    def _():
        y_ref[...] = jnp.zeros_like(y_ref)


def _experts(tile_expert, n_used, x_sorted, w_gate, w_up, w_down, *, tm):
    nt = tile_expert.shape[0]
    n_exp, d, de = w_gate.shape
    rpt = d // LANES
    emap = lambda t, te, nu: (te[t], 0, 0)
    kern = functools.partial(_experts_kernel, tm=tm, rpt=rpt)
    return pl.pallas_call(
        kern,
        grid_spec=pltpu.PrefetchScalarGridSpec(
            num_scalar_prefetch=2, grid=(nt,),
            in_specs=[pl.BlockSpec((tm * rpt, LANES), lambda t, te, nu: (jnp.minimum(t, nu[0] - 1), 0)),
                      pl.BlockSpec((1, d, de), emap), pl.BlockSpec((1, d, de), emap),
                      pl.BlockSpec((1, de, d), emap)],
            out_specs=pl.BlockSpec((tm * rpt, LANES), lambda t, te, nu: (t, 0)),
            scratch_shapes=[pltpu.VMEM((d, de), BF16), pltpu.VMEM((d, de), BF16), pltpu.VMEM((de, d), BF16)]),
        out_shape=jax.ShapeDtypeStruct((nt * tm * rpt, LANES), F32),
        compiler_params=_params(("arbitrary",)),
        name="experts",
    )(tile_expert, n_used, x_sorted, w_gate, w_up, w_down)


def _combine_kernel(idx_cur, idx_nxt, x2_ref, route_ref, ys_hbm, y_ref, buf, sem, *, tm, n_tiles, rpt):
    i = pl.program_id(0)
    slot = i % 2

    def gather(idx_ref, dst_slot):
        def body(j, c):
            for u in range(DMA_UNROLL // 2):
                r = j * (DMA_UNROLL // 2) + u
                for kk in range(2):
                    src = pl.multiple_of(idx_ref[0, kk, r] * rpt, rpt)
                    pltpu.make_async_copy(ys_hbm.at[pl.ds(src, rpt), :],
                                          buf.at[dst_slot, kk, pl.ds(r * rpt, rpt), :],
                                          sem.at[dst_slot]).start(priority=kk)
            return c
        lax.fori_loop(0, tm // (DMA_UNROLL // 2), body, 0)

    @pl.when(i == 0)
    def _():
        gather(idx_cur, 0)

    @pl.when(i + 1 < n_tiles)
    def _():
        gather(idx_nxt, 1 - slot)

    for kk in range(2):
        pltpu.make_async_copy(ys_hbm.at[pl.ds(0, tm * rpt), :], buf.at[slot, kk], sem.at[slot]).wait()
    route = route_ref[...]
    w0 = route[:, 2:3]
    w1 = route[:, 3:4]
    for j in range(rpt):
        sl = slice(j * LANES, (j + 1) * LANES)
        y_ref[:, sl] = (x2_ref[:, sl] + w0 * buf[slot, 0, pl.ds(j, tm, stride=rpt), :]
                        + w1 * buf[slot, 1, pl.ds(j, tm, stride=rpt), :])


def _combine(dest_t, x2, route, y_sorted, *, row0, rows, tm=256):
    d = x2.shape[1]
    rpt = d // LANES
    t0 = row0 // tm
    n_tiles = rows // tm
    kern = functools.partial(_combine_kernel, tm=tm, n_tiles=n_tiles, rpt=rpt)
    return pl.pallas_call(
        kern, grid=(n_tiles,),
        in_specs=[pl.BlockSpec((1, 8, tm), lambda i: (t0 + i, 0, 0), memory_space=pltpu.SMEM),
                  pl.BlockSpec((1, 8, tm), lambda i: (t0 + jnp.minimum(i + 1, n_tiles - 1), 0, 0),
                               memory_space=pltpu.SMEM),
                  pl.BlockSpec((tm, d), lambda i: (t0 + i, 0)),
                  pl.BlockSpec((tm, LANES), lambda i: (t0 + i, 0)),
                  pl.BlockSpec(memory_space=pl.ANY)],
        out_specs=pl.BlockSpec((tm, d), lambda i: (i, 0)),
        out_shape=jax.ShapeDtypeStruct((rows, d), F32),
        scratch_shapes=[pltpu.VMEM((2, 2, tm * rpt, LANES), F32), pltpu.SemaphoreType.DMA((2,))],
        compiler_params=_params(("arbitrary",)),
        name="combine",
    )(dest_t, dest_t, x2, route, y_sorted)


def _rank_block(i1, i2, lane_f, cnt_ref):
    tm = i1.shape[0]
    oh0 = lane_f == i1
    oh1 = lane_f == i2
    oh = jnp.where(oh0, 1.0, jnp.where(oh1, 1.0, 0.0))
    ri = lax.broadcasted_iota(I32, (tm, tm), 0)
    ci = lax.broadcasted_iota(I32, (tm, tm), 1)
    earlier = jnp.where(ci < ri, 1.0, 0.0).astype(BF16)
    base = cnt_ref[...] + jnp.dot(earlier, oh.astype(BF16), preferred_element_type=F32)
    r0 = jnp.sum(jnp.where(oh0, base, 0.0), axis=1, keepdims=True)
    r1 = jnp.sum(jnp.where(oh1, base, 0.0), axis=1, keepdims=True)
    cnt_ref[...] += jnp.sum(oh, axis=0, keepdims=True)
    return r0, r1


def _dest_kernel(route_ref, start_ref, dest_ref, *, tm, group):
    start = start_ref[...]
    for t in range(group):
        route = route_ref[t * tm:(t + 1) * tm, :]
        lane = lax.broadcasted_iota(I32, route.shape, 1)
        lane_f = lane.astype(F32)
        d0 = route[:, 4:5] + jnp.sum(jnp.where(lane_f == route[:, 0:1], start, 0.0), axis=1, keepdims=True)
        d1 = route[:, 5:6] + jnp.sum(jnp.where(lane_f == route[:, 1:2], start, 0.0), axis=1, keepdims=True)
        dd = jnp.where(lane == 0, d0, jnp.where(lane == 1, d1, 0.0))
        dest_ref[t] = dd.T[0:8, :].astype(I32)


def _dest(route, start_row, *, tm):
    n = route.shape[0]
    tiles = n // tm
    group = 4 if tiles % 4 == 0 else 1
    return pl.pallas_call(
        functools.partial(_dest_kernel, tm=tm, group=group), grid=(tiles // group,),
        in_specs=[pl.BlockSpec((group * tm, LANES), lambda i: (i, 0)),
                  pl.BlockSpec((1, LANES), lambda i: (0, 0))],
        out_specs=pl.BlockSpec((group, 8, tm), lambda i: (i, 0, 0)),
        out_shape=jax.ShapeDtypeStruct((tiles, 8, tm), I32),
        compiler_params=_params(("parallel",)),
        name="dest",
    )(route, start_row)


def _moe_plan(route, cnt_f, *, n_exp, tm):
    n = route.shape[0]
    nt = (2 * n + n_exp * (tm - 1)) // tm
    cnt = cnt_f[0, :n_exp].astype(I32)
    ptiles = (cnt + tm - 1) // tm
    tile_end = jnp.cumsum(ptiles)
    tile_start = tile_end - ptiles
    start_row = jnp.pad((tile_start * tm).astype(F32), (0, LANES - n_exp)).reshape(1, LANES)
    dest_t = _dest(route, start_row, tm=tm)
    n_used = tile_end[-1]
    tiles = jnp.arange(nt, dtype=I32)
    te = jnp.minimum(jnp.sum((tile_end[None, :] <= tiles[:, None]).astype(I32), axis=1), n_exp - 1)
    te_last = jnp.take(te, jnp.maximum(n_used - 1, 0))
    tile_expert = jnp.where(tiles < n_used, te, te_last)
    pad_start = tile_start * tm + cnt
    pad_len = ptiles * tm - cnt
    return tile_expert, n_used.reshape(1).astype(I32), dest_t, pad_start, pad_len, nt


def kernel(x_prompt, x_sample, cache_k, cache_v, cache_lf, state_C, state_n, state_m, page_table,
           norm1_g, w_in, b_igate, b_fgate_mlstm, b_fgate_fox, mlstm_norm_g, q_norm_g, k_norm_g,
           w_branch_mlstm, w_branch_fox, w_out, norm2_g, w_router_group, b_router_group,
           w_router_expert, b_router_expert, w_gate, w_up, w_down):
    depth = w_in.shape[0]
    assert depth == 1, "single-layer step"
    bp, tp, d = x_prompt.shape
    bs, ts, _ = x_sample.shape
    mh = b_igate.shape[-1]
    fh = b_fgate_fox.shape[-1]
    fd = q_norm_g.shape[-1]
    md = d // mh
    n_exp = w_gate.shape[1]
    n_pages = page_table.shape[1]
    page = cache_k.shape[2]
    assert 2 * mh + fh <= LANES and n_exp + N_GROUPS <= LANES
    np_tok, ns_tok = bp * tp, bs * ts
    l = 0

    w = w_in[l]
    o = 0
    secs = {}
    for name, width in (("m", 4 * d), ("mi", mh), ("mf", mh), ("f", 3 * d), ("ff", fh), ("g", 2 * d)):
        secs[name] = w[:, o:o + width]
        o += width
    pad = jnp.zeros((d, LANES - 2 * mh - fh), F32)
    w_secs = (secs["m"].astype(BF16), secs["f"].astype(BF16), secs["g"].astype(BF16),
              jnp.concatenate([secs["mi"], secs["mf"], secs["ff"], pad], axis=1).astype(BF16))
    bias_s = jnp.concatenate([b_igate[l], b_fgate_mlstm[l], b_fgate_fox[l],
                              jnp.zeros((LANES - 2 * mh - fh,), F32)]).reshape(1, LANES)
    g1 = norm1_g[l].reshape(1, d)
    qg = q_norm_g[l].reshape(1, fd)
    kg = k_norm_g[l].reshape(1, fd)
    g_m = mlstm_norm_g[l].reshape(1, d)
    wa = w_branch_mlstm[l].astype(BF16)
    wb = w_branch_fox[l].astype(BF16)
    wo = w_out[l].astype(BF16)
    g2 = norm2_g[l].reshape(1, d)
    wr32 = jnp.concatenate([w_router_expert[l], w_router_group[l],
                            jnp.zeros((d, LANES - n_exp - N_GROUPS), F32)], axis=1)
    wr_hi = wr32.astype(BF16)
    wr = jnp.concatenate([wr_hi, (wr32 - wr_hi.astype(F32)).astype(BF16)], axis=1)
    br = jnp.concatenate([b_router_expert[l], b_router_group[l],
                          jnp.zeros((LANES - n_exp - N_GROUPS,), F32)]).reshape(1, LANES)

    inproj = functools.partial(_inproj, g1=g1, w_secs=w_secs, bias_s=bias_s, qg=qg, kg=kg, mh=mh, fh=fh)
    pm_p, fq_p, k3_p, v3_p, kb_p, vb_p, gate_p, small_p, kt_p = inproj(
        x_prompt.reshape(np_tok, d), act_dtype=BF16, q_unit=LOG2E, emit_kt=True)
    pm_s, fq_s, k3_s, v3_s, kb_s, vb_s, gate_s, small_s = inproj(x_sample.reshape(ns_tok, d), act_dtype=F32,
                                                                 tm=256)

    ha_p, c_p, n_p, m_p = _mlstm_prompt(pm_p, kt_p, small_p, g_m, batch=bp, seq=tp, mh=mh)
    m0 = jnp.pad(state_m[l], ((0, 0), (0, LANES - mh))).reshape(bs, 1, LANES)
    ha_s, c_s, n_s, m_s = _mlstm(pm_s, small_s, g_m, batch=bs, seq=ts, mh=mh, nb=8,
                                 init=(state_C[l], state_n[l], m0), out_dtype=F32)

    lf_p = small_p[:, 2 * mh:2 * mh + fh]
    lf_s = small_s[:, 2 * mh:2 * mh + fh]
    qx_p, kx_p = _fbias(small_p, batch=bp, seq=tp, fh=fh, lf0=2 * mh)
    ob_p = _fox_prompt(fq_p, qx_p, kb_p, kx_p, vb_p, batch=bp, seq=tp, fh=fh)

    n_phys = cache_k.shape[1]
    lft_c = cache_lf[l].transpose(0, 2, 1).reshape(n_phys * fh, page)
    floc = _cumsum_lanes(lft_c, carry=False, block_rows=n_phys * fh // 8).reshape(n_phys, fh, page)
    ck = cache_k[l].reshape(n_phys, page * fh, fd)
    cv = cache_v[l].reshape(n_phys, page * fh, fd)
    ob_s = _fox_sample(page_table, fq_s, kb_s, vb_s, small_s, floc, ck, cv, fh=fh, lf0=2 * mh)

    tm = MOE_TILE
    x2, xn_rows, route, cnt_f = _post((x_prompt.reshape(np_tok, d), ha_p, ob_p, gate_p),
                                      (x_sample.reshape(ns_tok, d), ha_s, ob_s, gate_s),
                                      wa, wb, wo, g2, wr, br, n_exp=n_exp, n_groups=N_GROUPS, tm=tm)

    tile_expert, n_used, dest_t, pad_start, pad_len, nt = _moe_plan(route, cnt_f, n_exp=n_exp, tm=tm)
    x_sorted = _dispatch(pad_start, pad_len, n_used, dest_t, xn_rows, nt=nt, tm=tm)
    y_sorted = _experts(tile_expert, n_used, x_sorted.reshape(nt * tm * (d // LANES), LANES),
                        w_gate[l], w_up[l], w_down[l], tm=tm)
    y_p = _combine(dest_t, x2, route, y_sorted, row0=0, rows=np_tok, tm=tm)
    y_s = _combine(dest_t, x2, route, y_sorted, row0=np_tok, rows=ns_tok, tm=tm)

    return (y_p.reshape(bp, tp, d), y_s.reshape(bs, ts, d),
            k3_p.reshape(1, bp, tp, fh, fd), v3_p.reshape(1, bp, tp, fh, fd), lf_p.reshape(1, bp, tp, fh),
            k3_s.reshape(1, bs, ts, fh, fd), v3_s.reshape(1, bs, ts, fh, fd), lf_s.reshape(1, bs, ts, fh),
            c_p[None], n_p[None], m_p[:, 0, :mh][None],
            c_s[None], n_s[None], m_s[:, 0, :mh][None])
```

```python
import functools

import jax
import jax.numpy as jnp
import numpy as np
from jax import lax
from jax.experimental import pallas as pl
from jax.experimental.pallas import tpu as pltpu

F32 = jnp.float32
BF16 = jnp.bfloat16
I32 = jnp.int32
RMS_EPS = 1e-6
LANES = 128
MLSTM_CHUNK = 128
MLSTM_PROMPT_CHUNK = 256
N_GROUPS = 4
EXPERTS_PER_GROUP = 8
VMEM_LIMIT = 56 * 1024 * 1024
NEG_INF = float("-inf")
LOG2E = 1.4426950408889634
DMA_UNROLL = 8
MOE_TILE = 512
PAD_CHUNKS = (256, 128, 64, 32, 16, 8, 4, 2, 1)


def _params(sem, vmem=VMEM_LIMIT):
    return pltpu.CompilerParams(dimension_semantics=sem, vmem_limit_bytes=vmem)


def _log_sigmoid(x):
    return -(jnp.maximum(-x, 0.0) + jnp.log1p(jnp.exp(-jnp.abs(x))))


def _sigmoid(x):
    return 1.0 / (1.0 + jnp.exp(-x))


def _nt_dot(a, b):
    return lax.dot_general(a, b, (((1,), (1,)), ((), ())), preferred_element_type=F32)


def _tn_dot(a, b):
    return lax.dot_general(a, b, (((0,), (0,)), ((), ())), preferred_element_type=F32)


def _col_to_row(col, eye):
    return jnp.sum(jnp.where(eye, col, 0.0), axis=0, keepdims=True)


def _row_to_col(row, eye):
    return jnp.sum(jnp.where(eye, row, 0.0), axis=1, keepdims=True)


def _inproj_kernel(x_ref, g1_ref, wm_ref, wf_ref, wg_ref, ws_ref, bias_ref, qg_ref, kg_ref,
                   pm_ref, fq_ref, k3_ref, v3_ref, kb_ref, vb_ref, gate_ref, small_ref, *maybe_kt_ref,
                   d, mh, fh, fd, k_scale, q_scale):
    x = x_ref[...]
    ms = jnp.mean(x * x, axis=-1, keepdims=True)
    h = (x * lax.rsqrt(ms + RMS_EPS) * g1_ref[...]).astype(BF16)
    sections = ((0, wm_ref), (4 * d, wf_ref), (7 * d, wg_ref), (9 * d, ws_ref))

    def proj(c0, width):
        base, w_ref = [(b, r) for b, r in sections if b <= c0][-1]
        return jnp.dot(h, w_ref[:, c0 - base:c0 - base + width], preferred_element_type=F32)

    pm_ref[:, 0:d] = proj(0, d).astype(pm_ref.dtype)
    mk = proj(d, d) * k_scale
    pm_ref[:, d:2 * d] = mk.astype(pm_ref.dtype)
    if maybe_kt_ref:
        maybe_kt_ref[0][...] = mk.T.astype(maybe_kt_ref[0].dtype)
    pm_ref[:, 2 * d:3 * d] = proj(2 * d, d).astype(pm_ref.dtype)
    pm_ref[:, 3 * d:4 * d] = proj(3 * d, d).astype(pm_ref.dtype)

    def head_norm(a, g):
        ms_h = jnp.mean(a * a, axis=-1, keepdims=True)
        return a * lax.rsqrt(ms_h + RMS_EPS) * g

    fq = proj(4 * d, d)
    fk = proj(5 * d, d)
    fv = proj(6 * d, d)
    for hh in range(fh):
        sl = slice(hh * fd, (hh + 1) * fd)
        fq_ref[:, sl] = (head_norm(fq[:, sl], qg_ref[...]) * q_scale).astype(fq_ref.dtype)
        kn = head_norm(fk[:, sl], kg_ref[...])
        rows_h = pl.ds(hh, fq.shape[0], stride=fh)
        k3_ref[rows_h, :] = kn
        kb_ref[:, sl] = kn.astype(kb_ref.dtype)
        v3_ref[rows_h, :] = fv[:, sl]
    vb_ref[...] = fv.astype(vb_ref.dtype)

    gate_ref[...] = _sigmoid(proj(7 * d, 2 * d)).astype(gate_ref.dtype)

    sm = proj(9 * d, LANES) + bias_ref[...]
    lane = lax.broadcasted_iota(I32, sm.shape, 1)
    sm = jnp.where(lane < mh, sm, jnp.where(lane < 2 * mh + fh, _log_sigmoid(sm), 0.0))
    small_ref[...] = sm


def _inproj(x2d, g1, w_secs, bias_s, qg, kg, *, mh, fh, act_dtype, q_unit=1.0, emit_kt=False, tm=512):
    n, d = x2d.shape
    fd = d // fh
    md = d // mh
    kern = functools.partial(_inproj_kernel, d=d, mh=mh, fh=fh, fd=fd,
                             k_scale=md ** -0.5, q_scale=fd ** -0.5 * q_unit)
    row = lambda i: (i, 0)
    const = lambda i: (0, 0)
    kt_specs = [pl.BlockSpec((d, tm), lambda i: (0, i))] if emit_kt else []
    kt_shapes = [jax.ShapeDtypeStruct((d, n), act_dtype)] if emit_kt else []
    return pl.pallas_call(
        kern, grid=(n // tm,),
        in_specs=[pl.BlockSpec((tm, d), row),
                  pl.BlockSpec((1, d), const),
                  *[pl.BlockSpec(w.shape, const, pipeline_mode=pl.Buffered(1)) for w in w_secs],
                  pl.BlockSpec((1, LANES), const),
                  pl.BlockSpec((1, fd), const),
                  pl.BlockSpec((1, fd), const)],
        out_specs=[pl.BlockSpec((tm, 4 * d), row),
                   pl.BlockSpec((tm, d), row),
                   pl.BlockSpec((tm * fh, fd), row),
                   pl.BlockSpec((tm * fh, fd), row),
                   pl.BlockSpec((tm, d), row),
                   pl.BlockSpec((tm, d), row),
                   pl.BlockSpec((tm, 2 * d), row),
                   pl.BlockSpec((tm, LANES), row)] + kt_specs,
        out_shape=[jax.ShapeDtypeStruct((n, 4 * d), act_dtype),
                   jax.ShapeDtypeStruct((n, d), act_dtype),
                   jax.ShapeDtypeStruct((n * fh, fd), F32),
                   jax.ShapeDtypeStruct((n * fh, fd), F32),
                   jax.ShapeDtypeStruct((n, d), act_dtype),
                   jax.ShapeDtypeStruct((n, d), act_dtype),
                   jax.ShapeDtypeStruct((n, 2 * d), BF16),
                   jax.ShapeDtypeStruct((n, LANES), F32)] + kt_shapes,
        compiler_params=_params(("parallel",)),
        name="inproj",
    )(x2d, g1, *w_secs, bias_s, qg, kg)


def _mlstm_kernel(*refs, L, mh, md, nb, has_init):
    if has_init:
        (q_ref, k_ref, v_ref, o_ref, s_ref, g_ref, c0_ref, n0_ref, m0_ref,
         h_ref, c_ref, n_ref, m_ref) = refs
    else:
        q_ref, k_ref, v_ref, o_ref, s_ref, g_ref, h_ref, c_ref, n_ref, m_ref = refs

    @pl.when(pl.program_id(1) == 0)
    def _():
        if has_init:
            c_ref[...] = c0_ref[...]
            n_ref[...] = n0_ref[...]
            m_ref[...] = m0_ref[...]
        else:
            c_ref[...] = jnp.zeros_like(c_ref)
            n_ref[...] = jnp.zeros_like(n_ref)
            m_ref[...] = jnp.zeros_like(m_ref)

    ri = lax.broadcasted_iota(I32, (L, L), 0)
    ci = lax.broadcasted_iota(I32, (L, L), 1)
    tri = ci <= ri
    eye = ci == ri
    for bb in range(nb):
        s = s_ref[bb]
        m_all = m_ref[bb]
        lane = lax.broadcasted_iota(I32, m_all.shape, 1)
        m_out = m_all
        for hh in range(mh):
            sl = slice(hh * md, (hh + 1) * md)
            ig_col = s[:, hh:hh + 1]
            lf_col = s[:, mh + hh:mh + hh + 1]
            lf_row = _col_to_row(lf_col, eye)
            ig_row = _col_to_row(ig_col, eye)
            b_col = jnp.sum(jnp.where(tri, lf_row, 0.0), axis=1, keepdims=True)
            b_row = _col_to_row(b_col, eye)
            m_prev = m_all[:, hh:hh + 1]
            log_w = jnp.where(tri, b_col - b_row + ig_row, NEG_INF)
            log_inter = b_col + m_prev
            m_t = jnp.maximum(log_inter, jnp.max(log_w, axis=1, keepdims=True))
            w_intra = jnp.exp(log_w - m_t)
            w_inter = jnp.exp(log_inter - m_t)
            q = q_ref[bb, :, sl].astype(BF16)
            k = k_ref[bb, :, sl].astype(BF16)
            v = v_ref[bb, :, sl].astype(BF16)
            sm = _nt_dot(q, k) * w_intra
            c_prev = c_ref[bb, hh]
            n_prev = n_ref[bb, hh:hh + 1, :]
            num = (w_inter * jnp.dot(q, c_prev.astype(BF16), preferred_element_type=F32)
                   + jnp.dot(sm.astype(BF16), v, preferred_element_type=F32))
            den = (w_inter * jnp.sum(q.astype(F32) * n_prev, axis=1, keepdims=True)
                   + jnp.sum(sm, axis=1, keepdims=True))
            hv = num / jnp.maximum(jnp.abs(den), jnp.exp(-m_t))
            m_new = m_t[L - 1:L, :]
            b_last = b_col[L - 1:L, :]
            decay = jnp.exp(b_last + m_prev - m_new)
            w_rows = jnp.exp(b_last - b_col + ig_col - m_new)
            kw = k.astype(F32) * w_rows
            c_ref[bb, hh] = decay * c_prev + _tn_dot(kw.astype(BF16), v)
            n_ref[bb, hh:hh + 1, :] = decay * n_prev + jnp.sum(kw, axis=0, keepdims=True)
            m_out = jnp.where(lane == hh, m_new, m_out)
            ms = jnp.mean(hv * hv, axis=-1, keepdims=True)
            hn = hv * lax.rsqrt(ms + RMS_EPS) * g_ref[:, sl]
            h_ref[bb, :, sl] = (_sigmoid(o_ref[bb, :, sl].astype(F32)) * hn).astype(h_ref.dtype)
        m_ref[bb] = m_out


def _mlstm(pm, small, g_m, *, batch, seq, mh, nb=1, init=None, out_dtype=BF16):
    d = pm.shape[1] // 4
    md = d // mh
    L = MLSTM_CHUNK if seq % MLSTM_CHUNK == 0 else seq
    nc = seq // L
    ng = batch // nb
    pm4 = pm.reshape(ng, nb, seq, 4 * d)
    small4 = small.reshape(ng, nb, seq, LANES)
    sec = lambda j: (lambda b, c: (b, 0, c, j))
    in_specs = [pl.BlockSpec((None, nb, L, d), sec(0)), pl.BlockSpec((None, nb, L, d), sec(1)),
                pl.BlockSpec((None, nb, L, d), sec(2)), pl.BlockSpec((None, nb, L, d), sec(3)),
                pl.BlockSpec((None, nb, L, LANES), sec(0)),
                pl.BlockSpec((1, d), lambda b, c: (0, 0))]
    args = [pm4, pm4, pm4, pm4, small4, g_m]
    state_specs = [pl.BlockSpec((nb, mh, md, md), lambda b, c: (b, 0, 0, 0)),
                   pl.BlockSpec((nb, mh, md), lambda b, c: (b, 0, 0)),
                   pl.BlockSpec((nb, 1, LANES), lambda b, c: (b, 0, 0))]
    if init is not None:
        in_specs += state_specs
        args += list(init)
    kern = functools.partial(_mlstm_kernel, L=L, mh=mh, md=md, nb=nb, has_init=init is not None)
    h4, c_out, n_out, m_out = pl.pallas_call(
        kern, grid=(ng, nc),
        in_specs=in_specs,
        out_specs=[pl.BlockSpec((None, nb, L, d), lambda b, c: (b, 0, c, 0))] + state_specs,
        out_shape=[jax.ShapeDtypeStruct((ng, nb, seq, d), out_dtype),
                   jax.ShapeDtypeStruct((batch, mh, md, md), F32),
                   jax.ShapeDtypeStruct((batch, mh, md), F32),
                   jax.ShapeDtypeStruct((batch, 1, LANES), F32)],
        compiler_params=_params(("parallel", "arbitrary")),
        name="mlstm",
    )(*args)
    return h4.reshape(batch * seq, d), c_out, n_out, m_out


def _mlstm_prompt_kernel(q_ref, kt_ref, v_ref, o_ref, s_ref, g_ref, h_ref, c_ref, n_ref, m_ref, naug,
                         *, L, mh, md, nc):
    c = pl.program_id(1)

    @pl.when(c == 0)
    def _():
        c_ref[...] = jnp.zeros_like(c_ref)
        m_ref[...] = jnp.zeros_like(m_ref)
        naug[...] = jnp.zeros_like(naug)

    s = s_ref[...]
    ri = lax.broadcasted_iota(I32, (L, L), 0)
    ci = lax.broadcasted_iota(I32, (L, L), 1)
    tri = ci <= ri
    hp = lax.Precision.HIGHEST
    r = s.T[0:8, :]
    b_cols = jnp.dot(jnp.where(tri, 1.0, 0.0), s, precision=hp, preferred_element_type=F32)
    b_rows = pltpu.roll(jnp.dot(r, jnp.where(ri <= ci, 1.0, 0.0), precision=hp, preferred_element_type=F32),
                        8 - mh, axis=0)
    g = r - b_rows
    m_all = m_ref[0]
    lane = lax.broadcasted_iota(I32, m_all.shape, 1)
    m_out = m_all
    ones_v = jnp.ones((L, LANES), BF16)
    ones_r = jnp.ones((md, LANES), BF16)
    for hh in range(mh):
        sl = slice(hh * md, (hh + 1) * md)
        m_prev = m_all[:, hh:hh + 1]
        g_row = g[hh:hh + 1, :]
        b_col = b_cols[:, mh + hh:mh + hh + 1]
        a_col = jnp.maximum(m_prev, jnp.max(jnp.where(tri, g_row, NEG_INF), axis=1, keepdims=True))
        m_t = b_col + a_col
        m_new = m_t[L - 1:L, :]
        b_last = b_col[L - 1:L, :]
        w_rows = jnp.exp(b_last - b_rows[hh:hh + 1, :] + r[hh:hh + 1, :] - m_new)
        dc = jnp.exp(b_last + m_prev - m_new)
        m_out = jnp.where(lane == hh, m_new, m_out)
        q = q_ref[:, sl]
        kt = kt_ref[sl, :]
        v_aug = jnp.concatenate([v_ref[:, sl], ones_v], axis=1)
        w_intra = jnp.exp(jnp.where(tri, g_row - a_col, NEG_INF))
        sm = jnp.dot(q, kt, preferred_element_type=F32) * w_intra
        c_prev = c_ref[0, hh]
        n_prev = naug[hh]
        c_aug = jnp.concatenate([c_prev, n_prev], axis=1).astype(BF16)
        comb = (jnp.exp(m_prev - a_col) * jnp.dot(q, c_aug, preferred_element_type=F32)
                + jnp.dot(sm.astype(BF16), v_aug, preferred_element_type=F32))
        den = jnp.maximum(jnp.abs(comb[:, md:]), jnp.exp(-m_t))
        inv = 1.0 / den
        hv = comb[:, 0:md] * jnp.concatenate([inv] * (md // LANES), axis=1)
        ms = jnp.dot((hv * hv).astype(BF16), ones_r, preferred_element_type=F32) * (1.0 / md)
        rs = lax.rsqrt(ms + RMS_EPS)
        hn = hv * jnp.concatenate([rs] * (md // LANES), axis=1) * g_ref[:, sl]
        h_ref[:, sl] = (_sigmoid(o_ref[:, sl].astype(F32)) * hn).astype(h_ref.dtype)
        kw = (kt.astype(F32) * w_rows).astype(BF16)
        upd = jnp.dot(kw, v_aug, preferred_element_type=F32)
        c_ref[0, hh] = dc * c_prev + upd[:, 0:md]
        naug[hh] = dc * n_prev + upd[:, md:]
    m_ref[0] = m_out

    @pl.when(c == nc - 1)
    def _():
        ei = lax.broadcasted_iota(I32, (md, md), 0) == lax.broadcasted_iota(I32, (md, md), 1)
        for hh in range(mh):
            cols = jnp.concatenate([naug[hh]] * (md // LANES), axis=1)
            n_ref[0, hh:hh + 1, :] = jnp.sum(jnp.where(ei, cols, 0.0), axis=0, keepdims=True)


def _mlstm_prompt(pm, kt, small, g_m, *, batch, seq, mh):
    d = pm.shape[1] // 4
    md = d // mh
    L = MLSTM_PROMPT_CHUNK
    assert seq % L == 0
    nc = seq // L
    tok = lambda j: (lambda b, c: (b * nc + c, j))
    kern = functools.partial(_mlstm_prompt_kernel, L=L, mh=mh, md=md, nc=nc)
    return pl.pallas_call(
        kern, grid=(batch, nc),
        in_specs=[pl.BlockSpec((L, d), tok(0)),
                  pl.BlockSpec((d, L), lambda b, c: (0, b * nc + c)),
                  pl.BlockSpec((L, d), tok(2)), pl.BlockSpec((L, d), tok(3)),
                  pl.BlockSpec((L, LANES), tok(0)),
                  pl.BlockSpec((1, d), lambda b, c: (0, 0))],
        out_specs=[pl.BlockSpec((L, d), tok(0)),
                   pl.BlockSpec((1, mh, md, md), lambda b, c: (b, 0, 0, 0)),
                   pl.BlockSpec((1, mh, md), lambda b, c: (b, 0, 0)),
                   pl.BlockSpec((1, 1, LANES), lambda b, c: (b, 0, 0))],
        out_shape=[jax.ShapeDtypeStruct((batch * seq, d), BF16),
                   jax.ShapeDtypeStruct((batch, mh, md, md), F32),
                   jax.ShapeDtypeStruct((batch, mh, md), F32),
                   jax.ShapeDtypeStruct((batch, 1, LANES), F32)],
        scratch_shapes=[pltpu.VMEM((mh, md, LANES), F32)],
        compiler_params=_params(("parallel", "arbitrary")),
        name="mlstm_prompt",
    )(pm, kt, pm, pm, small, g_m)


def _cumsum_kernel(x_ref, o_ref, *, chunk, carry):
    rows, t = x_ref.shape
    si = lax.broadcasted_iota(I32, (chunk, chunk), 0)
    ti = lax.broadcasted_iota(I32, (chunk, chunk), 1)
    tri = (si <= ti).astype(F32)
    run = jnp.zeros((rows, 1), F32)
    for j in range(t // chunk):
        sl = slice(j * chunk, (j + 1) * chunk)
        loc = jnp.dot(x_ref[:, sl], tri, precision=lax.Precision.HIGHEST, preferred_element_type=F32)
        if carry:
            loc = loc + run
            run = loc[:, chunk - 1:chunk]
        o_ref[:, sl] = loc


def _cumsum_lanes(x, *, carry, block_rows):
    rows, t = x.shape
    kern = functools.partial(_cumsum_kernel, chunk=LANES, carry=carry)
    return pl.pallas_call(
        kern, grid=(rows // block_rows,),
        in_specs=[pl.BlockSpec((block_rows, t), lambda i: (i, 0))],
        out_specs=pl.BlockSpec((block_rows, t), lambda i: (i, 0)),
        out_shape=jax.ShapeDtypeStruct((rows, t), F32),
        compiler_params=_params(("parallel",)),
        name="cumsum",
    )(x)


BIAS_LANES = 8


def _fbias_kernel(s_ref, pq_ref, pk_ref, cq_ref, ck_ref, qx_ref, kx_ref, run_ref):
    @pl.when(pl.program_id(1) == 0)
    def _():
        run_ref[...] = jnp.zeros_like(run_ref)

    s = s_ref[...]
    L = s.shape[0]
    ti = lax.broadcasted_iota(I32, (L, L), 0)
    si = lax.broadcasted_iota(I32, (L, L), 1)
    tri = jnp.where(si <= ti, 1.0, 0.0).astype(BF16)

    def split3(x):
        hi = x.astype(BF16)
        r1 = x - hi.astype(F32)
        mid = r1.astype(BF16)
        return jnp.concatenate([hi, mid, (r1 - mid.astype(F32)).astype(BF16)], axis=1)

    cs = jnp.dot(tri, split3(s), preferred_element_type=F32)
    f_all = (cs[:, 0:LANES] + cs[:, LANES:2 * LANES] + cs[:, 2 * LANES:3 * LANES]) + run_ref[...]
    run_ref[...] = f_all[L - 1:L, :]
    terms = split3(f_all * LOG2E)
    qx_ref[...] = jnp.dot(terms, pq_ref[...], preferred_element_type=F32) + cq_ref[...]
    kx_ref[...] = (jnp.dot(terms, pk_ref[...], preferred_element_type=F32) + ck_ref[...]).astype(BF16)


def _fbias(small, *, batch, seq, fh, lf0):
    assert BIAS_LANES * fh <= LANES
    n = small.shape[0]
    L = 4 * LANES if seq % (4 * LANES) == 0 else LANES
    nc = seq // L
    pq = np.zeros((3 * LANES, LANES), np.float32)
    pk = np.zeros((3 * LANES, LANES), np.float32)
    cq = np.zeros((1, LANES), np.float32)
    ck = np.zeros((1, LANES), np.float32)
    for h in range(fh):
        for j in range(3):
            pq[j * LANES + lf0 + h, BIAS_LANES * h + j] = 1.0
            pk[j * LANES + lf0 + h, BIAS_LANES * h + 3 + j] = -1.0
            cq[0, BIAS_LANES * h + 3 + j] = 1.0
            ck[0, BIAS_LANES * h + j] = 1.0
    const = lambda b, c: (0, 0)
    tok = lambda b, c: (b * nc + c, 0)
    return pl.pallas_call(
        _fbias_kernel, grid=(batch, nc),
        in_specs=[pl.BlockSpec((L, LANES), tok),
                  pl.BlockSpec((3 * LANES, LANES), const), pl.BlockSpec((3 * LANES, LANES), const),
                  pl.BlockSpec((1, LANES), const), pl.BlockSpec((1, LANES), const)],
        out_specs=[pl.BlockSpec((L, LANES), tok), pl.BlockSpec((L, LANES), tok)],
        out_shape=[jax.ShapeDtypeStruct((n, LANES), F32), jax.ShapeDtypeStruct((n, LANES), BF16)],
        scratch_shapes=[pltpu.VMEM((1, LANES), F32)],
        compiler_params=_params(("parallel", "arbitrary")),
        name="fbias",
    )(small, jnp.asarray(pq, BF16), jnp.asarray(pk, BF16), jnp.asarray(cq), jnp.asarray(ck))


def _fox_prompt_kernel(q_ref, qx_ref, k_ref, kx_ref, v_ref, o_ref, *, seq, tq, gh, fd):
    nq = seq // tq
    ri = lax.broadcasted_iota(I32, (tq, tq), 0)
    ci = lax.broadcasted_iota(I32, (tq, tq), 1)
    causal = ci <= ri

    def softmax_step(carry, s, v):
        m, l, acc = carry
        m_new = jnp.maximum(m, jnp.max(s, axis=1, keepdims=True))
        a = jnp.exp2(m - m_new)
        p = jnp.exp2(s - m_new)
        l = a * l + jnp.sum(p, axis=1, keepdims=True)
        acc = a * acc + jnp.dot(p.astype(BF16), v, preferred_element_type=F32)
        return m_new, l, acc

    head0 = pl.program_id(1) * gh
    bias_head = lax.broadcasted_iota(I32, (tq, LANES), 1) // BIAS_LANES

    def q_body(qi, _):
        q0 = pl.multiple_of(qi * tq, tq)
        qx = qx_ref[pl.ds(q0, tq), :]
        qas = [jnp.concatenate([q_ref[pl.ds(q0, tq), g * fd:(g + 1) * fd],
                                jnp.where(bias_head == head0 + g, qx, 0.0).astype(BF16)], axis=1)
               for g in range(gh)]

        def block(g, carry_g, k0, masked):
            gs = slice(g * fd, (g + 1) * fd)
            ka = jnp.concatenate([k_ref[pl.ds(k0, tq), gs], kx_ref[pl.ds(k0, tq), :]], axis=1)
            s = _nt_dot(qas[g], ka)
            if masked:
                s = jnp.where(causal, s, NEG_INF)
            return softmax_step(carry_g, s, v_ref[pl.ds(k0, tq), gs])

        def kv_body(kj, carry):
            k0 = pl.multiple_of(kj * tq, tq)
            return tuple(block(g, carry[g], k0, False) for g in range(gh))

        init = tuple((jnp.full((tq, 1), NEG_INF, F32), jnp.zeros((tq, 1), F32), jnp.zeros((tq, fd), F32))
                     for _ in range(gh))
        carry = lax.fori_loop(0, qi, kv_body, init)
        for g in range(gh):
            _, l, acc = block(g, carry[g], q0, True)
            o_ref[pl.ds(q0, tq), g * fd:(g + 1) * fd] = (acc / l).astype(o_ref.dtype)
        return 0

    lax.fori_loop(0, nq, q_body, 0)


def _fox_prompt(fq, qx, kb, kx, vb, *, batch, seq, fh, tq=1024, gh=4):
    n, d = fq.shape
    fd = d // fh
    blk = lambda b, h: (b, h)
    bias = pl.BlockSpec((seq, LANES), lambda b, h: (b, 0))
    kern = functools.partial(_fox_prompt_kernel, seq=seq, tq=tq, gh=gh, fd=fd)
    return pl.pallas_call(
        kern, grid=(batch, fh // gh),
        in_specs=[pl.BlockSpec((seq, gh * fd), blk), bias, pl.BlockSpec((seq, gh * fd), blk), bias,
                  pl.BlockSpec((seq, gh * fd), blk)],
        out_specs=pl.BlockSpec((seq, gh * fd), blk),
        out_shape=jax.ShapeDtypeStruct((n, d), BF16),
        compiler_params=_params(("parallel", "parallel")),
        name="fox_prompt",
    )(fq, qx, kb, kx, vb)


def _fox_sample_kernel(pt_ref, q_ref, kn_ref, vn_ref, s_ref, *rest, n_pages, page, fh, fd, ts, lf0):
    floc = rest[0:n_pages]
    kpages = rest[n_pages:2 * n_pages]
    vpages = rest[2 * n_pages:3 * n_pages]
    o_ref = rest[3 * n_pages]
    p_scr, pn_scr, l_scr = rest[3 * n_pages + 1:]
    d = fh * fd
    rows = fh * ts

    def scores():
        q = q_ref[...].astype(F32)
        qt = jnp.concatenate([q] * fh, axis=0)
        r_head = lax.broadcasted_iota(I32, (rows, d), 0) // ts
        c_head = lax.broadcasted_iota(I32, (rows, d), 1) // fd
        qbd = jnp.where(r_head == c_head, qt, 0.0).astype(BF16)

        off = jnp.zeros((fh, 1), F32)
        fk_pages = []
        for i in range(n_pages):
            fp = floc[i][...] + off
            fk_pages.append(fp)
            off = fp[:, page - 1:page]
        fk = jnp.concatenate(fk_pages, axis=1)
        fk_rows = jnp.concatenate(
            [jnp.broadcast_to(fk[h:h + 1, :], (ts, fk.shape[1])) for h in range(fh)], axis=0)

        sm = s_ref[...]
        ri = lax.broadcasted_iota(I32, (ts, ts), 0)
        ci = lax.broadcasted_iota(I32, (ts, ts), 1)
        eye = ci == ri
        tri = ci <= ri
        fq_cols, bias_new = [], []
        for h in range(fh):
            lf_col = sm[:, lf0 + h:lf0 + h + 1]
            lf_row = _col_to_row(lf_col, eye)
            cum_col = jnp.sum(jnp.where(tri, lf_row, 0.0), axis=1, keepdims=True)
            fq_h = off[h:h + 1, :] + cum_col
            fq_cols.append(fq_h)
            bias_new.append(fq_h - _col_to_row(fq_h, eye))
        fq_col = jnp.concatenate(fq_cols, axis=0)
        bias_n = jnp.concatenate(bias_new, axis=0)
        causal_n = jnp.concatenate([tri] * fh, axis=0)

        s_parts = []
        for i in range(0, n_pages, 2):
            kp = jnp.concatenate(
                [jnp.concatenate([kpages[i + j][pl.ds(h, page, stride=fh), :] for h in range(fh)], axis=1)
                 for j in range(2)],
                axis=0).astype(BF16)
            s_parts.append(_nt_dot(qbd, kp))
        s_past = jnp.concatenate(s_parts, axis=1) + (fq_col - fk_rows)
        s_new = _nt_dot(qbd, kn_ref[...].astype(BF16)) + bias_n
        s_new = jnp.where(causal_n, s_new, NEG_INF)
        m = jnp.maximum(jnp.max(s_past, axis=1, keepdims=True), jnp.max(s_new, axis=1, keepdims=True))
        p_past = jnp.exp(s_past - m)
        p_new = jnp.exp(s_new - m)
        l_scr[...] = jnp.sum(p_past, axis=1, keepdims=True) + jnp.sum(p_new, axis=1, keepdims=True)
        p_scr[...] = p_past.astype(BF16)
        pn_scr[...] = p_new

    def values():
        acc = jnp.dot(pn_scr[...].astype(BF16), vn_ref[...].astype(BF16), preferred_element_type=F32)
        for i in range(0, n_pages, 2):
            vp = jnp.concatenate(
                [jnp.concatenate([vpages[i + j][pl.ds(h, page, stride=fh), :] for h in range(fh)], axis=1)
                 for j in range(2)],
                axis=0).astype(BF16)
            acc = acc + jnp.dot(p_scr[:, i * page:(i + 2) * page], vp, preferred_element_type=F32)
        acc = acc / l_scr[...]
        o_ref[...] = jnp.concatenate(
            [acc[h * ts:(h + 1) * ts, h * fd:(h + 1) * fd] for h in range(fh)], axis=1)

    scores()
    values()


def _fox_sample(page_table, fq, kb, vb, small, floc, cache_k, cache_v, *, fh, lf0):
    bs, n_pages = page_table.shape
    n, d = fq.shape
    ts = n // bs
    fd = d // fh
    page = cache_k.shape[1] // fh
    rows = fh * ts
    tok = lambda b, pt: (b, 0)

    def pmap(i):
        return lambda b, pt: (pt[b, i], 0, 0)

    in_specs = ([pl.BlockSpec((ts, d), tok), pl.BlockSpec((ts, d), tok), pl.BlockSpec((ts, d), tok),
                 pl.BlockSpec((ts, LANES), tok)]
                + [pl.BlockSpec((None, fh, page), pmap(i)) for i in range(n_pages)]
                + [pl.BlockSpec((None, page * fh, fd), pmap(i)) for i in range(n_pages)]
                + [pl.BlockSpec((None, page * fh, fd), pmap(i)) for i in range(n_pages)])
    kern = functools.partial(_fox_sample_kernel, n_pages=n_pages, page=page, fh=fh, fd=fd, ts=ts, lf0=lf0)
    return pl.pallas_call(
        kern,
        grid_spec=pltpu.PrefetchScalarGridSpec(
            num_scalar_prefetch=1, grid=(bs,), in_specs=in_specs,
            out_specs=pl.BlockSpec((ts, d), tok),
            scratch_shapes=[pltpu.VMEM((rows, n_pages * page), BF16),
                            pltpu.VMEM((rows, ts), F32),
                            pltpu.VMEM((rows, 1), F32)]),
        out_shape=jax.ShapeDtypeStruct((n, d), F32),
        compiler_params=_params(("arbitrary",)),
        name="fox_sample",
    )(page_table, fq, kb, vb, small, *([floc] * n_pages), *([cache_k] * n_pages), *([cache_v] * n_pages))


def _post_kernel(xp_ref, hap_ref, obp_ref, gp_ref, xs_ref, has_ref, obs_ref, gs_ref, *rest, tiles_p, **kw):
    cnt_ref = rest[-1]

    @pl.when(pl.program_id(0) == 0)
    def _():
        cnt_ref[...] = jnp.zeros_like(cnt_ref)

    @pl.when(pl.program_id(0) < tiles_p)
    def _():
        _post_body(xp_ref, hap_ref, obp_ref, gp_ref, *rest, **kw)

    @pl.when(pl.program_id(0) >= tiles_p)
    def _():
        _post_body(xs_ref, has_ref, obs_ref, gs_ref, *rest, **kw)


def _post_body(x_ref, ha_ref, ob_ref, gate_ref, wa_ref, wb_ref, wo_ref, g2_ref, wr_ref, br_ref,
               x2_ref, xn_ref, route_ref, cnt_ref, *, d, n_exp, n_groups):
    ba = jnp.dot(ha_ref[...].astype(BF16), wa_ref[...], preferred_element_type=F32)
    bb = jnp.dot(ob_ref[...].astype(BF16), wb_ref[...], preferred_element_type=F32)
    gates = gate_ref[...].astype(F32)
    merged = gates[:, 0:d] * ba + gates[:, d:2 * d] * bb
    x2 = x_ref[...] + jnp.dot(merged.astype(BF16), wo_ref[...], preferred_element_type=F32)
    x2_ref[...] = x2
    ms = jnp.mean(x2 * x2, axis=-1, keepdims=True)
    xn = x2 * lax.rsqrt(ms + RMS_EPS) * g2_ref[...]
    tm = xn.shape[0]
    for j in range(d // LANES):
        xn_ref[pl.ds(j, tm, stride=d // LANES), :] = xn[:, j * LANES:(j + 1) * LANES]

    xh = xn.astype(BF16)
    xl = (xn - xh.astype(F32)).astype(BF16)
    hh = jnp.dot(xh, wr_ref[...], preferred_element_type=F32)
    logits = (hh[:, 0:LANES] + hh[:, LANES:2 * LANES]
              + jnp.dot(xl, wr_ref[:, 0:LANES], preferred_element_type=F32)
              + br_ref[...])
    lane = lax.broadcasted_iota(I32, logits.shape, 1)
    lane_f = lane.astype(F32)
    big = float(LANES)
    epg = n_exp // n_groups
    in_groups = (lane >= n_exp) & (lane < n_exp + n_groups)
    gl = jnp.where(in_groups, logits, NEG_INF)
    gmax = jnp.max(gl, axis=1, keepdims=True)
    gidx = jnp.min(jnp.where(gl == gmax, lane_f, big), axis=1, keepdims=True) - float(n_exp)
    g_p = 1.0 / jnp.sum(jnp.exp(gl - gmax), axis=1, keepdims=True)
    in_group = (lane < n_exp) & ((lane // epg).astype(F32) == gidx)
    el = jnp.where(in_group, logits, NEG_INF)
    m1 = jnp.max(el, axis=1, keepdims=True)
    i1 = jnp.min(jnp.where(el == m1, lane_f, big), axis=1, keepdims=True)
    el2 = jnp.where(lane_f == i1, NEG_INF, el)
    m2 = jnp.max(el2, axis=1, keepdims=True)
    i2 = jnp.min(jnp.where(el2 == m2, lane_f, big), axis=1, keepdims=True)
    esum = jnp.sum(jnp.exp(el - m1), axis=1, keepdims=True)
    p1 = 1.0 / esum
    p2 = jnp.exp(m2 - m1) / esum
    psum = p1 + p2
    w1 = p1 / psum * g_p
    w2 = p2 / psum * g_p
    r0, r1 = _rank_block(i1, i2, lane_f, cnt_ref)
    route_ref[...] = jnp.where(lane == 0, i1, jnp.where(lane == 1, i2, jnp.where(
        lane == 2, w1, jnp.where(lane == 3, w2, jnp.where(lane == 4, r0, jnp.where(lane == 5, r1, 0.0))))))


def _post(acts_p, acts_s, wa, wb, wo, g2, wr, br, *, n_exp, n_groups, tm=256):
    n_p, d = acts_p[0].shape
    n_s = acts_s[0].shape[0]
    tiles_p, tiles_s = n_p // tm, n_s // tm
    n = n_p + n_s
    rpt = d // LANES
    row_p = lambda i: (jnp.minimum(i, tiles_p - 1), 0)
    row_s = lambda i: (jnp.maximum(i - tiles_p, 0), 0)
    row = lambda i: (i, 0)
    const = lambda i: (0, 0)
    kern = functools.partial(_post_kernel, tiles_p=tiles_p, d=d, n_exp=n_exp, n_groups=n_groups)
    wspec = lambda: pl.BlockSpec((d, d), const, pipeline_mode=pl.Buffered(1))
    act_specs = lambda r: [pl.BlockSpec((tm, d), r), pl.BlockSpec((tm, d), r), pl.BlockSpec((tm, d), r),
                           pl.BlockSpec((tm, 2 * d), r)]
    return pl.pallas_call(
        kern, grid=(tiles_p + tiles_s,),
        in_specs=act_specs(row_p) + act_specs(row_s) + [
            wspec(), wspec(), wspec(),
            pl.BlockSpec((1, d), const), pl.BlockSpec((d, 2 * LANES), const), pl.BlockSpec((1, LANES), const)],
        out_specs=[pl.BlockSpec((tm, d), row), pl.BlockSpec((tm * rpt, LANES), row),
                   pl.BlockSpec((tm, LANES), row), pl.BlockSpec((1, LANES), const)],
        out_shape=[jax.ShapeDtypeStruct((n, d), F32), jax.ShapeDtypeStruct((n * rpt, LANES), F32),
                   jax.ShapeDtypeStruct((n, LANES), F32), jax.ShapeDtypeStruct((1, LANES), F32)],
        compiler_params=_params(("arbitrary",)),
        name="post",
    )(*acts_p, *acts_s, wa, wb, wo, g2, wr, br)


def _dispatch_kernel(ps_ref, pl_ref, nu_ref, dest_ref, xn_ref, xs_hbm, stage, zbuf, sem_rows, sem_pad,
                     *, tm, n_tiles, n_exp, nt):
    i = pl.program_id(0)
    slot = i % 2

    def pad_dmas(act):
        for e in range(n_exp):
            pos = ps_ref[e]
            left = pl_ref[e]
            for c in PAD_CHUNKS:
                @pl.when((left & c) != 0)
                def _(pos=pos, c=c):
                    act(pltpu.make_async_copy(zbuf.at[pl.ds(0, c)], xs_hbm.at[pl.ds(pos, c)], sem_pad))
                pos = pos + (left & c)

        def unused_tile(t, carry):
            for part in range(tm // PAD_CHUNKS[0]):
                act(pltpu.make_async_copy(
                    zbuf, xs_hbm.at[pl.ds(t * tm + part * PAD_CHUNKS[0], PAD_CHUNKS[0])], sem_pad))
            return carry
        lax.fori_loop(nu_ref[0], nt, unused_tile, 0)

    @pl.when(i == 0)
    def _():
        zbuf[...] = jnp.zeros_like(zbuf)
        pad_dmas(lambda cp: cp.start())

    def wait_tile(s):
        for _ in range(2):
            pltpu.make_async_copy(stage.at[s], xs_hbm.at[pl.ds(0, tm)], sem_rows.at[s]).wait()

    @pl.when(i >= 2)
    def _():
        wait_tile(slot)

    stage[slot] = xn_ref[...]

    def body(j, c):
        for u in range(DMA_UNROLL // 2):
            r = j * (DMA_UNROLL // 2) + u
            for kk in range(2):
                pltpu.make_async_copy(stage.at[slot, r], xs_hbm.at[dest_ref[0, kk, r]],
                                      sem_rows.at[slot]).start(priority=kk)
        return c
    lax.fori_loop(0, tm // (DMA_UNROLL // 2), body, 0)

    @pl.when(i == n_tiles - 1)
    def _():
        if n_tiles > 1:
            wait_tile(1 - slot)
        wait_tile(slot)
        pad_dmas(lambda cp: cp.wait())


def _dispatch(pad_start, pad_len, n_used, dest_t, xn_rows, *, nt, tm):
    n_tiles = dest_t.shape[0]
    n_exp = pad_start.shape[0]
    rpt = xn_rows.shape[0] // (n_tiles * tm)
    xn3 = xn_rows.reshape(n_tiles * tm, rpt, LANES)
    kern = functools.partial(_dispatch_kernel, tm=tm, n_tiles=n_tiles, n_exp=n_exp, nt=nt)
    return pl.pallas_call(
        kern,
        grid_spec=pltpu.PrefetchScalarGridSpec(
            num_scalar_prefetch=3, grid=(n_tiles,),
            in_specs=[pl.BlockSpec((1, 8, tm), lambda i, ps, pln, nu: (i, 0, 0), memory_space=pltpu.SMEM),
                      pl.BlockSpec((tm, rpt, LANES), lambda i, ps, pln, nu: (i, 0, 0))],
            out_specs=pl.BlockSpec(memory_space=pl.ANY),
            scratch_shapes=[pltpu.VMEM((2, tm, rpt, LANES), F32),
                            pltpu.VMEM((PAD_CHUNKS[0], rpt, LANES), F32),
                            pltpu.SemaphoreType.DMA((2,)), pltpu.SemaphoreType.DMA(())]),
        out_shape=jax.ShapeDtypeStruct((nt * tm, rpt, LANES), F32),
        compiler_params=_params(("arbitrary",)),
        name="dispatch",
    )(pad_start, pad_len, n_used, dest_t, xn3)


def _experts_kernel(te_ref, nu_ref, x_ref, wg_ref, wu_ref, wd_ref, y_ref, wg_b, wu_b, wd_b, *, tm, rpt):
    t = pl.program_id(0)

    @pl.when((t == 0) | (te_ref[t] != te_ref[jnp.maximum(t - 1, 0)]))
    def _():
        wg_b[...] = wg_ref[0].astype(BF16)
        wu_b[...] = wu_ref[0].astype(BF16)
        wd_b[...] = wd_ref[0].astype(BF16)

    @pl.when(t < nu_ref[0])
    def _():
        x = jnp.concatenate([x_ref[pl.ds(j, tm, stride=rpt), :] for j in range(rpt)], axis=1).astype(BF16)
        g = jnp.dot(x, wg_b[...], preferred_element_type=F32)
        u = jnp.dot(x, wu_b[...], preferred_element_type=F32)
        hg = (g * _sigmoid(g)) * u
        y = jnp.dot(hg.astype(BF16), wd_b[...], preferred_element_type=F32)
        for j in range(rpt):
            y_ref[pl.ds(j, tm, stride=rpt), :] = y[:, j * LANES:(j + 1) * LANES]

    @pl.when(t >= nu_ref[0])
    def _():
        y_ref[...] = jnp.zeros_like(y_ref)


def _experts(tile_expert, n_used, x_sorted, w_gate, w_up, w_down, *, tm):
    nt = tile_expert.shape[0]
    n_exp, d, de = w_gate.shape
    rpt = d // LANES
    emap = lambda t, te, nu: (te[t], 0, 0)
    kern = functools.partial(_experts_kernel, tm=tm, rpt=rpt)
    return pl.pallas_call(
        kern,
        grid_spec=pltpu.PrefetchScalarGridSpec(
            num_scalar_prefetch=2, grid=(nt,),
            in_specs=[pl.BlockSpec((tm * rpt, LANES), lambda t, te, nu: (jnp.minimum(t, nu[0] - 1), 0)),
                      pl.BlockSpec((1, d, de), emap), pl.BlockSpec((1, d, de), emap),
                      pl.BlockSpec((1, de, d), emap)],
            out_specs=pl.BlockSpec((tm * rpt, LANES), lambda t, te, nu: (t, 0)),
            scratch_shapes=[pltpu.VMEM((d, de), BF16), pltpu.VMEM((d, de), BF16), pltpu.VMEM((de, d), BF16)]),
        out_shape=jax.ShapeDtypeStruct((nt * tm * rpt, LANES), F32),
        compiler_params=_params(("arbitrary",)),
        name="experts",
    )(tile_expert, n_used, x_sorted, w_gate, w_up, w_down)


def _combine_kernel(idx_cur, idx_nxt, x2_ref, route_ref, ys_hbm, y_ref, buf, sem, *, tm, n_tiles, rpt):
    i = pl.program_id(0)
    slot = i % 2

    def gather(idx_ref, dst_slot):
        def body(j, c):
            for u in range(DMA_UNROLL // 2):
                r = j * (DMA_UNROLL // 2) + u
                for kk in range(2):
                    src = pl.multiple_of(idx_ref[0, kk, r] * rpt, rpt)
                    pltpu.make_async_copy(ys_hbm.at[pl.ds(src, rpt), :],
                                          buf.at[dst_slot, kk, pl.ds(r * rpt, rpt), :],
                                          sem.at[dst_slot]).start(priority=kk)
            return c
        lax.fori_loop(0, tm // (DMA_UNROLL // 2), body, 0)

    @pl.when(i == 0)
    def _():
        gather(idx_cur, 0)

    @pl.when(i + 1 < n_tiles)
    def _():
        gather(idx_nxt, 1 - slot)

    for kk in range(2):
        pltpu.make_async_copy(ys_hbm.at[pl.ds(0, tm * rpt), :], buf.at[slot, kk], sem.at[slot]).wait()
    route = route_ref[...]
    w0 = route[:, 2:3]
    w1 = route[:, 3:4]
    for j in range(rpt):
        sl = slice(j * LANES, (j + 1) * LANES)
        y_ref[:, sl] = (x2_ref[:, sl] + w0 * buf[slot, 0, pl.ds(j, tm, stride=rpt), :]
                        + w1 * buf[slot, 1, pl.ds(j, tm, stride=rpt), :])


def _combine(dest_t, x2, route, y_sorted, *, row0, rows, tm=256):
    d = x2.shape[1]
    rpt = d // LANES
    t0 = row0 // tm
    n_tiles = rows // tm
    kern = functools.partial(_combine_kernel, tm=tm, n_tiles=n_tiles, rpt=rpt)
    return pl.pallas_call(
        kern, grid=(n_tiles,),
        in_specs=[pl.BlockSpec((1, 8, tm), lambda i: (t0 + i, 0, 0), memory_space=pltpu.SMEM),
                  pl.BlockSpec((1, 8, tm), lambda i: (t0 + jnp.minimum(i + 1, n_tiles - 1), 0, 0),
                               memory_space=pltpu.SMEM),
                  pl.BlockSpec((tm, d), lambda i: (t0 + i, 0)),
                  pl.BlockSpec((tm, LANES), lambda i: (t0 + i, 0)),
                  pl.BlockSpec(memory_space=pl.ANY)],
        out_specs=pl.BlockSpec((tm, d), lambda i: (i, 0)),
        out_shape=jax.ShapeDtypeStruct((rows, d), F32),
        scratch_shapes=[pltpu.VMEM((2, 2, tm * rpt, LANES), F32), pltpu.SemaphoreType.DMA((2,))],
        compiler_params=_params(("arbitrary",)),
        name="combine",
    )(dest_t, dest_t, x2, route, y_sorted)


def _rank_block(i1, i2, lane_f, cnt_ref):
    tm = i1.shape[0]
    oh0 = lane_f == i1
    oh1 = lane_f == i2
    oh = jnp.where(oh0, 1.0, jnp.where(oh1, 1.0, 0.0))
    ri = lax.broadcasted_iota(I32, (tm, tm), 0)
    ci = lax.broadcasted_iota(I32, (tm, tm), 1)
    earlier = jnp.where(ci < ri, 1.0, 0.0).astype(BF16)
    base = cnt_ref[...] + jnp.dot(earlier, oh.astype(BF16), preferred_element_type=F32)
    r0 = jnp.sum(jnp.where(oh0, base, 0.0), axis=1, keepdims=True)
    r1 = jnp.sum(jnp.where(oh1, base, 0.0), axis=1, keepdims=True)
    cnt_ref[...] += jnp.sum(oh, axis=0, keepdims=True)
    return r0, r1


def _dest_kernel(route_ref, start_ref, dest_ref, *, tm, group):
    start = start_ref[...]
    for t in range(group):
        route = route_ref[t * tm:(t + 1) * tm, :]
        lane = lax.broadcasted_iota(I32, route.shape, 1)
        lane_f = lane.astype(F32)
        d0 = route[:, 4:5] + jnp.sum(jnp.where(lane_f == route[:, 0:1], start, 0.0), axis=1, keepdims=True)
        d1 = route[:, 5:6] + jnp.sum(jnp.where(lane_f == route[:, 1:2], start, 0.0), axis=1, keepdims=True)
        dd = jnp.where(lane == 0, d0, jnp.where(lane == 1, d1, 0.0))
        dest_ref[t] = dd.T[0:8, :].astype(I32)


def _dest(route, start_row, *, tm):
    n = route.shape[0]
    tiles = n // tm
    group = 4 if tiles % 4 == 0 else 1
    return pl.pallas_call(
        functools.partial(_dest_kernel, tm=tm, group=group), grid=(tiles // group,),
        in_specs=[pl.BlockSpec((group * tm, LANES), lambda i: (i, 0)),
                  pl.BlockSpec((1, LANES), lambda i: (0, 0))],
        out_specs=pl.BlockSpec((group, 8, tm), lambda i: (i, 0, 0)),
        out_shape=jax.ShapeDtypeStruct((tiles, 8, tm), I32),
        compiler_params=_params(("parallel",)),
        name="dest",
    )(route, start_row)


def _moe_plan(route, cnt_f, *, n_exp, tm):
    n = route.shape[0]
    nt = (2 * n + n_exp * (tm - 1)) // tm
    cnt = cnt_f[0, :n_exp].astype(I32)
    ptiles = (cnt + tm - 1) // tm
    tile_end = jnp.cumsum(ptiles)
    tile_start = tile_end - ptiles
    start_row = jnp.pad((tile_start * tm).astype(F32), (0, LANES - n_exp)).reshape(1, LANES)
    dest_t = _dest(route, start_row, tm=tm)
    n_used = tile_end[-1]
    tiles = jnp.arange(nt, dtype=I32)
    te = jnp.minimum(jnp.sum((tile_end[None, :] <= tiles[:, None]).astype(I32), axis=1), n_exp - 1)
    te_last = jnp.take(te, jnp.maximum(n_used - 1, 0))
    tile_expert = jnp.where(tiles < n_used, te, te_last)
    pad_start = tile_start * tm + cnt
    pad_len = ptiles * tm - cnt
    return tile_expert, n_used.reshape(1).astype(I32), dest_t, pad_start, pad_len, nt


def kernel(x_prompt, x_sample, cache_k, cache_v, cache_lf, state_C, state_n, state_m, page_table,
           norm1_g, w_in, b_igate, b_fgate_mlstm, b_fgate_fox, mlstm_norm_g, q_norm_g, k_norm_g,
           w_branch_mlstm, w_branch_fox, w_out, norm2_g, w_router_group, b_router_group,
           w_router_expert, b_router_expert, w_gate, w_up, w_down):
    depth = w_in.shape[0]
    assert depth == 1, "single-layer step"
    bp, tp, d = x_prompt.shape
    bs, ts, _ = x_sample.shape
    mh = b_igate.shape[-1]
    fh = b_fgate_fox.shape[-1]
    fd = q_norm_g.shape[-1]
    md = d // mh
    n_exp = w_gate.shape[1]
    n_pages = page_table.shape[1]
    page = cache_k.shape[2]
    assert 2 * mh + fh <= LANES and n_exp + N_GROUPS <= LANES
    np_tok, ns_tok = bp * tp, bs * ts
    l = 0

    w = w_in[l]
    o = 0
    secs = {}
    for name, width in (("m", 4 * d), ("mi", mh), ("mf", mh), ("f", 3 * d), ("ff", fh), ("g", 2 * d)):
        secs[name] = w[:, o:o + width]
        o += width
    pad = jnp.zeros((d, LANES - 2 * mh - fh), F32)
    w_secs = (secs["m"].astype(BF16), secs["f"].astype(BF16), secs["g"].astype(BF16),
              jnp.concatenate([secs["mi"], secs["mf"], secs["ff"], pad], axis=1).astype(BF16))
    bias_s = jnp.concatenate([b_igate[l], b_fgate_mlstm[l], b_fgate_fox[l],
                              jnp.zeros((LANES - 2 * mh - fh,), F32)]).reshape(1, LANES)
    g1 = norm1_g[l].reshape(1, d)
    qg = q_norm_g[l].reshape(1, fd)
    kg = k_norm_g[l].reshape(1, fd)
    g_m = mlstm_norm_g[l].reshape(1, d)
    wa = w_branch_mlstm[l].astype(BF16)
    wb = w_branch_fox[l].astype(BF16)
    wo = w_out[l].astype(BF16)
    g2 = norm2_g[l].reshape(1, d)
    wr32 = jnp.concatenate([w_router_expert[l], w_router_group[l],
                            jnp.zeros((d, LANES - n_exp - N_GROUPS), F32)], axis=1)
    wr_hi = wr32.astype(BF16)
    wr = jnp.concatenate([wr_hi, (wr32 - wr_hi.astype(F32)).astype(BF16)], axis=1)
    br = jnp.concatenate([b_router_expert[l], b_router_group[l],
                          jnp.zeros((LANES - n_exp - N_GROUPS,), F32)]).reshape(1, LANES)

    inproj = functools.partial(_inproj, g1=g1, w_secs=w_secs, bias_s=bias_s, qg=qg, kg=kg, mh=mh, fh=fh)
    pm_p, fq_p, k3_p, v3_p, kb_p, vb_p, gate_p, small_p, kt_p = inproj(
        x_prompt.reshape(np_tok, d), act_dtype=BF16, q_unit=LOG2E, emit_kt=True)
    pm_s, fq_s, k3_s, v3_s, kb_s, vb_s, gate_s, small_s = inproj(x_sample.reshape(ns_tok, d), act_dtype=F32,
                                                                 tm=256)

    ha_p, c_p, n_p, m_p = _mlstm_prompt(pm_p, kt_p, small_p, g_m, batch=bp, seq=tp, mh=mh)
    m0 = jnp.pad(state_m[l], ((0, 0), (0, LANES - mh))).reshape(bs, 1, LANES)
    ha_s, c_s, n_s, m_s = _mlstm(pm_s, small_s, g_m, batch=bs, seq=ts, mh=mh, nb=8,
                                 init=(state_C[l], state_n[l], m0), out_dtype=F32)

    lf_p = small_p[:, 2 * mh:2 * mh + fh]
    lf_s = small_s[:, 2 * mh:2 * mh + fh]
    qx_p, kx_p = _fbias(small_p, batch=bp, seq=tp, fh=fh, lf0=2 * mh)
    ob_p = _fox_prompt(fq_p, qx_p, kb_p, kx_p, vb_p, batch=bp, seq=tp, fh=fh)

    n_phys = cache_k.shape[1]
    lft_c = cache_lf[l].transpose(0, 2, 1).reshape(n_phys * fh, page)
    floc = _cumsum_lanes(lft_c, carry=False, block_rows=n_phys * fh // 8).reshape(n_phys, fh, page)
    ck = cache_k[l].reshape(n_phys, page * fh, fd)
    cv = cache_v[l].reshape(n_phys, page * fh, fd)
    ob_s = _fox_sample(page_table, fq_s, kb_s, vb_s, small_s, floc, ck, cv, fh=fh, lf0=2 * mh)

    tm = MOE_TILE
    x2, xn_rows, route, cnt_f = _post((x_prompt.reshape(np_tok, d), ha_p, ob_p, gate_p),
                                      (x_sample.reshape(ns_tok, d), ha_s, ob_s, gate_s),
                                      wa, wb, wo, g2, wr, br, n_exp=n_exp, n_groups=N_GROUPS, tm=tm)

    tile_expert, n_used, dest_t, pad_start, pad_len, nt = _moe_plan(route, cnt_f, n_exp=n_exp, tm=tm)
    x_sorted = _dispatch(pad_start, pad_len, n_used, dest_t, xn_rows, nt=nt, tm=tm)
    y_sorted = _experts(tile_expert, n_used, x_sorted.reshape(nt * tm * (d // LANES), LANES),
                        w_gate[l], w_up[l], w_down[l], tm=tm)
    y_p = _combine(dest_t, x2, route, y_sorted, row0=0, rows=np_tok, tm=tm)
    y_s = _combine(dest_t, x2, route, y_sorted, row0=np_tok, rows=ns_tok, tm=tm)

    return (y_p.reshape(bp, tp, d), y_s.reshape(bs, ts, d),
            k3_p.reshape(1, bp, tp, fh, fd), v3_p.reshape(1, bp, tp, fh, fd), lf_p.reshape(1, bp, tp, fh),
            k3_s.reshape(1, bs, ts, fh, fd), v3_s.reshape(1, bs, ts, fh, fd), lf_s.reshape(1, bs, ts, fh),
            c_p[None], n_p[None], m_p[:, 0, :mh][None],
            c_s[None], n_s[None], m_s[:, 0, :mh][None])
```

```python
import functools

import jax
import jax.numpy as jnp
import numpy as np
from jax import lax
from jax.experimental import pallas as pl
from jax.experimental.pallas import tpu as pltpu

F32 = jnp.float32
BF16 = jnp.bfloat16
I32 = jnp.int32
RMS_EPS = 1e-6
LANES = 128
MLSTM_CHUNK = 128
MLSTM_PROMPT_CHUNK = 256
N_GROUPS = 4
VMEM_LIMIT = 56 * 1024 * 1024
NEG_INF = float("-inf")
LOG2E = 1.4426950408889634
DMA_UNROLL = 8
MOE_TILE = 512
PAD_CHUNKS = tuple(MOE_TILE >> i for i in range(1, MOE_TILE.bit_length()))
BIAS_LANES = 8


def _params(sem, vmem=VMEM_LIMIT):
    return pltpu.CompilerParams(dimension_semantics=sem, vmem_limit_bytes=vmem)


def _log_sigmoid(x):
    return -(jnp.maximum(-x, 0.0) + jnp.log1p(jnp.exp(-jnp.abs(x))))


def _sigmoid(x):
    return 1.0 / (1.0 + jnp.exp(-x))


def _nt_dot(a, b):
    return lax.dot_general(a, b, (((1,), (1,)), ((), ())), preferred_element_type=F32)


def _tn_dot(a, b):
    return lax.dot_general(a, b, (((0,), (0,)), ((), ())), preferred_element_type=F32)


def _col_to_row(col, eye):
    return jnp.sum(jnp.where(eye, col, 0.0), axis=0, keepdims=True)


def _inproj_kernel(x_ref, g1_ref, wm_ref, wf_ref, wg_ref, ws_ref, bias_ref, qg_ref, kg_ref,
                   pm_ref, fq_ref, k3_ref, v3_ref, kb_ref, vb_ref, gate_ref, small_ref, *maybe_kt_ref,
                   d, mh, fh, fd, k_scale, q_scale):
    x = x_ref[...]
    ms = jnp.mean(x * x, axis=-1, keepdims=True)
    h = (x * lax.rsqrt(ms + RMS_EPS) * g1_ref[...]).astype(BF16)
    sections = ((0, wm_ref), (4 * d, wf_ref), (7 * d, wg_ref), (9 * d, ws_ref))

    def proj(c0, width):
        base, w_ref = [(b, r) for b, r in sections if b <= c0][-1]
        return jnp.dot(h, w_ref[:, c0 - base:c0 - base + width], preferred_element_type=F32)

    pm_ref[:, 0:d] = proj(0, d).astype(pm_ref.dtype)
    mk = proj(d, d) * k_scale
    pm_ref[:, d:2 * d] = mk.astype(pm_ref.dtype)
    if maybe_kt_ref:
        maybe_kt_ref[0][...] = mk.T.astype(maybe_kt_ref[0].dtype)
    pm_ref[:, 2 * d:3 * d] = proj(2 * d, d).astype(pm_ref.dtype)
    pm_ref[:, 3 * d:4 * d] = proj(3 * d, d).astype(pm_ref.dtype)

    def head_norm(a, g):
        ms_h = jnp.mean(a * a, axis=-1, keepdims=True)
        return a * lax.rsqrt(ms_h + RMS_EPS) * g

    fq = proj(4 * d, d)
    fk = proj(5 * d, d)
    fv = proj(6 * d, d)
    for hh in range(fh):
        sl = slice(hh * fd, (hh + 1) * fd)
        fq_ref[:, sl] = (head_norm(fq[:, sl], qg_ref[...]) * q_scale).astype(fq_ref.dtype)
        kn = head_norm(fk[:, sl], kg_ref[...])
        rows_h = pl.ds(hh, fq.shape[0], stride=fh)
        k3_ref[rows_h, :] = kn
        kb_ref[:, sl] = kn.astype(kb_ref.dtype)
        v3_ref[rows_h, :] = fv[:, sl]
    vb_ref[...] = fv.astype(vb_ref.dtype)

    gate_ref[...] = _sigmoid(proj(7 * d, 2 * d)).astype(gate_ref.dtype)

    sm = proj(9 * d, LANES) + bias_ref[...]
    lane = lax.broadcasted_iota(I32, sm.shape, 1)
    sm = jnp.where(lane < mh, sm, jnp.where(lane < 2 * mh + fh, _log_sigmoid(sm), 0.0))
    small_ref[...] = sm


def _inproj(x2d, g1, w_secs, bias_s, qg, kg, *, mh, fh, act_dtype, q_unit=1.0, emit_kt=False, tm=512):
    n, d = x2d.shape
    fd = d // fh
    md = d // mh
    kern = functools.partial(_inproj_kernel, d=d, mh=mh, fh=fh, fd=fd,
                             k_scale=md ** -0.5, q_scale=fd ** -0.5 * q_unit)
    row = lambda i: (i, 0)
    const = lambda i: (0, 0)
    kt_specs = [pl.BlockSpec((d, tm), lambda i: (0, i))] if emit_kt else []
    kt_shapes = [jax.ShapeDtypeStruct((d, n), act_dtype)] if emit_kt else []
    return pl.pallas_call(
        kern, grid=(n // tm,),
        in_specs=[pl.BlockSpec((tm, d), row),
                  pl.BlockSpec((1, d), const),
                  *[pl.BlockSpec(w.shape, const, pipeline_mode=pl.Buffered(1)) for w in w_secs],
                  pl.BlockSpec((1, LANES), const),
                  pl.BlockSpec((1, fd), const),
                  pl.BlockSpec((1, fd), const)],
        out_specs=[pl.BlockSpec((tm, 4 * d), row),
                   pl.BlockSpec((tm, d), row),
                   pl.BlockSpec((tm * fh, fd), row),
                   pl.BlockSpec((tm * fh, fd), row),
                   pl.BlockSpec((tm, d), row),
                   pl.BlockSpec((tm, d), row),
                   pl.BlockSpec((tm, 2 * d), row),
                   pl.BlockSpec((tm, LANES), row)] + kt_specs,
        out_shape=[jax.ShapeDtypeStruct((n, 4 * d), act_dtype),
                   jax.ShapeDtypeStruct((n, d), act_dtype),
                   jax.ShapeDtypeStruct((n * fh, fd), F32),
                   jax.ShapeDtypeStruct((n * fh, fd), F32),
                   jax.ShapeDtypeStruct((n, d), act_dtype),
                   jax.ShapeDtypeStruct((n, d), act_dtype),
                   jax.ShapeDtypeStruct((n, 2 * d), BF16),
                   jax.ShapeDtypeStruct((n, LANES), F32)] + kt_shapes,
        compiler_params=_params(("parallel",)),
        name="inproj",
    )(x2d, g1, *w_secs, bias_s, qg, kg)


def _mlstm_kernel(*refs, L, mh, md, nb, has_init):
    if has_init:
        (q_ref, k_ref, v_ref, o_ref, s_ref, g_ref, c0_ref, n0_ref, m0_ref,
         h_ref, c_ref, n_ref, m_ref) = refs
    else:
        q_ref, k_ref, v_ref, o_ref, s_ref, g_ref, h_ref, c_ref, n_ref, m_ref = refs

    @pl.when(pl.program_id(1) == 0)
    def _():
        if has_init:
            c_ref[...] = c0_ref[...]
            n_ref[...] = n0_ref[...]
            m_ref[...] = m0_ref[...]
        else:
            c_ref[...] = jnp.zeros_like(c_ref)
            n_ref[...] = jnp.zeros_like(n_ref)
            m_ref[...] = jnp.zeros_like(m_ref)

    ri = lax.broadcasted_iota(I32, (L, L), 0)
    ci = lax.broadcasted_iota(I32, (L, L), 1)
    tri = ci <= ri
    eye = ci == ri
    for bb in range(nb):
        s = s_ref[bb]
        m_all = m_ref[bb]
        lane = lax.broadcasted_iota(I32, m_all.shape, 1)
        m_out = m_all
        for hh in range(mh):
            sl = slice(hh * md, (hh + 1) * md)
            ig_col = s[:, hh:hh + 1]
            lf_col = s[:, mh + hh:mh + hh + 1]
            lf_row = _col_to_row(lf_col, eye)
            ig_row = _col_to_row(ig_col, eye)
            b_col = jnp.sum(jnp.where(tri, lf_row, 0.0), axis=1, keepdims=True)
            b_row = _col_to_row(b_col, eye)
            m_prev = m_all[:, hh:hh + 1]
            log_w = jnp.where(tri, b_col - b_row + ig_row, NEG_INF)
            log_inter = b_col + m_prev
            m_t = jnp.maximum(log_inter, jnp.max(log_w, axis=1, keepdims=True))
            w_intra = jnp.exp(log_w - m_t)
            w_inter = jnp.exp(log_inter - m_t)
            q = q_ref[bb, :, sl].astype(BF16)
            k = k_ref[bb, :, sl].astype(BF16)
            v = v_ref[bb, :, sl].astype(BF16)
            sm = _nt_dot(q, k) * w_intra
            c_prev = c_ref[bb, hh]
            n_prev = n_ref[bb, hh:hh + 1, :]
            num = (w_inter * jnp.dot(q, c_prev.astype(BF16), preferred_element_type=F32)
                   + jnp.dot(sm.astype(BF16), v, preferred_element_type=F32))
            den = (w_inter * jnp.sum(q.astype(F32) * n_prev, axis=1, keepdims=True)
                   + jnp.sum(sm, axis=1, keepdims=True))
            hv = num / jnp.maximum(jnp.abs(den), jnp.exp(-m_t))
            m_new = m_t[L - 1:L, :]
            b_last = b_col[L - 1:L, :]
            decay = jnp.exp(b_last + m_prev - m_new)
            w_rows = jnp.exp(b_last - b_col + ig_col - m_new)
            kw = k.astype(F32) * w_rows
            c_ref[bb, hh] = decay * c_prev + _tn_dot(kw.astype(BF16), v)
            n_ref[bb, hh:hh + 1, :] = decay * n_prev + jnp.sum(kw, axis=0, keepdims=True)
            m_out = jnp.where(lane == hh, m_new, m_out)
            ms = jnp.mean(hv * hv, axis=-1, keepdims=True)
            hn = hv * lax.rsqrt(ms + RMS_EPS) * g_ref[:, sl]
            h_ref[bb, :, sl] = (_sigmoid(o_ref[bb, :, sl].astype(F32)) * hn).astype(h_ref.dtype)
        m_ref[bb] = m_out


def _mlstm(pm, small, g_m, *, batch, seq, mh, nb=1, init=None, out_dtype=BF16):
    d = pm.shape[1] // 4
    md = d // mh
    L = MLSTM_CHUNK if seq % MLSTM_CHUNK == 0 else seq
    nc = seq // L
    ng = batch // nb
    pm4 = pm.reshape(ng, nb, seq, 4 * d)
    small4 = small.reshape(ng, nb, seq, LANES)
    sec = lambda j: (lambda b, c: (b, 0, c, j))
    in_specs = [pl.BlockSpec((None, nb, L, d), sec(0)), pl.BlockSpec((None, nb, L, d), sec(1)),
                pl.BlockSpec((None, nb, L, d), sec(2)), pl.BlockSpec((None, nb, L, d), sec(3)),
                pl.BlockSpec((None, nb, L, LANES), sec(0)),
                pl.BlockSpec((1, d), lambda b, c: (0, 0))]
    args = [pm4, pm4, pm4, pm4, small4, g_m]
    state_specs = [pl.BlockSpec((nb, mh, md, md), lambda b, c: (b, 0, 0, 0)),
                   pl.BlockSpec((nb, mh, md), lambda b, c: (b, 0, 0)),
                   pl.BlockSpec((nb, 1, LANES), lambda b, c: (b, 0, 0))]
    if init is not None:
        in_specs += state_specs
        args += list(init)
    kern = functools.partial(_mlstm_kernel, L=L, mh=mh, md=md, nb=nb, has_init=init is not None)
    h4, c_out, n_out, m_out = pl.pallas_call(
        kern, grid=(ng, nc),
        in_specs=in_specs,
        out_specs=[pl.BlockSpec((None, nb, L, d), lambda b, c: (b, 0, c, 0))] + state_specs,
        out_shape=[jax.ShapeDtypeStruct((ng, nb, seq, d), out_dtype),
                   jax.ShapeDtypeStruct((batch, mh, md, md), F32),
                   jax.ShapeDtypeStruct((batch, mh, md), F32),
                   jax.ShapeDtypeStruct((batch, 1, LANES), F32)],
        compiler_params=_params(("parallel", "arbitrary")),
        name="mlstm",
    )(*args)
    return h4.reshape(batch * seq, d), c_out, n_out, m_out


def _mlstm_prompt_kernel(q_ref, kt_ref, v_ref, o_ref, s_ref, g_ref, h_ref, c_ref, n_ref, m_ref, naug,
                         *, L, mh, md, nc):
    c = pl.program_id(1)

    @pl.when(c == 0)
    def _():
        c_ref[...] = jnp.zeros_like(c_ref)
        m_ref[...] = jnp.zeros_like(m_ref)
        naug[...] = jnp.zeros_like(naug)

    s = s_ref[...]
    ri = lax.broadcasted_iota(I32, (L, L), 0)
    ci = lax.broadcasted_iota(I32, (L, L), 1)
    tri = ci <= ri
    hp = lax.Precision.HIGHEST
    r = s.T[0:8, :]
    b_cols = jnp.dot(jnp.where(tri, 1.0, 0.0), s, precision=hp, preferred_element_type=F32)
    b_rows = pltpu.roll(jnp.dot(r, jnp.where(ri <= ci, 1.0, 0.0), precision=hp, preferred_element_type=F32),
                        8 - mh, axis=0)
    g = r - b_rows
    m_all = m_ref[0]
    lane = lax.broadcasted_iota(I32, m_all.shape, 1)
    m_out = m_all
    ones_v = jnp.ones((L, LANES), BF16)
    ones_r = jnp.ones((md, LANES), BF16)
    for hh in range(mh):
        sl = slice(hh * md, (hh + 1) * md)
        m_prev = m_all[:, hh:hh + 1]
        g_row = g[hh:hh + 1, :]
        b_col = b_cols[:, mh + hh:mh + hh + 1]
        a_col = jnp.maximum(m_prev, jnp.max(jnp.where(tri, g_row, NEG_INF), axis=1, keepdims=True))
        m_t = b_col + a_col
        m_new = m_t[L - 1:L, :]
        b_last = b_col[L - 1:L, :]
        w_rows = jnp.exp(b_last - b_rows[hh:hh + 1, :] + r[hh:hh + 1, :] - m_new)
        dc = jnp.exp(b_last + m_prev - m_new)
        m_out = jnp.where(lane == hh, m_new, m_out)
        q = q_ref[:, sl]
        kt = kt_ref[sl, :]
        v_aug = jnp.concatenate([v_ref[:, sl], ones_v], axis=1)
        w_intra = jnp.exp(jnp.where(tri, g_row - a_col, NEG_INF))
        sm = jnp.dot(q, kt, preferred_element_type=F32) * w_intra
        c_prev = c_ref[0, hh]
        n_prev = naug[hh]
        c_aug = jnp.concatenate([c_prev, n_prev], axis=1).astype(BF16)
        comb = (jnp.exp(m_prev - a_col) * jnp.dot(q, c_aug, preferred_element_type=F32)
                + jnp.dot(sm.astype(BF16), v_aug, preferred_element_type=F32))
        den = jnp.maximum(jnp.abs(comb[:, md:]), jnp.exp(-m_t))
        inv = 1.0 / den
        hv = comb[:, 0:md] * jnp.concatenate([inv] * (md // LANES), axis=1)
        ms = jnp.dot((hv * hv).astype(BF16), ones_r, preferred_element_type=F32) * (1.0 / md)
        rs = lax.rsqrt(ms + RMS_EPS)
        hn = hv * jnp.concatenate([rs] * (md // LANES), axis=1) * g_ref[:, sl]
        h_ref[:, sl] = (_sigmoid(o_ref[:, sl].astype(F32)) * hn).astype(h_ref.dtype)
        kw = (kt.astype(F32) * w_rows).astype(BF16)
        upd = jnp.dot(kw, v_aug, preferred_element_type=F32)
        c_ref[0, hh] = dc * c_prev + upd[:, 0:md]
        naug[hh] = dc * n_prev + upd[:, md:]
    m_ref[0] = m_out

    @pl.when(c == nc - 1)
    def _():
        ei = lax.broadcasted_iota(I32, (md, md), 0) == lax.broadcasted_iota(I32, (md, md), 1)
        for hh in range(mh):
            cols = jnp.concatenate([naug[hh]] * (md // LANES), axis=1)
            n_ref[0, hh:hh + 1, :] = jnp.sum(jnp.where(ei, cols, 0.0), axis=0, keepdims=True)


def _mlstm_prompt(pm, kt, small, g_m, *, batch, seq, mh):
    d = pm.shape[1] // 4
    md = d // mh
    L = MLSTM_PROMPT_CHUNK
    assert seq % L == 0
    nc = seq // L
    tok = lambda j: (lambda b, c: (b * nc + c, j))
    kern = functools.partial(_mlstm_prompt_kernel, L=L, mh=mh, md=md, nc=nc)
    return pl.pallas_call(
        kern, grid=(batch, nc),
        in_specs=[pl.BlockSpec((L, d), tok(0)),
                  pl.BlockSpec((d, L), lambda b, c: (0, b * nc + c)),
                  pl.BlockSpec((L, d), tok(2)), pl.BlockSpec((L, d), tok(3)),
                  pl.BlockSpec((L, LANES), tok(0)),
                  pl.BlockSpec((1, d), lambda b, c: (0, 0))],
        out_specs=[pl.BlockSpec((L, d), tok(0)),
                   pl.BlockSpec((1, mh, md, md), lambda b, c: (b, 0, 0, 0)),
                   pl.BlockSpec((1, mh, md), lambda b, c: (b, 0, 0)),
                   pl.BlockSpec((1, 1, LANES), lambda b, c: (b, 0, 0))],
        out_shape=[jax.ShapeDtypeStruct((batch * seq, d), BF16),
                   jax.ShapeDtypeStruct((batch, mh, md, md), F32),
                   jax.ShapeDtypeStruct((batch, mh, md), F32),
                   jax.ShapeDtypeStruct((batch, 1, LANES), F32)],
        scratch_shapes=[pltpu.VMEM((mh, md, LANES), F32)],
        compiler_params=_params(("parallel", "arbitrary")),
        name="mlstm_prompt",
    )(pm, kt, pm, pm, small, g_m)


def _cumsum_kernel(x_ref, o_ref):
    t = x_ref.shape[1]
    si = lax.broadcasted_iota(I32, (t, t), 0)
    ti = lax.broadcasted_iota(I32, (t, t), 1)
    tri = (si <= ti).astype(F32)
    o_ref[...] = jnp.dot(x_ref[...], tri, precision=lax.Precision.HIGHEST, preferred_element_type=F32)


def _cumsum_lanes(x, *, block_rows):
    rows, t = x.shape
    return pl.pallas_call(
        _cumsum_kernel, grid=(rows // block_rows,),
        in_specs=[pl.BlockSpec((block_rows, t), lambda i: (i, 0))],
        out_specs=pl.BlockSpec((block_rows, t), lambda i: (i, 0)),
        out_shape=jax.ShapeDtypeStruct((rows, t), F32),
        compiler_params=_params(("parallel",)),
        name="cumsum",
    )(x)


def _fbias_kernel(s_ref, pq_ref, pk_ref, cq_ref, ck_ref, qx_ref, kx_ref, run_ref):
    @pl.when(pl.program_id(1) == 0)
    def _():
        run_ref[...] = jnp.zeros_like(run_ref)

    s = s_ref[...]
    L = s.shape[0]
    ti = lax.broadcasted_iota(I32, (L, L), 0)
    si = lax.broadcasted_iota(I32, (L, L), 1)
    tri = jnp.where(si <= ti, 1.0, 0.0).astype(BF16)

    def split3(x):
        hi = x.astype(BF16)
        r1 = x - hi.astype(F32)
        mid = r1.astype(BF16)
        return jnp.concatenate([hi, mid, (r1 - mid.astype(F32)).astype(BF16)], axis=1)

    cs = jnp.dot(tri, split3(s), preferred_element_type=F32)
    f_all = (cs[:, 0:LANES] + cs[:, LANES:2 * LANES] + cs[:, 2 * LANES:3 * LANES]) + run_ref[...]
    run_ref[...] = f_all[L - 1:L, :]
    terms = split3(f_all * LOG2E)
    qx_ref[...] = jnp.dot(terms, pq_ref[...], preferred_element_type=F32) + cq_ref[...]
    kx_ref[...] = (jnp.dot(terms, pk_ref[...], preferred_element_type=F32) + ck_ref[...]).astype(BF16)


def _fbias(small, *, batch, seq, fh, lf0):
    assert BIAS_LANES * fh <= LANES
    n = small.shape[0]
    L = 4 * LANES if seq % (4 * LANES) == 0 else LANES
    nc = seq // L
    pq = np.zeros((3 * LANES, LANES), np.float32)
    pk = np.zeros((3 * LANES, LANES), np.float32)
    cq = np.zeros((1, LANES), np.float32)
    ck = np.zeros((1, LANES), np.float32)
    for h in range(fh):
        for j in range(3):
            pq[j * LANES + lf0 + h, BIAS_LANES * h + j] = 1.0
            pk[j * LANES + lf0 + h, BIAS_LANES * h + 3 + j] = -1.0
            cq[0, BIAS_LANES * h + 3 + j] = 1.0
            ck[0, BIAS_LANES * h + j] = 1.0
    const = lambda b, c: (0, 0)
    tok = lambda b, c: (b * nc + c, 0)
    return pl.pallas_call(
        _fbias_kernel, grid=(batch, nc),
        in_specs=[pl.BlockSpec((L, LANES), tok),
                  pl.BlockSpec((3 * LANES, LANES), const), pl.BlockSpec((3 * LANES, LANES), const),
                  pl.BlockSpec((1, LANES), const), pl.BlockSpec((1, LANES), const)],
        out_specs=[pl.BlockSpec((L, LANES), tok), pl.BlockSpec((L, LANES), tok)],
        out_shape=[jax.ShapeDtypeStruct((n, LANES), F32), jax.ShapeDtypeStruct((n, LANES), BF16)],
        scratch_shapes=[pltpu.VMEM((1, LANES), F32)],
        compiler_params=_params(("parallel", "arbitrary")),
        name="fbias",
    )(small, jnp.asarray(pq, BF16), jnp.asarray(pk, BF16), jnp.asarray(cq), jnp.asarray(ck))


def _fox_prompt_kernel(q_ref, qx_ref, k_ref, kx_ref, v_ref, o_ref, *, seq, tq, gh, fd):
    nq = seq // tq
    half = tq // 2
    causal_top = lax.broadcasted_iota(I32, (half, half), 1) <= lax.broadcasted_iota(I32, (half, half), 0)
    causal_bot = lax.broadcasted_iota(I32, (half, tq), 1) <= lax.broadcasted_iota(I32, (half, tq), 0) + half

    def softmax_step(carry, s, v):
        m, l, acc = carry
        m_new = jnp.maximum(m, jnp.max(s, axis=1, keepdims=True))
        a = jnp.exp2(m - m_new)
        p = jnp.exp2(s - m_new)
        l = a * l + jnp.sum(p, axis=1, keepdims=True)
        acc = a * acc + jnp.dot(p.astype(BF16), v, preferred_element_type=F32)
        return m_new, l, acc

    head0 = pl.program_id(1) * gh
    bias_head = lax.broadcasted_iota(I32, (tq, LANES), 1) // BIAS_LANES

    def q_body(qi, _):
        q0 = pl.multiple_of(qi * tq, tq)
        qx = qx_ref[pl.ds(q0, tq), :]
        qas = [jnp.concatenate([q_ref[pl.ds(q0, tq), g * fd:(g + 1) * fd],
                                jnp.where(bias_head == head0 + g, qx, 0.0).astype(BF16)], axis=1)
               for g in range(gh)]

        def block(g, carry_g, k0):
            gs = slice(g * fd, (g + 1) * fd)
            ka = jnp.concatenate([k_ref[pl.ds(k0, tq), gs], kx_ref[pl.ds(k0, tq), :]], axis=1)
            s = _nt_dot(qas[g], ka)
            return softmax_step(carry_g, s, v_ref[pl.ds(k0, tq), gs])

        def kv_body(kj, carry):
            k0 = pl.multiple_of(kj * tq, tq)
            return tuple(block(g, carry[g], k0) for g in range(gh))

        init = tuple((jnp.full((tq, 1), NEG_INF, F32), jnp.zeros((tq, 1), F32), jnp.zeros((tq, fd), F32))
                     for _ in range(gh))
        carry = lax.fori_loop(0, qi, kv_body, init)
        for g in range(gh):
            gs = slice(g * fd, (g + 1) * fd)
            m, l, acc = carry[g]
            for r0, nk, mask in ((0, half, causal_top), (half, tq, causal_bot)):
                ka = jnp.concatenate([k_ref[pl.ds(q0, nk), gs], kx_ref[pl.ds(q0, nk), :]], axis=1)
                s = jnp.where(mask, _nt_dot(qas[g][r0:r0 + half], ka), NEG_INF)
                part = (m[r0:r0 + half], l[r0:r0 + half], acc[r0:r0 + half])
                _, l_out, acc_out = softmax_step(part, s, v_ref[pl.ds(q0, nk), gs])
                o_ref[pl.ds(q0 + r0, half), gs] = (acc_out / l_out).astype(o_ref.dtype)
        return 0

    lax.fori_loop(0, nq, q_body, 0)


def _fox_prompt(fq, qx, kb, kx, vb, *, batch, seq, fh, tq=1024, gh=4):
    n, d = fq.shape
    fd = d // fh
    blk = lambda b, h: (b, h)
    bias = pl.BlockSpec((seq, LANES), lambda b, h: (b, 0))
    kern = functools.partial(_fox_prompt_kernel, seq=seq, tq=tq, gh=gh, fd=fd)
    return pl.pallas_call(
        kern, grid=(batch, fh // gh),
        in_specs=[pl.BlockSpec((seq, gh * fd), blk), bias, pl.BlockSpec((seq, gh * fd), blk), bias,
                  pl.BlockSpec((seq, gh * fd), blk)],
        out_specs=pl.BlockSpec((seq, gh * fd), blk),
        out_shape=jax.ShapeDtypeStruct((n, d), BF16),
        compiler_params=_params(("parallel", "parallel")),
        name="fox_prompt",
    )(fq, qx, kb, kx, vb)


def _fox_sample_kernel(pt_ref, q_ref, kn_ref, vn_ref, s_ref, *rest, n_pages, page, fh, fd, ts, lf0):
    floc = rest[0:n_pages]
    kpages = rest[n_pages:2 * n_pages]
    vpages = rest[2 * n_pages:3 * n_pages]
    o_ref = rest[3 * n_pages]
    p_scr, pn_scr, l_scr = rest[3 * n_pages + 1:]
    d = fh * fd
    rows = fh * ts

    def scores():
        q = q_ref[...].astype(F32)
        qt = jnp.concatenate([q] * fh, axis=0)
        r_head = lax.broadcasted_iota(I32, (rows, d), 0) // ts
        c_head = lax.broadcasted_iota(I32, (rows, d), 1) // fd
        qbd = jnp.where(r_head == c_head, qt, 0.0).astype(BF16)

        off = jnp.zeros((fh, 1), F32)
        fk_pages = []
        for i in range(n_pages):
            fp = floc[i][...] + off
            fk_pages.append(fp)
            off = fp[:, page - 1:page]
        fk = jnp.concatenate(fk_pages, axis=1)
        fk_rows = jnp.concatenate(
            [jnp.broadcast_to(fk[h:h + 1, :], (ts, fk.shape[1])) for h in range(fh)], axis=0)

        sm = s_ref[...]
        ri = lax.broadcasted_iota(I32, (ts, ts), 0)
        ci = lax.broadcasted_iota(I32, (ts, ts), 1)
        eye = ci == ri
        tri = ci <= ri
        fq_cols, bias_new = [], []
        for h in range(fh):
            lf_col = sm[:, lf0 + h:lf0 + h + 1]
            lf_row = _col_to_row(lf_col, eye)
            cum_col = jnp.sum(jnp.where(tri, lf_row, 0.0), axis=1, keepdims=True)
            fq_h = off[h:h + 1, :] + cum_col
            fq_cols.append(fq_h)
            bias_new.append(fq_h - _col_to_row(fq_h, eye))
        fq_col = jnp.concatenate(fq_cols, axis=0)
        bias_n = jnp.concatenate(bias_new, axis=0)
        causal_n = jnp.concatenate([tri] * fh, axis=0)

        s_parts = []
        for i in range(0, n_pages, 2):
            kp = jnp.concatenate(
                [jnp.concatenate([kpages[i + j][pl.ds(h, page, stride=fh), :] for h in range(fh)], axis=1)
                 for j in range(2)],
                axis=0).astype(BF16)
            s_parts.append(_nt_dot(qbd, kp))
        s_past = jnp.concatenate(s_parts, axis=1) + (fq_col - fk_rows)
        s_new = _nt_dot(qbd, kn_ref[...].astype(BF16)) + bias_n
        s_new = jnp.where(causal_n, s_new, NEG_INF)
        m = jnp.maximum(jnp.max(s_past, axis=1, keepdims=True), jnp.max(s_new, axis=1, keepdims=True))
        p_past = jnp.exp(s_past - m)
        p_new = jnp.exp(s_new - m)
        l_scr[...] = jnp.sum(p_past, axis=1, keepdims=True) + jnp.sum(p_new, axis=1, keepdims=True)
        p_scr[...] = p_past.astype(BF16)
        pn_scr[...] = p_new

    def values():
        acc = jnp.dot(pn_scr[...].astype(BF16), vn_ref[...].astype(BF16), preferred_element_type=F32)
        for i in range(0, n_pages, 2):
            vp = jnp.concatenate(
                [jnp.concatenate([vpages[i + j][pl.ds(h, page, stride=fh), :] for h in range(fh)], axis=1)
                 for j in range(2)],
                axis=0).astype(BF16)
            acc = acc + jnp.dot(p_scr[:, i * page:(i + 2) * page], vp, preferred_element_type=F32)
        acc = acc / l_scr[...]
        o_ref[...] = jnp.concatenate(
            [acc[h * ts:(h + 1) * ts, h * fd:(h + 1) * fd] for h in range(fh)], axis=1)

    scores()
    values()


def _fox_sample(page_table, fq, kb, vb, small, floc, cache_k, cache_v, *, fh, lf0):
    bs, n_pages = page_table.shape
    n, d = fq.shape
    ts = n // bs
    fd = d // fh
    page = cache_k.shape[1] // fh
    rows = fh * ts
    tok = lambda b, pt: (b, 0)

    def pmap(i):
        return lambda b, pt: (pt[b, i], 0, 0)

    in_specs = ([pl.BlockSpec((ts, d), tok), pl.BlockSpec((ts, d), tok), pl.BlockSpec((ts, d), tok),
                 pl.BlockSpec((ts, LANES), tok)]
                + [pl.BlockSpec((None, fh, page), pmap(i)) for i in range(n_pages)]
                + [pl.BlockSpec((None, page * fh, fd), pmap(i)) for i in range(n_pages)]
                + [pl.BlockSpec((None, page * fh, fd), pmap(i)) for i in range(n_pages)])
    kern = functools.partial(_fox_sample_kernel, n_pages=n_pages, page=page, fh=fh, fd=fd, ts=ts, lf0=lf0)
    return pl.pallas_call(
        kern,
        grid_spec=pltpu.PrefetchScalarGridSpec(
            num_scalar_prefetch=1, grid=(bs,), in_specs=in_specs,
            out_specs=pl.BlockSpec((ts, d), tok),
            scratch_shapes=[pltpu.VMEM((rows, n_pages * page), BF16),
                            pltpu.VMEM((rows, ts), F32),
                            pltpu.VMEM((rows, 1), F32)]),
        out_shape=jax.ShapeDtypeStruct((n, d), F32),
        compiler_params=_params(("arbitrary",)),
        name="fox_sample",
    )(page_table, fq, kb, vb, small, *([floc] * n_pages), *([cache_k] * n_pages), *([cache_v] * n_pages))


def _post_kernel(xp_ref, hap_ref, obp_ref, gp_ref, xs_ref, has_ref, obs_ref, gs_ref, *rest, tiles_p, **kw):
    cnt_ref = rest[-1]

    @pl.when(pl.program_id(0) == 0)
    def _():
        cnt_ref[...] = jnp.zeros_like(cnt_ref)

    @pl.when(pl.program_id(0) < tiles_p)
    def _():
        _post_body(xp_ref, hap_ref, obp_ref, gp_ref, *rest, **kw)

    @pl.when(pl.program_id(0) >= tiles_p)
    def _():
        _post_body(xs_ref, has_ref, obs_ref, gs_ref, *rest, **kw)


def _post_body(x_ref, ha_ref, ob_ref, gate_ref, wa_ref, wb_ref, wo_ref, g2_ref, wr_ref, br_ref,
               x2_ref, xn_ref, route_ref, cnt_ref, *, d, n_exp, n_groups):
    ba = jnp.dot(ha_ref[...].astype(BF16), wa_ref[...], preferred_element_type=F32)
    bb = jnp.dot(ob_ref[...].astype(BF16), wb_ref[...], preferred_element_type=F32)
    gates = gate_ref[...].astype(F32)
    merged = gates[:, 0:d] * ba + gates[:, d:2 * d] * bb
    x2 = x_ref[...] + jnp.dot(merged.astype(BF16), wo_ref[...], preferred_element_type=F32)
    x2_ref[...] = x2
    ms = jnp.mean(x2 * x2, axis=-1, keepdims=True)
    xn = x2 * lax.rsqrt(ms + RMS_EPS) * g2_ref[...]
    tm = xn.shape[0]
    for j in range(d // LANES):
        xn_ref[pl.ds(j, tm, stride=d // LANES), :] = xn[:, j * LANES:(j + 1) * LANES]

    xh = xn.astype(BF16)
    xl = (xn - xh.astype(F32)).astype(BF16)
    hh = jnp.dot(xh, wr_ref[...], preferred_element_type=F32)
    logits = (hh[:, 0:LANES] + hh[:, LANES:2 * LANES]
              + jnp.dot(xl, wr_ref[:, 0:LANES], preferred_element_type=F32)
              + br_ref[...])
    lane = lax.broadcasted_iota(I32, logits.shape, 1)
    lane_f = lane.astype(F32)
    big = float(LANES)
    epg = n_exp // n_groups
    in_groups = (lane >= n_exp) & (lane < n_exp + n_groups)
    gl = jnp.where(in_groups, logits, NEG_INF)
    gmax = jnp.max(gl, axis=1, keepdims=True)
    gidx = jnp.min(jnp.where(gl == gmax, lane_f, big), axis=1, keepdims=True) - float(n_exp)
    g_p = 1.0 / jnp.sum(jnp.exp(gl - gmax), axis=1, keepdims=True)
    in_group = (lane < n_exp) & ((lane // epg).astype(F32) == gidx)
    el = jnp.where(in_group, logits, NEG_INF)
    m1 = jnp.max(el, axis=1, keepdims=True)
    i1 = jnp.min(jnp.where(el == m1, lane_f, big), axis=1, keepdims=True)
    el2 = jnp.where(lane_f == i1, NEG_INF, el)
    m2 = jnp.max(el2, axis=1, keepdims=True)
    i2 = jnp.min(jnp.where(el2 == m2, lane_f, big), axis=1, keepdims=True)
    esum = jnp.sum(jnp.exp(el - m1), axis=1, keepdims=True)
    p1 = 1.0 / esum
    p2 = jnp.exp(m2 - m1) / esum
    psum = p1 + p2
    w1 = p1 / psum * g_p
    w2 = p2 / psum * g_p
    r0, r1 = _rank_block(i1, i2, lane_f, cnt_ref)
    route_ref[...] = jnp.where(lane == 0, i1, jnp.where(lane == 1, i2, jnp.where(
        lane == 2, w1, jnp.where(lane == 3, w2, jnp.where(lane == 4, r0, jnp.where(lane == 5, r1, 0.0))))))


def _post(acts_p, acts_s, wa, wb, wo, g2, wr, br, *, n_exp, n_groups, tm=256):
    n_p, d = acts_p[0].shape
    n_s = acts_s[0].shape[0]
    tiles_p, tiles_s = n_p // tm, n_s // tm
    n = n_p + n_s
    rpt = d // LANES
    row_p = lambda i: (jnp.minimum(i, tiles_p - 1), 0)
    row_s = lambda i: (jnp.maximum(i - tiles_p, 0), 0)
    row = lambda i: (i, 0)
    const = lambda i: (0, 0)
    kern = functools.partial(_post_kernel, tiles_p=tiles_p, d=d, n_exp=n_exp, n_groups=n_groups)
    wspec = lambda: pl.BlockSpec((d, d), const, pipeline_mode=pl.Buffered(1))
    act_specs = lambda r: [pl.BlockSpec((tm, d), r), pl.BlockSpec((tm, d), r), pl.BlockSpec((tm, d), r),
                           pl.BlockSpec((tm, 2 * d), r)]
    return pl.pallas_call(
        kern, grid=(tiles_p + tiles_s,),
        in_specs=act_specs(row_p) + act_specs(row_s) + [
            wspec(), wspec(), wspec(),
            pl.BlockSpec((1, d), const), pl.BlockSpec((d, 2 * LANES), const), pl.BlockSpec((1, LANES), const)],
        out_specs=[pl.BlockSpec((tm, d), row), pl.BlockSpec((tm * rpt, LANES), row),
                   pl.BlockSpec((tm, LANES), row), pl.BlockSpec((1, LANES), const)],
        out_shape=[jax.ShapeDtypeStruct((n, d), F32), jax.ShapeDtypeStruct((n * rpt, LANES), F32),
                   jax.ShapeDtypeStruct((n, LANES), F32), jax.ShapeDtypeStruct((1, LANES), F32)],
        compiler_params=_params(("arbitrary",)),
        name="post",
    )(*acts_p, *acts_s, wa, wb, wo, g2, wr, br)


def _dispatch_kernel(ps_ref, pl_ref, nu_ref, dest_ref, xn_ref, xs_hbm, stage, zbuf, sem_rows, sem_pad,
                     *, tm, n_tiles, n_exp, nt):
    i = pl.program_id(0)
    slot = i % 2

    def pad_dmas(act):
        for e in range(n_exp):
            pos = ps_ref[e]
            left = pl_ref[e]
            for c in PAD_CHUNKS:
                @pl.when((left & c) != 0)
                def _(pos=pos, c=c):
                    act(pltpu.make_async_copy(zbuf.at[pl.ds(0, c)], xs_hbm.at[pl.ds(pos, c)], sem_pad))
                pos = pos + (left & c)

        def unused_tile(t, carry):
            for part in range(tm // PAD_CHUNKS[0]):
                act(pltpu.make_async_copy(
                    zbuf, xs_hbm.at[pl.ds(t * tm + part * PAD_CHUNKS[0], PAD_CHUNKS[0])], sem_pad))
            return carry
        lax.fori_loop(nu_ref[0], nt, unused_tile, 0)

    @pl.when(i == 0)
    def _():
        zbuf[...] = jnp.zeros_like(zbuf)
        pad_dmas(lambda cp: cp.start())

    def wait_tile(s):
        for _ in range(2):
            pltpu.make_async_copy(stage.at[s], xs_hbm.at[pl.ds(0, tm)], sem_rows.at[s]).wait()

    @pl.when(i >= 2)
    def _():
        wait_tile(slot)

    stage[slot] = xn_ref[...]

    def body(j, c):
        for u in range(DMA_UNROLL // 2):
            r = j * (DMA_UNROLL // 2) + u
            for kk in range(2):
                pltpu.make_async_copy(stage.at[slot, r], xs_hbm.at[dest_ref[0, kk, r]],
                                      sem_rows.at[slot]).start(priority=kk)
        return c
    lax.fori_loop(0, tm // (DMA_UNROLL // 2), body, 0)

    @pl.when(i == n_tiles - 1)
    def _():
        if n_tiles > 1:
            wait_tile(1 - slot)
        wait_tile(slot)
        pad_dmas(lambda cp: cp.wait())


def _dispatch(pad_start, pad_len, n_used, dest_t, xn_rows, *, nt, tm):
    n_tiles = dest_t.shape[0]
    n_exp = pad_start.shape[0]
    rpt = xn_rows.shape[0] // (n_tiles * tm)
    xn3 = xn_rows.reshape(n_tiles * tm, rpt, LANES)
    kern = functools.partial(_dispatch_kernel, tm=tm, n_tiles=n_tiles, n_exp=n_exp, nt=nt)
    return pl.pallas_call(
        kern,
        grid_spec=pltpu.PrefetchScalarGridSpec(
            num_scalar_prefetch=3, grid=(n_tiles,),
            in_specs=[pl.BlockSpec((1, 8, tm), lambda i, ps, pln, nu: (i, 0, 0), memory_space=pltpu.SMEM),
                      pl.BlockSpec((tm, rpt, LANES), lambda i, ps, pln, nu: (i, 0, 0))],
            out_specs=pl.BlockSpec(memory_space=pl.ANY),
            scratch_shapes=[pltpu.VMEM((2, tm, rpt, LANES), F32),
                            pltpu.VMEM((PAD_CHUNKS[0], rpt, LANES), F32),
                            pltpu.SemaphoreType.DMA((2,)), pltpu.SemaphoreType.DMA(())]),
        out_shape=jax.ShapeDtypeStruct((nt * tm, rpt, LANES), F32),
        compiler_params=_params(("arbitrary",)),
        name="dispatch",
    )(pad_start, pad_len, n_used, dest_t, xn3)


def _experts_kernel(te_ref, nu_ref, x_ref, wg_ref, wu_ref, wd_ref, y_ref, wg_b, wu_b, wd_b, *, tm, rpt):
    t = pl.program_id(0)

    @pl.when((t == 0) | (te_ref[t] != te_ref[jnp.maximum(t - 1, 0)]))
    def _():
        wg_b[...] = wg_ref[0].astype(BF16)
        wu_b[...] = wu_ref[0].astype(BF16)
        wd_b[...] = wd_ref[0].astype(BF16)

    @pl.when(t < nu_ref[0])
    def _():
        x = jnp.concatenate([x_ref[pl.ds(j, tm, stride=rpt), :] for j in range(rpt)], axis=1).astype(BF16)
        g = jnp.dot(x, wg_b[...], preferred_element_type=F32)
        u = jnp.dot(x, wu_b[...], preferred_element_type=F32)
        hg = (g * _sigmoid(g)) * u
        y = jnp.dot(hg.astype(BF16), wd_b[...], preferred_element_type=F32)
        for j in range(rpt):
            y_ref[pl.ds(j, tm, stride=rpt), :] = y[:, j * LANES:(j + 1) * LANES]

    @pl.when(t >= nu_ref[0])
    def _():
        y_ref[...] = jnp.zeros_like(y_ref)


def _experts(tile_expert, n_used, x_sorted, w_gate, w_up, w_down, *, tm):
    nt = tile_expert.shape[0]
    n_exp, d, de = w_gate.shape
    rpt = d // LANES
    emap = lambda t, te, nu: (te[t], 0, 0)
    kern = functools.partial(_experts_kernel, tm=tm, rpt=rpt)
    return pl.pallas_call(
        kern,
        grid_spec=pltpu.PrefetchScalarGridSpec(
            num_scalar_prefetch=2, grid=(nt,),
            in_specs=[pl.BlockSpec((tm * rpt, LANES), lambda t, te, nu: (jnp.minimum(t, nu[0] - 1), 0)),
                      pl.BlockSpec((1, d, de), emap), pl.BlockSpec((1, d, de), emap),
                      pl.BlockSpec((1, de, d), emap)],
            out_specs=pl.BlockSpec((tm * rpt, LANES), lambda t, te, nu: (t, 0)),
            scratch_shapes=[pltpu.VMEM((d, de), BF16), pltpu.VMEM((d, de), BF16), pltpu.VMEM((de, d), BF16)]),
        out_shape=jax.ShapeDtypeStruct((nt * tm * rpt, LANES), F32),
        compiler_params=_params(("arbitrary",)),
        name="experts",
    )(tile_expert, n_used, x_sorted, w_gate, w_up, w_down)


def _combine_kernel(idx_cur, idx_nxt, x2_ref, route_ref, ys_hbm, y_ref, buf, sem, *, tm, n_tiles, rpt):
    i = pl.program_id(0)
    slot = i % 2

    def gather(idx_ref, dst_slot):
        def body(j, c):
            for u in range(DMA_UNROLL // 2):
                r = j * (DMA_UNROLL // 2) + u
                for kk in range(2):
                    src = pl.multiple_of(idx_ref[0, kk, r] * rpt, rpt)
                    pltpu.make_async_copy(ys_hbm.at[pl.ds(src, rpt), :],
                                          buf.at[dst_slot, kk, pl.ds(r * rpt, rpt), :],
                                          sem.at[dst_slot]).start(priority=kk)
            return c
        lax.fori_loop(0, tm // (DMA_UNROLL // 2), body, 0)

    @pl.when(i == 0)
    def _():
        gather(idx_cur, 0)

    @pl.when(i + 1 < n_tiles)
    def _():
        gather(idx_nxt, 1 - slot)

    for kk in range(2):
        pltpu.make_async_copy(ys_hbm.at[pl.ds(0, tm * rpt), :], buf.at[slot, kk], sem.at[slot]).wait()
    route = route_ref[...]
    w0 = route[:, 2:3]
    w1 = route[:, 3:4]
    for j in range(rpt):
        sl = slice(j * LANES, (j + 1) * LANES)
        y_ref[:, sl] = (x2_ref[:, sl] + w0 * buf[slot, 0, pl.ds(j, tm, stride=rpt), :]
                        + w1 * buf[slot, 1, pl.ds(j, tm, stride=rpt), :])


def _combine(dest_t, x2, route, y_sorted, *, row0, rows, tm=256):
    d = x2.shape[1]
    rpt = d // LANES
    t0 = row0 // tm
    n_tiles = rows // tm
    kern = functools.partial(_combine_kernel, tm=tm, n_tiles=n_tiles, rpt=rpt)
    return pl.pallas_call(
        kern, grid=(n_tiles,),
        in_specs=[pl.BlockSpec((1, 8, tm), lambda i: (t0 + i, 0, 0), memory_space=pltpu.SMEM),
                  pl.BlockSpec((1, 8, tm), lambda i: (t0 + jnp.minimum(i + 1, n_tiles - 1), 0, 0),
                               memory_space=pltpu.SMEM),
                  pl.BlockSpec((tm, d), lambda i: (t0 + i, 0)),
                  pl.BlockSpec((tm, LANES), lambda i: (t0 + i, 0)),
                  pl.BlockSpec(memory_space=pl.ANY)],
        out_specs=pl.BlockSpec((tm, d), lambda i: (i, 0)),
        out_shape=jax.ShapeDtypeStruct((rows, d), F32),
        scratch_shapes=[pltpu.VMEM((2, 2, tm * rpt, LANES), F32), pltpu.SemaphoreType.DMA((2,))],
        compiler_params=_params(("arbitrary",)),
        name="combine",
    )(dest_t, dest_t, x2, route, y_sorted)


def _rank_block(i1, i2, lane_f, cnt_ref):
    tm = i1.shape[0]
    oh0 = lane_f == i1
    oh1 = lane_f == i2
    oh = jnp.where(oh0, 1.0, jnp.where(oh1, 1.0, 0.0))
    ri = lax.broadcasted_iota(I32, (tm, tm), 0)
    ci = lax.broadcasted_iota(I32, (tm, tm), 1)
    earlier = jnp.where(ci < ri, 1.0, 0.0).astype(BF16)
    base = cnt_ref[...] + jnp.dot(earlier, oh.astype(BF16), preferred_element_type=F32)
    r0 = jnp.sum(jnp.where(oh0, base, 0.0), axis=1, keepdims=True)
    r1 = jnp.sum(jnp.where(oh1, base, 0.0), axis=1, keepdims=True)
    cnt_ref[...] += jnp.sum(oh, axis=0, keepdims=True)
    return r0, r1


def _dest_kernel(route_ref, start_ref, dest_ref, *, tm, group):
    start = start_ref[...]
    for t in range(group):
        route = route_ref[t * tm:(t + 1) * tm, :]
        lane = lax.broadcasted_iota(I32, route.shape, 1)
        lane_f = lane.astype(F32)
        d0 = route[:, 4:5] + jnp.sum(jnp.where(lane_f == route[:, 0:1], start, 0.0), axis=1, keepdims=True)
        d1 = route[:, 5:6] + jnp.sum(jnp.where(lane_f == route[:, 1:2], start, 0.0), axis=1, keepdims=True)
        dd = jnp.where(lane == 0, d0, jnp.where(lane == 1, d1, 0.0))
        dest_ref[t] = dd.T[0:8, :].astype(I32)


def _dest(route, start_row, *, tm):
    n = route.shape[0]
    tiles = n // tm
    group = 4 if tiles % 4 == 0 else 1
    return pl.pallas_call(
        functools.partial(_dest_kernel, tm=tm, group=group), grid=(tiles // group,),
        in_specs=[pl.BlockSpec((group * tm, LANES), lambda i: (i, 0)),
                  pl.BlockSpec((1, LANES), lambda i: (0, 0))],
        out_specs=pl.BlockSpec((group, 8, tm), lambda i: (i, 0, 0)),
        out_shape=jax.ShapeDtypeStruct((tiles, 8, tm), I32),
        compiler_params=_params(("parallel",)),
        name="dest",
    )(route, start_row)


def _moe_plan(route, cnt_f, *, n_exp, tm):
    n = route.shape[0]
    nt = (2 * n + n_exp * (tm - 1)) // tm
    cnt = cnt_f[0, :n_exp].astype(I32)
    ptiles = (cnt + tm - 1) // tm
    tile_end = jnp.cumsum(ptiles)
    tile_start = tile_end - ptiles
    start_row = jnp.pad((tile_start * tm).astype(F32), (0, LANES - n_exp)).reshape(1, LANES)
    dest_t = _dest(route, start_row, tm=tm)
    n_used = tile_end[-1]
    tiles = jnp.arange(nt, dtype=I32)
    te = jnp.minimum(jnp.sum((tile_end[None, :] <= tiles[:, None]).astype(I32), axis=1), n_exp - 1)
    te_last = jnp.take(te, jnp.maximum(n_used - 1, 0))
    tile_expert = jnp.where(tiles < n_used, te, te_last)
    pad_start = tile_start * tm + cnt
    pad_len = ptiles * tm - cnt
    return tile_expert, n_used.reshape(1).astype(I32), dest_t, pad_start, pad_len, nt


def kernel(x_prompt, x_sample, cache_k, cache_v, cache_lf, state_C, state_n, state_m, page_table,
           norm1_g, w_in, b_igate, b_fgate_mlstm, b_fgate_fox, mlstm_norm_g, q_norm_g, k_norm_g,
           w_branch_mlstm, w_branch_fox, w_out, norm2_g, w_router_group, b_router_group,
           w_router_expert, b_router_expert, w_gate, w_up, w_down):
    depth = w_in.shape[0]
    assert depth == 1, "single-layer step"
    bp, tp, d = x_prompt.shape
    bs, ts, _ = x_sample.shape
    mh = b_igate.shape[-1]
    fh = b_fgate_fox.shape[-1]
    fd = q_norm_g.shape[-1]
    md = d // mh
    n_exp = w_gate.shape[1]
    n_pages = page_table.shape[1]
    page = cache_k.shape[2]
    assert 2 * mh + fh <= LANES and n_exp + N_GROUPS <= LANES
    np_tok, ns_tok = bp * tp, bs * ts
    l = 0

    w = w_in[l]
    o = 0
    secs = {}
    for name, width in (("m", 4 * d), ("mi", mh), ("mf", mh), ("f", 3 * d), ("ff", fh), ("g", 2 * d)):
        secs[name] = w[:, o:o + width]
        o += width
    pad = jnp.zeros((d, LANES - 2 * mh - fh), F32)
    w_secs = (secs["m"].astype(BF16), secs["f"].astype(BF16), secs["g"].astype(BF16),
              jnp.concatenate([secs["mi"], secs["mf"], secs["ff"], pad], axis=1).astype(BF16))
    bias_s = jnp.concatenate([b_igate[l], b_fgate_mlstm[l], b_fgate_fox[l],
                              jnp.zeros((LANES - 2 * mh - fh,), F32)]).reshape(1, LANES)
    g1 = norm1_g[l].reshape(1, d)
    qg = q_norm_g[l].reshape(1, fd)
    kg = k_norm_g[l].reshape(1, fd)
    g_m = mlstm_norm_g[l].reshape(1, d)
    wa = w_branch_mlstm[l].astype(BF16)
    wb = w_branch_fox[l].astype(BF16)
    wo = w_out[l].astype(BF16)
    g2 = norm2_g[l].reshape(1, d)
    wr32 = jnp.concatenate([w_router_expert[l], w_router_group[l],
                            jnp.zeros((d, LANES - n_exp - N_GROUPS), F32)], axis=1)
    wr_hi = wr32.astype(BF16)
    wr = jnp.concatenate([wr_hi, (wr32 - wr_hi.astype(F32)).astype(BF16)], axis=1)
    br = jnp.concatenate([b_router_expert[l], b_router_group[l],
                          jnp.zeros((LANES - n_exp - N_GROUPS,), F32)]).reshape(1, LANES)

    inproj = functools.partial(_inproj, g1=g1, w_secs=w_secs, bias_s=bias_s, qg=qg, kg=kg, mh=mh, fh=fh)
    pm_p, fq_p, k3_p, v3_p, kb_p, vb_p, gate_p, small_p, kt_p = inproj(
        x_prompt.reshape(np_tok, d), act_dtype=BF16, q_unit=LOG2E, emit_kt=True)
    pm_s, fq_s, k3_s, v3_s, kb_s, vb_s, gate_s, small_s = inproj(x_sample.reshape(ns_tok, d), act_dtype=F32,
                                                                 tm=256)

    ha_p, c_p, n_p, m_p = _mlstm_prompt(pm_p, kt_p, small_p, g_m, batch=bp, seq=tp, mh=mh)
    m0 = jnp.pad(state_m[l], ((0, 0), (0, LANES - mh))).reshape(bs, 1, LANES)
    ha_s, c_s, n_s, m_s = _mlstm(pm_s, small_s, g_m, batch=bs, seq=ts, mh=mh, nb=8,
                                 init=(state_C[l], state_n[l], m0), out_dtype=F32)

    lf_p = small_p[:, 2 * mh:2 * mh + fh]
    lf_s = small_s[:, 2 * mh:2 * mh + fh]
    qx_p, kx_p = _fbias(small_p, batch=bp, seq=tp, fh=fh, lf0=2 * mh)
    ob_p = _fox_prompt(fq_p, qx_p, kb_p, kx_p, vb_p, batch=bp, seq=tp, fh=fh)

    n_phys = cache_k.shape[1]
    lft_c = cache_lf[l].transpose(0, 2, 1).reshape(n_phys * fh, page)
    floc = _cumsum_lanes(lft_c, block_rows=n_phys * fh // 8).reshape(n_phys, fh, page)
    ck = cache_k[l].reshape(n_phys, page * fh, fd)
    cv = cache_v[l].reshape(n_phys, page * fh, fd)
    ob_s = _fox_sample(page_table, fq_s, kb_s, vb_s, small_s, floc, ck, cv, fh=fh, lf0=2 * mh)

    tm = MOE_TILE
    x2, xn_rows, route, cnt_f = _post((x_prompt.reshape(np_tok, d), ha_p, ob_p, gate_p),
                                      (x_sample.reshape(ns_tok, d), ha_s, ob_s, gate_s),
                                      wa, wb, wo, g2, wr, br, n_exp=n_exp, n_groups=N_GROUPS, tm=tm)

    tile_expert, n_used, dest_t, pad_start, pad_len, nt = _moe_plan(route, cnt_f, n_exp=n_exp, tm=tm)
    x_sorted = _dispatch(pad_start, pad_len, n_used, dest_t, xn_rows, nt=nt, tm=tm)
    y_sorted = _experts(tile_expert, n_used, x_sorted.reshape(nt * tm * (d // LANES), LANES),
                        w_gate[l], w_up[l], w_down[l], tm=tm)
    y_p = _combine(dest_t, x2, route, y_sorted, row0=0, rows=np_tok, tm=tm)
    y_s = _combine(dest_t, x2, route, y_sorted, row0=np_tok, rows=ns_tok, tm=tm)

    return (y_p.reshape(bp, tp, d), y_s.reshape(bs, ts, d),
            k3_p.reshape(1, bp, tp, fh, fd), v3_p.reshape(1, bp, tp, fh, fd), lf_p.reshape(1, bp, tp, fh),
            k3_s.reshape(1, bs, ts, fh, fd), v3_s.reshape(1, bs, ts, fh, fd), lf_s.reshape(1, bs, ts, fh),
            c_p[None], n_p[None], m_p[:, 0, :mh][None],
            c_s[None], n_s[None], m_s[:, 0, :mh][None])
```

```python
import functools

import jax
import jax.numpy as jnp
import numpy as np
from jax import lax
from jax.experimental import pallas as pl
from jax.experimental.pallas import tpu as pltpu

F32 = jnp.float32
BF16 = jnp.bfloat16
I32 = jnp.int32
RMS_EPS = 1e-6
LANES = 128
MLSTM_CHUNK = 128
MLSTM_PROMPT_CHUNK = 256
N_GROUPS = 4
VMEM_LIMIT = 56 * 1024 * 1024
NEG_INF = float("-inf")
LOG2E = 1.4426950408889634
DMA_UNROLL = 8
MOE_TILE = 512
PAD_CHUNKS = tuple(MOE_TILE >> i for i in range(1, MOE_TILE.bit_length()))
BIAS_LANES = 8


def _params(sem, vmem=VMEM_LIMIT):
    return pltpu.CompilerParams(dimension_semantics=sem, vmem_limit_bytes=vmem)


def _log_sigmoid(x):
    return -(jnp.maximum(-x, 0.0) + jnp.log1p(jnp.exp(-jnp.abs(x))))


def _sigmoid(x):
    return 1.0 / (1.0 + jnp.exp(-x))


def _nt_dot(a, b):
    return lax.dot_general(a, b, (((1,), (1,)), ((), ())), preferred_element_type=F32)


def _tn_dot(a, b):
    return lax.dot_general(a, b, (((0,), (0,)), ((), ())), preferred_element_type=F32)


def _col_to_row(col, eye):
    return jnp.sum(jnp.where(eye, col, 0.0), axis=0, keepdims=True)


def _inproj_kernel(x_ref, g1_ref, wm_ref, wf_ref, wg_ref, ws_ref, bias_ref, qg_ref, kg_ref,
                   pm_ref, fq_ref, k3_ref, v3_ref, kb_ref, vb_ref, gate_ref, small_ref, *maybe_kt_ref,
                   d, mh, fh, fd, k_scale, q_scale):
    x = x_ref[...]
    ms = jnp.mean(x * x, axis=-1, keepdims=True)
    h = (x * lax.rsqrt(ms + RMS_EPS) * g1_ref[...]).astype(BF16)
    sections = ((0, wm_ref), (4 * d, wf_ref), (7 * d, wg_ref), (9 * d, ws_ref))

    def proj(c0, width):
        base, w_ref = [(b, r) for b, r in sections if b <= c0][-1]
        return jnp.dot(h, w_ref[:, c0 - base:c0 - base + width], preferred_element_type=F32)

    pm_ref[:, 0:d] = proj(0, d).astype(pm_ref.dtype)
    mk = proj(d, d) * k_scale
    pm_ref[:, d:2 * d] = mk.astype(pm_ref.dtype)
    if maybe_kt_ref:
        maybe_kt_ref[0][...] = mk.T.astype(maybe_kt_ref[0].dtype)
    pm_ref[:, 2 * d:3 * d] = proj(2 * d, d).astype(pm_ref.dtype)
    pm_ref[:, 3 * d:4 * d] = proj(3 * d, d).astype(pm_ref.dtype)

    def head_norm(a, g):
        ms_h = jnp.mean(a * a, axis=-1, keepdims=True)
        return a * lax.rsqrt(ms_h + RMS_EPS) * g

    fq = proj(4 * d, d)
    fk = proj(5 * d, d)
    fv = proj(6 * d, d)
    for hh in range(fh):
        sl = slice(hh * fd, (hh + 1) * fd)
        fq_ref[:, sl] = (head_norm(fq[:, sl], qg_ref[...]) * q_scale).astype(fq_ref.dtype)
        kn = head_norm(fk[:, sl], kg_ref[...])
        rows_h = pl.ds(hh, fq.shape[0], stride=fh)
        k3_ref[rows_h, :] = kn
        kb_ref[:, sl] = kn.astype(kb_ref.dtype)
        v3_ref[rows_h, :] = fv[:, sl]
    vb_ref[...] = fv.astype(vb_ref.dtype)

    gate_ref[...] = _sigmoid(proj(7 * d, 2 * d)).astype(gate_ref.dtype)

    sm = proj(9 * d, LANES) + bias_ref[...]
    lane = lax.broadcasted_iota(I32, sm.shape, 1)
    sm = jnp.where(lane < mh, sm, jnp.where(lane < 2 * mh + fh, _log_sigmoid(sm), 0.0))
    small_ref[...] = sm


def _inproj(x2d, g1, w_secs, bias_s, qg, kg, *, mh, fh, act_dtype, q_unit=1.0, emit_kt=False, tm=512):
    n, d = x2d.shape
    fd = d // fh
    md = d // mh
    kern = functools.partial(_inproj_kernel, d=d, mh=mh, fh=fh, fd=fd,
                             k_scale=md ** -0.5, q_scale=fd ** -0.5 * q_unit)
    row = lambda i: (i, 0)
    const = lambda i: (0, 0)
    kt_specs = [pl.BlockSpec((d, tm), lambda i: (0, i))] if emit_kt else []
    kt_shapes = [jax.ShapeDtypeStruct((d, n), act_dtype)] if emit_kt else []
    return pl.pallas_call(
        kern, grid=(n // tm,),
        in_specs=[pl.BlockSpec((tm, d), row),
                  pl.BlockSpec((1, d), const),
                  *[pl.BlockSpec(w.shape, const, pipeline_mode=pl.Buffered(1)) for w in w_secs],
                  pl.BlockSpec((1, LANES), const),
                  pl.BlockSpec((1, fd), const),
                  pl.BlockSpec((1, fd), const)],
        out_specs=[pl.BlockSpec((tm, 4 * d), row),
                   pl.BlockSpec((tm, d), row),
                   pl.BlockSpec((tm * fh, fd), row),
                   pl.BlockSpec((tm * fh, fd), row),
                   pl.BlockSpec((tm, d), row),
                   pl.BlockSpec((tm, d), row),
                   pl.BlockSpec((tm, 2 * d), row),
                   pl.BlockSpec((tm, LANES), row)] + kt_specs,
        out_shape=[jax.ShapeDtypeStruct((n, 4 * d), act_dtype),
                   jax.ShapeDtypeStruct((n, d), act_dtype),
                   jax.ShapeDtypeStruct((n * fh, fd), F32),
                   jax.ShapeDtypeStruct((n * fh, fd), F32),
                   jax.ShapeDtypeStruct((n, d), act_dtype),
                   jax.ShapeDtypeStruct((n, d), act_dtype),
                   jax.ShapeDtypeStruct((n, 2 * d), BF16),
                   jax.ShapeDtypeStruct((n, LANES), F32)] + kt_shapes,
        compiler_params=_params(("parallel",)),
        name="inproj",
    )(x2d, g1, *w_secs, bias_s, qg, kg)


def _mlstm_kernel(q_ref, k_ref, v_ref, o_ref, s_ref, g_ref, c0_ref, n0_ref, m0_ref,
                  h_ref, c_ref, n_ref, m_ref, *, L, mh, md, nb, nc):
    if nc == 1:
        c_in, n_in, m_in = c0_ref, n0_ref, m0_ref
    else:
        c_in, n_in, m_in = c_ref, n_ref, m_ref

        @pl.when(pl.program_id(1) == 0)
        def _():
            c_ref[...] = c0_ref[...]
            n_ref[...] = n0_ref[...]
            m_ref[...] = m0_ref[...]

    ri = lax.broadcasted_iota(I32, (L, L), 0)
    ci = lax.broadcasted_iota(I32, (L, L), 1)
    tri = ci <= ri
    eye = ci == ri
    for bb in range(nb):
        s = s_ref[bb]
        m_all = m_in[bb]
        lane = lax.broadcasted_iota(I32, m_all.shape, 1)
        m_out = m_all
        for hh in range(mh):
            sl = slice(hh * md, (hh + 1) * md)
            ig_col = s[:, hh:hh + 1]
            lf_col = s[:, mh + hh:mh + hh + 1]
            lf_row = _col_to_row(lf_col, eye)
            ig_row = _col_to_row(ig_col, eye)
            b_col = jnp.sum(jnp.where(tri, lf_row, 0.0), axis=1, keepdims=True)
            b_row = _col_to_row(b_col, eye)
            m_prev = m_all[:, hh:hh + 1]
            log_w = jnp.where(tri, b_col - b_row + ig_row, NEG_INF)
            log_inter = b_col + m_prev
            m_t = jnp.maximum(log_inter, jnp.max(log_w, axis=1, keepdims=True))
            w_intra = jnp.exp(log_w - m_t)
            w_inter = jnp.exp(log_inter - m_t)
            q = q_ref[bb, :, sl].astype(BF16)
            k = k_ref[bb, :, sl].astype(BF16)
            v = v_ref[bb, :, sl].astype(BF16)
            sm = _nt_dot(q, k) * w_intra
            c_prev = c_in[bb, hh]
            n_prev = n_in[bb, hh:hh + 1, :]
            num = (w_inter * jnp.dot(q, c_prev.astype(BF16), preferred_element_type=F32)
                   + jnp.dot(sm.astype(BF16), v, preferred_element_type=F32))
            den = (w_inter * jnp.sum(q.astype(F32) * n_prev, axis=1, keepdims=True)
                   + jnp.sum(sm, axis=1, keepdims=True))
            hv = num / jnp.maximum(jnp.abs(den), jnp.exp(-m_t))
            m_new = m_t[L - 1:L, :]
            b_last = b_col[L - 1:L, :]
            decay = jnp.exp(b_last + m_prev - m_new)
            w_rows = jnp.exp(b_last - b_col + ig_col - m_new)
            kw = k.astype(F32) * w_rows
            c_ref[bb, hh] = decay * c_prev + _tn_dot(kw.astype(BF16), v)
            n_ref[bb, hh:hh + 1, :] = decay * n_prev + jnp.sum(kw, axis=0, keepdims=True)
            m_out = jnp.where(lane == hh, m_new, m_out)
            ms = jnp.mean(hv * hv, axis=-1, keepdims=True)
            hn = hv * lax.rsqrt(ms + RMS_EPS) * g_ref[:, sl]
            h_ref[bb, :, sl] = (_sigmoid(o_ref[bb, :, sl].astype(F32)) * hn).astype(h_ref.dtype)
        m_ref[bb] = m_out


def _mlstm(pm, small, g_m, init, *, batch, seq, mh, nb=1, out_dtype=BF16):
    d = pm.shape[1] // 4
    md = d // mh
    L = MLSTM_CHUNK if seq % MLSTM_CHUNK == 0 else seq
    nc = seq // L
    ng = batch // nb
    pm4 = pm.reshape(ng, nb, seq, 4 * d)
    small4 = small.reshape(ng, nb, seq, LANES)
    sec = lambda j: (lambda b, c: (b, 0, c, j))
    in_specs = [pl.BlockSpec((None, nb, L, d), sec(0)), pl.BlockSpec((None, nb, L, d), sec(1)),
                pl.BlockSpec((None, nb, L, d), sec(2)), pl.BlockSpec((None, nb, L, d), sec(3)),
                pl.BlockSpec((None, nb, L, LANES), sec(0)),
                pl.BlockSpec((1, d), lambda b, c: (0, 0))]
    args = [pm4, pm4, pm4, pm4, small4, g_m]
    state_specs = [pl.BlockSpec((nb, mh, md, md), lambda b, c: (b, 0, 0, 0)),
                   pl.BlockSpec((nb, mh, md), lambda b, c: (b, 0, 0)),
                   pl.BlockSpec((nb, 1, LANES), lambda b, c: (b, 0, 0))]
    in_specs += state_specs
    args += list(init)
    kern = functools.partial(_mlstm_kernel, L=L, mh=mh, md=md, nb=nb, nc=nc)
    h4, c_out, n_out, m_out = pl.pallas_call(
        kern, grid=(ng, nc),
        in_specs=in_specs,
        out_specs=[pl.BlockSpec((None, nb, L, d), lambda b, c: (b, 0, c, 0))] + state_specs,
        out_shape=[jax.ShapeDtypeStruct((ng, nb, seq, d), out_dtype),
                   jax.ShapeDtypeStruct((batch, mh, md, md), F32),
                   jax.ShapeDtypeStruct((batch, mh, md), F32),
                   jax.ShapeDtypeStruct((batch, 1, LANES), F32)],
        compiler_params=_params(("parallel", "arbitrary")),
        name="mlstm",
    )(*args)
    return h4.reshape(batch * seq, d), c_out, n_out, m_out


def _mlstm_prompt_kernel(q_ref, kt_ref, v_ref, o_ref, s_ref, g_ref, h_ref, c_ref, n_ref, m_ref, naug,
                         *, L, mh, md, nc):
    c = pl.program_id(1)

    @pl.when(c == 0)
    def _():
        c_ref[...] = jnp.zeros_like(c_ref)
        m_ref[...] = jnp.zeros_like(m_ref)
        naug[...] = jnp.zeros_like(naug)

    s = s_ref[...]
    ri = lax.broadcasted_iota(I32, (L, L), 0)
    ci = lax.broadcasted_iota(I32, (L, L), 1)
    tri = ci <= ri
    hp = lax.Precision.HIGHEST
    r = s.T[0:8, :]
    b_cols = jnp.dot(jnp.where(tri, 1.0, 0.0), s, precision=hp, preferred_element_type=F32)
    b_rows = pltpu.roll(jnp.dot(r, jnp.where(ri <= ci, 1.0, 0.0), precision=hp, preferred_element_type=F32),
                        8 - mh, axis=0)
    g = r - b_rows
    m_all = m_ref[0]
    lane = lax.broadcasted_iota(I32, m_all.shape, 1)
    m_out = m_all
    ones_v = jnp.ones((L, LANES), BF16)
    ones_r = jnp.ones((md, LANES), BF16)
    for hh in range(mh):
        sl = slice(hh * md, (hh + 1) * md)
        m_prev = m_all[:, hh:hh + 1]
        g_row = g[hh:hh + 1, :]
        b_col = b_cols[:, mh + hh:mh + hh + 1]
        a_col = jnp.maximum(m_prev, jnp.max(jnp.where(tri, g_row, NEG_INF), axis=1, keepdims=True))
        m_t = b_col + a_col
        m_new = m_t[L - 1:L, :]
        b_last = b_col[L - 1:L, :]
        w_rows = jnp.exp(b_last - b_rows[hh:hh + 1, :] + r[hh:hh + 1, :] - m_new)
        dc = jnp.exp(b_last + m_prev - m_new)
        m_out = jnp.where(lane == hh, m_new, m_out)
        q = q_ref[:, sl]
        kt = kt_ref[sl, :]
        v_aug = jnp.concatenate([v_ref[:, sl], ones_v], axis=1)
        w_intra = jnp.exp(jnp.where(tri, g_row - a_col, NEG_INF))
        sm = jnp.dot(q, kt, preferred_element_type=F32) * w_intra
        c_prev = c_ref[0, hh]
        n_prev = naug[hh]
        c_aug = jnp.concatenate([c_prev, n_prev], axis=1).astype(BF16)
        comb = (jnp.exp(m_prev - a_col) * jnp.dot(q, c_aug, preferred_element_type=F32)
                + jnp.dot(sm.astype(BF16), v_aug, preferred_element_type=F32))
        den = jnp.maximum(jnp.abs(comb[:, md:]), jnp.exp(-m_t))
        inv = 1.0 / den
        hv = comb[:, 0:md] * jnp.concatenate([inv] * (md // LANES), axis=1)
        ms = jnp.dot((hv * hv).astype(BF16), ones_r, preferred_element_type=F32) * (1.0 / md)
        rs = lax.rsqrt(ms + RMS_EPS)
        hn = hv * jnp.concatenate([rs] * (md // LANES), axis=1) * g_ref[:, sl]
        h_ref[:, sl] = (_sigmoid(o_ref[:, sl].astype(F32)) * hn).astype(h_ref.dtype)
        kw = (kt.astype(F32) * w_rows).astype(BF16)
        upd = jnp.dot(kw, v_aug, preferred_element_type=F32)
        c_ref[0, hh] = dc * c_prev + upd[:, 0:md]
        naug[hh] = dc * n_prev + upd[:, md:]
    m_ref[0] = m_out

    @pl.when(c == nc - 1)
    def _():
        ei = lax.broadcasted_iota(I32, (md, md), 0) == lax.broadcasted_iota(I32, (md, md), 1)
        for hh in range(mh):
            cols = jnp.concatenate([naug[hh]] * (md // LANES), axis=1)
            n_ref[0, hh:hh + 1, :] = jnp.sum(jnp.where(ei, cols, 0.0), axis=0, keepdims=True)


def _mlstm_prompt(pm, kt, small, g_m, *, batch, seq, mh):
    d = pm.shape[1] // 4
    md = d // mh
    L = MLSTM_PROMPT_CHUNK
    assert seq % L == 0
    nc = seq // L
    tok = lambda j: (lambda b, c: (b * nc + c, j))
    kern = functools.partial(_mlstm_prompt_kernel, L=L, mh=mh, md=md, nc=nc)
    return pl.pallas_call(
        kern, grid=(batch, nc),
        in_specs=[pl.BlockSpec((L, d), tok(0)),
                  pl.BlockSpec((d, L), lambda b, c: (0, b * nc + c)),
                  pl.BlockSpec((L, d), tok(2)), pl.BlockSpec((L, d), tok(3)),
                  pl.BlockSpec((L, LANES), tok(0)),
                  pl.BlockSpec((1, d), lambda b, c: (0, 0))],
        out_specs=[pl.BlockSpec((L, d), tok(0)),
                   pl.BlockSpec((1, mh, md, md), lambda b, c: (b, 0, 0, 0)),
                   pl.BlockSpec((1, mh, md), lambda b, c: (b, 0, 0)),
                   pl.BlockSpec((1, 1, LANES), lambda b, c: (b, 0, 0))],
        out_shape=[jax.ShapeDtypeStruct((batch * seq, d), BF16),
                   jax.ShapeDtypeStruct((batch, mh, md, md), F32),
                   jax.ShapeDtypeStruct((batch, mh, md), F32),
                   jax.ShapeDtypeStruct((batch, 1, LANES), F32)],
        scratch_shapes=[pltpu.VMEM((mh, md, LANES), F32)],
        compiler_params=_params(("parallel", "arbitrary")),
        name="mlstm_prompt",
    )(pm, kt, pm, pm, small, g_m)


def _cumsum_kernel(x_ref, o_ref):
    t = x_ref.shape[1]
    si = lax.broadcasted_iota(I32, (t, t), 0)
    ti = lax.broadcasted_iota(I32, (t, t), 1)
    tri = (si <= ti).astype(F32)
    o_ref[...] = jnp.dot(x_ref[...], tri, precision=lax.Precision.HIGHEST, preferred_element_type=F32)


def _cumsum_lanes(x, *, block_rows):
    rows, t = x.shape
    return pl.pallas_call(
        _cumsum_kernel, grid=(rows // block_rows,),
        in_specs=[pl.BlockSpec((block_rows, t), lambda i: (i, 0))],
        out_specs=pl.BlockSpec((block_rows, t), lambda i: (i, 0)),
        out_shape=jax.ShapeDtypeStruct((rows, t), F32),
        compiler_params=_params(("parallel",)),
        name="cumsum",
    )(x)


def _fbias_kernel(s_ref, pq_ref, pk_ref, cq_ref, ck_ref, qx_ref, kx_ref, run_ref):
    @pl.when(pl.program_id(1) == 0)
    def _():
        run_ref[...] = jnp.zeros_like(run_ref)

    s = s_ref[...]
    L = s.shape[0]
    ti = lax.broadcasted_iota(I32, (L, L), 0)
    si = lax.broadcasted_iota(I32, (L, L), 1)
    tri = jnp.where(si <= ti, 1.0, 0.0).astype(BF16)

    def split3(x):
        hi = x.astype(BF16)
        r1 = x - hi.astype(F32)
        mid = r1.astype(BF16)
        return jnp.concatenate([hi, mid, (r1 - mid.astype(F32)).astype(BF16)], axis=1)

    cs = jnp.dot(tri, split3(s), preferred_element_type=F32)
    f_all = (cs[:, 0:LANES] + cs[:, LANES:2 * LANES] + cs[:, 2 * LANES:3 * LANES]) + run_ref[...]
    run_ref[...] = f_all[L - 1:L, :]
    terms = split3(f_all * LOG2E)
    qx_ref[...] = jnp.dot(terms, pq_ref[...], preferred_element_type=F32) + cq_ref[...]
    kx_ref[...] = (jnp.dot(terms, pk_ref[...], preferred_element_type=F32) + ck_ref[...]).astype(BF16)


def _fbias(small, *, batch, seq, fh, lf0):
    assert BIAS_LANES * fh <= LANES
    n = small.shape[0]
    L = 4 * LANES if seq % (4 * LANES) == 0 else LANES
    nc = seq // L
    pq = np.zeros((3 * LANES, LANES), np.float32)
    pk = np.zeros((3 * LANES, LANES), np.float32)
    cq = np.zeros((1, LANES), np.float32)
    ck = np.zeros((1, LANES), np.float32)
    for h in range(fh):
        for j in range(3):
            pq[j * LANES + lf0 + h, BIAS_LANES * h + j] = 1.0
            pk[j * LANES + lf0 + h, BIAS_LANES * h + 3 + j] = -1.0
            cq[0, BIAS_LANES * h + 3 + j] = 1.0
            ck[0, BIAS_LANES * h + j] = 1.0
    const = lambda b, c: (0, 0)
    tok = lambda b, c: (b * nc + c, 0)
    return pl.pallas_call(
        _fbias_kernel, grid=(batch, nc),
        in_specs=[pl.BlockSpec((L, LANES), tok),
                  pl.BlockSpec((3 * LANES, LANES), const), pl.BlockSpec((3 * LANES, LANES), const),
                  pl.BlockSpec((1, LANES), const), pl.BlockSpec((1, LANES), const)],
        out_specs=[pl.BlockSpec((L, LANES), tok), pl.BlockSpec((L, LANES), tok)],
        out_shape=[jax.ShapeDtypeStruct((n, LANES), F32), jax.ShapeDtypeStruct((n, LANES), BF16)],
        scratch_shapes=[pltpu.VMEM((1, LANES), F32)],
        compiler_params=_params(("parallel", "arbitrary")),
        name="fbias",
    )(small, jnp.asarray(pq, BF16), jnp.asarray(pk, BF16), jnp.asarray(cq), jnp.asarray(ck))


def _fox_prompt_kernel(q_ref, qx_ref, k_ref, kx_ref, v_ref, o_ref, *, seq, tq, gh, fd):
    nq = seq // tq
    half = tq // 2
    causal_top = lax.broadcasted_iota(I32, (half, half), 1) <= lax.broadcasted_iota(I32, (half, half), 0)
    causal_bot = lax.broadcasted_iota(I32, (half, tq), 1) <= lax.broadcasted_iota(I32, (half, tq), 0) + half

    def softmax_step(carry, s, v):
        m, l, acc = carry
        m_new = jnp.maximum(m, jnp.max(s, axis=1, keepdims=True))
        a = jnp.exp2(m - m_new)
        p = jnp.exp2(s - m_new)
        l = a * l + jnp.sum(p, axis=1, keepdims=True)
        acc = a * acc + jnp.dot(p.astype(BF16), v, preferred_element_type=F32)
        return m_new, l, acc

    head0 = pl.program_id(1) * gh
    bias_head = lax.broadcasted_iota(I32, (tq, LANES), 1) // BIAS_LANES

    def q_body(qi, _):
        q0 = pl.multiple_of(qi * tq, tq)
        qx = qx_ref[pl.ds(q0, tq), :]
        qas = [jnp.concatenate([q_ref[pl.ds(q0, tq), g * fd:(g + 1) * fd],
                                jnp.where(bias_head == head0 + g, qx, 0.0).astype(BF16)], axis=1)
               for g in range(gh)]

        def block(g, carry_g, k0):
            gs = slice(g * fd, (g + 1) * fd)
            ka = jnp.concatenate([k_ref[pl.ds(k0, tq), gs], kx_ref[pl.ds(k0, tq), :]], axis=1)
            s = _nt_dot(qas[g], ka)
            return softmax_step(carry_g, s, v_ref[pl.ds(k0, tq), gs])

        def kv_body(kj, carry):
            k0 = pl.multiple_of(kj * tq, tq)
            return tuple(block(g, carry[g], k0) for g in range(gh))

        init = tuple((jnp.full((tq, 1), NEG_INF, F32), jnp.zeros((tq, 1), F32), jnp.zeros((tq, fd), F32))
                     for _ in range(gh))
        carry = lax.fori_loop(0, qi, kv_body, init)
        for g in range(gh):
            gs = slice(g * fd, (g + 1) * fd)
            m, l, acc = carry[g]
            for r0, nk, mask in ((0, half, causal_top), (half, tq, causal_bot)):
                ka = jnp.concatenate([k_ref[pl.ds(q0, nk), gs], kx_ref[pl.ds(q0, nk), :]], axis=1)
                s = jnp.where(mask, _nt_dot(qas[g][r0:r0 + half], ka), NEG_INF)
                part = (m[r0:r0 + half], l[r0:r0 + half], acc[r0:r0 + half])
                _, l_out, acc_out = softmax_step(part, s, v_ref[pl.ds(q0, nk), gs])
                o_ref[pl.ds(q0 + r0, half), gs] = (acc_out / l_out).astype(o_ref.dtype)
        return 0

    lax.fori_loop(0, nq, q_body, 0)


def _fox_prompt(fq, qx, kb, kx, vb, *, batch, seq, fh, tq=1024, gh=4):
    n, d = fq.shape
    fd = d // fh
    blk = lambda b, h: (b, h)
    bias = pl.BlockSpec((seq, LANES), lambda b, h: (b, 0))
    kern = functools.partial(_fox_prompt_kernel, seq=seq, tq=tq, gh=gh, fd=fd)
    return pl.pallas_call(
        kern, grid=(batch, fh // gh),
        in_specs=[pl.BlockSpec((seq, gh * fd), blk), bias, pl.BlockSpec((seq, gh * fd), blk), bias,
                  pl.BlockSpec((seq, gh * fd), blk)],
        out_specs=pl.BlockSpec((seq, gh * fd), blk),
        out_shape=jax.ShapeDtypeStruct((n, d), BF16),
        compiler_params=_params(("parallel", "parallel")),
        name="fox_prompt",
    )(fq, qx, kb, kx, vb)


def _fox_sample_kernel(pt_ref, q_ref, kn_ref, vn_ref, s_ref, *rest, n_pages, page, fh, fd, ts, lf0):
    floc = rest[0:n_pages]
    kpages = rest[n_pages:2 * n_pages]
    vpages = rest[2 * n_pages:3 * n_pages]
    o_ref = rest[3 * n_pages]
    p_scr, pn_scr, l_scr = rest[3 * n_pages + 1:]
    d = fh * fd
    rows = fh * ts

    def scores():
        q = q_ref[...].astype(F32)
        qt = jnp.concatenate([q] * fh, axis=0)
        r_head = lax.broadcasted_iota(I32, (rows, d), 0) // ts
        c_head = lax.broadcasted_iota(I32, (rows, d), 1) // fd
        qbd = jnp.where(r_head == c_head, qt, 0.0).astype(BF16)

        off = jnp.zeros((fh, 1), F32)
        fk_pages = []
        for i in range(n_pages):
            fp = floc[i][...] + off
            fk_pages.append(fp)
            off = fp[:, page - 1:page]
        fk = jnp.concatenate(fk_pages, axis=1)
        fk_rows = jnp.concatenate(
            [jnp.broadcast_to(fk[h:h + 1, :], (ts, fk.shape[1])) for h in range(fh)], axis=0)

        sm = s_ref[...]
        ri = lax.broadcasted_iota(I32, (ts, ts), 0)
        ci = lax.broadcasted_iota(I32, (ts, ts), 1)
        eye = ci == ri
        tri = ci <= ri
        fq_cols, bias_new = [], []
        for h in range(fh):
            lf_col = sm[:, lf0 + h:lf0 + h + 1]
            lf_row = _col_to_row(lf_col, eye)
            cum_col = jnp.sum(jnp.where(tri, lf_row, 0.0), axis=1, keepdims=True)
            fq_h = off[h:h + 1, :] + cum_col
            fq_cols.append(fq_h)
            bias_new.append(fq_h - _col_to_row(fq_h, eye))
        fq_col = jnp.concatenate(fq_cols, axis=0)
        bias_n = jnp.concatenate(bias_new, axis=0)
        causal_n = jnp.concatenate([tri] * fh, axis=0)

        s_parts = []
        for i in range(0, n_pages, 2):
            kp = jnp.concatenate(
                [jnp.concatenate([kpages[i + j][pl.ds(h, page, stride=fh), :] for h in range(fh)], axis=1)
                 for j in range(2)],
                axis=0).astype(BF16)
            s_parts.append(_nt_dot(qbd, kp))
        s_past = jnp.concatenate(s_parts, axis=1) + (fq_col - fk_rows)
        s_new = _nt_dot(qbd, kn_ref[...].astype(BF16)) + bias_n
        s_new = jnp.where(causal_n, s_new, NEG_INF)
        m = jnp.maximum(jnp.max(s_past, axis=1, keepdims=True), jnp.max(s_new, axis=1, keepdims=True))
        p_past = jnp.exp(s_past - m)
        p_new = jnp.exp(s_new - m)
        l_scr[...] = jnp.sum(p_past, axis=1, keepdims=True) + jnp.sum(p_new, axis=1, keepdims=True)
        p_scr[...] = p_past.astype(BF16)
        pn_scr[...] = p_new

    def values():
        acc = jnp.dot(pn_scr[...].astype(BF16), vn_ref[...].astype(BF16), preferred_element_type=F32)
        for i in range(0, n_pages, 2):
            vp = jnp.concatenate(
                [jnp.concatenate([vpages[i + j][pl.ds(h, page, stride=fh), :] for h in range(fh)], axis=1)
                 for j in range(2)],
                axis=0).astype(BF16)
            acc = acc + jnp.dot(p_scr[:, i * page:(i + 2) * page], vp, preferred_element_type=F32)
        acc = acc / l_scr[...]
        o_ref[...] = jnp.concatenate(
            [acc[h * ts:(h + 1) * ts, h * fd:(h + 1) * fd] for h in range(fh)], axis=1)

    scores()
    values()


def _fox_sample(page_table, fq, kb, vb, small, floc, cache_k, cache_v, *, fh, lf0):
    bs, n_pages = page_table.shape
    n, d = fq.shape
    ts = n // bs
    fd = d // fh
    page = cache_k.shape[1] // fh
    rows = fh * ts
    tok = lambda b, pt: (b, 0)

    def pmap(i):
        return lambda b, pt: (pt[b, i], 0, 0)

    in_specs = ([pl.BlockSpec((ts, d), tok), pl.BlockSpec((ts, d), tok), pl.BlockSpec((ts, d), tok),
                 pl.BlockSpec((ts, LANES), tok)]
                + [pl.BlockSpec((None, fh, page), pmap(i)) for i in range(n_pages)]
                + [pl.BlockSpec((None, page * fh, fd), pmap(i)) for i in range(n_pages)]
                + [pl.BlockSpec((None, page * fh, fd), pmap(i)) for i in range(n_pages)])
    kern = functools.partial(_fox_sample_kernel, n_pages=n_pages, page=page, fh=fh, fd=fd, ts=ts, lf0=lf0)
    return pl.pallas_call(
        kern,
        grid_spec=pltpu.PrefetchScalarGridSpec(
            num_scalar_prefetch=1, grid=(bs,), in_specs=in_specs,
            out_specs=pl.BlockSpec((ts, d), tok),
            scratch_shapes=[pltpu.VMEM((rows, n_pages * page), BF16),
                            pltpu.VMEM((rows, ts), F32),
                            pltpu.VMEM((rows, 1), F32)]),
        out_shape=jax.ShapeDtypeStruct((n, d), F32),
        compiler_params=_params(("arbitrary",)),
        name="fox_sample",
    )(page_table, fq, kb, vb, small, *([floc] * n_pages), *([cache_k] * n_pages), *([cache_v] * n_pages))


def _post_kernel(xp_ref, hap_ref, obp_ref, gp_ref, xs_ref, has_ref, obs_ref, gs_ref, *rest, tiles_p, **kw):
    cnt_ref = rest[-1]

    @pl.when(pl.program_id(0) == 0)
    def _():
        cnt_ref[...] = jnp.zeros_like(cnt_ref)

    @pl.when(pl.program_id(0) < tiles_p)
    def _():
        _post_body(xp_ref, hap_ref, obp_ref, gp_ref, *rest, **kw)

    @pl.when(pl.program_id(0) >= tiles_p)
    def _():
        _post_body(xs_ref, has_ref, obs_ref, gs_ref, *rest, **kw)


def _post_body(x_ref, ha_ref, ob_ref, gate_ref, wa_ref, wb_ref, wo_ref, g2_ref, wr_ref, br_ref,
               x2_ref, xn_ref, route_ref, cnt_ref, *, d, n_exp, n_groups):
    ba = jnp.dot(ha_ref[...].astype(BF16), wa_ref[...], preferred_element_type=F32)
    bb = jnp.dot(ob_ref[...].astype(BF16), wb_ref[...], preferred_element_type=F32)
    gates = gate_ref[...].astype(F32)
    merged = gates[:, 0:d] * ba + gates[:, d:2 * d] * bb
    x2 = x_ref[...] + jnp.dot(merged.astype(BF16), wo_ref[...], preferred_element_type=F32)
    x2_ref[...] = x2
    ms = jnp.mean(x2 * x2, axis=-1, keepdims=True)
    xn = x2 * lax.rsqrt(ms + RMS_EPS) * g2_ref[...]
    tm = xn.shape[0]
    for j in range(d // LANES):
        xn_ref[pl.ds(j, tm, stride=d // LANES), :] = xn[:, j * LANES:(j + 1) * LANES]

    xh = xn.astype(BF16)
    xl = (xn - xh.astype(F32)).astype(BF16)
    hh = jnp.dot(xh, wr_ref[...], preferred_element_type=F32)
    logits = (hh[:, 0:LANES] + hh[:, LANES:2 * LANES]
              + jnp.dot(xl, wr_ref[:, 0:LANES], preferred_element_type=F32)
              + br_ref[...])
    lane = lax.broadcasted_iota(I32, logits.shape, 1)
    lane_f = lane.astype(F32)
    big = float(LANES)
    epg = n_exp // n_groups
    in_groups = (lane >= n_exp) & (lane < n_exp + n_groups)
    gl = jnp.where(in_groups, logits, NEG_INF)
    gmax = jnp.max(gl, axis=1, keepdims=True)
    gidx = jnp.min(jnp.where(gl == gmax, lane_f, big), axis=1, keepdims=True) - float(n_exp)
    g_p = 1.0 / jnp.sum(jnp.exp(gl - gmax), axis=1, keepdims=True)
    in_group = (lane < n_exp) & ((lane // epg).astype(F32) == gidx)
    el = jnp.where(in_group, logits, NEG_INF)
    m1 = jnp.max(el, axis=1, keepdims=True)
    i1 = jnp.min(jnp.where(el == m1, lane_f, big), axis=1, keepdims=True)
    el2 = jnp.where(lane_f == i1, NEG_INF, el)
    m2 = jnp.max(el2, axis=1, keepdims=True)
    i2 = jnp.min(jnp.where(el2 == m2, lane_f, big), axis=1, keepdims=True)
    esum = jnp.sum(jnp.exp(el - m1), axis=1, keepdims=True)
    p1 = 1.0 / esum
    p2 = jnp.exp(m2 - m1) / esum
    psum = p1 + p2
    w1 = p1 / psum * g_p
    w2 = p2 / psum * g_p
    r0, r1 = _rank_block(i1, i2, lane_f, cnt_ref)
    route_ref[...] = jnp.where(lane == 0, i1, jnp.where(lane == 1, i2, jnp.where(
        lane == 2, w1, jnp.where(lane == 3, w2, jnp.where(lane == 4, r0, jnp.where(lane == 5, r1, 0.0))))))


def _post(acts_p, acts_s, wa, wb, wo, g2, wr, br, *, n_exp, n_groups, tm=256):
    n_p, d = acts_p[0].shape
    n_s = acts_s[0].shape[0]
    tiles_p, tiles_s = n_p // tm, n_s // tm
    n = n_p + n_s
    rpt = d // LANES
    row_p = lambda i: (jnp.minimum(i, tiles_p - 1), 0)
    row_s = lambda i: (jnp.maximum(i - tiles_p, 0), 0)
    row = lambda i: (i, 0)
    const = lambda i: (0, 0)
    kern = functools.partial(_post_kernel, tiles_p=tiles_p, d=d, n_exp=n_exp, n_groups=n_groups)
    wspec = lambda: pl.BlockSpec((d, d), const, pipeline_mode=pl.Buffered(1))
    act_specs = lambda r: [pl.BlockSpec((tm, d), r), pl.BlockSpec((tm, d), r), pl.BlockSpec((tm, d), r),
                           pl.BlockSpec((tm, 2 * d), r)]
    return pl.pallas_call(
        kern, grid=(tiles_p + tiles_s,),
        in_specs=act_specs(row_p) + act_specs(row_s) + [
            wspec(), wspec(), wspec(),
            pl.BlockSpec((1, d), const), pl.BlockSpec((d, 2 * LANES), const), pl.BlockSpec((1, LANES), const)],
        out_specs=[pl.BlockSpec((tm, d), row), pl.BlockSpec((tm * rpt, LANES), row),
                   pl.BlockSpec((tm, LANES), row), pl.BlockSpec((1, LANES), const)],
        out_shape=[jax.ShapeDtypeStruct((n, d), F32), jax.ShapeDtypeStruct((n * rpt, LANES), F32),
                   jax.ShapeDtypeStruct((n, LANES), F32), jax.ShapeDtypeStruct((1, LANES), F32)],
        compiler_params=_params(("arbitrary",)),
        name="post",
    )(*acts_p, *acts_s, wa, wb, wo, g2, wr, br)


def _dispatch_kernel(ps_ref, pl_ref, nu_ref, dest_ref, xn_ref, xs_hbm, stage, zbuf, sem_rows, sem_pad,
                     *, tm, n_tiles, n_exp, nt):
    i = pl.program_id(0)
    slot = i % 2

    def pad_dmas(act):
        for e in range(n_exp):
            pos = ps_ref[e]
            left = pl_ref[e]
            for c in PAD_CHUNKS:
                @pl.when((left & c) != 0)
                def _(pos=pos, c=c):
                    act(pltpu.make_async_copy(zbuf.at[pl.ds(0, c)], xs_hbm.at[pl.ds(pos, c)], sem_pad))
                pos = pos + (left & c)

        def unused_tile(t, carry):
            for part in range(tm // PAD_CHUNKS[0]):
                act(pltpu.make_async_copy(
                    zbuf, xs_hbm.at[pl.ds(t * tm + part * PAD_CHUNKS[0], PAD_CHUNKS[0])], sem_pad))
            return carry
        lax.fori_loop(nu_ref[0], nt, unused_tile, 0)

    @pl.when(i == 0)
    def _():
        zbuf[...] = jnp.zeros_like(zbuf)
        pad_dmas(lambda cp: cp.start())

    def wait_tile(s):
        for _ in range(2):
            pltpu.make_async_copy(stage.at[s], xs_hbm.at[pl.ds(0, tm)], sem_rows.at[s]).wait()

    @pl.when(i >= 2)
    def _():
        wait_tile(slot)

    stage[slot] = xn_ref[...]

    def body(j, c):
        for u in range(DMA_UNROLL // 2):
            r = j * (DMA_UNROLL // 2) + u
            for kk in range(2):
                pltpu.make_async_copy(stage.at[slot, r], xs_hbm.at[dest_ref[0, kk, r]],
                                      sem_rows.at[slot]).start(priority=kk)
        return c
    lax.fori_loop(0, tm // (DMA_UNROLL // 2), body, 0)

    @pl.when(i == n_tiles - 1)
    def _():
        if n_tiles > 1:
            wait_tile(1 - slot)
        wait_tile(slot)
        pad_dmas(lambda cp: cp.wait())


def _dispatch(pad_start, pad_len, n_used, dest_t, xn_rows, *, nt, tm):
    n_tiles = dest_t.shape[0]
    n_exp = pad_start.shape[0]
    rpt = xn_rows.shape[0] // (n_tiles * tm)
    xn3 = xn_rows.reshape(n_tiles * tm, rpt, LANES)
    kern = functools.partial(_dispatch_kernel, tm=tm, n_tiles=n_tiles, n_exp=n_exp, nt=nt)
    return pl.pallas_call(
        kern,
        grid_spec=pltpu.PrefetchScalarGridSpec(
            num_scalar_prefetch=3, grid=(n_tiles,),
            in_specs=[pl.BlockSpec((1, 8, tm), lambda i, ps, pln, nu: (i, 0, 0), memory_space=pltpu.SMEM),
                      pl.BlockSpec((tm, rpt, LANES), lambda i, ps, pln, nu: (i, 0, 0))],
            out_specs=pl.BlockSpec(memory_space=pl.ANY),
            scratch_shapes=[pltpu.VMEM((2, tm, rpt, LANES), F32),
                            pltpu.VMEM((PAD_CHUNKS[0], rpt, LANES), F32),
                            pltpu.SemaphoreType.DMA((2,)), pltpu.SemaphoreType.DMA(())]),
        out_shape=jax.ShapeDtypeStruct((nt * tm, rpt, LANES), F32),
        compiler_params=_params(("arbitrary",)),
        name="dispatch",
    )(pad_start, pad_len, n_used, dest_t, xn3)


def _experts_kernel(te_ref, nu_ref, x_ref, wg_ref, wu_ref, wd_ref, y_ref, wg_b, wu_b, wd_b, *, tm, rpt):
    t = pl.program_id(0)

    @pl.when((t == 0) | (te_ref[t] != te_ref[jnp.maximum(t - 1, 0)]))
    def _():
        wg_b[...] = wg_ref[0].astype(BF16)
        wu_b[...] = wu_ref[0].astype(BF16)
        wd_b[...] = wd_ref[0].astype(BF16)

    @pl.when(t < nu_ref[0])
    def _():
        x = jnp.concatenate([x_ref[pl.ds(j, tm, stride=rpt), :] for j in range(rpt)], axis=1).astype(BF16)
        g = jnp.dot(x, wg_b[...], preferred_element_type=F32)
        u = jnp.dot(x, wu_b[...], preferred_element_type=F32)
        hg = (g * _sigmoid(g)) * u
        y = jnp.dot(hg.astype(BF16), wd_b[...], preferred_element_type=F32)
        for j in range(rpt):
            y_ref[pl.ds(j, tm, stride=rpt), :] = y[:, j * LANES:(j + 1) * LANES]

    @pl.when(t >= nu_ref[0])
    def _():
        y_ref[...] = jnp.zeros_like(y_ref)


def _experts(tile_expert, n_used, x_sorted, w_gate, w_up, w_down, *, tm):
    nt = tile_expert.shape[0]
    n_exp, d, de = w_gate.shape
    rpt = d // LANES
    emap = lambda t, te, nu: (te[t], 0, 0)
    kern = functools.partial(_experts_kernel, tm=tm, rpt=rpt)
    return pl.pallas_call(
        kern,
        grid_spec=pltpu.PrefetchScalarGridSpec(
            num_scalar_prefetch=2, grid=(nt,),
            in_specs=[pl.BlockSpec((tm * rpt, LANES), lambda t, te, nu: (jnp.minimum(t, nu[0] - 1), 0)),
                      pl.BlockSpec((1, d, de), emap), pl.BlockSpec((1, d, de), emap),
                      pl.BlockSpec((1, de, d), emap)],
            out_specs=pl.BlockSpec((tm * rpt, LANES), lambda t, te, nu: (t, 0)),
            scratch_shapes=[pltpu.VMEM((d, de), BF16), pltpu.VMEM((d, de), BF16), pltpu.VMEM((de, d), BF16)]),
        out_shape=jax.ShapeDtypeStruct((nt * tm * rpt, LANES), F32),
        compiler_params=_params(("arbitrary",)),
        name="experts",
    )(tile_expert, n_used, x_sorted, w_gate, w_up, w_down)


def _combine_kernel(idx_cur, idx_nxt, x2_ref, route_ref, ys_hbm, y_ref, buf, sem, *, tm, n_tiles, rpt):
    i = pl.program_id(0)
    slot = i % 2

    def gather(idx_ref, dst_slot):
        def body(j, c):
            for u in range(DMA_UNROLL // 2):
                r = j * (DMA_UNROLL // 2) + u
                for kk in range(2):
                    src = pl.multiple_of(idx_ref[0, kk, r] * rpt, rpt)
                    pltpu.make_async_copy(ys_hbm.at[pl.ds(src, rpt), :],
                                          buf.at[dst_slot, kk, pl.ds(r * rpt, rpt), :],
                                          sem.at[dst_slot]).start(priority=kk)
            return c
        lax.fori_loop(0, tm // (DMA_UNROLL // 2), body, 0)

    @pl.when(i == 0)
    def _():
        gather(idx_cur, 0)

    @pl.when(i + 1 < n_tiles)
    def _():
        gather(idx_nxt, 1 - slot)

    for kk in range(2):
        pltpu.make_async_copy(ys_hbm.at[pl.ds(0, tm * rpt), :], buf.at[slot, kk], sem.at[slot]).wait()
    route = route_ref[...]
    w0 = route[:, 2:3]
    w1 = route[:, 3:4]
    for j in range(rpt):
        sl = slice(j * LANES, (j + 1) * LANES)
        y_ref[:, sl] = (x2_ref[:, sl] + w0 * buf[slot, 0, pl.ds(j, tm, stride=rpt), :]
                        + w1 * buf[slot, 1, pl.ds(j, tm, stride=rpt), :])


def _combine(dest_t, x2, route, y_sorted, *, row0, rows, tm=256):
    d = x2.shape[1]
    rpt = d // LANES
    t0 = row0 // tm
    n_tiles = rows // tm
    kern = functools.partial(_combine_kernel, tm=tm, n_tiles=n_tiles, rpt=rpt)
    return pl.pallas_call(
        kern, grid=(n_tiles,),
        in_specs=[pl.BlockSpec((1, 8, tm), lambda i: (t0 + i, 0, 0), memory_space=pltpu.SMEM),
                  pl.BlockSpec((1, 8, tm), lambda i: (t0 + jnp.minimum(i + 1, n_tiles - 1), 0, 0),
                               memory_space=pltpu.SMEM),
                  pl.BlockSpec((tm, d), lambda i: (t0 + i, 0)),
                  pl.BlockSpec((tm, LANES), lambda i: (t0 + i, 0)),
                  pl.BlockSpec(memory_space=pl.ANY)],
        out_specs=pl.BlockSpec((tm, d), lambda i: (i, 0)),
        out_shape=jax.ShapeDtypeStruct((rows, d), F32),
        scratch_shapes=[pltpu.VMEM((2, 2, tm * rpt, LANES), F32), pltpu.SemaphoreType.DMA((2,))],
        compiler_params=_params(("arbitrary",)),
        name="combine",
    )(dest_t, dest_t, x2, route, y_sorted)


def _rank_block(i1, i2, lane_f, cnt_ref):
    tm = i1.shape[0]
    oh0 = lane_f == i1
    oh1 = lane_f == i2
    oh = jnp.where(oh0, 1.0, jnp.where(oh1, 1.0, 0.0))
    ri = lax.broadcasted_iota(I32, (tm, tm), 0)
    ci = lax.broadcasted_iota(I32, (tm, tm), 1)
    earlier = jnp.where(ci < ri, 1.0, 0.0).astype(BF16)
    base = cnt_ref[...] + jnp.dot(earlier, oh.astype(BF16), preferred_element_type=F32)
    r0 = jnp.sum(jnp.where(oh0, base, 0.0), axis=1, keepdims=True)
    r1 = jnp.sum(jnp.where(oh1, base, 0.0), axis=1, keepdims=True)
    cnt_ref[...] += jnp.sum(oh, axis=0, keepdims=True)
    return r0, r1


def _dest_kernel(route_ref, start_ref, dest_ref, *, tm, group):
    start = start_ref[...]
    for t in range(group):
        route = route_ref[t * tm:(t + 1) * tm, :]
        lane = lax.broadcasted_iota(I32, route.shape, 1)
        lane_f = lane.astype(F32)
        d0 = route[:, 4:5] + jnp.sum(jnp.where(lane_f == route[:, 0:1], start, 0.0), axis=1, keepdims=True)
        d1 = route[:, 5:6] + jnp.sum(jnp.where(lane_f == route[:, 1:2], start, 0.0), axis=1, keepdims=True)
        dd = jnp.where(lane == 0, d0, jnp.where(lane == 1, d1, 0.0))
        dest_ref[t] = dd.T[0:8, :].astype(I32)


def _dest(route, start_row, *, tm):
    n = route.shape[0]
    tiles = n // tm
    group = 4 if tiles % 4 == 0 else 1
    return pl.pallas_call(
        functools.partial(_dest_kernel, tm=tm, group=group), grid=(tiles // group,),
        in_specs=[pl.BlockSpec((group * tm, LANES), lambda i: (i, 0)),
                  pl.BlockSpec((1, LANES), lambda i: (0, 0))],
        out_specs=pl.BlockSpec((group, 8, tm), lambda i: (i, 0, 0)),
        out_shape=jax.ShapeDtypeStruct((tiles, 8, tm), I32),
        compiler_params=_params(("parallel",)),
        name="dest",
    )(route, start_row)


def _moe_plan(route, cnt_f, *, n_exp, tm):
    n = route.shape[0]
    nt = (2 * n + n_exp * (tm - 1)) // tm
    cnt = cnt_f[0, :n_exp].astype(I32)
    ptiles = (cnt + tm - 1) // tm
    tile_end = jnp.cumsum(ptiles)
    tile_start = tile_end - ptiles
    start_row = jnp.pad((tile_start * tm).astype(F32), (0, LANES - n_exp)).reshape(1, LANES)
    dest_t = _dest(route, start_row, tm=tm)
    n_used = tile_end[-1]
    tiles = jnp.arange(nt, dtype=I32)
    te = jnp.minimum(jnp.sum((tile_end[None, :] <= tiles[:, None]).astype(I32), axis=1), n_exp - 1)
    te_last = jnp.take(te, jnp.maximum(n_used - 1, 0))
    tile_expert = jnp.where(tiles < n_used, te, te_last)
    pad_start = tile_start * tm + cnt
    pad_len = ptiles * tm - cnt
    return tile_expert, n_used.reshape(1).astype(I32), dest_t, pad_start, pad_len, nt


def kernel(x_prompt, x_sample, cache_k, cache_v, cache_lf, state_C, state_n, state_m, page_table,
           norm1_g, w_in, b_igate, b_fgate_mlstm, b_fgate_fox, mlstm_norm_g, q_norm_g, k_norm_g,
           w_branch_mlstm, w_branch_fox, w_out, norm2_g, w_router_group, b_router_group,
           w_router_expert, b_router_expert, w_gate, w_up, w_down):
    depth = w_in.shape[0]
    assert depth == 1, "single-layer step"
    bp, tp, d = x_prompt.shape
    bs, ts, _ = x_sample.shape
    mh = b_igate.shape[-1]
    fh = b_fgate_fox.shape[-1]
    fd = q_norm_g.shape[-1]
    md = d // mh
    n_exp = w_gate.shape[1]
    n_pages = page_table.shape[1]
    page = cache_k.shape[2]
    assert 2 * mh + fh <= LANES and n_exp + N_GROUPS <= LANES
    np_tok, ns_tok = bp * tp, bs * ts
    l = 0

    w = w_in[l]
    o = 0
    secs = {}
    for name, width in (("m", 4 * d), ("mi", mh), ("mf", mh), ("f", 3 * d), ("ff", fh), ("g", 2 * d)):
        secs[name] = w[:, o:o + width]
        o += width
    pad = jnp.zeros((d, LANES - 2 * mh - fh), F32)
    w_secs = (secs["m"].astype(BF16), secs["f"].astype(BF16), secs["g"].astype(BF16),
              jnp.concatenate([secs["mi"], secs["mf"], secs["ff"], pad], axis=1).astype(BF16))
    bias_s = jnp.concatenate([b_igate[l], b_fgate_mlstm[l], b_fgate_fox[l],
                              jnp.zeros((LANES - 2 * mh - fh,), F32)]).reshape(1, LANES)
    g1 = norm1_g[l].reshape(1, d)
    qg = q_norm_g[l].reshape(1, fd)
    kg = k_norm_g[l].reshape(1, fd)
    g_m = mlstm_norm_g[l].reshape(1, d)
    wa = w_branch_mlstm[l].astype(BF16)
    wb = w_branch_fox[l].astype(BF16)
    wo = w_out[l].astype(BF16)
    g2 = norm2_g[l].reshape(1, d)
    wr32 = jnp.concatenate([w_router_expert[l], w_router_group[l],
                            jnp.zeros((d, LANES - n_exp - N_GROUPS), F32)], axis=1)
    wr_hi = wr32.astype(BF16)
    wr = jnp.concatenate([wr_hi, (wr32 - wr_hi.astype(F32)).astype(BF16)], axis=1)
    br = jnp.concatenate([b_router_expert[l], b_router_group[l],
                          jnp.zeros((LANES - n_exp - N_GROUPS,), F32)]).reshape(1, LANES)

    inproj = functools.partial(_inproj, g1=g1, w_secs=w_secs, bias_s=bias_s, qg=qg, kg=kg, mh=mh, fh=fh)
    pm_p, fq_p, k3_p, v3_p, kb_p, vb_p, gate_p, small_p, kt_p = inproj(
        x_prompt.reshape(np_tok, d), act_dtype=BF16, q_unit=LOG2E, emit_kt=True)
    pm_s, fq_s, k3_s, v3_s, kb_s, vb_s, gate_s, small_s = inproj(x_sample.reshape(ns_tok, d), act_dtype=F32,
                                                                 tm=256)

    ha_p, c_p, n_p, m_p = _mlstm_prompt(pm_p, kt_p, small_p, g_m, batch=bp, seq=tp, mh=mh)
    m0 = jnp.pad(state_m[l], ((0, 0), (0, LANES - mh))).reshape(bs, 1, LANES)
    ha_s, c_s, n_s, m_s = _mlstm(pm_s, small_s, g_m, (state_C[l], state_n[l], m0),
                                 batch=bs, seq=ts, mh=mh, nb=8, out_dtype=F32)

    lf_p = small_p[:, 2 * mh:2 * mh + fh]
    lf_s = small_s[:, 2 * mh:2 * mh + fh]
    qx_p, kx_p = _fbias(small_p, batch=bp, seq=tp, fh=fh, lf0=2 * mh)
    ob_p = _fox_prompt(fq_p, qx_p, kb_p, kx_p, vb_p, batch=bp, seq=tp, fh=fh)

    n_phys = cache_k.shape[1]
    lft_c = cache_lf[l].transpose(0, 2, 1).reshape(n_phys * fh, page)
    floc = _cumsum_lanes(lft_c, block_rows=n_phys * fh // 8).reshape(n_phys, fh, page)
    ck = cache_k[l].reshape(n_phys, page * fh, fd)
    cv = cache_v[l].reshape(n_phys, page * fh, fd)
    ob_s = _fox_sample(page_table, fq_s, kb_s, vb_s, small_s, floc, ck, cv, fh=fh, lf0=2 * mh)

    tm = MOE_TILE
    x2, xn_rows, route, cnt_f = _post((x_prompt.reshape(np_tok, d), ha_p, ob_p, gate_p),
                                      (x_sample.reshape(ns_tok, d), ha_s, ob_s, gate_s),
                                      wa, wb, wo, g2, wr, br, n_exp=n_exp, n_groups=N_GROUPS, tm=tm)

    tile_expert, n_used, dest_t, pad_start, pad_len, nt = _moe_plan(route, cnt_f, n_exp=n_exp, tm=tm)
    x_sorted = _dispatch(pad_start, pad_len, n_used, dest_t, xn_rows, nt=nt, tm=tm)
    y_sorted = _experts(tile_expert, n_used, x_sorted.reshape(nt * tm * (d // LANES), LANES),
                        w_gate[l], w_up[l], w_down[l], tm=tm)
    y_p = _combine(dest_t, x2, route, y_sorted, row0=0, rows=np_tok, tm=tm)
    y_s = _combine(dest_t, x2, route, y_sorted, row0=np_tok, rows=ns_tok, tm=tm)

    return (y_p.reshape(bp, tp, d), y_s.reshape(bs, ts, d),
            k3_p.reshape(1, bp, tp, fh, fd), v3_p.reshape(1, bp, tp, fh, fd), lf_p.reshape(1, bp, tp, fh),
            k3_s.reshape(1, bs, ts, fh, fd), v3_s.reshape(1, bs, ts, fh, fd), lf_s.reshape(1, bs, ts, fh),
            c_p[None], n_p[None], m_p[:, 0, :mh][None],
            c_s[None], n_s[None], m_s[:, 0, :mh][None])
```

```python
import functools

import jax
import jax.numpy as jnp
import numpy as np
from jax import lax
from jax.experimental import pallas as pl
from jax.experimental.pallas import tpu as pltpu

F32 = jnp.float32
BF16 = jnp.bfloat16
I32 = jnp.int32
RMS_EPS = 1e-6
LANES = 128
MLSTM_CHUNK = 128
MLSTM_PROMPT_CHUNK = 256
N_GROUPS = 4
VMEM_LIMIT = 56 * 1024 * 1024
NEG_INF = float("-inf")
LOG2E = 1.4426950408889634
DMA_UNROLL = 8
MOE_TILE = 512
PAD_CHUNKS = tuple(MOE_TILE >> i for i in range(1, MOE_TILE.bit_length()))
BIAS_LANES = 8
X_BUFFERS = 3


def _params(sem, vmem=VMEM_LIMIT):
    return pltpu.CompilerParams(dimension_semantics=sem, vmem_limit_bytes=vmem)


def _log_sigmoid(x):
    return -(jnp.maximum(-x, 0.0) + jnp.log1p(jnp.exp(-jnp.abs(x))))


def _sigmoid(x):
    return 1.0 / (1.0 + jnp.exp(-x))


def _nt_dot(a, b):
    return lax.dot_general(a, b, (((1,), (1,)), ((), ())), preferred_element_type=F32)


def _tn_dot(a, b):
    return lax.dot_general(a, b, (((0,), (0,)), ((), ())), preferred_element_type=F32)


def _col_to_row(col, eye):
    return jnp.sum(jnp.where(eye, col, 0.0), axis=0, keepdims=True)


def _inproj_kernel(x_ref, g1_ref, wm_ref, wf_ref, wg_ref, ws_ref, bias_ref, qg_ref, kg_ref,
                   pm_ref, fq_ref, k3_ref, v3_ref, kb_ref, vb_ref, gate_ref, small_ref, *maybe_kt_ref,
                   d, mh, fh, fd, k_scale, q_scale):
    x = x_ref[...]
    ms = jnp.mean(x * x, axis=-1, keepdims=True)
    h = (x * lax.rsqrt(ms + RMS_EPS) * g1_ref[...]).astype(BF16)
    sections = ((0, wm_ref), (4 * d, wf_ref), (7 * d, wg_ref), (9 * d, ws_ref))

    def proj(c0, width):
        base, w_ref = [(b, r) for b, r in sections if b <= c0][-1]
        return jnp.dot(h, w_ref[:, c0 - base:c0 - base + width], preferred_element_type=F32)

    pm_ref[:, 0:d] = proj(0, d).astype(pm_ref.dtype)
    mk = proj(d, d) * k_scale
    pm_ref[:, d:2 * d] = mk.astype(pm_ref.dtype)
    if maybe_kt_ref:
        maybe_kt_ref[0][...] = mk.T.astype(maybe_kt_ref[0].dtype)
    pm_ref[:, 2 * d:3 * d] = proj(2 * d, d).astype(pm_ref.dtype)
    pm_ref[:, 3 * d:4 * d] = proj(3 * d, d).astype(pm_ref.dtype)

    def head_norm(a, g):
        ms_h = jnp.mean(a * a, axis=-1, keepdims=True)
        return a * lax.rsqrt(ms_h + RMS_EPS) * g

    fq = proj(4 * d, d)
    fk = proj(5 * d, d)
    fv = proj(6 * d, d)
    for hh in range(fh):
        sl = slice(hh * fd, (hh + 1) * fd)
        fq_ref[:, sl] = (head_norm(fq[:, sl], qg_ref[...]) * q_scale).astype(fq_ref.dtype)
        kn = head_norm(fk[:, sl], kg_ref[...])
        rows_h = pl.ds(hh, fq.shape[0], stride=fh)
        k3_ref[rows_h, :] = kn
        kb_ref[:, sl] = kn.astype(kb_ref.dtype)
        v3_ref[rows_h, :] = fv[:, sl]
    vb_ref[...] = fv.astype(vb_ref.dtype)

    gate_ref[...] = _sigmoid(proj(7 * d, 2 * d)).astype(gate_ref.dtype)

    sm = proj(9 * d, LANES) + bias_ref[...]
    lane = lax.broadcasted_iota(I32, sm.shape, 1)
    sm = jnp.where(lane < mh, sm, jnp.where(lane < 2 * mh + fh, _log_sigmoid(sm), 0.0))
    small_ref[...] = sm


def _inproj(x2d, g1, w_secs, bias_s, qg, kg, *, mh, fh, act_dtype, q_unit=1.0, emit_kt=False, tm=512):
    n, d = x2d.shape
    fd = d // fh
    md = d // mh
    kern = functools.partial(_inproj_kernel, d=d, mh=mh, fh=fh, fd=fd,
                             k_scale=md ** -0.5, q_scale=fd ** -0.5 * q_unit)
    row = lambda i: (i, 0)
    const = lambda i: (0, 0)
    kt_specs = [pl.BlockSpec((d, tm), lambda i: (0, i))] if emit_kt else []
    kt_shapes = [jax.ShapeDtypeStruct((d, n), act_dtype)] if emit_kt else []
    return pl.pallas_call(
        kern, grid=(n // tm,),
        in_specs=[pl.BlockSpec((tm, d), row),
                  pl.BlockSpec((1, d), const),
                  *[pl.BlockSpec(w.shape, const, pipeline_mode=pl.Buffered(1)) for w in w_secs],
                  pl.BlockSpec((1, LANES), const),
                  pl.BlockSpec((1, fd), const),
                  pl.BlockSpec((1, fd), const)],
        out_specs=[pl.BlockSpec((tm, 4 * d), row),
                   pl.BlockSpec((tm, d), row),
                   pl.BlockSpec((tm * fh, fd), row),
                   pl.BlockSpec((tm * fh, fd), row),
                   pl.BlockSpec((tm, d), row),
                   pl.BlockSpec((tm, d), row),
                   pl.BlockSpec((tm, 2 * d), row),
                   pl.BlockSpec((tm, LANES), row)] + kt_specs,
        out_shape=[jax.ShapeDtypeStruct((n, 4 * d), act_dtype),
                   jax.ShapeDtypeStruct((n, d), act_dtype),
                   jax.ShapeDtypeStruct((n * fh, fd), F32),
                   jax.ShapeDtypeStruct((n * fh, fd), F32),
                   jax.ShapeDtypeStruct((n, d), act_dtype),
                   jax.ShapeDtypeStruct((n, d), act_dtype),
                   jax.ShapeDtypeStruct((n, 2 * d), BF16),
                   jax.ShapeDtypeStruct((n, LANES), F32)] + kt_shapes,
        compiler_params=_params(("parallel",)),
        name="inproj",
    )(x2d, g1, *w_secs, bias_s, qg, kg)


def _mlstm_kernel(q_ref, k_ref, v_ref, o_ref, s_ref, g_ref, c0_ref, n0_ref, m0_ref,
                  h_ref, c_ref, n_ref, m_ref, *, L, mh, md, nb, nc):
    if nc == 1:
        c_in, n_in, m_in = c0_ref, n0_ref, m0_ref
    else:
        c_in, n_in, m_in = c_ref, n_ref, m_ref

        @pl.when(pl.program_id(1) == 0)
        def _():
            c_ref[...] = c0_ref[...]
            n_ref[...] = n0_ref[...]
            m_ref[...] = m0_ref[...]

    ri = lax.broadcasted_iota(I32, (L, L), 0)
    ci = lax.broadcasted_iota(I32, (L, L), 1)
    tri = ci <= ri
    eye = ci == ri
    for bb in range(nb):
        s = s_ref[bb]
        m_all = m_in[bb]
        lane = lax.broadcasted_iota(I32, m_all.shape, 1)
        m_out = m_all
        for hh in range(mh):
            sl = slice(hh * md, (hh + 1) * md)
            ig_col = s[:, hh:hh + 1]
            lf_col = s[:, mh + hh:mh + hh + 1]
            lf_row = _col_to_row(lf_col, eye)
            ig_row = _col_to_row(ig_col, eye)
            b_col = jnp.sum(jnp.where(tri, lf_row, 0.0), axis=1, keepdims=True)
            b_row = _col_to_row(b_col, eye)
            m_prev = m_all[:, hh:hh + 1]
            log_w = jnp.where(tri, b_col - b_row + ig_row, NEG_INF)
            log_inter = b_col + m_prev
            m_t = jnp.maximum(log_inter, jnp.max(log_w, axis=1, keepdims=True))
            w_intra = jnp.exp(log_w - m_t)
            w_inter = jnp.exp(log_inter - m_t)
            q = q_ref[bb, :, sl].astype(BF16)
            k = k_ref[bb, :, sl].astype(BF16)
            v = v_ref[bb, :, sl].astype(BF16)
            sm = _nt_dot(q, k) * w_intra
            c_prev = c_in[bb, hh]
            n_prev = n_in[bb, hh:hh + 1, :]
            num = (w_inter * jnp.dot(q, c_prev.astype(BF16), preferred_element_type=F32)
                   + jnp.dot(sm.astype(BF16), v, preferred_element_type=F32))
            den = (w_inter * jnp.sum(q.astype(F32) * n_prev, axis=1, keepdims=True)
                   + jnp.sum(sm, axis=1, keepdims=True))
            hv = num / jnp.maximum(jnp.abs(den), jnp.exp(-m_t))
            m_new = m_t[L - 1:L, :]
            b_last = b_col[L - 1:L, :]
            decay = jnp.exp(b_last + m_prev - m_new)
            w_rows = jnp.exp(b_last - b_col + ig_col - m_new)
            kw = k.astype(F32) * w_rows
            c_ref[bb, hh] = decay * c_prev + _tn_dot(kw.astype(BF16), v)
            n_ref[bb, hh:hh + 1, :] = decay * n_prev + jnp.sum(kw, axis=0, keepdims=True)
            m_out = jnp.where(lane == hh, m_new, m_out)
            ms = jnp.mean(hv * hv, axis=-1, keepdims=True)
            hn = hv * lax.rsqrt(ms + RMS_EPS) * g_ref[:, sl]
            h_ref[bb, :, sl] = (_sigmoid(o_ref[bb, :, sl].astype(F32)) * hn).astype(h_ref.dtype)
        m_ref[bb] = m_out


def _mlstm(pm, small, g_m, init, *, batch, seq, mh, nb=1, out_dtype=BF16):
    d = pm.shape[1] // 4
    md = d // mh
    L = MLSTM_CHUNK if seq % MLSTM_CHUNK == 0 else seq
    nc = seq // L
    ng = batch // nb
    pm4 = pm.reshape(ng, nb, seq, 4 * d)
    small4 = small.reshape(ng, nb, seq, LANES)
    sec = lambda j: (lambda b, c: (b, 0, c, j))
    in_specs = [pl.BlockSpec((None, nb, L, d), sec(0)), pl.BlockSpec((None, nb, L, d), sec(1)),
                pl.BlockSpec((None, nb, L, d), sec(2)), pl.BlockSpec((None, nb, L, d), sec(3)),
                pl.BlockSpec((None, nb, L, LANES), sec(0)),
                pl.BlockSpec((1, d), lambda b, c: (0, 0))]
    args = [pm4, pm4, pm4, pm4, small4, g_m]
    state_specs = [pl.BlockSpec((nb, mh, md, md), lambda b, c: (b, 0, 0, 0)),
                   pl.BlockSpec((nb, mh, md), lambda b, c: (b, 0, 0)),
                   pl.BlockSpec((nb, 1, LANES), lambda b, c: (b, 0, 0))]
    in_specs += state_specs
    args += list(init)
    kern = functools.partial(_mlstm_kernel, L=L, mh=mh, md=md, nb=nb, nc=nc)
    h4, c_out, n_out, m_out = pl.pallas_call(
        kern, grid=(ng, nc),
        in_specs=in_specs,
        out_specs=[pl.BlockSpec((None, nb, L, d), lambda b, c: (b, 0, c, 0))] + state_specs,
        out_shape=[jax.ShapeDtypeStruct((ng, nb, seq, d), out_dtype),
                   jax.ShapeDtypeStruct((batch, mh, md, md), F32),
                   jax.ShapeDtypeStruct((batch, mh, md), F32),
                   jax.ShapeDtypeStruct((batch, 1, LANES), F32)],
        compiler_params=_params(("parallel", "arbitrary")),
        name="mlstm",
    )(*args)
    return h4.reshape(batch * seq, d), c_out, n_out, m_out


def _mlstm_prompt_kernel(q_ref, kt_ref, v_ref, o_ref, s_ref, g_ref, h_ref, c_ref, n_ref, m_ref, naug,
                         *, L, mh, md, nc):
    c = pl.program_id(1)

    @pl.when(c == 0)
    def _():
        c_ref[...] = jnp.zeros_like(c_ref)
        m_ref[...] = jnp.zeros_like(m_ref)
        naug[...] = jnp.zeros_like(naug)

    s = s_ref[...]
    ri = lax.broadcasted_iota(I32, (L, L), 0)
    ci = lax.broadcasted_iota(I32, (L, L), 1)
    tri = ci <= ri
    hp = lax.Precision.HIGHEST
    r = s.T[0:8, :]
    b_cols = jnp.dot(jnp.where(tri, 1.0, 0.0), s, precision=hp, preferred_element_type=F32)
    b_rows = pltpu.roll(jnp.dot(r, jnp.where(ri <= ci, 1.0, 0.0), precision=hp, preferred_element_type=F32),
                        8 - mh, axis=0)
    g = r - b_rows
    m_all = m_ref[0]
    lane = lax.broadcasted_iota(I32, m_all.shape, 1)
    m_out = m_all
    ones_v = jnp.ones((L, LANES), BF16)
    ones_r = jnp.ones((md, LANES), BF16)
    for hh in range(mh):
        sl = slice(hh * md, (hh + 1) * md)
        m_prev = m_all[:, hh:hh + 1]
        g_row = g[hh:hh + 1, :]
        b_col = b_cols[:, mh + hh:mh + hh + 1]
        a_col = jnp.maximum(m_prev, jnp.max(jnp.where(tri, g_row, NEG_INF), axis=1, keepdims=True))
        m_t = b_col + a_col
        m_new = m_t[L - 1:L, :]
        b_last = b_col[L - 1:L, :]
        w_rows = jnp.exp(b_last - b_rows[hh:hh + 1, :] + r[hh:hh + 1, :] - m_new)
        dc = jnp.exp(b_last + m_prev - m_new)
        m_out = jnp.where(lane == hh, m_new, m_out)
        q = q_ref[:, sl]
        kt = kt_ref[sl, :]
        v_aug = jnp.concatenate([v_ref[:, sl], ones_v], axis=1)
        w_intra = jnp.exp(jnp.where(tri, g_row - a_col, NEG_INF))
        sm = jnp.dot(q, kt, preferred_element_type=F32) * w_intra
        c_prev = c_ref[0, hh]
        n_prev = naug[hh]
        c_aug = jnp.concatenate([c_prev, n_prev], axis=1).astype(BF16)
        comb = (jnp.exp(m_prev - a_col) * jnp.dot(q, c_aug, preferred_element_type=F32)
                + jnp.dot(sm.astype(BF16), v_aug, preferred_element_type=F32))
        den = jnp.maximum(jnp.abs(comb[:, md:]), jnp.exp(-m_t))
        inv = 1.0 / den
        hv = comb[:, 0:md] * jnp.concatenate([inv] * (md // LANES), axis=1)
        ms = jnp.dot((hv * hv).astype(BF16), ones_r, preferred_element_type=F32) * (1.0 / md)
        rs = lax.rsqrt(ms + RMS_EPS)
        hn = hv * jnp.concatenate([rs] * (md // LANES), axis=1) * g_ref[:, sl]
        h_ref[:, sl] = (_sigmoid(o_ref[:, sl].astype(F32)) * hn).astype(h_ref.dtype)
        kw = (kt.astype(F32) * w_rows).astype(BF16)
        upd = jnp.dot(kw, v_aug, preferred_element_type=F32)
        c_ref[0, hh] = dc * c_prev + upd[:, 0:md]
        naug[hh] = dc * n_prev + upd[:, md:]
    m_ref[0] = m_out

    @pl.when(c == nc - 1)
    def _():
        ei = lax.broadcasted_iota(I32, (md, md), 0) == lax.broadcasted_iota(I32, (md, md), 1)
        for hh in range(mh):
            cols = jnp.concatenate([naug[hh]] * (md // LANES), axis=1)
            n_ref[0, hh:hh + 1, :] = jnp.sum(jnp.where(ei, cols, 0.0), axis=0, keepdims=True)


def _mlstm_prompt(pm, kt, small, g_m, *, batch, seq, mh):
    d = pm.shape[1] // 4
    md = d // mh
    L = MLSTM_PROMPT_CHUNK
    assert seq % L == 0
    nc = seq // L
    tok = lambda j: (lambda b, c: (b * nc + c, j))
    kern = functools.partial(_mlstm_prompt_kernel, L=L, mh=mh, md=md, nc=nc)
    return pl.pallas_call(
        kern, grid=(batch, nc),
        in_specs=[pl.BlockSpec((L, d), tok(0)),
                  pl.BlockSpec((d, L), lambda b, c: (0, b * nc + c)),
                  pl.BlockSpec((L, d), tok(2)), pl.BlockSpec((L, d), tok(3)),
                  pl.BlockSpec((L, LANES), tok(0)),
                  pl.BlockSpec((1, d), lambda b, c: (0, 0))],
        out_specs=[pl.BlockSpec((L, d), tok(0)),
                   pl.BlockSpec((1, mh, md, md), lambda b, c: (b, 0, 0, 0)),
                   pl.BlockSpec((1, mh, md), lambda b, c: (b, 0, 0)),
                   pl.BlockSpec((1, 1, LANES), lambda b, c: (b, 0, 0))],
        out_shape=[jax.ShapeDtypeStruct((batch * seq, d), BF16),
                   jax.ShapeDtypeStruct((batch, mh, md, md), F32),
                   jax.ShapeDtypeStruct((batch, mh, md), F32),
                   jax.ShapeDtypeStruct((batch, 1, LANES), F32)],
        scratch_shapes=[pltpu.VMEM((mh, md, LANES), F32)],
        compiler_params=_params(("parallel", "arbitrary")),
        name="mlstm_prompt",
    )(pm, kt, pm, pm, small, g_m)


def _cumsum_kernel(x_ref, o_ref):
    t = x_ref.shape[1]
    si = lax.broadcasted_iota(I32, (t, t), 0)
    ti = lax.broadcasted_iota(I32, (t, t), 1)
    tri = (si <= ti).astype(F32)
    o_ref[...] = jnp.dot(x_ref[...], tri, precision=lax.Precision.HIGHEST, preferred_element_type=F32)


def _cumsum_lanes(x, *, block_rows):
    rows, t = x.shape
    return pl.pallas_call(
        _cumsum_kernel, grid=(rows // block_rows,),
        in_specs=[pl.BlockSpec((block_rows, t), lambda i: (i, 0))],
        out_specs=pl.BlockSpec((block_rows, t), lambda i: (i, 0)),
        out_shape=jax.ShapeDtypeStruct((rows, t), F32),
        compiler_params=_params(("parallel",)),
        name="cumsum",
    )(x)


def _fbias_kernel(s_ref, pq_ref, pk_ref, cq_ref, ck_ref, qx_ref, kx_ref, run_ref):
    @pl.when(pl.program_id(1) == 0)
    def _():
        run_ref[...] = jnp.zeros_like(run_ref)

    s = s_ref[...]
    L = s.shape[0]
    ti = lax.broadcasted_iota(I32, (L, L), 0)
    si = lax.broadcasted_iota(I32, (L, L), 1)
    tri = jnp.where(si <= ti, 1.0, 0.0).astype(BF16)

    def split3(x):
        hi = x.astype(BF16)
        r1 = x - hi.astype(F32)
        mid = r1.astype(BF16)
        return jnp.concatenate([hi, mid, (r1 - mid.astype(F32)).astype(BF16)], axis=1)

    cs = jnp.dot(tri, split3(s), preferred_element_type=F32)
    f_all = (cs[:, 0:LANES] + cs[:, LANES:2 * LANES] + cs[:, 2 * LANES:3 * LANES]) + run_ref[...]
    run_ref[...] = f_all[L - 1:L, :]
    terms = split3(f_all * LOG2E)
    qx_ref[...] = jnp.dot(terms, pq_ref[...], preferred_element_type=F32) + cq_ref[...]
    kx_ref[...] = (jnp.dot(terms, pk_ref[...], preferred_element_type=F32) + ck_ref[...]).astype(BF16)


def _fbias(small, *, batch, seq, fh, lf0):
    assert BIAS_LANES * fh <= LANES
    n = small.shape[0]
    L = 4 * LANES if seq % (4 * LANES) == 0 else LANES
    nc = seq // L
    pq = np.zeros((3 * LANES, LANES), np.float32)
    pk = np.zeros((3 * LANES, LANES), np.float32)
    cq = np.zeros((1, LANES), np.float32)
    ck = np.zeros((1, LANES), np.float32)
    for h in range(fh):
        for j in range(3):
            pq[j * LANES + lf0 + h, BIAS_LANES * h + j] = 1.0
            pk[j * LANES + lf0 + h, BIAS_LANES * h + 3 + j] = -1.0
            cq[0, BIAS_LANES * h + 3 + j] = 1.0
            ck[0, BIAS_LANES * h + j] = 1.0
    const = lambda b, c: (0, 0)
    tok = lambda b, c: (b * nc + c, 0)
    return pl.pallas_call(
        _fbias_kernel, grid=(batch, nc),
        in_specs=[pl.BlockSpec((L, LANES), tok),
                  pl.BlockSpec((3 * LANES, LANES), const), pl.BlockSpec((3 * LANES, LANES), const),
                  pl.BlockSpec((1, LANES), const), pl.BlockSpec((1, LANES), const)],
        out_specs=[pl.BlockSpec((L, LANES), tok), pl.BlockSpec((L, LANES), tok)],
        out_shape=[jax.ShapeDtypeStruct((n, LANES), F32), jax.ShapeDtypeStruct((n, LANES), BF16)],
        scratch_shapes=[pltpu.VMEM((1, LANES), F32)],
        compiler_params=_params(("parallel", "arbitrary")),
        name="fbias",
    )(small, jnp.asarray(pq, BF16), jnp.asarray(pk, BF16), jnp.asarray(cq), jnp.asarray(ck))


def _fox_prompt_kernel(q_ref, qx_ref, k_ref, kx_ref, v_ref, o_ref, *, seq, tq, gh, fd):
    nq = seq // tq
    half = tq // 2
    causal_top = lax.broadcasted_iota(I32, (half, half), 1) <= lax.broadcasted_iota(I32, (half, half), 0)
    causal_bot = lax.broadcasted_iota(I32, (half, tq), 1) <= lax.broadcasted_iota(I32, (half, tq), 0) + half

    def softmax_step(carry, s, v):
        m, l, acc = carry
        m_new = jnp.maximum(m, jnp.max(s, axis=1, keepdims=True))
        a = jnp.exp2(m - m_new)
        p = jnp.exp2(s - m_new)
        l = a * l + jnp.sum(p, axis=1, keepdims=True)
        acc = a * acc + jnp.dot(p.astype(BF16), v, preferred_element_type=F32)
        return m_new, l, acc

    head0 = pl.program_id(1) * gh
    bias_head = lax.broadcasted_iota(I32, (tq, LANES), 1) // BIAS_LANES

    def q_body(qi, _):
        q0 = pl.multiple_of(qi * tq, tq)
        qx = qx_ref[pl.ds(q0, tq), :]
        qas = [jnp.concatenate([q_ref[pl.ds(q0, tq), g * fd:(g + 1) * fd],
                                jnp.where(bias_head == head0 + g, qx, 0.0).astype(BF16)], axis=1)
               for g in range(gh)]

        def block(g, carry_g, k0):
            gs = slice(g * fd, (g + 1) * fd)
            ka = jnp.concatenate([k_ref[pl.ds(k0, tq), gs], kx_ref[pl.ds(k0, tq), :]], axis=1)
            s = _nt_dot(qas[g], ka)
            return softmax_step(carry_g, s, v_ref[pl.ds(k0, tq), gs])

        def kv_body(kj, carry):
            k0 = pl.multiple_of(kj * tq, tq)
            return tuple(block(g, carry[g], k0) for g in range(gh))

        init = tuple((jnp.full((tq, 1), NEG_INF, F32), jnp.zeros((tq, 1), F32), jnp.zeros((tq, fd), F32))
                     for _ in range(gh))
        carry = lax.fori_loop(0, qi, kv_body, init)
        for g in range(gh):
            gs = slice(g * fd, (g + 1) * fd)
            m, l, acc = carry[g]
            for r0, nk, mask in ((0, half, causal_top), (half, tq, causal_bot)):
                ka = jnp.concatenate([k_ref[pl.ds(q0, nk), gs], kx_ref[pl.ds(q0, nk), :]], axis=1)
                s = jnp.where(mask, _nt_dot(qas[g][r0:r0 + half], ka), NEG_INF)
                part = (m[r0:r0 + half], l[r0:r0 + half], acc[r0:r0 + half])
                _, l_out, acc_out = softmax_step(part, s, v_ref[pl.ds(q0, nk), gs])
                o_ref[pl.ds(q0 + r0, half), gs] = (acc_out / l_out).astype(o_ref.dtype)
        return 0

    lax.fori_loop(0, nq, q_body, 0)


def _fox_prompt(fq, qx, kb, kx, vb, *, batch, seq, fh, tq=1024, gh=4):
    n, d = fq.shape
    fd = d // fh
    blk = lambda b, h: (b, h)
    bias = pl.BlockSpec((seq, LANES), lambda b, h: (b, 0))
    kern = functools.partial(_fox_prompt_kernel, seq=seq, tq=tq, gh=gh, fd=fd)
    return pl.pallas_call(
        kern, grid=(batch, fh // gh),
        in_specs=[pl.BlockSpec((seq, gh * fd), blk), bias, pl.BlockSpec((seq, gh * fd), blk), bias,
                  pl.BlockSpec((seq, gh * fd), blk)],
        out_specs=pl.BlockSpec((seq, gh * fd), blk),
        out_shape=jax.ShapeDtypeStruct((n, d), BF16),
        compiler_params=_params(("parallel", "parallel")),
        name="fox_prompt",
    )(fq, qx, kb, kx, vb)


def _fox_sample_kernel(pt_ref, q_ref, kn_ref, vn_ref, s_ref, *rest, n_pages, page, fh, fd, ts, lf0):
    floc = rest[0:n_pages]
    kpages = rest[n_pages:2 * n_pages]
    vpages = rest[2 * n_pages:3 * n_pages]
    o_ref = rest[3 * n_pages]
    p_scr, pn_scr, l_scr = rest[3 * n_pages + 1:]
    d = fh * fd
    rows = fh * ts

    def scores():
        q = q_ref[...].astype(F32)
        qt = jnp.concatenate([q] * fh, axis=0)
        r_head = lax.broadcasted_iota(I32, (rows, d), 0) // ts
        c_head = lax.broadcasted_iota(I32, (rows, d), 1) // fd
        qbd = jnp.where(r_head == c_head, qt, 0.0).astype(BF16)

        off = jnp.zeros((fh, 1), F32)
        fk_pages = []
        for i in range(n_pages):
            fp = floc[i][...] + off
            fk_pages.append(fp)
            off = fp[:, page - 1:page]
        fk = jnp.concatenate(fk_pages, axis=1)
        fk_rows = jnp.concatenate(
            [jnp.broadcast_to(fk[h:h + 1, :], (ts, fk.shape[1])) for h in range(fh)], axis=0)

        sm = s_ref[...]
        ri = lax.broadcasted_iota(I32, (ts, ts), 0)
        ci = lax.broadcasted_iota(I32, (ts, ts), 1)
        eye = ci == ri
        tri = ci <= ri
        fq_cols, bias_new = [], []
        for h in range(fh):
            lf_col = sm[:, lf0 + h:lf0 + h + 1]
            lf_row = _col_to_row(lf_col, eye)
            cum_col = jnp.sum(jnp.where(tri, lf_row, 0.0), axis=1, keepdims=True)
            fq_h = off[h:h + 1, :] + cum_col
            fq_cols.append(fq_h)
            bias_new.append(fq_h - _col_to_row(fq_h, eye))
        fq_col = jnp.concatenate(fq_cols, axis=0)
        bias_n = jnp.concatenate(bias_new, axis=0)
        causal_n = jnp.concatenate([tri] * fh, axis=0)

        s_parts = []
        for i in range(0, n_pages, 2):
            kp = jnp.concatenate(
                [jnp.concatenate([kpages[i + j][pl.ds(h, page, stride=fh), :] for h in range(fh)], axis=1)
                 for j in range(2)],
                axis=0).astype(BF16)
            s_parts.append(_nt_dot(qbd, kp))
        s_past = jnp.concatenate(s_parts, axis=1) + (fq_col - fk_rows)
        s_new = _nt_dot(qbd, kn_ref[...].astype(BF16)) + bias_n
        s_new = jnp.where(causal_n, s_new, NEG_INF)
        m = jnp.maximum(jnp.max(s_past, axis=1, keepdims=True), jnp.max(s_new, axis=1, keepdims=True))
        p_past = jnp.exp(s_past - m)
        p_new = jnp.exp(s_new - m)
        l_scr[...] = jnp.sum(p_past, axis=1, keepdims=True) + jnp.sum(p_new, axis=1, keepdims=True)
        p_scr[...] = p_past.astype(BF16)
        pn_scr[...] = p_new

    def values():
        acc = jnp.dot(pn_scr[...].astype(BF16), vn_ref[...].astype(BF16), preferred_element_type=F32)
        for i in range(0, n_pages, 2):
            vp = jnp.concatenate(
                [jnp.concatenate([vpages[i + j][pl.ds(h, page, stride=fh), :] for h in range(fh)], axis=1)
                 for j in range(2)],
                axis=0).astype(BF16)
            acc = acc + jnp.dot(p_scr[:, i * page:(i + 2) * page], vp, preferred_element_type=F32)
        acc = acc / l_scr[...]
        o_ref[...] = jnp.concatenate(
            [acc[h * ts:(h + 1) * ts, h * fd:(h + 1) * fd] for h in range(fh)], axis=1)

    scores()
    values()


def _fox_sample(page_table, fq, kb, vb, small, floc, cache_k, cache_v, *, fh, lf0):
    bs, n_pages = page_table.shape
    n, d = fq.shape
    ts = n // bs
    fd = d // fh
    page = cache_k.shape[1] // fh
    rows = fh * ts
    tok = lambda b, pt: (b, 0)

    def pmap(i):
        return lambda b, pt: (pt[b, i], 0, 0)

    in_specs = ([pl.BlockSpec((ts, d), tok), pl.BlockSpec((ts, d), tok), pl.BlockSpec((ts, d), tok),
                 pl.BlockSpec((ts, LANES), tok)]
                + [pl.BlockSpec((None, fh, page), pmap(i)) for i in range(n_pages)]
                + [pl.BlockSpec((None, page * fh, fd), pmap(i)) for i in range(n_pages)]
                + [pl.BlockSpec((None, page * fh, fd), pmap(i)) for i in range(n_pages)])
    kern = functools.partial(_fox_sample_kernel, n_pages=n_pages, page=page, fh=fh, fd=fd, ts=ts, lf0=lf0)
    return pl.pallas_call(
        kern,
        grid_spec=pltpu.PrefetchScalarGridSpec(
            num_scalar_prefetch=1, grid=(bs,), in_specs=in_specs,
            out_specs=pl.BlockSpec((ts, d), tok),
            scratch_shapes=[pltpu.VMEM((rows, n_pages * page), BF16),
                            pltpu.VMEM((rows, ts), F32),
                            pltpu.VMEM((rows, 1), F32)]),
        out_shape=jax.ShapeDtypeStruct((n, d), F32),
        compiler_params=_params(("arbitrary",)),
        name="fox_sample",
    )(page_table, fq, kb, vb, small, *([floc] * n_pages), *([cache_k] * n_pages), *([cache_v] * n_pages))


def _post_kernel(xp_ref, hap_ref, obp_ref, gp_ref, xs_ref, has_ref, obs_ref, gs_ref, *rest, tiles_p, **kw):
    cnt_ref = rest[-1]

    @pl.when(pl.program_id(0) == 0)
    def _():
        cnt_ref[...] = jnp.zeros_like(cnt_ref)

    @pl.when(pl.program_id(0) < tiles_p)
    def _():
        _post_body(xp_ref, hap_ref, obp_ref, gp_ref, *rest, **kw)

    @pl.when(pl.program_id(0) >= tiles_p)
    def _():
        _post_body(xs_ref, has_ref, obs_ref, gs_ref, *rest, **kw)


def _post_body(x_ref, ha_ref, ob_ref, gate_ref, wa_ref, wb_ref, wo_ref, g2_ref, wr_ref, br_ref,
               x2_ref, xn_ref, route_ref, cnt_ref, *, d, n_exp, n_groups):
    ba = jnp.dot(ha_ref[...].astype(BF16), wa_ref[...], preferred_element_type=F32)
    bb = jnp.dot(ob_ref[...].astype(BF16), wb_ref[...], preferred_element_type=F32)
    gates = gate_ref[...].astype(F32)
    merged = gates[:, 0:d] * ba + gates[:, d:2 * d] * bb
    x2 = x_ref[...] + jnp.dot(merged.astype(BF16), wo_ref[...], preferred_element_type=F32)
    x2_ref[...] = x2
    ms = jnp.mean(x2 * x2, axis=-1, keepdims=True)
    xn = x2 * lax.rsqrt(ms + RMS_EPS) * g2_ref[...]
    tm = xn.shape[0]
    for j in range(d // LANES):
        xn_ref[pl.ds(j, tm, stride=d // LANES), :] = xn[:, j * LANES:(j + 1) * LANES]

    xh = xn.astype(BF16)
    xl = (xn - xh.astype(F32)).astype(BF16)
    hh = jnp.dot(xh, wr_ref[...], preferred_element_type=F32)
    logits = (hh[:, 0:LANES] + hh[:, LANES:2 * LANES]
              + jnp.dot(xl, wr_ref[:, 0:LANES], preferred_element_type=F32)
              + br_ref[...])
    lane = lax.broadcasted_iota(I32, logits.shape, 1)
    lane_f = lane.astype(F32)
    big = float(LANES)
    epg = n_exp // n_groups
    in_groups = (lane >= n_exp) & (lane < n_exp + n_groups)
    gl = jnp.where(in_groups, logits, NEG_INF)
    gmax = jnp.max(gl, axis=1, keepdims=True)
    gidx = jnp.min(jnp.where(gl == gmax, lane_f, big), axis=1, keepdims=True) - float(n_exp)
    g_p = 1.0 / jnp.sum(jnp.exp(gl - gmax), axis=1, keepdims=True)
    in_group = (lane < n_exp) & ((lane // epg).astype(F32) == gidx)
    el = jnp.where(in_group, logits, NEG_INF)
    m1 = jnp.max(el, axis=1, keepdims=True)
    i1 = jnp.min(jnp.where(el == m1, lane_f, big), axis=1, keepdims=True)
    el2 = jnp.where(lane_f == i1, NEG_INF, el)
    m2 = jnp.max(el2, axis=1, keepdims=True)
    i2 = jnp.min(jnp.where(el2 == m2, lane_f, big), axis=1, keepdims=True)
    esum = jnp.sum(jnp.exp(el - m1), axis=1, keepdims=True)
    p1 = 1.0 / esum
    p2 = jnp.exp(m2 - m1) / esum
    psum = p1 + p2
    w1 = p1 / psum * g_p
    w2 = p2 / psum * g_p
    r0, r1 = _rank_block(i1, i2, lane_f, cnt_ref)
    route_ref[...] = jnp.where(lane == 0, i1, jnp.where(lane == 1, i2, jnp.where(
        lane == 2, w1, jnp.where(lane == 3, w2, jnp.where(lane == 4, r0, jnp.where(lane == 5, r1, 0.0))))))


def _post(acts_p, acts_s, wa, wb, wo, g2, wr, br, *, n_exp, n_groups, tm=256):
    n_p, d = acts_p[0].shape
    n_s = acts_s[0].shape[0]
    tiles_p, tiles_s = n_p // tm, n_s // tm
    n = n_p + n_s
    rpt = d // LANES
    row_p = lambda i: (jnp.minimum(i, tiles_p - 1), 0)
    row_s = lambda i: (jnp.maximum(i - tiles_p, 0), 0)
    row = lambda i: (i, 0)
    const = lambda i: (0, 0)
    kern = functools.partial(_post_kernel, tiles_p=tiles_p, d=d, n_exp=n_exp, n_groups=n_groups)
    wspec = lambda: pl.BlockSpec((d, d), const, pipeline_mode=pl.Buffered(1))
    act_specs = lambda r: [pl.BlockSpec((tm, d), r), pl.BlockSpec((tm, d), r), pl.BlockSpec((tm, d), r),
                           pl.BlockSpec((tm, 2 * d), r)]
    return pl.pallas_call(
        kern, grid=(tiles_p + tiles_s,),
        in_specs=act_specs(row_p) + act_specs(row_s) + [
            wspec(), wspec(), wspec(),
            pl.BlockSpec((1, d), const), pl.BlockSpec((d, 2 * LANES), const), pl.BlockSpec((1, LANES), const)],
        out_specs=[pl.BlockSpec((tm, d), row), pl.BlockSpec((tm * rpt, LANES), row),
                   pl.BlockSpec((tm, LANES), row), pl.BlockSpec((1, LANES), const)],
        out_shape=[jax.ShapeDtypeStruct((n, d), F32), jax.ShapeDtypeStruct((n * rpt, LANES), F32),
                   jax.ShapeDtypeStruct((n, LANES), F32), jax.ShapeDtypeStruct((1, LANES), F32)],
        compiler_params=_params(("arbitrary",)),
        name="post",
    )(*acts_p, *acts_s, wa, wb, wo, g2, wr, br)


def _dispatch_kernel(ps_ref, pl_ref, nu_ref, dest_ref, xn_ref, xs_hbm, stage, zbuf, sem_rows, sem_pad,
                     *, tm, n_tiles, n_exp, nt):
    i = pl.program_id(0)
    slot = i % 2

    def pad_dmas(act):
        for e in range(n_exp):
            pos = ps_ref[e]
            left = pl_ref[e]
            for c in PAD_CHUNKS:
                @pl.when((left & c) != 0)
                def _(pos=pos, c=c):
                    act(pltpu.make_async_copy(zbuf.at[pl.ds(0, c)], xs_hbm.at[pl.ds(pos, c)], sem_pad))
                pos = pos + (left & c)

        def unused_tile(t, carry):
            for part in range(tm // PAD_CHUNKS[0]):
                act(pltpu.make_async_copy(
                    zbuf, xs_hbm.at[pl.ds(t * tm + part * PAD_CHUNKS[0], PAD_CHUNKS[0])], sem_pad))
            return carry
        lax.fori_loop(nu_ref[0], nt, unused_tile, 0)

    @pl.when(i == 0)
    def _():
        zbuf[...] = jnp.zeros_like(zbuf)
        pad_dmas(lambda cp: cp.start())

    def wait_tile(s):
        for _ in range(2):
            pltpu.make_async_copy(stage.at[s], xs_hbm.at[pl.ds(0, tm)], sem_rows.at[s]).wait()

    @pl.when(i >= 2)
    def _():
        wait_tile(slot)

    stage[slot] = xn_ref[...]

    def body(j, c):
        for u in range(DMA_UNROLL // 2):
            r = j * (DMA_UNROLL // 2) + u
            for kk in range(2):
                pltpu.make_async_copy(stage.at[slot, r], xs_hbm.at[dest_ref[0, kk, r]],
                                      sem_rows.at[slot]).start(priority=kk)
        return c
    lax.fori_loop(0, tm // (DMA_UNROLL // 2), body, 0)

    @pl.when(i == n_tiles - 1)
    def _():
        if n_tiles > 1:
            wait_tile(1 - slot)
        wait_tile(slot)
        pad_dmas(lambda cp: cp.wait())


def _dispatch(pad_start, pad_len, n_used, dest_t, xn_rows, *, nt, tm):
    n_tiles = dest_t.shape[0]
    n_exp = pad_start.shape[0]
    rpt = xn_rows.shape[0] // (n_tiles * tm)
    xn3 = xn_rows.reshape(n_tiles * tm, rpt, LANES)
    kern = functools.partial(_dispatch_kernel, tm=tm, n_tiles=n_tiles, n_exp=n_exp, nt=nt)
    return pl.pallas_call(
        kern,
        grid_spec=pltpu.PrefetchScalarGridSpec(
            num_scalar_prefetch=3, grid=(n_tiles,),
            in_specs=[pl.BlockSpec((1, 8, tm), lambda i, ps, pln, nu: (i, 0, 0), memory_space=pltpu.SMEM),
                      pl.BlockSpec((tm, rpt, LANES), lambda i, ps, pln, nu: (i, 0, 0))],
            out_specs=pl.BlockSpec(memory_space=pl.ANY),
            scratch_shapes=[pltpu.VMEM((2, tm, rpt, LANES), F32),
                            pltpu.VMEM((PAD_CHUNKS[0], rpt, LANES), F32),
                            pltpu.SemaphoreType.DMA((2,)), pltpu.SemaphoreType.DMA(())]),
        out_shape=jax.ShapeDtypeStruct((nt * tm, rpt, LANES), F32),
        compiler_params=_params(("arbitrary",)),
        name="dispatch",
    )(pad_start, pad_len, n_used, dest_t, xn3)


def _experts_kernel(te_ref, nu_ref, x_hbm, wg_ref, wu_ref, wd_ref, y_ref, wg_b, wu_b, wd_b, xbuf, sem,
                    *, tm, rpt):
    t = pl.program_id(0)
    n_used = nu_ref[0]
    rows = tm * rpt

    def fetch(tile):
        slot = tile % X_BUFFERS
        start = pl.multiple_of(tile * rows, rows)
        return pltpu.make_async_copy(x_hbm.at[pl.ds(start, rows), :], xbuf.at[slot], sem.at[slot])

    @pl.when(t == 0)
    def _():
        for ahead in range(X_BUFFERS - 1):
            @pl.when(ahead < n_used)
            def _(ahead=ahead):
                fetch(ahead).start()

    @pl.when(t + (X_BUFFERS - 1) < n_used)
    def _():
        fetch(t + (X_BUFFERS - 1)).start()

    @pl.when((t == 0) | (te_ref[t] != te_ref[jnp.maximum(t - 1, 0)]))
    def _():
        wg_b[...] = wg_ref[0].astype(BF16)
        wu_b[...] = wu_ref[0].astype(BF16)
        wd_b[...] = wd_ref[0].astype(BF16)

    @pl.when(t < n_used)
    def _():
        fetch(t).wait()
        slot = t % X_BUFFERS
        x = jnp.concatenate([xbuf[slot, pl.ds(j, tm, stride=rpt), :] for j in range(rpt)],
                            axis=1).astype(BF16)
        g = jnp.dot(x, wg_b[...], preferred_element_type=F32)
        u = jnp.dot(x, wu_b[...], preferred_element_type=F32)
        hg = (g * _sigmoid(g)) * u
        y = jnp.dot(hg.astype(BF16), wd_b[...], preferred_element_type=F32)
        for j in range(rpt):
            y_ref[pl.ds(j, tm, stride=rpt), :] = y[:, j * LANES:(j + 1) * LANES]

    @pl.when(t >= n_used)
    def _():
        y_ref[...] = jnp.zeros_like(y_ref)


def _experts(tile_expert, n_used, x_sorted, w_gate, w_up, w_down, *, tm):
    nt = tile_expert.shape[0]
    n_exp, d, de = w_gate.shape
    rpt = d // LANES
    emap = lambda t, te, nu: (te[t], 0, 0)
    kern = functools.partial(_experts_kernel, tm=tm, rpt=rpt)
    return pl.pallas_call(
        kern,
        grid_spec=pltpu.PrefetchScalarGridSpec(
            num_scalar_prefetch=2, grid=(nt,),
            in_specs=[pl.BlockSpec(memory_space=pl.ANY),
                      pl.BlockSpec((1, d, de), emap), pl.BlockSpec((1, d, de), emap),
                      pl.BlockSpec((1, de, d), emap)],
            out_specs=pl.BlockSpec((tm * rpt, LANES), lambda t, te, nu: (t, 0)),
            scratch_shapes=[pltpu.VMEM((d, de), BF16), pltpu.VMEM((d, de), BF16), pltpu.VMEM((de, d), BF16),
                            pltpu.VMEM((X_BUFFERS, tm * rpt, LANES), F32),
                            pltpu.SemaphoreType.DMA((X_BUFFERS,))]),
        out_shape=jax.ShapeDtypeStruct((nt * tm * rpt, LANES), F32),
        compiler_params=_params(("arbitrary",)),
        name="experts",
    )(tile_expert, n_used, x_sorted, w_gate, w_up, w_down)


def _combine_kernel(idx_cur, idx_nxt, x2_ref, route_ref, ys_hbm, y_ref, buf, sem, *, tm, n_tiles, rpt):
    i = pl.program_id(0)
    slot = i % 2

    def gather(idx_ref, dst_slot):
        def body(j, c):
            for u in range(DMA_UNROLL // 2):
                r = j * (DMA_UNROLL // 2) + u
                for kk in range(2):
                    src = pl.multiple_of(idx_ref[0, kk, r] * rpt, rpt)
                    pltpu.make_async_copy(ys_hbm.at[pl.ds(src, rpt), :],
                                          buf.at[dst_slot, kk, pl.ds(r * rpt, rpt), :],
                                          sem.at[dst_slot]).start(priority=kk)
            return c
        lax.fori_loop(0, tm // (DMA_UNROLL // 2), body, 0)

    @pl.when(i == 0)
    def _():
        gather(idx_cur, 0)

    @pl.when(i + 1 < n_tiles)
    def _():
        gather(idx_nxt, 1 - slot)

    for kk in range(2):
        pltpu.make_async_copy(ys_hbm.at[pl.ds(0, tm * rpt), :], buf.at[slot, kk], sem.at[slot]).wait()
    route = route_ref[...]
    w0 = route[:, 2:3]
    w1 = route[:, 3:4]
    for j in range(rpt):
        sl = slice(j * LANES, (j + 1) * LANES)
        y_ref[:, sl] = (x2_ref[:, sl] + w0 * buf[slot, 0, pl.ds(j, tm, stride=rpt), :]
                        + w1 * buf[slot, 1, pl.ds(j, tm, stride=rpt), :])


def _combine(dest_t, x2, route, y_sorted, *, row0, rows, tm=256):
    d = x2.shape[1]
    rpt = d // LANES
    t0 = row0 // tm
    n_tiles = rows // tm
    kern = functools.partial(_combine_kernel, tm=tm, n_tiles=n_tiles, rpt=rpt)
    return pl.pallas_call(
        kern, grid=(n_tiles,),
        in_specs=[pl.BlockSpec((1, 8, tm), lambda i: (t0 + i, 0, 0), memory_space=pltpu.SMEM),
                  pl.BlockSpec((1, 8, tm), lambda i: (t0 + jnp.minimum(i + 1, n_tiles - 1), 0, 0),
                               memory_space=pltpu.SMEM),
                  pl.BlockSpec((tm, d), lambda i: (t0 + i, 0)),
                  pl.BlockSpec((tm, LANES), lambda i: (t0 + i, 0)),
                  pl.BlockSpec(memory_space=pl.ANY)],
        out_specs=pl.BlockSpec((tm, d), lambda i: (i, 0)),
        out_shape=jax.ShapeDtypeStruct((rows, d), F32),
        scratch_shapes=[pltpu.VMEM((2, 2, tm * rpt, LANES), F32), pltpu.SemaphoreType.DMA((2,))],
        compiler_params=_params(("arbitrary",)),
        name="combine",
    )(dest_t, dest_t, x2, route, y_sorted)


def _rank_block(i1, i2, lane_f, cnt_ref):
    tm = i1.shape[0]
    oh0 = lane_f == i1
    oh1 = lane_f == i2
    oh = jnp.where(oh0, 1.0, jnp.where(oh1, 1.0, 0.0))
    ri = lax.broadcasted_iota(I32, (tm, tm), 0)
    ci = lax.broadcasted_iota(I32, (tm, tm), 1)
    earlier = jnp.where(ci < ri, 1.0, 0.0).astype(BF16)
    base = cnt_ref[...] + jnp.dot(earlier, oh.astype(BF16), preferred_element_type=F32)
    r0 = jnp.sum(jnp.where(oh0, base, 0.0), axis=1, keepdims=True)
    r1 = jnp.sum(jnp.where(oh1, base, 0.0), axis=1, keepdims=True)
    cnt_ref[...] += jnp.sum(oh, axis=0, keepdims=True)
    return r0, r1


def _dest_kernel(route_ref, start_ref, dest_ref, *, tm, group):
    start = start_ref[...]
    for t in range(group):
        route = route_ref[t * tm:(t + 1) * tm, :]
        lane = lax.broadcasted_iota(I32, route.shape, 1)
        lane_f = lane.astype(F32)
        d0 = route[:, 4:5] + jnp.sum(jnp.where(lane_f == route[:, 0:1], start, 0.0), axis=1, keepdims=True)
        d1 = route[:, 5:6] + jnp.sum(jnp.where(lane_f == route[:, 1:2], start, 0.0), axis=1, keepdims=True)
        dd = jnp.where(lane == 0, d0, jnp.where(lane == 1, d1, 0.0))
        dest_ref[t] = dd.T[0:8, :].astype(I32)


def _dest(route, start_row, *, tm):
    n = route.shape[0]
    tiles = n // tm
    group = 4 if tiles % 4 == 0 else 1
    return pl.pallas_call(
        functools.partial(_dest_kernel, tm=tm, group=group), grid=(tiles // group,),
        in_specs=[pl.BlockSpec((group * tm, LANES), lambda i: (i, 0)),
                  pl.BlockSpec((1, LANES), lambda i: (0, 0))],
        out_specs=pl.BlockSpec((group, 8, tm), lambda i: (i, 0, 0)),
        out_shape=jax.ShapeDtypeStruct((tiles, 8, tm), I32),
        compiler_params=_params(("parallel",)),
        name="dest",
    )(route, start_row)


def _moe_plan(route, cnt_f, *, n_exp, tm):
    n = route.shape[0]
    nt = (2 * n + n_exp * (tm - 1)) // tm
    cnt = cnt_f[0, :n_exp].astype(I32)
    ptiles = (cnt + tm - 1) // tm
    tile_end = jnp.cumsum(ptiles)
    tile_start = tile_end - ptiles
    start_row = jnp.pad((tile_start * tm).astype(F32), (0, LANES - n_exp)).reshape(1, LANES)
    dest_t = _dest(route, start_row, tm=tm)
    n_used = tile_end[-1]
    tiles = jnp.arange(nt, dtype=I32)
    te = jnp.minimum(jnp.sum((tile_end[None, :] <= tiles[:, None]).astype(I32), axis=1), n_exp - 1)
    te_last = jnp.take(te, jnp.maximum(n_used - 1, 0))
    tile_expert = jnp.where(tiles < n_used, te, te_last)
    pad_start = tile_start * tm + cnt
    pad_len = ptiles * tm - cnt
    return tile_expert, n_used.reshape(1).astype(I32), dest_t, pad_start, pad_len, nt


def kernel(x_prompt, x_sample, cache_k, cache_v, cache_lf, state_C, state_n, state_m, page_table,
           norm1_g, w_in, b_igate, b_fgate_mlstm, b_fgate_fox, mlstm_norm_g, q_norm_g, k_norm_g,
           w_branch_mlstm, w_branch_fox, w_out, norm2_g, w_router_group, b_router_group,
           w_router_expert, b_router_expert, w_gate, w_up, w_down):
    depth = w_in.shape[0]
    assert depth == 1, "single-layer step"
    bp, tp, d = x_prompt.shape
    bs, ts, _ = x_sample.shape
    mh = b_igate.shape[-1]
    fh = b_fgate_fox.shape[-1]
    fd = q_norm_g.shape[-1]
    md = d // mh
    n_exp = w_gate.shape[1]
    n_pages = page_table.shape[1]
    page = cache_k.shape[2]
    assert 2 * mh + fh <= LANES and n_exp + N_GROUPS <= LANES
    np_tok, ns_tok = bp * tp, bs * ts
    l = 0

    w = w_in[l]
    o = 0
    secs = {}
    for name, width in (("m", 4 * d), ("mi", mh), ("mf", mh), ("f", 3 * d), ("ff", fh), ("g", 2 * d)):
        secs[name] = w[:, o:o + width]
        o += width
    pad = jnp.zeros((d, LANES - 2 * mh - fh), F32)
    w_secs = (secs["m"].astype(BF16), secs["f"].astype(BF16), secs["g"].astype(BF16),
              jnp.concatenate([secs["mi"], secs["mf"], secs["ff"], pad], axis=1).astype(BF16))
    bias_s = jnp.concatenate([b_igate[l], b_fgate_mlstm[l], b_fgate_fox[l],
                              jnp.zeros((LANES - 2 * mh - fh,), F32)]).reshape(1, LANES)
    g1 = norm1_g[l].reshape(1, d)
    qg = q_norm_g[l].reshape(1, fd)
    kg = k_norm_g[l].reshape(1, fd)
    g_m = mlstm_norm_g[l].reshape(1, d)
    wa = w_branch_mlstm[l].astype(BF16)
    wb = w_branch_fox[l].astype(BF16)
    wo = w_out[l].astype(BF16)
    g2 = norm2_g[l].reshape(1, d)
    wr32 = jnp.concatenate([w_router_expert[l], w_router_group[l],
                            jnp.zeros((d, LANES - n_exp - N_GROUPS), F32)], axis=1)
    wr_hi = wr32.astype(BF16)
    wr = jnp.concatenate([wr_hi, (wr32 - wr_hi.astype(F32)).astype(BF16)], axis=1)
    br = jnp.concatenate([b_router_expert[l], b_router_group[l],
                          jnp.zeros((LANES - n_exp - N_GROUPS,), F32)]).reshape(1, LANES)

    inproj = functools.partial(_inproj, g1=g1, w_secs=w_secs, bias_s=bias_s, qg=qg, kg=kg, mh=mh, fh=fh)
    pm_p, fq_p, k3_p, v3_p, kb_p, vb_p, gate_p, small_p, kt_p = inproj(
        x_prompt.reshape(np_tok, d), act_dtype=BF16, q_unit=LOG2E, emit_kt=True)
    pm_s, fq_s, k3_s, v3_s, kb_s, vb_s, gate_s, small_s = inproj(x_sample.reshape(ns_tok, d), act_dtype=F32,
                                                                 tm=256)

    ha_p, c_p, n_p, m_p = _mlstm_prompt(pm_p, kt_p, small_p, g_m, batch=bp, seq=tp, mh=mh)
    m0 = jnp.pad(state_m[l], ((0, 0), (0, LANES - mh))).reshape(bs, 1, LANES)
    ha_s, c_s, n_s, m_s = _mlstm(pm_s, small_s, g_m, (state_C[l], state_n[l], m0),
                                 batch=bs, seq=ts, mh=mh, nb=8, out_dtype=F32)

    lf_p = small_p[:, 2 * mh:2 * mh + fh]
    lf_s = small_s[:, 2 * mh:2 * mh + fh]
    qx_p, kx_p = _fbias(small_p, batch=bp, seq=tp, fh=fh, lf0=2 * mh)
    ob_p = _fox_prompt(fq_p, qx_p, kb_p, kx_p, vb_p, batch=bp, seq=tp, fh=fh)

    n_phys = cache_k.shape[1]
    lft_c = cache_lf[l].transpose(0, 2, 1).reshape(n_phys * fh, page)
    floc = _cumsum_lanes(lft_c, block_rows=n_phys * fh // 8).reshape(n_phys, fh, page)
    ck = cache_k[l].reshape(n_phys, page * fh, fd)
    cv = cache_v[l].reshape(n_phys, page * fh, fd)
    ob_s = _fox_sample(page_table, fq_s, kb_s, vb_s, small_s, floc, ck, cv, fh=fh, lf0=2 * mh)

    tm = MOE_TILE
    x2, xn_rows, route, cnt_f = _post((x_prompt.reshape(np_tok, d), ha_p, ob_p, gate_p),
                                      (x_sample.reshape(ns_tok, d), ha_s, ob_s, gate_s),
                                      wa, wb, wo, g2, wr, br, n_exp=n_exp, n_groups=N_GROUPS, tm=tm)

    tile_expert, n_used, dest_t, pad_start, pad_len, nt = _moe_plan(route, cnt_f, n_exp=n_exp, tm=tm)
    x_sorted = _dispatch(pad_start, pad_len, n_used, dest_t, xn_rows, nt=nt, tm=tm)
    y_sorted = _experts(tile_expert, n_used, x_sorted.reshape(nt * tm * (d // LANES), LANES),
                        w_gate[l], w_up[l], w_down[l], tm=tm)
    y_p = _combine(dest_t, x2, route, y_sorted, row0=0, rows=np_tok, tm=tm)
    y_s = _combine(dest_t, x2, route, y_sorted, row0=np_tok, rows=ns_tok, tm=tm)

    return (y_p.reshape(bp, tp, d), y_s.reshape(bs, ts, d),
            k3_p.reshape(1, bp, tp, fh, fd), v3_p.reshape(1, bp, tp, fh, fd), lf_p.reshape(1, bp, tp, fh),
            k3_s.reshape(1, bs, ts, fh, fd), v3_s.reshape(1, bs, ts, fh, fd), lf_s.reshape(1, bs, ts, fh),
            c_p[None], n_p[None], m_p[:, 0, :mh][None],
            c_s[None], n_s[None], m_s[:, 0, :mh][None])
```
